```python
import math
import jax, jax.numpy as jnp
from jax import lax
import numpy as np

D_MODEL = 1024
BATCH = 8
SEQ = 2048
DEPTH = 4

N_MIXERS = 3
N_LAYERS_A = len(range(0, DEPTH, N_MIXERS))
N_LAYERS_B = len(range(1, DEPTH, N_MIXERS))
N_LAYERS_C = len(range(2, DEPTH, N_MIXERS))

DEEPNORM_ALPHA = (2.0 * DEPTH) ** 0.25
DEEPNORM_BETA = (8.0 * DEPTH) ** -0.25
LN_EPS = 1e-5

A_HEADS = 8
A_HEAD_DIM = D_MODEL // A_HEADS
MOBA_BLOCK = 256
MOBA_TOPK = 3
MOBA_Q_CHUNK = 16
REL_BUCKETS = 32
REL_MAX_DIST = 128

B_GROUPS = 8
B_WIDTH = 2 * D_MODEL
B_CHUNK = 128

C_HEADS = 4
C_KEY_DIM = D_MODEL // 2
C_VAL_DIM = D_MODEL
C_DK = C_KEY_DIM // C_HEADS
C_DV = C_VAL_DIM // C_HEADS
C_GATE_RANK = 16
C_GATE_NORMALIZER = 16.0
C_CHUNK = 64
C_IN_WIDTH = 2 * C_KEY_DIM + 2 * C_VAL_DIM + C_GATE_RANK

N_EXPERTS = 32
TOP_K = 4
D_EXPERT = D_MODEL
SWIGLU_LIMIT = 7.0
SWIGLU_ALPHA = 1.702
MOE_ROW_BLOCK = 128

kernel_name = "hybrid_moba_gmlp_gla_moe_deepnorm"


def layer_norm(x, g, b):
    xf = x.astype(jnp.float32)
    mu = xf.mean(-1, keepdims=True)
    var = jnp.square(xf - mu).mean(-1, keepdims=True)
    y = (xf - mu) * lax.rsqrt(var + LN_EPS) * g.astype(jnp.float32) + b.astype(jnp.float32)
    return y.astype(x.dtype)


def rms_norm(x, g):
    xf = x.astype(jnp.float32)
    return xf * lax.rsqrt(jnp.mean(xf * xf, -1, keepdims=True) + LN_EPS) * g.astype(jnp.float32)


def t5_bucket(rel):
    n = jnp.maximum(rel, 0)
    max_exact = REL_BUCKETS // 2
    nf = jnp.maximum(n, 1).astype(jnp.float32)
    large = max_exact + (jnp.log(nf / max_exact) / math.log(REL_MAX_DIST / max_exact)
                         * (REL_BUCKETS - max_exact)).astype(jnp.int32)
    large = jnp.minimum(large, REL_BUCKETS - 1)
    return jnp.where(n < max_exact, n, large)


def moba_attention(x, w_in, w_out, rel_bias):
    B, S, _ = x.shape
    H, Dh, BLK, QC = A_HEADS, A_HEAD_DIM, MOBA_BLOCK, MOBA_Q_CHUNK
    qkv = (x @ w_in).reshape(B, S, 3, H, Dh)
    q = qkv[:, :, 0].transpose(0, 2, 1, 3)
    k = qkv[:, :, 1].transpose(0, 2, 1, 3)
    v = qkv[:, :, 2].transpose(0, 2, 1, 3)
    nb = -(-S // BLK)
    pad = nb * BLK - S
    kb = jnp.pad(k, ((0, 0), (0, 0), (0, pad), (0, 0))).reshape(B, H, nb, BLK, Dh)
    vb = jnp.pad(v, ((0, 0), (0, 0), (0, pad), (0, 0))).reshape(B, H, nb, BLK, Dh)
    k_mean = kb.astype(jnp.float32).mean(axis=3)
    gate = jnp.einsum('bhsd,bhnd->bhsn', q.astype(jnp.float32), k_mean)
    pos = jnp.arange(S)
    past = jnp.arange(nb)[None, :] < (pos // BLK)[:, None]
    gate = jnp.where(past, gate, -jnp.inf)
    k_sel = min(MOBA_TOPK, nb)
    sel_score, sel_idx = lax.top_k(gate, k_sel)
    sel_valid = jnp.isfinite(sel_score)

    rel_t = rel_bias.T
    bi = jnp.arange(B)[:, None, None, None]
    hi = jnp.arange(H)[None, :, None, None]
    scale = Dh ** -0.5
    offs = jnp.arange(BLK)

    def attend_chunk(t0):
        qc = lax.dynamic_slice_in_dim(q, t0, QC, axis=2)
        idx = lax.dynamic_slice_in_dim(sel_idx, t0, QC, axis=2)
        valid = lax.dynamic_slice_in_dim(sel_valid, t0, QC, axis=2)
        qpos = t0 + jnp.arange(QC)
        kg = kb[bi, hi, idx]
        vg = vb[bi, hi, idx]
        s_sel = jnp.einsum('bhqd,bhqnjd->bhqnj', qc, kg).astype(jnp.float32) * scale
        rel_sel = qpos[None, None, :, None, None] - (idx[..., None] * BLK + offs)
        s_sel = jnp.where(valid[..., None],
                          s_sel + rel_t[hi[..., None], t5_bucket(rel_sel)], -jnp.inf)
        own = t0 // BLK
        ko = lax.dynamic_index_in_dim(kb, own, axis=2, keepdims=False)
        vo = lax.dynamic_index_in_dim(vb, own, axis=2, keepdims=False)
        rel_own = qpos[:, None] - (own * BLK + offs)[None, :]
        s_own = (jnp.einsum('bhqd,bhjd->bhqj', qc, ko).astype(jnp.float32) * scale
                 + rel_t[:, t5_bucket(rel_own)])
        s_own = jnp.where(rel_own >= 0, s_own, -jnp.inf)
        logits = jnp.concatenate([s_sel.reshape(B, H, QC, k_sel * BLK), s_own], axis=-1)
        p = jax.nn.softmax(logits, axis=-1).astype(v.dtype)
        p_sel = p[..., :k_sel * BLK].reshape(B, H, QC, k_sel, BLK)
        p_own = p[..., k_sel * BLK:]
        return (jnp.einsum('bhqnj,bhqnjd->bhqd', p_sel, vg)
                + jnp.einsum('bhqj,bhjd->bhqd', p_own, vo))

    o = lax.map(attend_chunk, jnp.arange(0, S, QC))
    o = o.transpose(1, 0, 3, 2, 4).reshape(B, S, H * Dh)
    return o @ w_out


def gmlp_mixer(x, w_in, ln_g, ln_b, w_s, b_s, w_out):
    B, S, _ = x.shape
    z = jax.nn.gelu(x @ w_in, approximate=False)
    u, v = jnp.split(z, 2, axis=-1)
    v = layer_norm(v, ln_g, ln_b)
    gw = B_WIDTH // B_GROUPS
    vc = v.reshape(B, S // B_CHUNK, B_CHUNK, B_GROUPS, gw)
    ws = w_s * jnp.tril(jnp.ones((B_CHUNK, B_CHUNK), w_s.dtype))
    mixed = jnp.einsum('gts,bcsge->bctge', ws, vc) + b_s.T[None, None, :, :, None]
    y = u * mixed.reshape(B, S, B_WIDTH)
    return y @ w_out


def gla_mixer(x, w_in, w_gk_up, b_gk, norm_g, w_out):
    B, S, _ = x.shape
    proj = x @ w_in
    splits = [C_KEY_DIM, 2 * C_KEY_DIM, 2 * C_KEY_DIM + C_VAL_DIM, 2 * C_KEY_DIM + 2 * C_VAL_DIM]
    q, k, v, g, gk_low = jnp.split(proj, splits, axis=-1)
    log_a = jax.nn.log_sigmoid((gk_low @ w_gk_up + b_gk).astype(jnp.float32)) / C_GATE_NORMALIZER
    nc = S // C_CHUNK

    def heads(t, dh):
        return t.astype(jnp.float32).reshape(B, nc, C_CHUNK, C_HEADS, dh).transpose(1, 0, 3, 2, 4)

    qh = heads(q, C_DK) * (C_DK ** -0.5)
    kh = heads(k, C_DK)
    vh = heads(v, C_DV)
    ah = heads(log_a, C_DK)
    causal = jnp.tril(jnp.ones((C_CHUNK, C_CHUNK), bool))

    def step(state, inp):
        qc, kc, vc, ac = inp
        b = jnp.cumsum(ac, axis=2)
        o_inter = jnp.einsum('bhck,bhkv->bhcv', qc * jnp.exp(b), state)
        diff = b[:, :, :, None, :] - b[:, :, None, :, :]
        diff = jnp.where(causal[None, None, :, :, None], diff, -jnp.inf)
        att = jnp.einsum('bhik,bhjk,bhijk->bhij', qc, kc, jnp.exp(diff))
        o_intra = jnp.einsum('bhij,bhjv->bhiv', att, vc)
        b_last = b[:, :, -1:, :]
        new_state = (jnp.exp(b_last[:, :, 0, :])[..., None] * state
                     + jnp.einsum('bhck,bhcv->bhkv', kc * jnp.exp(b_last - b), vc))
        return new_state, o_inter + o_intra

    state0 = jnp.zeros((B, C_HEADS, C_DK, C_DV), jnp.float32)
    _, o = lax.scan(step, state0, (qh, kh, vh, ah))
    o = o.transpose(1, 0, 3, 2, 4).reshape(B, S, C_HEADS, C_DV)
    o = rms_norm(o, norm_g) * jax.nn.silu(g.reshape(B, S, C_HEADS, C_DV).astype(jnp.float32))
    return o.reshape(B, S, C_VAL_DIM).astype(x.dtype) @ w_out


def moe_ffn(x, w_router, b_router, w_gu, b_gu, w_down, b_down):
    B, S, D = x.shape
    T = B * S
    xt = x.reshape(T, D)
    logits = (xt @ w_router + b_router).astype(jnp.float32)
    top_logit, top_e = lax.top_k(logits, TOP_K)
    gate = jax.nn.softmax(top_logit, axis=-1)
    n_assign = T * TOP_K
    flat_e = top_e.reshape(-1)
    flat_tok = jnp.arange(n_assign) // TOP_K
    order = jnp.argsort(flat_e)
    e_sorted = flat_e[order]
    tok_sorted = flat_tok[order]
    gate_sorted = gate.reshape(-1)[order]
    counts = jnp.zeros((N_EXPERTS,), jnp.int32).at[flat_e].add(1)
    padded = (counts + MOE_ROW_BLOCK - 1) // MOE_ROW_BLOCK * MOE_ROW_BLOCK
    start = jnp.cumsum(counts) - counts
    pend = jnp.cumsum(padded)
    pstart = pend - padded
    dest = pstart[e_sorted] + (jnp.arange(n_assign) - start[e_sorted])
    n_blocks = -(-n_assign // MOE_ROW_BLOCK) + N_EXPERTS
    n_rows = n_blocks * MOE_ROW_BLOCK
    row_tok = jnp.full((n_rows,), T, jnp.int32).at[dest].set(tok_sorted)
    x_pad = jnp.concatenate([xt, jnp.zeros((1, D), xt.dtype)], axis=0)
    xs = x_pad[row_tok].reshape(n_blocks, MOE_ROW_BLOCK, D)
    block_e = jnp.minimum(
        jnp.searchsorted(pend, jnp.arange(n_blocks) * MOE_ROW_BLOCK, side='right'), N_EXPERTS - 1)

    def expert_block(args):
        xb, e = args
        h = xb @ w_gu[e] + b_gu[e]
        h_gate, h_up = jnp.split(h, 2, axis=-1)
        h_gate = jnp.minimum(h_gate, SWIGLU_LIMIT)
        h_up = jnp.clip(h_up, -SWIGLU_LIMIT, SWIGLU_LIMIT)
        act = h_gate * jax.nn.sigmoid(SWIGLU_ALPHA * h_gate) * (h_up + 1.0)
        return act @ w_down[e] + b_down[e]

    ys = lax.map(expert_block, (xs, block_e)).reshape(n_rows, D)
    y_assign = ys[dest] * gate_sorted[:, None].astype(ys.dtype)
    y = jax.ops.segment_sum(y_assign, tok_sorted, num_segments=T)
    return y.reshape(B, S, D)


def setup_inputs(seed: int = 0) -> dict:
    key = jax.random.key(seed)
    ks = jax.random.split(key, 23)
    f32 = jnp.float32
    D = D_MODEL

    def nrm(k, shape, scale):
        return jax.random.normal(k, shape, f32) * scale

    return {
        "x": nrm(ks[0], (BATCH, SEQ, D), 1.0),
        "rel_bias": nrm(ks[1], (REL_BUCKETS, A_HEADS), 0.5),
        "a_w_in": nrm(ks[2], (N_LAYERS_A, D, 3 * D), D ** -0.5),
        "a_w_out": nrm(ks[3], (N_LAYERS_A, D, D), D ** -0.5 * DEEPNORM_BETA),
        "b_w_in": nrm(ks[4], (N_LAYERS_B, D, 2 * B_WIDTH), D ** -0.5),
        "b_ln_g": 1.0 + nrm(ks[5], (N_LAYERS_B, B_WIDTH), 0.02),
        "b_ln_b": nrm(ks[6], (N_LAYERS_B, B_WIDTH), 0.02),
        "b_w_s": nrm(ks[7], (N_LAYERS_B, B_GROUPS, B_CHUNK, B_CHUNK), B_CHUNK ** -0.5),
        "b_b_s": 1.0 + nrm(ks[8], (N_LAYERS_B, B_GROUPS, B_CHUNK), 0.02),
        "b_w_out": nrm(ks[9], (N_LAYERS_B, B_WIDTH, D), B_WIDTH ** -0.5 * DEEPNORM_BETA),
        "c_w_in": nrm(ks[10], (N_LAYERS_C, D, C_IN_WIDTH), D ** -0.5),
        "c_w_gk_up": nrm(ks[11], (N_LAYERS_C, C_GATE_RANK, C_KEY_DIM), C_GATE_RANK ** -0.5),
        "c_b_gk": nrm(ks[12], (N_LAYERS_C, C_KEY_DIM), 0.1),
        "c_norm_g": 1.0 + nrm(ks[13], (N_LAYERS_C, C_DV), 0.02),
        "c_w_out": nrm(ks[14], (N_LAYERS_C, C_VAL_DIM, D), C_VAL_DIM ** -0.5 * DEEPNORM_BETA),
        "ln_g": 1.0 + nrm(ks[15], (DEPTH, 2, D), 0.02),
        "ln_b": nrm(ks[16], (DEPTH, 2, D), 0.02),
        "moe_w_router": nrm(ks[17], (DEPTH, D, N_EXPERTS), D ** -0.5),
        "moe_b_router": nrm(ks[18], (DEPTH, N_EXPERTS), 0.01),
        "moe_w_gate_up": nrm(ks[19], (DEPTH, N_EXPERTS, D, 2 * D_EXPERT), D ** -0.5),
        "moe_b_gate_up": nrm(ks[20], (DEPTH, N_EXPERTS, 2 * D_EXPERT), 0.01),
        "moe_w_down": nrm(ks[21], (DEPTH, N_EXPERTS, D_EXPERT, D), D_EXPERT ** -0.5 * DEEPNORM_BETA),
        "moe_b_down": nrm(ks[22], (DEPTH, N_EXPERTS, D), 0.01),
    }


def reference(x, rel_bias, a_w_in, a_w_out, b_w_in, b_ln_g, b_ln_b, b_w_s, b_b_s, b_w_out,
              c_w_in, c_w_gk_up, c_b_gk, c_norm_g, c_w_out, ln_g, ln_b,
              moe_w_router, moe_b_router, moe_w_gate_up, moe_b_gate_up, moe_w_down, moe_b_down):
    h = x
    for i in range(DEPTH):
        j = i // N_MIXERS
        mixer = i % N_MIXERS
        if mixer == 0:
            t = moba_attention(h, a_w_in[j], a_w_out[j], rel_bias)
        elif mixer == 1:
            t = gmlp_mixer(h, b_w_in[j], b_ln_g[j], b_ln_b[j], b_w_s[j], b_b_s[j], b_w_out[j])
        else:
            t = gla_mixer(h, c_w_in[j], c_w_gk_up[j], c_b_gk[j], c_norm_g[j], c_w_out[j])
        h = layer_norm(DEEPNORM_ALPHA * h + t, ln_g[i, 0], ln_b[i, 0])
        f = moe_ffn(h, moe_w_router[i], moe_b_router[i], moe_w_gate_up[i], moe_b_gate_up[i],
                    moe_w_down[i], moe_b_down[i])
        h = layer_norm(DEEPNORM_ALPHA * h + f, ln_g[i, 1], ln_b[i, 1])
    return h
```

```python
import functools
import math

import numpy as np
import jax
import jax.numpy as jnp
from jax import lax
from jax.experimental import pallas as pl
from jax.experimental.pallas import tpu as pltpu

F32 = jnp.float32
BF16 = jnp.bfloat16
I32 = jnp.int32
HI = lax.Precision.HIGHEST

D_MODEL = 1024
DEPTH = 4
N_MIXERS = 3
ALPHA = (2.0 * DEPTH) ** 0.25
LN_EPS = 1e-5

A_HEADS = 8
A_DH = D_MODEL // A_HEADS
MOBA_BLOCK = 256
MOBA_TOPK = 3
REL_BUCKETS = 32
REL_MAX_DIST = 128

B_GROUPS = 8
B_WIDTH = 2 * D_MODEL
B_CHUNK = 128
B_GW = B_WIDTH // B_GROUPS

C_HEADS = 4
C_KEY_DIM = D_MODEL // 2
C_VAL_DIM = D_MODEL
C_DK = C_KEY_DIM // C_HEADS
C_DV = C_VAL_DIM // C_HEADS
C_GATE_RANK = 16
C_GATE_NORMALIZER = 16.0
C_IN_WIDTH = 2 * C_KEY_DIM + 2 * C_VAL_DIM + C_GATE_RANK

N_EXPERTS = 32
TOP_K = 4
SWIGLU_LIMIT = 7.0
SWIGLU_ALPHA = 1.702

LANES = 128
SUBLANES = 8
ROW_TILES = D_MODEL // LANES
VMEM_LIMIT = 56 * 1024 * 1024

GLA_CHUNK = 128
GLA_SUB = 16
MOE_BM = 256
ROUTER_TM = 512
DISPATCH_TM = 256
COMBINE_TM = 128


def _cparams(*sem):
    return pltpu.CompilerParams(dimension_semantics=sem, vmem_limit_bytes=VMEM_LIMIT)


def _ln(x, g, b):
    mu = jnp.mean(x, axis=-1, keepdims=True)
    xc = x - mu
    var = jnp.mean(xc * xc, axis=-1, keepdims=True)
    return xc * lax.rsqrt(var + LN_EPS) * g + b


def _mm_body(x_ref, w_ref, o_ref, wb_ref, *, act):
    @pl.when(pl.program_id(1) == 0)
    def _():
        wb_ref[...] = w_ref[...].astype(BF16)

    acc = jnp.dot(x_ref[...].astype(BF16), wb_ref[...], preferred_element_type=F32)
    if act == "gelu":
        acc = 0.5 * acc * (1.0 + lax.erf(acc * (2.0 ** -0.5)))
    o_ref[...] = acc


def _matmul(x, w, *, act=None, tm=512, tn=1024):
    M, K = x.shape
    N = w.shape[1]
    assert M % tm == 0 and N % tn == 0
    return pl.pallas_call(
        functools.partial(_mm_body, act=act),
        grid=(N // tn, M // tm),
        in_specs=[pl.BlockSpec((tm, K), lambda j, i: (i, 0)),
                  pl.BlockSpec((K, tn), lambda j, i: (0, j))],
        out_specs=pl.BlockSpec((tm, tn), lambda j, i: (i, j)),
        out_shape=jax.ShapeDtypeStruct((M, N), F32),
        scratch_shapes=[pltpu.VMEM((K, tn), BF16)],
        compiler_params=_cparams("arbitrary", "arbitrary"),
        name="proj_matmul",
    )(x, w)


def _mm_res_ln_body(x_ref, w_ref, h_ref, g_ref, b_ref, o_ref, wb_ref):
    @pl.when(pl.program_id(0) == 0)
    def _():
        wb_ref[...] = w_ref[...].astype(BF16)

    t = jnp.dot(x_ref[...].astype(BF16), wb_ref[...], preferred_element_type=F32)
    o_ref[...] = _ln(ALPHA * h_ref[...] + t, g_ref[...], b_ref[...])


def _matmul_res_ln(x, w, h, g, b, *, tm=512):
    M, K = x.shape
    N = w.shape[1]
    return pl.pallas_call(
        _mm_res_ln_body,
        grid=(M // tm,),
        in_specs=[pl.BlockSpec((tm, K), lambda i: (i, 0)),
                  pl.BlockSpec((K, N), lambda i: (0, 0)),
                  pl.BlockSpec((tm, N), lambda i: (i, 0)),
                  pl.BlockSpec((1, N), lambda i: (0, 0)),
                  pl.BlockSpec((1, N), lambda i: (0, 0))],
        out_specs=pl.BlockSpec((tm, N), lambda i: (i, 0)),
        out_shape=jax.ShapeDtypeStruct((M, N), F32),
        scratch_shapes=[pltpu.VMEM((K, N), BF16)],
        compiler_params=_cparams("arbitrary"),
        name="outproj_res_ln",
    )(x, w, h, g.reshape(1, N), b.reshape(1, N))


def _t5_bucket_lower_bounds():
    n = np.arange(0, 4 * REL_MAX_DIST, dtype=np.int64)
    max_exact = REL_BUCKETS // 2
    nf = np.maximum(n, 1).astype(np.float32)
    large = max_exact + (np.log(nf / np.float32(max_exact)) / np.float32(math.log(REL_MAX_DIST / max_exact))
                         * np.float32(REL_BUCKETS - max_exact)).astype(np.int32)
    large = np.minimum(large, REL_BUCKETS - 1)
    bucket = np.where(n < max_exact, n, large)
    assert np.all(np.diff(bucket) >= 0) and bucket[-1] == REL_BUCKETS - 1
    return [int(np.argmax(bucket >= b)) for b in range(REL_BUCKETS)]


_BUCKET_LO = _t5_bucket_lower_bounds()


def _bias_table_body(rel_ref, o_ref):
    blk = MOBA_BLOCK
    row = lax.broadcasted_iota(I32, (blk, blk), 0)
    col = lax.broadcasted_iota(I32, (blk, blk), 1)
    for t in range(2):
        d = row - col + t * blk
        for h in range(A_HEADS):
            val = jnp.full((blk, blk), rel_ref[REL_BUCKETS - 1, h], F32)
            for b in range(REL_BUCKETS - 2, -1, -1):
                val = jnp.where(d < _BUCKET_LO[b + 1], rel_ref[b, h], val)
            if t == 0:
                val = jnp.where(d >= 0, val, -jnp.inf)
            o_ref[h, t] = val


def _bias_tables(rel_bias):
    return pl.pallas_call(
        _bias_table_body,
        in_specs=[pl.BlockSpec(memory_space=pltpu.SMEM)],
        out_shape=jax.ShapeDtypeStruct((A_HEADS, 2, MOBA_BLOCK, MOBA_BLOCK), F32),
        compiler_params=pltpu.CompilerParams(vmem_limit_bytes=VMEM_LIMIT),
        name="rel_bias_tables",
    )(rel_bias)


def _moba_body(rel_ref, q_ref, k_ref, v_ref, tb_ref, o_ref, kb_s, vb_s, *, seq):
    blk, dh = MOBA_BLOCK, A_DH
    nb = seq // blk
    h = pl.program_id(1)
    scale = dh ** -0.5
    k = k_ref[0]
    kb_s[...] = k.astype(BF16)
    vb_s[...] = v_ref[0].astype(BF16)
    kmean = jnp.mean(k.reshape(nb, blk, dh), axis=1)
    kmean_pad = jnp.concatenate([kmean, jnp.zeros((LANES - nb, dh), F32)], axis=0)
    b_far = rel_ref[REL_BUCKETS - 1, h]
    lane = lax.broadcasted_iota(I32, (blk, LANES), 1)
    for c in range(nb):
        q = q_ref[0, c * blk:(c + 1) * blk, :]
        n_keys = (c + 1) * blk
        s = lax.dot_general(q.astype(BF16), kb_s[0:n_keys, :], (((1,), (1,)), ((), ())),
                            preferred_element_type=F32) * scale
        if c > MOBA_TOPK:
            gate = lax.dot_general(q, kmean_pad, (((1,), (1,)), ((), ())),
                                   precision=HI, preferred_element_type=F32)
            gate = jnp.where(lane < c, gate, -jnp.inf)
        pieces = []
        for n in range(c):
            bias = tb_ref[0, 1] if n == c - 1 else b_far
            piece = s[:, n * blk:(n + 1) * blk] + bias
            if c > MOBA_TOPK:
                gn = gate[:, n:n + 1]
                beats = (gate > gn) | ((gate == gn) & (lane < n))
                rank = jnp.sum(beats.astype(F32), axis=1, keepdims=True)
                piece = jnp.where(rank < MOBA_TOPK, piece, -jnp.inf)
            pieces.append(piece)
        pieces.append(s[:, c * blk:(c + 1) * blk] + tb_ref[0, 0])
        logits = jnp.concatenate(pieces, axis=1) if c > 0 else pieces[0]
        m = jnp.max(logits, axis=1, keepdims=True)
        p = jnp.exp(logits - m)
        denom = jnp.sum(p, axis=1, keepdims=True)
        o = jnp.dot(p.astype(BF16), vb_s[0:n_keys, :], preferred_element_type=F32)
        o_ref[0, c * blk:(c + 1) * blk, :] = o / denom


def _moba_attention(qkv, rel_bias, tables):
    B, S, _ = qkv.shape
    H, dh = A_HEADS, A_DH
    assert S % MOBA_BLOCK == 0 and S // MOBA_BLOCK <= SUBLANES
    return pl.pallas_call(
        functools.partial(_moba_body, seq=S),
        grid=(B, H),
        in_specs=[pl.BlockSpec(memory_space=pltpu.SMEM),
                  pl.BlockSpec((1, S, dh), lambda b, h: (b, 0, h)),
                  pl.BlockSpec((1, S, dh), lambda b, h: (b, 0, H + h)),
                  pl.BlockSpec((1, S, dh), lambda b, h: (b, 0, 2 * H + h)),
                  pl.BlockSpec((1, 2, MOBA_BLOCK, MOBA_BLOCK), lambda b, h: (h, 0, 0, 0))],
        out_specs=pl.BlockSpec((1, S, dh), lambda b, h: (b, 0, h)),
        out_shape=jax.ShapeDtypeStruct((B, S, H * dh), F32),
        scratch_shapes=[pltpu.VMEM((S, dh), BF16), pltpu.VMEM((S, dh), BF16)],
        compiler_params=_cparams("arbitrary", "arbitrary"),
        name="moba_attention",
    )(rel_bias, qkv, qkv, qkv, tables)


def _gmlp_body(u_ref, v_ref, h_ref, vg_ref, vb_ref, ws_ref, bs_ref, wo_ref, g_ref, b_ref, o_ref,
               wst_s, wob_s, *, tm):
    @pl.when(pl.program_id(0) == 0)
    def _():
        row = lax.broadcasted_iota(I32, (B_CHUNK, B_CHUNK), 0)
        col = lax.broadcasted_iota(I32, (B_CHUNK, B_CHUNK), 1)
        for g in range(B_GROUPS):
            wst_s[g] = jnp.where(row >= col, ws_ref[g], 0.0).astype(BF16)
        wob_s[...] = wo_ref[...].astype(BF16)

    vn = _ln(v_ref[...], vg_ref[...], vb_ref[...]).astype(BF16)
    u = u_ref[...]
    rows = []
    for c in range(tm // B_CHUNK):
        r0 = c * B_CHUNK
        cols = []
        for g in range(B_GROUPS):
            c0 = g * B_GW
            mixed = jnp.dot(wst_s[g], vn[r0:r0 + B_CHUNK, c0:c0 + B_GW],
                            preferred_element_type=F32) + bs_ref[g]
            cols.append(u[r0:r0 + B_CHUNK, c0:c0 + B_GW] * mixed)
        rows.append(jnp.concatenate(cols, axis=1))
    y = jnp.concatenate(rows, axis=0).astype(BF16)
    t = jnp.dot(y, wob_s[...], preferred_element_type=F32)
    o_ref[...] = _ln(ALPHA * h_ref[...] + t, g_ref[...], b_ref[...])


def _gmlp_gate_out(z, h, v_g, v_b, w_s, b_s, w_out, g, b, *, tm=256):
    T = z.shape[0]
    D = D_MODEL
    return pl.pallas_call(
        functools.partial(_gmlp_body, tm=tm),
        grid=(T // tm,),
        in_specs=[pl.BlockSpec((tm, B_WIDTH), lambda i: (i, 0)),
                  pl.BlockSpec((tm, B_WIDTH), lambda i: (i, 1)),
                  pl.BlockSpec((tm, D), lambda i: (i, 0)),
                  pl.BlockSpec((1, B_WIDTH), lambda i: (0, 0)),
                  pl.BlockSpec((1, B_WIDTH), lambda i: (0, 0)),
                  pl.BlockSpec((B_GROUPS, B_CHUNK, B_CHUNK), lambda i: (0, 0, 0)),
                  pl.BlockSpec((B_GROUPS, B_CHUNK, 1), lambda i: (0, 0, 0)),
                  pl.BlockSpec((B_WIDTH, D), lambda i: (0, 0)),
                  pl.BlockSpec((1, D), lambda i: (0, 0)),
                  pl.BlockSpec((1, D), lambda i: (0, 0))],
        out_specs=pl.BlockSpec((tm, D), lambda i: (i, 0)),
        out_shape=jax.ShapeDtypeStruct((T, D), F32),
        scratch_shapes=[pltpu.VMEM((B_GROUPS, B_CHUNK, B_CHUNK), BF16), pltpu.VMEM((B_WIDTH, D), BF16)],
        compiler_params=_cparams("arbitrary"),
        name="gmlp_gate_out",
    )(z, z, h, v_g.reshape(1, B_WIDTH), v_b.reshape(1, B_WIDTH), w_s, b_s.reshape(B_GROUPS, B_CHUNK, 1),
      w_out, g.reshape(1, D), b.reshape(1, D))


def _gla_body(q_ref, k_ref, v_ref, g_ref, gl_ref, wup_ref, bgk_ref, ng_ref, o_ref, la_s, st_s, raw_s, *, seq):
    ch, sub, dk, dv = GLA_CHUNK, GLA_SUB, C_DK, C_DV
    x = jnp.dot(gl_ref[0], wup_ref[...], precision=HI, preferred_element_type=F32) + bgk_ref[...]
    la_s[...] = (jnp.minimum(x, 0.0) - jnp.log1p(jnp.exp(-jnp.abs(x)))) * (1.0 / C_GATE_NORMALIZER)
    st_s[...] = jnp.zeros((dk, dv), F32)

    row_c = lax.broadcasted_iota(I32, (ch, ch), 0)
    col_c = lax.broadcasted_iota(I32, (ch, ch), 1)
    tri = (row_c >= col_c).astype(F32)
    eye = (row_c == col_c).astype(F32)
    sub_i = lax.broadcasted_iota(I32, (sub, ch), 0)
    lane_j = lax.broadcasted_iota(I32, (sub, ch), 1)

    def chunk(c, carry):
        s0 = pl.multiple_of(c * ch, ch)
        a = la_s[pl.ds(s0, ch), :]
        b = jnp.dot(tri, a, precision=HI, preferred_element_type=F32)
        q = q_ref[0, pl.ds(s0, ch), :] * (dk ** -0.5)
        k = k_ref[0, pl.ds(s0, ch), :]
        v = v_ref[0, pl.ds(s0, ch), :].astype(BF16)
        st = st_s[...]
        o = jnp.dot((q * jnp.exp(b)).astype(BF16), st.astype(BF16), preferred_element_type=F32)

        blocks = []
        for sb in range(ch // sub):
            r0 = sb * sub
            q_i = q[r0:r0 + sub]
            b_i = b[r0:r0 + sub]
            k_i = k[r0:r0 + sub]
            att = jnp.zeros((sub, ch), F32)
            for j in range(sub):
                e = jnp.exp(jnp.minimum(b_i - b_i[j:j + 1, :], 0.0))
                colv = jnp.sum(q_i * k_i[j:j + 1, :] * e, axis=1, keepdims=True)
                att = jnp.where((lane_j == r0 + j) & (sub_i >= j), colv, att)
            if sb > 0:
                ref_b = b[r0 - 1:r0, :]
                q_t = q_i * jnp.exp(b_i - ref_b)
                k_t = k * jnp.exp(jnp.minimum(ref_b - b, 0.0))
                off = lax.dot_general(q_t, k_t, (((1,), (1,)), ((), ())),
                                      precision=HI, preferred_element_type=F32)
                att = jnp.where(lane_j < r0, off, att)
            blocks.append(att)
        att_full = jnp.concatenate(blocks, axis=0)
        o = o + jnp.dot(att_full.astype(BF16), v, preferred_element_type=F32)
        raw_s[pl.ds(s0, ch), :] = o

        b_last = b[ch - 1:ch, :]
        k_d = k * jnp.exp(b_last - b)
        upd = jnp.dot(k_d.T.astype(BF16), v, preferred_element_type=F32)
        decay_col = lax.dot_general(eye, jnp.broadcast_to(jnp.exp(b_last), (SUBLANES, dk)),
                                    (((1,), (1,)), ((), ())), precision=HI,
                                    preferred_element_type=F32)[:, 0:1]
        st_s[...] = decay_col * st + upd
        return carry

    lax.fori_loop(0, seq // ch, chunk, 0)

    o = raw_s[...]
    rms = o * lax.rsqrt(jnp.mean(o * o, axis=-1, keepdims=True) + LN_EPS) * ng_ref[...]
    gg = g_ref[0]
    o_ref[0] = rms * (gg * jax.nn.sigmoid(gg))


def _gla_core(proj, w_up_pad, b_gk, norm_g):
    B, S, _ = proj.shape
    H, dk, dv = C_HEADS, C_DK, C_DV
    assert dk == LANES and S % GLA_CHUNK == 0
    k_off = C_KEY_DIM // dk
    v_off = 2 * C_KEY_DIM // dv
    g_off = (2 * C_KEY_DIM + C_VAL_DIM) // dv
    gl_off = (2 * C_KEY_DIM + 2 * C_VAL_DIM) // LANES
    return pl.pallas_call(
        functools.partial(_gla_body, seq=S),
        grid=(B, H),
        in_specs=[pl.BlockSpec((1, S, dk), lambda b, h: (b, 0, h)),
                  pl.BlockSpec((1, S, dk), lambda b, h: (b, 0, k_off + h)),
                  pl.BlockSpec((1, S, dv), lambda b, h: (b, 0, v_off + h)),
                  pl.BlockSpec((1, S, dv), lambda b, h: (b, 0, g_off + h)),
                  pl.BlockSpec((1, S, LANES), lambda b, h: (b, 0, gl_off)),
                  pl.BlockSpec((LANES, dk), lambda b, h: (0, h)),
                  pl.BlockSpec((1, dk), lambda b, h: (0, h)),
                  pl.BlockSpec((1, dv), lambda b, h: (0, 0))],
        out_specs=pl.BlockSpec((1, S, dv), lambda b, h: (b, 0, h)),
        out_shape=jax.ShapeDtypeStruct((B, S, H * dv), F32),
        scratch_shapes=[pltpu.VMEM((S, dk), F32), pltpu.VMEM((dk, dv), F32), pltpu.VMEM((S, dv), F32)],
        compiler_params=_cparams("arbitrary", "arbitrary"),
        name="gla_core",
    )(proj, proj, proj, proj, proj, w_up_pad, b_gk.reshape(1, C_KEY_DIM), norm_g.reshape(1, dv))


def _router_body(h_ref, w_ref, b_ref, e_ref, gate_ref, rank_ref, cnt_ref, cnt_s, *, tm):
    @pl.when(pl.program_id(0) == 0)
    def _():
        cnt_s[...] = jnp.zeros((1, N_EXPERTS), F32)

    logits = jnp.dot(h_ref[...], w_ref[...], precision=HI, preferred_element_type=F32) + b_ref[...]
    lane = lax.broadcasted_iota(I32, (tm, N_EXPERTS), 1)
    rest = logits
    top_val, top_idx, onehot = [], [], []
    for _ in range(TOP_K):
        m = jnp.max(rest, axis=1, keepdims=True)
        idx = jnp.min(jnp.where(rest == m, lane, N_EXPERTS), axis=1, keepdims=True)
        oh = lane == idx
        rest = jnp.where(oh, -jnp.inf, rest)
        top_val.append(m)
        top_idx.append(idx)
        onehot.append(oh)
    ex = [jnp.exp(v - top_val[0]) for v in top_val]
    den = ex[0] + ex[1] + ex[2] + ex[3]
    chosen = sum(oh.astype(F32) for oh in onehot)
    row = lax.broadcasted_iota(I32, (tm, tm), 0)
    col = lax.broadcasted_iota(I32, (tm, tm), 1)
    before = (row > col).astype(BF16)
    seen = jnp.dot(before, chosen.astype(BF16), preferred_element_type=F32) + cnt_s[...]
    lane_k = lax.broadcasted_iota(I32, (tm, TOP_K), 1)
    e_out = jnp.zeros((tm, TOP_K), I32)
    g_out = jnp.zeros((tm, TOP_K), F32)
    r_out = jnp.zeros((tm, TOP_K), I32)
    for kk in range(TOP_K):
        rk = jnp.sum(jnp.where(onehot[kk], seen, 0.0), axis=1, keepdims=True)
        e_out = jnp.where(lane_k == kk, top_idx[kk], e_out)
        g_out = jnp.where(lane_k == kk, ex[kk] / den, g_out)
        r_out = jnp.where(lane_k == kk, rk.astype(I32), r_out)
    e_ref[...] = e_out
    gate_ref[...] = g_out
    rank_ref[...] = r_out
    cnt_s[...] = cnt_s[...] + jnp.sum(chosen, axis=0, keepdims=True)
    cnt_ref[...] = cnt_s[...]


def _router(h, w, b, *, tm=ROUTER_TM):
    T, D = h.shape
    E = N_EXPERTS
    return pl.pallas_call(
        functools.partial(_router_body, tm=tm),
        grid=(T // tm,),
        in_specs=[pl.BlockSpec((tm, D), lambda i: (i, 0)),
                  pl.BlockSpec((D, E), lambda i: (0, 0)),
                  pl.BlockSpec((1, E), lambda i: (0, 0))],
        out_specs=[pl.BlockSpec((tm, TOP_K), lambda i: (i, 0)),
                   pl.BlockSpec((tm, TOP_K), lambda i: (i, 0)),
                   pl.BlockSpec((tm, TOP_K), lambda i: (i, 0)),
                   pl.BlockSpec((1, E), lambda i: (0, 0))],
        out_shape=[jax.ShapeDtypeStruct((T, TOP_K), I32),
                   jax.ShapeDtypeStruct((T, TOP_K), F32),
                   jax.ShapeDtypeStruct((T, TOP_K), I32),
                   jax.ShapeDtypeStruct((1, E), F32)],
        scratch_shapes=[pltpu.VMEM((1, E), F32)],
        compiler_params=_cparams("arbitrary"),
        name="moe_router",
    )(h, w, b.reshape(1, E))


def _dispatch_body(dest_ref, h_ref, xs_zero_ref, xs_ref, buf, sem, *, tm):
    del xs_zero_ref
    i = pl.program_id(0)
    for s in range(ROW_TILES):
        buf[pl.ds(s, tm, stride=ROW_TILES), :] = h_ref[:, s * LANES:(s + 1) * LANES]

    def issue(t, carry):
        src = buf.at[pl.ds(pl.multiple_of(t * ROW_TILES, ROW_TILES), ROW_TILES), :]
        for kk in range(TOP_K):
            d = dest_ref[(i * tm + t) * TOP_K + kk]
            pltpu.make_async_copy(src, xs_ref.at[pl.ds(pl.multiple_of(d, ROW_TILES), ROW_TILES), :], sem).start()
        return carry

    lax.fori_loop(0, tm, issue, 0)
    for _ in range(TOP_K):
        pltpu.make_async_copy(buf, xs_ref.at[pl.ds(0, tm * ROW_TILES), :], sem).wait()


def _dispatch(dest_rows, h, n_rows, *, tm=DISPATCH_TM):
    T, D = h.shape
    xs0 = jnp.zeros((n_rows * ROW_TILES, LANES), F32)
    grid_spec = pltpu.PrefetchScalarGridSpec(
        num_scalar_prefetch=1,
        grid=(T // tm,),
        in_specs=[pl.BlockSpec((tm, D), lambda i, dest: (i, 0)),
                  pl.BlockSpec(memory_space=pl.ANY)],
        out_specs=pl.BlockSpec(memory_space=pl.ANY),
        scratch_shapes=[pltpu.VMEM((tm * ROW_TILES, LANES), F32), pltpu.SemaphoreType.DMA],
    )
    return pl.pallas_call(
        functools.partial(_dispatch_body, tm=tm),
        grid_spec=grid_spec,
        out_shape=jax.ShapeDtypeStruct((n_rows * ROW_TILES, LANES), F32),
        input_output_aliases={2: 0},
        compiler_params=pltpu.CompilerParams(dimension_semantics=("arbitrary",), vmem_limit_bytes=VMEM_LIMIT,
                                             has_side_effects=True),
        name="moe_dispatch",
    )(dest_rows, h, xs0)


def _expert_body(be_ref, nb_ref, xs_ref, wgu_ref, bgu_ref, wd_ref, bd_ref, ys_ref, wgu_s, wd_s, *, bm):
    i = pl.program_id(0)
    prev = be_ref[jnp.maximum(i - 1, 0)]
    live = i < nb_ref[0]

    @pl.when(live & ((i == 0) | (be_ref[i] != prev)))
    def _():
        wgu_s[...] = wgu_ref[0].astype(BF16)
        wd_s[...] = wd_ref[0].astype(BF16)

    @pl.when(live)
    def _():
        x = jnp.concatenate([xs_ref[pl.ds(s, bm, stride=ROW_TILES), :] for s in range(ROW_TILES)], axis=1)
        hh = jnp.dot(x.astype(BF16), wgu_s[...], preferred_element_type=F32) + bgu_ref[0]
        d_e = hh.shape[1] // 2
        h_gate = jnp.minimum(hh[:, :d_e], SWIGLU_LIMIT)
        h_up = jnp.clip(hh[:, d_e:], -SWIGLU_LIMIT, SWIGLU_LIMIT)
        act = h_gate * jax.nn.sigmoid(SWIGLU_ALPHA * h_gate) * (h_up + 1.0)
        y = jnp.dot(act.astype(BF16), wd_s[...], preferred_element_type=F32) + bd_ref[0]
        for s in range(ROW_TILES):
            ys_ref[pl.ds(s, bm, stride=ROW_TILES), :] = y[:, s * LANES:(s + 1) * LANES]

    @pl.when(jnp.logical_not(live))
    def _():
        ys_ref[...] = jnp.zeros(ys_ref.shape, F32)


def _experts(block_e, n_live, xs, w_gu, b_gu, w_down, b_down, *, bm=MOE_BM):
    E, D, D2 = w_gu.shape
    n_blocks = xs.shape[0] // (bm * ROW_TILES)

    def blk(i, be, nb):
        return jnp.minimum(i, nb[0] - 1)

    grid_spec = pltpu.PrefetchScalarGridSpec(
        num_scalar_prefetch=2,
        grid=(n_blocks,),
        in_specs=[pl.BlockSpec((bm * ROW_TILES, LANES), lambda i, be, nb: (blk(i, be, nb), 0)),
                  pl.BlockSpec((1, D, D2), lambda i, be, nb: (be[blk(i, be, nb)], 0, 0)),
                  pl.BlockSpec((1, 1, D2), lambda i, be, nb: (be[blk(i, be, nb)], 0, 0)),
                  pl.BlockSpec((1, D2 // 2, D), lambda i, be, nb: (be[blk(i, be, nb)], 0, 0)),
                  pl.BlockSpec((1, 1, D), lambda i, be, nb: (be[blk(i, be, nb)], 0, 0))],
        out_specs=pl.BlockSpec((bm * ROW_TILES, LANES), lambda i, be, nb: (i, 0)),
        scratch_shapes=[pltpu.VMEM((D, D2), BF16), pltpu.VMEM((D2 // 2, D), BF16)],
    )
    return pl.pallas_call(
        functools.partial(_expert_body, bm=bm),
        grid_spec=grid_spec,
        out_shape=jax.ShapeDtypeStruct(xs.shape, F32),
        compiler_params=_cparams("arbitrary"),
        name="moe_experts",
    )(block_e, n_live, xs, w_gu, b_gu.reshape(E, 1, D2), w_down, b_down.reshape(E, 1, D))


def _combine_body(dest_ref, ys_ref, gate_ref, h_ref, g_ref, b_ref, o_ref, buf, sem, *, tm):
    i = pl.program_id(0)

    def issue(t, carry):
        for kk in range(TOP_K):
            d = dest_ref[(i * tm + t) * TOP_K + kk]
            slot = pl.multiple_of((t * TOP_K + kk) * ROW_TILES, ROW_TILES)
            pltpu.make_async_copy(ys_ref.at[pl.ds(pl.multiple_of(d, ROW_TILES), ROW_TILES), :],
                                  buf.at[pl.ds(slot, ROW_TILES), :], sem).start()
        return carry

    lax.fori_loop(0, tm, issue, 0)
    pltpu.make_async_copy(ys_ref.at[pl.ds(0, tm * TOP_K * ROW_TILES), :], buf, sem).wait()

    gate = gate_ref[...]
    acc = jnp.zeros((tm, D_MODEL), F32)
    stride = TOP_K * ROW_TILES
    for kk in range(TOP_K):
        yk = jnp.concatenate([buf[pl.ds(kk * ROW_TILES + s, tm, stride=stride), :] for s in range(ROW_TILES)],
                             axis=1)
        acc = acc + gate[:, kk:kk + 1] * yk
    o_ref[...] = _ln(ALPHA * h_ref[...] + acc, g_ref[...], b_ref[...])


def _combine(dest_rows, ys, gate, h, g, b, *, tm=COMBINE_TM):
    T, D = h.shape
    grid_spec = pltpu.PrefetchScalarGridSpec(
        num_scalar_prefetch=1,
        grid=(T // tm,),
        in_specs=[pl.BlockSpec(memory_space=pl.ANY),
                  pl.BlockSpec((tm, TOP_K), lambda i, dest: (i, 0)),
                  pl.BlockSpec((tm, D), lambda i, dest: (i, 0)),
                  pl.BlockSpec((1, D), lambda i, dest: (0, 0)),
                  pl.BlockSpec((1, D), lambda i, dest: (0, 0))],
        out_specs=pl.BlockSpec((tm, D), lambda i, dest: (i, 0)),
        scratch_shapes=[pltpu.VMEM((tm * TOP_K * ROW_TILES, LANES), F32), pltpu.SemaphoreType.DMA],
    )
    return pl.pallas_call(
        functools.partial(_combine_body, tm=tm),
        grid_spec=grid_spec,
        out_shape=jax.ShapeDtypeStruct((T, D), F32),
        compiler_params=_cparams("arbitrary"),
        name="moe_combine",
    )(dest_rows, ys, gate, h, g.reshape(1, D), b.reshape(1, D))


def _moe_block(h, w_router, b_router, w_gu, b_gu, w_down, b_down, g, b):
    T = h.shape[0]
    bm = MOE_BM
    top_e, gate, rank, counts = _router(h, w_router, b_router)
    counts = counts[0].astype(I32)
    padded = (counts + bm - 1) // bm * bm
    pend = jnp.cumsum(padded)
    pstart = pend - padded
    n_blocks = T * TOP_K // bm + N_EXPERTS
    dest_rows = ((pstart[top_e] + rank) * ROW_TILES).reshape(-1)
    block_e = jnp.minimum(jnp.searchsorted(pend, jnp.arange(n_blocks, dtype=I32) * bm, side="right"),
                          N_EXPERTS - 1).astype(I32)
    n_live = (pend[-1:] // bm).astype(I32)
    xs = _dispatch(dest_rows, h, n_blocks * bm)
    ys = _experts(block_e, n_live, xs, w_gu, b_gu, w_down, b_down)
    return _combine(dest_rows, ys, gate, h, g, b)


def kernel(x, rel_bias, a_w_in, a_w_out, b_w_in, b_ln_g, b_ln_b, b_w_s, b_b_s, b_w_out,
           c_w_in, c_w_gk_up, c_b_gk, c_norm_g, c_w_out, ln_g, ln_b,
           moe_w_router, moe_b_router, moe_w_gate_up, moe_b_gate_up, moe_w_down, moe_b_down):
    B, S, D = x.shape
    T = B * S
    h = x.reshape(T, D)
    tables = _bias_tables(rel_bias)
    for i in range(DEPTH):
        j = i // N_MIXERS
        mixer = i % N_MIXERS
        if mixer == 0:
            qkv = _matmul(h, a_w_in[j], tn=1024)
            o = _moba_attention(qkv.reshape(B, S, 3 * D), rel_bias, tables)
            h = _matmul_res_ln(o.reshape(T, D), a_w_out[j], h, ln_g[i, 0], ln_b[i, 0])
        elif mixer == 1:
            z = _matmul(h, b_w_in[j], act="gelu", tn=1024)
            h = _gmlp_gate_out(z, h, b_ln_g[j], b_ln_b[j], b_w_s[j], b_b_s[j], b_w_out[j],
                               ln_g[i, 0], ln_b[i, 0])
        else:
            pad = LANES - C_GATE_RANK
            w_in = jnp.pad(c_w_in[j], ((0, 0), (0, pad)))
            w_up = jnp.pad(c_w_gk_up[j], ((0, pad), (0, 0)))
            proj = _matmul(h, w_in, tn=640)
            o = _gla_core(proj.reshape(B, S, C_IN_WIDTH + pad), w_up, c_b_gk[j], c_norm_g[j])
            h = _matmul_res_ln(o.reshape(T, D), c_w_out[j], h, ln_g[i, 0], ln_b[i, 0])
        h = _moe_block(h, moe_w_router[i], moe_b_router[i], moe_w_gate_up[i], moe_b_gate_up[i],
                       moe_w_down[i], moe_b_down[i], ln_g[i, 1], ln_b[i, 1])
    return h.reshape(B, S, D)
```

```python
import functools
import math

import numpy as np
import jax
import jax.numpy as jnp
from jax import lax
from jax.experimental import pallas as pl
from jax.experimental.pallas import tpu as pltpu

F32 = jnp.float32
BF16 = jnp.bfloat16
I32 = jnp.int32
HI = lax.Precision.HIGHEST

D_MODEL = 1024
DEPTH = 4
N_MIXERS = 3
ALPHA = (2.0 * DEPTH) ** 0.25
LN_EPS = 1e-5

A_HEADS = 8
A_DH = D_MODEL // A_HEADS
MOBA_BLOCK = 256
MOBA_TOPK = 3
REL_BUCKETS = 32
REL_MAX_DIST = 128

B_GROUPS = 8
B_WIDTH = 2 * D_MODEL
B_CHUNK = 128
B_GW = B_WIDTH // B_GROUPS

C_HEADS = 4
C_KEY_DIM = D_MODEL // 2
C_VAL_DIM = D_MODEL
C_DK = C_KEY_DIM // C_HEADS
C_DV = C_VAL_DIM // C_HEADS
C_GATE_RANK = 16
C_GATE_NORMALIZER = 16.0
C_IN_WIDTH = 2 * C_KEY_DIM + 2 * C_VAL_DIM + C_GATE_RANK

N_EXPERTS = 32
TOP_K = 4
SWIGLU_LIMIT = 7.0
SWIGLU_ALPHA = 1.702

LANES = 128
SUBLANES = 8
ROW_TILES = D_MODEL // LANES
VMEM_LIMIT = 56 * 1024 * 1024

GLA_CHUNK = 128
GLA_SUB = 16
MOE_BM = 512
ROUTER_TM = 512
DISPATCH_TM = 256
COMBINE_TM = 128


def _cparams(*sem):
    return pltpu.CompilerParams(dimension_semantics=sem, vmem_limit_bytes=VMEM_LIMIT)


def _ln(x, g, b):
    mu = jnp.mean(x, axis=-1, keepdims=True)
    xc = x - mu
    var = jnp.mean(xc * xc, axis=-1, keepdims=True)
    return xc * lax.rsqrt(var + LN_EPS) * g + b


def _mm_body(x_ref, w_ref, o_ref, wb_ref, *, act):
    @pl.when(pl.program_id(1) == 0)
    def _():
        wb_ref[...] = w_ref[...].astype(BF16)

    acc = jnp.dot(x_ref[...].astype(BF16), wb_ref[...], preferred_element_type=F32)
    if act == "gelu":
        acc = 0.5 * acc * (1.0 + lax.erf(acc * (2.0 ** -0.5)))
    o_ref[...] = acc


def _matmul(x, w, *, act=None, tm=512, tn=1024):
    M, K = x.shape
    N = w.shape[1]
    assert M % tm == 0 and N % tn == 0
    return pl.pallas_call(
        functools.partial(_mm_body, act=act),
        grid=(N // tn, M // tm),
        in_specs=[pl.BlockSpec((tm, K), lambda j, i: (i, 0)),
                  pl.BlockSpec((K, tn), lambda j, i: (0, j))],
        out_specs=pl.BlockSpec((tm, tn), lambda j, i: (i, j)),
        out_shape=jax.ShapeDtypeStruct((M, N), F32),
        scratch_shapes=[pltpu.VMEM((K, tn), BF16)],
        compiler_params=_cparams("arbitrary", "arbitrary"),
        name="proj_matmul",
    )(x, w)


def _mm_res_ln_body(x_ref, w_ref, h_ref, g_ref, b_ref, o_ref, wb_ref):
    @pl.when(pl.program_id(0) == 0)
    def _():
        wb_ref[...] = w_ref[...].astype(BF16)

    t = jnp.dot(x_ref[...].astype(BF16), wb_ref[...], preferred_element_type=F32)
    o_ref[...] = _ln(ALPHA * h_ref[...] + t, g_ref[...], b_ref[...])


def _matmul_res_ln(x, w, h, g, b, *, tm=512):
    M, K = x.shape
    N = w.shape[1]
    return pl.pallas_call(
        _mm_res_ln_body,
        grid=(M // tm,),
        in_specs=[pl.BlockSpec((tm, K), lambda i: (i, 0)),
                  pl.BlockSpec((K, N), lambda i: (0, 0)),
                  pl.BlockSpec((tm, N), lambda i: (i, 0)),
                  pl.BlockSpec((1, N), lambda i: (0, 0)),
                  pl.BlockSpec((1, N), lambda i: (0, 0))],
        out_specs=pl.BlockSpec((tm, N), lambda i: (i, 0)),
        out_shape=jax.ShapeDtypeStruct((M, N), F32),
        scratch_shapes=[pltpu.VMEM((K, N), BF16)],
        compiler_params=_cparams("arbitrary"),
        name="outproj_res_ln",
    )(x, w, h, g.reshape(1, N), b.reshape(1, N))


def _t5_bucket_lower_bounds():
    n = np.arange(0, 4 * REL_MAX_DIST, dtype=np.int64)
    max_exact = REL_BUCKETS // 2
    nf = np.maximum(n, 1).astype(np.float32)
    large = max_exact + (np.log(nf / np.float32(max_exact)) / np.float32(math.log(REL_MAX_DIST / max_exact))
                         * np.float32(REL_BUCKETS - max_exact)).astype(np.int32)
    large = np.minimum(large, REL_BUCKETS - 1)
    bucket = np.where(n < max_exact, n, large)
    assert np.all(np.diff(bucket) >= 0) and bucket[-1] == REL_BUCKETS - 1
    return [int(np.argmax(bucket >= b)) for b in range(REL_BUCKETS)]


_BUCKET_LO = _t5_bucket_lower_bounds()


def _bias_table_body(rel_ref, o_ref):
    blk = MOBA_BLOCK
    row = lax.broadcasted_iota(I32, (blk, blk), 0)
    col = lax.broadcasted_iota(I32, (blk, blk), 1)
    for t in range(2):
        d = row - col + t * blk
        for h in range(A_HEADS):
            val = jnp.full((blk, blk), rel_ref[REL_BUCKETS - 1, h], F32)
            for b in range(REL_BUCKETS - 2, -1, -1):
                val = jnp.where(d < _BUCKET_LO[b + 1], rel_ref[b, h], val)
            if t == 0:
                val = jnp.where(d >= 0, val, -jnp.inf)
            o_ref[h, t] = val


def _bias_tables(rel_bias):
    return pl.pallas_call(
        _bias_table_body,
        in_specs=[pl.BlockSpec(memory_space=pltpu.SMEM)],
        out_shape=jax.ShapeDtypeStruct((A_HEADS, 2, MOBA_BLOCK, MOBA_BLOCK), F32),
        compiler_params=pltpu.CompilerParams(vmem_limit_bytes=VMEM_LIMIT),
        name="rel_bias_tables",
    )(rel_bias)


def _moba_body(rel_ref, q_ref, k_ref, v_ref, tb_ref, o_ref, kb_s, vb_s, *, seq):
    blk, dh = MOBA_BLOCK, A_DH
    nb = seq // blk
    h = pl.program_id(1)
    scale = dh ** -0.5
    k = k_ref[0]
    kb_s[...] = k.astype(BF16)
    vb_s[...] = v_ref[0].astype(BF16)
    kmean = jnp.mean(k.reshape(nb, blk, dh), axis=1)
    kmean_pad = jnp.concatenate([kmean, jnp.zeros((LANES - nb, dh), F32)], axis=0)
    b_far = rel_ref[REL_BUCKETS - 1, h]
    lane = lax.broadcasted_iota(I32, (blk, LANES), 1)
    for c in range(nb):
        q = q_ref[0, c * blk:(c + 1) * blk, :]
        n_keys = (c + 1) * blk
        s = lax.dot_general(q.astype(BF16), kb_s[0:n_keys, :], (((1,), (1,)), ((), ())),
                            preferred_element_type=F32) * scale
        if c > MOBA_TOPK:
            gate = lax.dot_general(q, kmean_pad, (((1,), (1,)), ((), ())),
                                   precision=HI, preferred_element_type=F32)
            gate = jnp.where(lane < c, gate, -jnp.inf)
        pieces = []
        for n in range(c):
            bias = tb_ref[0, 1] if n == c - 1 else b_far
            piece = s[:, n * blk:(n + 1) * blk] + bias
            if c > MOBA_TOPK:
                gn = gate[:, n:n + 1]
                beats = (gate > gn) | ((gate == gn) & (lane < n))
                rank = jnp.sum(beats.astype(F32), axis=1, keepdims=True)
                piece = jnp.where(rank < MOBA_TOPK, piece, -jnp.inf)
            pieces.append(piece)
        pieces.append(s[:, c * blk:(c + 1) * blk] + tb_ref[0, 0])
        logits = jnp.concatenate(pieces, axis=1) if c > 0 else pieces[0]
        m = jnp.max(logits, axis=1, keepdims=True)
        p = jnp.exp(logits - m)
        denom = jnp.sum(p, axis=1, keepdims=True)
        o = jnp.dot(p.astype(BF16), vb_s[0:n_keys, :], preferred_element_type=F32)
        o_ref[0, c * blk:(c + 1) * blk, :] = o / denom


def _moba_attention(qkv, rel_bias, tables):
    B, S, _ = qkv.shape
    H, dh = A_HEADS, A_DH
    assert S % MOBA_BLOCK == 0 and S // MOBA_BLOCK <= SUBLANES
    return pl.pallas_call(
        functools.partial(_moba_body, seq=S),
        grid=(B, H),
        in_specs=[pl.BlockSpec(memory_space=pltpu.SMEM),
                  pl.BlockSpec((1, S, dh), lambda b, h: (b, 0, h)),
                  pl.BlockSpec((1, S, dh), lambda b, h: (b, 0, H + h)),
                  pl.BlockSpec((1, S, dh), lambda b, h: (b, 0, 2 * H + h)),
                  pl.BlockSpec((1, 2, MOBA_BLOCK, MOBA_BLOCK), lambda b, h: (h, 0, 0, 0))],
        out_specs=pl.BlockSpec((1, S, dh), lambda b, h: (b, 0, h)),
        out_shape=jax.ShapeDtypeStruct((B, S, H * dh), F32),
        scratch_shapes=[pltpu.VMEM((S, dh), BF16), pltpu.VMEM((S, dh), BF16)],
        compiler_params=_cparams("arbitrary", "arbitrary"),
        name="moba_attention",
    )(rel_bias, qkv, qkv, qkv, tables)


def _gmlp_body(u_ref, v_ref, h_ref, vg_ref, vb_ref, ws_ref, bs_ref, wo_ref, g_ref, b_ref, o_ref,
               wst_s, wob_s, *, tm):
    @pl.when(pl.program_id(0) == 0)
    def _():
        row = lax.broadcasted_iota(I32, (B_CHUNK, B_CHUNK), 0)
        col = lax.broadcasted_iota(I32, (B_CHUNK, B_CHUNK), 1)
        for g in range(B_GROUPS):
            wst_s[g] = jnp.where(row >= col, ws_ref[g], 0.0).astype(BF16)
        wob_s[...] = wo_ref[...].astype(BF16)

    vn = _ln(v_ref[...], vg_ref[...], vb_ref[...]).astype(BF16)
    u = u_ref[...]
    rows = []
    for c in range(tm // B_CHUNK):
        r0 = c * B_CHUNK
        cols = []
        for g in range(B_GROUPS):
            c0 = g * B_GW
            mixed = jnp.dot(wst_s[g], vn[r0:r0 + B_CHUNK, c0:c0 + B_GW],
                            preferred_element_type=F32) + bs_ref[g]
            cols.append(u[r0:r0 + B_CHUNK, c0:c0 + B_GW] * mixed)
        rows.append(jnp.concatenate(cols, axis=1))
    y = jnp.concatenate(rows, axis=0).astype(BF16)
    t = jnp.dot(y, wob_s[...], preferred_element_type=F32)
    o_ref[...] = _ln(ALPHA * h_ref[...] + t, g_ref[...], b_ref[...])


def _gmlp_gate_out(z, h, v_g, v_b, w_s, b_s, w_out, g, b, *, tm=256):
    T = z.shape[0]
    D = D_MODEL
    return pl.pallas_call(
        functools.partial(_gmlp_body, tm=tm),
        grid=(T // tm,),
        in_specs=[pl.BlockSpec((tm, B_WIDTH), lambda i: (i, 0)),
                  pl.BlockSpec((tm, B_WIDTH), lambda i: (i, 1)),
                  pl.BlockSpec((tm, D), lambda i: (i, 0)),
                  pl.BlockSpec((1, B_WIDTH), lambda i: (0, 0)),
                  pl.BlockSpec((1, B_WIDTH), lambda i: (0, 0)),
                  pl.BlockSpec((B_GROUPS, B_CHUNK, B_CHUNK), lambda i: (0, 0, 0)),
                  pl.BlockSpec((B_GROUPS, B_CHUNK, 1), lambda i: (0, 0, 0)),
                  pl.BlockSpec((B_WIDTH, D), lambda i: (0, 0)),
                  pl.BlockSpec((1, D), lambda i: (0, 0)),
                  pl.BlockSpec((1, D), lambda i: (0, 0))],
        out_specs=pl.BlockSpec((tm, D), lambda i: (i, 0)),
        out_shape=jax.ShapeDtypeStruct((T, D), F32),
        scratch_shapes=[pltpu.VMEM((B_GROUPS, B_CHUNK, B_CHUNK), BF16), pltpu.VMEM((B_WIDTH, D), BF16)],
        compiler_params=_cparams("arbitrary"),
        name="gmlp_gate_out",
    )(z, z, h, v_g.reshape(1, B_WIDTH), v_b.reshape(1, B_WIDTH), w_s, b_s.reshape(B_GROUPS, B_CHUNK, 1),
      w_out, g.reshape(1, D), b.reshape(1, D))


def _gla_body(q_ref, k_ref, v_ref, g_ref, gl_ref, wup_ref, bgk_ref, ng_ref, o_ref, la_s, st_s, raw_s, *, seq):
    ch, sub, dk, dv = GLA_CHUNK, GLA_SUB, C_DK, C_DV
    x = jnp.dot(gl_ref[0], wup_ref[...], precision=HI, preferred_element_type=F32) + bgk_ref[...]
    la_s[...] = (jnp.minimum(x, 0.0) - jnp.log1p(jnp.exp(-jnp.abs(x)))) * (1.0 / C_GATE_NORMALIZER)
    st_s[...] = jnp.zeros((dk, dv), F32)

    row_c = lax.broadcasted_iota(I32, (ch, ch), 0)
    col_c = lax.broadcasted_iota(I32, (ch, ch), 1)
    tri = (row_c >= col_c).astype(F32)
    eye = (row_c == col_c).astype(F32)
    sub_i = lax.broadcasted_iota(I32, (sub, ch), 0)
    lane_j = lax.broadcasted_iota(I32, (sub, ch), 1)

    def chunk(c, carry):
        s0 = pl.multiple_of(c * ch, ch)
        a = la_s[pl.ds(s0, ch), :]
        b = jnp.dot(tri, a, precision=HI, preferred_element_type=F32)
        q = q_ref[0, pl.ds(s0, ch), :] * (dk ** -0.5)
        k = k_ref[0, pl.ds(s0, ch), :]
        v = v_ref[0, pl.ds(s0, ch), :].astype(BF16)
        st = st_s[...]
        o = jnp.dot((q * jnp.exp(b)).astype(BF16), st.astype(BF16), preferred_element_type=F32)

        blocks = []
        for sb in range(ch // sub):
            r0 = sb * sub
            q_i = q[r0:r0 + sub]
            b_i = b[r0:r0 + sub]
            k_i = k[r0:r0 + sub]
            att = jnp.zeros((sub, ch), F32)
            for j in range(sub):
                e = jnp.exp(jnp.minimum(b_i - b_i[j:j + 1, :], 0.0))
                colv = jnp.sum(q_i * k_i[j:j + 1, :] * e, axis=1, keepdims=True)
                att = jnp.where((lane_j == r0 + j) & (sub_i >= j), colv, att)
            if sb > 0:
                ref_b = b[r0 - 1:r0, :]
                q_t = q_i * jnp.exp(b_i - ref_b)
                k_t = k * jnp.exp(jnp.minimum(ref_b - b, 0.0))
                off = lax.dot_general(q_t, k_t, (((1,), (1,)), ((), ())),
                                      precision=HI, preferred_element_type=F32)
                att = jnp.where(lane_j < r0, off, att)
            blocks.append(att)
        att_full = jnp.concatenate(blocks, axis=0)
        o = o + jnp.dot(att_full.astype(BF16), v, preferred_element_type=F32)
        raw_s[pl.ds(s0, ch), :] = o

        b_last = b[ch - 1:ch, :]
        k_d = k * jnp.exp(b_last - b)
        upd = jnp.dot(k_d.T.astype(BF16), v, preferred_element_type=F32)
        decay_col = lax.dot_general(eye, jnp.broadcast_to(jnp.exp(b_last), (SUBLANES, dk)),
                                    (((1,), (1,)), ((), ())), precision=HI,
                                    preferred_element_type=F32)[:, 0:1]
        st_s[...] = decay_col * st + upd
        return carry

    lax.fori_loop(0, seq // ch, chunk, 0)

    o = raw_s[...]
    rms = o * lax.rsqrt(jnp.mean(o * o, axis=-1, keepdims=True) + LN_EPS) * ng_ref[...]
    gg = g_ref[0]
    o_ref[0] = rms * (gg * jax.nn.sigmoid(gg))


def _gla_core(proj, w_up_pad, b_gk, norm_g):
    B, S, _ = proj.shape
    H, dk, dv = C_HEADS, C_DK, C_DV
    assert dk == LANES and S % GLA_CHUNK == 0
    k_off = C_KEY_DIM // dk
    v_off = 2 * C_KEY_DIM // dv
    g_off = (2 * C_KEY_DIM + C_VAL_DIM) // dv
    gl_off = (2 * C_KEY_DIM + 2 * C_VAL_DIM) // LANES
    return pl.pallas_call(
        functools.partial(_gla_body, seq=S),
        grid=(B, H),
        in_specs=[pl.BlockSpec((1, S, dk), lambda b, h: (b, 0, h)),
                  pl.BlockSpec((1, S, dk), lambda b, h: (b, 0, k_off + h)),
                  pl.BlockSpec((1, S, dv), lambda b, h: (b, 0, v_off + h)),
                  pl.BlockSpec((1, S, dv), lambda b, h: (b, 0, g_off + h)),
                  pl.BlockSpec((1, S, LANES), lambda b, h: (b, 0, gl_off)),
                  pl.BlockSpec((LANES, dk), lambda b, h: (0, h)),
                  pl.BlockSpec((1, dk), lambda b, h: (0, h)),
                  pl.BlockSpec((1, dv), lambda b, h: (0, 0))],
        out_specs=pl.BlockSpec((1, S, dv), lambda b, h: (b, 0, h)),
        out_shape=jax.ShapeDtypeStruct((B, S, H * dv), F32),
        scratch_shapes=[pltpu.VMEM((S, dk), F32), pltpu.VMEM((dk, dv), F32), pltpu.VMEM((S, dv), F32)],
        compiler_params=_cparams("arbitrary", "arbitrary"),
        name="gla_core",
    )(proj, proj, proj, proj, proj, w_up_pad, b_gk.reshape(1, C_KEY_DIM), norm_g.reshape(1, dv))


def _router_body(h_ref, w_ref, b_ref, e_ref, gate_ref, rank_ref, cnt_ref, cnt_s, *, tm):
    @pl.when(pl.program_id(0) == 0)
    def _():
        cnt_s[...] = jnp.zeros((1, N_EXPERTS), F32)

    logits = jnp.dot(h_ref[...], w_ref[...], precision=HI, preferred_element_type=F32) + b_ref[...]
    lane = lax.broadcasted_iota(I32, (tm, N_EXPERTS), 1)
    rest = logits
    top_val, top_idx, onehot = [], [], []
    for _ in range(TOP_K):
        m = jnp.max(rest, axis=1, keepdims=True)
        idx = jnp.min(jnp.where(rest == m, lane, N_EXPERTS), axis=1, keepdims=True)
        oh = lane == idx
        rest = jnp.where(oh, -jnp.inf, rest)
        top_val.append(m)
        top_idx.append(idx)
        onehot.append(oh)
    ex = [jnp.exp(v - top_val[0]) for v in top_val]
    den = ex[0] + ex[1] + ex[2] + ex[3]
    chosen = sum(oh.astype(F32) for oh in onehot)
    row = lax.broadcasted_iota(I32, (tm, tm), 0)
    col = lax.broadcasted_iota(I32, (tm, tm), 1)
    before = (row > col).astype(BF16)
    seen = jnp.dot(before, chosen.astype(BF16), preferred_element_type=F32) + cnt_s[...]
    lane_k = lax.broadcasted_iota(I32, (tm, TOP_K), 1)
    e_out = jnp.zeros((tm, TOP_K), I32)
    g_out = jnp.zeros((tm, TOP_K), F32)
    r_out = jnp.zeros((tm, TOP_K), I32)
    for kk in range(TOP_K):
        rk = jnp.sum(jnp.where(onehot[kk], seen, 0.0), axis=1, keepdims=True)
        e_out = jnp.where(lane_k == kk, top_idx[kk], e_out)
        g_out = jnp.where(lane_k == kk, ex[kk] / den, g_out)
        r_out = jnp.where(lane_k == kk, rk.astype(I32), r_out)
    e_ref[...] = e_out
    gate_ref[...] = g_out
    rank_ref[...] = r_out
    cnt_s[...] = cnt_s[...] + jnp.sum(chosen, axis=0, keepdims=True)
    cnt_ref[...] = cnt_s[...]


def _router(h, w, b, *, tm=ROUTER_TM):
    T, D = h.shape
    E = N_EXPERTS
    return pl.pallas_call(
        functools.partial(_router_body, tm=tm),
        grid=(T // tm,),
        in_specs=[pl.BlockSpec((tm, D), lambda i: (i, 0)),
                  pl.BlockSpec((D, E), lambda i: (0, 0)),
                  pl.BlockSpec((1, E), lambda i: (0, 0))],
        out_specs=[pl.BlockSpec((tm, TOP_K), lambda i: (i, 0)),
                   pl.BlockSpec((tm, TOP_K), lambda i: (i, 0)),
                   pl.BlockSpec((tm, TOP_K), lambda i: (i, 0)),
                   pl.BlockSpec((1, E), lambda i: (0, 0))],
        out_shape=[jax.ShapeDtypeStruct((T, TOP_K), I32),
                   jax.ShapeDtypeStruct((T, TOP_K), F32),
                   jax.ShapeDtypeStruct((T, TOP_K), I32),
                   jax.ShapeDtypeStruct((1, E), F32)],
        scratch_shapes=[pltpu.VMEM((1, E), F32)],
        compiler_params=_cparams("arbitrary"),
        name="moe_router",
    )(h, w, b.reshape(1, E))


def _dispatch_body(dest_ref, zf_ref, h_ref, xs_ref, buf, zbuf, sem, *, tm, bm):
    i = pl.program_id(0)

    @pl.when(i == 0)
    def _():
        zbuf[...] = jnp.zeros(zbuf.shape, F32)

        def zero_copy(j):
            return pltpu.make_async_copy(
                zbuf, xs_ref.at[pl.ds(pl.multiple_of(zf_ref[j], ROW_TILES), bm * ROW_TILES), :], sem)

        def start(j, carry):
            @pl.when(zf_ref[j] >= 0)
            def _():
                zero_copy(j).start()
            return carry

        def wait(j, carry):
            @pl.when(zf_ref[j] >= 0)
            def _():
                zero_copy(j).wait()
            return carry

        lax.fori_loop(0, 2 * N_EXPERTS, start, 0)
        lax.fori_loop(0, 2 * N_EXPERTS, wait, 0)

    for s in range(ROW_TILES):
        buf[pl.ds(s, tm, stride=ROW_TILES), :] = h_ref[:, s * LANES:(s + 1) * LANES]

    def issue(t, carry):
        src = buf.at[pl.ds(pl.multiple_of(t * ROW_TILES, ROW_TILES), ROW_TILES), :]
        for kk in range(TOP_K):
            d = dest_ref[(i * tm + t) * TOP_K + kk]
            pltpu.make_async_copy(src, xs_ref.at[pl.ds(pl.multiple_of(d, ROW_TILES), ROW_TILES), :], sem).start()
        return carry

    lax.fori_loop(0, tm, issue, 0)
    for _ in range(TOP_K):
        pltpu.make_async_copy(buf, xs_ref.at[pl.ds(0, tm * ROW_TILES), :], sem).wait()


def _dispatch(dest_rows, zero_rows, h, n_rows, *, tm=DISPATCH_TM, bm=MOE_BM):
    T, D = h.shape
    grid_spec = pltpu.PrefetchScalarGridSpec(
        num_scalar_prefetch=2,
        grid=(T // tm,),
        in_specs=[pl.BlockSpec((tm, D), lambda i, dest, zf: (i, 0))],
        out_specs=pl.BlockSpec(memory_space=pl.ANY),
        scratch_shapes=[pltpu.VMEM((tm * ROW_TILES, LANES), F32), pltpu.VMEM((bm * ROW_TILES, LANES), F32),
                        pltpu.SemaphoreType.DMA],
    )
    return pl.pallas_call(
        functools.partial(_dispatch_body, tm=tm, bm=bm),
        grid_spec=grid_spec,
        out_shape=jax.ShapeDtypeStruct((n_rows * ROW_TILES, LANES), F32),
        compiler_params=pltpu.CompilerParams(dimension_semantics=("arbitrary",), vmem_limit_bytes=VMEM_LIMIT,
                                             has_side_effects=True),
        name="moe_dispatch",
    )(dest_rows, zero_rows, h)


def _expert_body(be_ref, nb_ref, xs_ref, wgu_ref, bgu_ref, wd_ref, bd_ref, ys_ref, wgu_s, wd_s, *, bm):
    i = pl.program_id(0)
    prev = be_ref[jnp.maximum(i - 1, 0)]
    live = i < nb_ref[0]

    @pl.when(live & ((i == 0) | (be_ref[i] != prev)))
    def _():
        wgu_s[...] = wgu_ref[...].astype(BF16)
        wd_s[...] = wd_ref[...].astype(BF16)

    @pl.when(live)
    def _():
        x = jnp.concatenate([xs_ref[pl.ds(s, bm, stride=ROW_TILES), :] for s in range(ROW_TILES)], axis=1)
        hh = jnp.dot(x.astype(BF16), wgu_s[...], preferred_element_type=F32) + bgu_ref[...]
        d_e = hh.shape[1] // 2
        h_gate = jnp.minimum(hh[:, :d_e], SWIGLU_LIMIT)
        h_up = jnp.clip(hh[:, d_e:], -SWIGLU_LIMIT, SWIGLU_LIMIT)
        act = h_gate * jax.nn.sigmoid(SWIGLU_ALPHA * h_gate) * (h_up + 1.0)
        y = jnp.dot(act.astype(BF16), wd_s[...], preferred_element_type=F32) + bd_ref[...]
        for s in range(ROW_TILES):
            ys_ref[pl.ds(s, bm, stride=ROW_TILES), :] = y[:, s * LANES:(s + 1) * LANES]

    @pl.when(jnp.logical_not(live))
    def _():
        ys_ref[...] = jnp.zeros(ys_ref.shape, F32)


def _experts(block_e, n_live, xs, layer, w_gu, b_gu, w_down, b_down, *, bm=MOE_BM):
    L, E, D, D2 = w_gu.shape
    n_blocks = xs.shape[0] // (bm * ROW_TILES)

    def blk(i, be, nb):
        return jnp.minimum(i, nb[0] - 1)

    grid_spec = pltpu.PrefetchScalarGridSpec(
        num_scalar_prefetch=2,
        grid=(n_blocks,),
        in_specs=[pl.BlockSpec((bm * ROW_TILES, LANES), lambda i, be, nb: (blk(i, be, nb), 0)),
                  pl.BlockSpec((None, None, D, D2), lambda i, be, nb: (layer, be[blk(i, be, nb)], 0, 0)),
                  pl.BlockSpec((None, None, 1, D2), lambda i, be, nb: (layer, be[blk(i, be, nb)], 0, 0)),
                  pl.BlockSpec((None, None, D2 // 2, D), lambda i, be, nb: (layer, be[blk(i, be, nb)], 0, 0)),
                  pl.BlockSpec((None, None, 1, D), lambda i, be, nb: (layer, be[blk(i, be, nb)], 0, 0))],
        out_specs=pl.BlockSpec((bm * ROW_TILES, LANES), lambda i, be, nb: (i, 0)),
        scratch_shapes=[pltpu.VMEM((D, D2), BF16), pltpu.VMEM((D2 // 2, D), BF16)],
    )
    return pl.pallas_call(
        functools.partial(_expert_body, bm=bm),
        grid_spec=grid_spec,
        out_shape=jax.ShapeDtypeStruct(xs.shape, F32),
        compiler_params=_cparams("arbitrary"),
        name="moe_experts",
    )(block_e, n_live, xs, w_gu, b_gu.reshape(L, E, 1, D2), w_down, b_down.reshape(L, E, 1, D))


def _combine_body(dest_ref, ys_ref, gate_ref, h_ref, g_ref, b_ref, o_ref, buf, sem, *, tm):
    i = pl.program_id(0)

    def issue(t, carry):
        for kk in range(TOP_K):
            d = dest_ref[(i * tm + t) * TOP_K + kk]
            slot = pl.multiple_of((t * TOP_K + kk) * ROW_TILES, ROW_TILES)
            pltpu.make_async_copy(ys_ref.at[pl.ds(pl.multiple_of(d, ROW_TILES), ROW_TILES), :],
                                  buf.at[pl.ds(slot, ROW_TILES), :], sem).start()
        return carry

    lax.fori_loop(0, tm, issue, 0)
    pltpu.make_async_copy(ys_ref.at[pl.ds(0, tm * TOP_K * ROW_TILES), :], buf, sem).wait()

    gate = gate_ref[...]
    acc = jnp.zeros((tm, D_MODEL), F32)
    stride = TOP_K * ROW_TILES
    for kk in range(TOP_K):
        yk = jnp.concatenate([buf[pl.ds(kk * ROW_TILES + s, tm, stride=stride), :] for s in range(ROW_TILES)],
                             axis=1)
        acc = acc + gate[:, kk:kk + 1] * yk
    o_ref[...] = _ln(ALPHA * h_ref[...] + acc, g_ref[...], b_ref[...])


def _combine(dest_rows, ys, gate, h, g, b, *, tm=COMBINE_TM):
    T, D = h.shape
    grid_spec = pltpu.PrefetchScalarGridSpec(
        num_scalar_prefetch=1,
        grid=(T // tm,),
        in_specs=[pl.BlockSpec(memory_space=pl.ANY),
                  pl.BlockSpec((tm, TOP_K), lambda i, dest: (i, 0)),
                  pl.BlockSpec((tm, D), lambda i, dest: (i, 0)),
                  pl.BlockSpec((1, D), lambda i, dest: (0, 0)),
                  pl.BlockSpec((1, D), lambda i, dest: (0, 0))],
        out_specs=pl.BlockSpec((tm, D), lambda i, dest: (i, 0)),
        scratch_shapes=[pltpu.VMEM((tm * TOP_K * ROW_TILES, LANES), F32), pltpu.SemaphoreType.DMA],
    )
    return pl.pallas_call(
        functools.partial(_combine_body, tm=tm),
        grid_spec=grid_spec,
        out_shape=jax.ShapeDtypeStruct((T, D), F32),
        compiler_params=_cparams("arbitrary"),
        name="moe_combine",
    )(dest_rows, ys, gate, h, g.reshape(1, D), b.reshape(1, D))


def _moe_block(h, layer, w_router, b_router, w_gu, b_gu, w_down, b_down, g, b):
    T = h.shape[0]
    bm = MOE_BM
    top_e, gate, rank, counts = _router(h, w_router, b_router)
    counts = counts[0].astype(I32)
    padded = (counts + bm - 1) // bm * bm
    pend = jnp.cumsum(padded)
    pstart = pend - padded
    n_blocks = T * TOP_K // bm + N_EXPERTS
    dest_rows = ((pstart[top_e] + rank) * ROW_TILES).reshape(-1)
    block_start = jnp.arange(n_blocks, dtype=I32) * bm
    block_e = jnp.minimum(jnp.sum((pend[None, :] <= block_start[:, None]).astype(I32), axis=1), N_EXPERTS - 1)
    n_live = (pend[-1:] // bm).astype(I32)
    last_blk = jnp.where(padded > 0, pend - bm, -1)
    tail_blk = n_live[0] + jnp.arange(N_EXPERTS, dtype=I32)
    tail_blk = jnp.where(tail_blk < n_blocks, tail_blk * bm, -1)
    zero_rows = jnp.concatenate([last_blk, tail_blk])
    zero_rows = jnp.where(zero_rows >= 0, zero_rows * ROW_TILES, -1).astype(I32)
    xs = _dispatch(dest_rows, zero_rows, h, n_blocks * bm)
    ys = _experts(block_e, n_live, xs, layer, w_gu, b_gu, w_down, b_down)
    return _combine(dest_rows, ys, gate, h, g, b)


def kernel(x, rel_bias, a_w_in, a_w_out, b_w_in, b_ln_g, b_ln_b, b_w_s, b_b_s, b_w_out,
           c_w_in, c_w_gk_up, c_b_gk, c_norm_g, c_w_out, ln_g, ln_b,
           moe_w_router, moe_b_router, moe_w_gate_up, moe_b_gate_up, moe_w_down, moe_b_down):
    B, S, D = x.shape
    T = B * S
    h = x.reshape(T, D)
    tables = _bias_tables(rel_bias)
    for i in range(DEPTH):
        j = i // N_MIXERS
        mixer = i % N_MIXERS
        if mixer == 0:
            qkv = _matmul(h, a_w_in[j], tn=1024)
            o = _moba_attention(qkv.reshape(B, S, 3 * D), rel_bias, tables)
            h = _matmul_res_ln(o.reshape(T, D), a_w_out[j], h, ln_g[i, 0], ln_b[i, 0])
        elif mixer == 1:
            z = _matmul(h, b_w_in[j], act="gelu", tn=1024)
            h = _gmlp_gate_out(z, h, b_ln_g[j], b_ln_b[j], b_w_s[j], b_b_s[j], b_w_out[j],
                               ln_g[i, 0], ln_b[i, 0])
        else:
            pad = LANES - C_GATE_RANK
            w_in = jnp.pad(c_w_in[j], ((0, 0), (0, pad)))
            w_up = jnp.pad(c_w_gk_up[j], ((0, pad), (0, 0)))
            proj = _matmul(h, w_in, tn=640)
            o = _gla_core(proj.reshape(B, S, C_IN_WIDTH + pad), w_up, c_b_gk[j], c_norm_g[j])
            h = _matmul_res_ln(o.reshape(T, D), c_w_out[j], h, ln_g[i, 0], ln_b[i, 0])
        h = _moe_block(h, i, moe_w_router[i], moe_b_router[i], moe_w_gate_up, moe_b_gate_up,
                       moe_w_down, moe_b_down, ln_g[i, 1], ln_b[i, 1])
    return h.reshape(B, S, D)
```

```python
import functools
import math

import numpy as np
import jax
import jax.numpy as jnp
from jax import lax
from jax.experimental import pallas as pl
from jax.experimental.pallas import tpu as pltpu

F32 = jnp.float32
BF16 = jnp.bfloat16
I32 = jnp.int32
HI = lax.Precision.HIGHEST

D_MODEL = 1024
DEPTH = 4
N_MIXERS = 3
ALPHA = (2.0 * DEPTH) ** 0.25
LN_EPS = 1e-5

A_HEADS = 8
A_DH = D_MODEL // A_HEADS
MOBA_BLOCK = 256
MOBA_TOPK = 3
REL_BUCKETS = 32
REL_MAX_DIST = 128

B_GROUPS = 8
B_WIDTH = 2 * D_MODEL
B_CHUNK = 128
B_GW = B_WIDTH // B_GROUPS

C_HEADS = 4
C_KEY_DIM = D_MODEL // 2
C_VAL_DIM = D_MODEL
C_DK = C_KEY_DIM // C_HEADS
C_DV = C_VAL_DIM // C_HEADS
C_GATE_RANK = 16
C_GATE_NORMALIZER = 16.0
C_IN_WIDTH = 2 * C_KEY_DIM + 2 * C_VAL_DIM + C_GATE_RANK

N_EXPERTS = 32
TOP_K = 4
SWIGLU_LIMIT = 7.0
SWIGLU_ALPHA = 1.702

LANES = 128
SUBLANES = 8
VMEM_LIMIT = 56 * 1024 * 1024

GLA_CHUNK = 128
GLA_SUB = 16
MOE_BM = 512
MOE_TM = 512
MOE_SEG = SUBLANES
MOE_SEG_BITS = (MOE_TM // MOE_SEG).bit_length()
MOE_STAGE = MOE_TM * TOP_K + N_EXPERTS * MOE_SEG
MOE_PERM_CHUNK = 256


def _cparams(*sem):
    return pltpu.CompilerParams(dimension_semantics=sem, vmem_limit_bytes=VMEM_LIMIT)


def _ln(x, g, b):
    mu = jnp.mean(x, axis=-1, keepdims=True)
    xc = x - mu
    var = jnp.mean(xc * xc, axis=-1, keepdims=True)
    return xc * lax.rsqrt(var + LN_EPS) * g + b


def _mm_body(x_ref, w_ref, o_ref, wb_ref, *, act):
    @pl.when(pl.program_id(1) == 0)
    def _():
        wb_ref[...] = w_ref[...].astype(BF16)

    acc = jnp.dot(x_ref[...].astype(BF16), wb_ref[...], preferred_element_type=F32)
    if act == "gelu":
        acc = 0.5 * acc * (1.0 + lax.erf(acc * (2.0 ** -0.5)))
    o_ref[...] = acc


def _matmul(x, w, *, act=None, tm=512, tn=1024):
    M, K = x.shape
    N = w.shape[1]
    assert M % tm == 0 and N % tn == 0
    return pl.pallas_call(
        functools.partial(_mm_body, act=act),
        grid=(N // tn, M // tm),
        in_specs=[pl.BlockSpec((tm, K), lambda j, i: (i, 0)),
                  pl.BlockSpec((K, tn), lambda j, i: (0, j))],
        out_specs=pl.BlockSpec((tm, tn), lambda j, i: (i, j)),
        out_shape=jax.ShapeDtypeStruct((M, N), F32),
        scratch_shapes=[pltpu.VMEM((K, tn), BF16)],
        compiler_params=_cparams("arbitrary", "arbitrary"),
        name="proj_matmul",
    )(x, w)


def _mm_res_ln_body(x_ref, w_ref, h_ref, g_ref, b_ref, o_ref, wb_ref):
    @pl.when(pl.program_id(0) == 0)
    def _():
        wb_ref[...] = w_ref[...].astype(BF16)

    t = jnp.dot(x_ref[...].astype(BF16), wb_ref[...], preferred_element_type=F32)
    o_ref[...] = _ln(ALPHA * h_ref[...] + t, g_ref[...], b_ref[...])


def _matmul_res_ln(x, w, h, g, b, *, tm=512):
    M, K = x.shape
    N = w.shape[1]
    return pl.pallas_call(
        _mm_res_ln_body,
        grid=(M // tm,),
        in_specs=[pl.BlockSpec((tm, K), lambda i: (i, 0)),
                  pl.BlockSpec((K, N), lambda i: (0, 0)),
                  pl.BlockSpec((tm, N), lambda i: (i, 0)),
                  pl.BlockSpec((1, N), lambda i: (0, 0)),
                  pl.BlockSpec((1, N), lambda i: (0, 0))],
        out_specs=pl.BlockSpec((tm, N), lambda i: (i, 0)),
        out_shape=jax.ShapeDtypeStruct((M, N), F32),
        scratch_shapes=[pltpu.VMEM((K, N), BF16)],
        compiler_params=_cparams("arbitrary"),
        name="outproj_res_ln",
    )(x, w, h, g.reshape(1, N), b.reshape(1, N))


def _t5_bucket_lower_bounds():
    n = np.arange(0, 4 * REL_MAX_DIST, dtype=np.int64)
    max_exact = REL_BUCKETS // 2
    nf = np.maximum(n, 1).astype(np.float32)
    large = max_exact + (np.log(nf / np.float32(max_exact)) / np.float32(math.log(REL_MAX_DIST / max_exact))
                         * np.float32(REL_BUCKETS - max_exact)).astype(np.int32)
    large = np.minimum(large, REL_BUCKETS - 1)
    bucket = np.where(n < max_exact, n, large)
    assert np.all(np.diff(bucket) >= 0) and bucket[-1] == REL_BUCKETS - 1
    return [int(np.argmax(bucket >= b)) for b in range(REL_BUCKETS)]


_BUCKET_LO = _t5_bucket_lower_bounds()


def _bias_table_body(rel_ref, o_ref):
    blk = MOBA_BLOCK
    row = lax.broadcasted_iota(I32, (blk, blk), 0)
    col = lax.broadcasted_iota(I32, (blk, blk), 1)
    for t in range(2):
        d = row - col + t * blk
        for h in range(A_HEADS):
            val = jnp.full((blk, blk), rel_ref[REL_BUCKETS - 1, h], F32)
            for b in range(REL_BUCKETS - 2, -1, -1):
                val = jnp.where(d < _BUCKET_LO[b + 1], rel_ref[b, h], val)
            if t == 0:
                val = jnp.where(d >= 0, val, -jnp.inf)
            o_ref[h, t] = val


def _bias_tables(rel_bias):
    return pl.pallas_call(
        _bias_table_body,
        in_specs=[pl.BlockSpec(memory_space=pltpu.SMEM)],
        out_shape=jax.ShapeDtypeStruct((A_HEADS, 2, MOBA_BLOCK, MOBA_BLOCK), F32),
        compiler_params=pltpu.CompilerParams(vmem_limit_bytes=VMEM_LIMIT),
        name="rel_bias_tables",
    )(rel_bias)


def _moba_body(rel_ref, q_ref, k_ref, v_ref, tb_ref, o_ref, kb_s, vb_s, *, seq):
    blk, dh = MOBA_BLOCK, A_DH
    nb = seq // blk
    h = pl.program_id(1)
    scale = dh ** -0.5
    k = k_ref[0]
    kb_s[...] = k.astype(BF16)
    vb_s[...] = v_ref[0].astype(BF16)
    kmean = jnp.mean(k.reshape(nb, blk, dh), axis=1)
    kmean_pad = jnp.concatenate([kmean, jnp.zeros((LANES - nb, dh), F32)], axis=0)
    b_far = rel_ref[REL_BUCKETS - 1, h]
    lane = lax.broadcasted_iota(I32, (blk, LANES), 1)
    for c in range(nb):
        q = q_ref[0, c * blk:(c + 1) * blk, :]
        n_keys = (c + 1) * blk
        s = lax.dot_general(q.astype(BF16), kb_s[0:n_keys, :], (((1,), (1,)), ((), ())),
                            preferred_element_type=F32) * scale
        if c > MOBA_TOPK:
            gate = lax.dot_general(q, kmean_pad, (((1,), (1,)), ((), ())),
                                   precision=HI, preferred_element_type=F32)
            gate = jnp.where(lane < c, gate, -jnp.inf)
        pieces = []
        for n in range(c):
            bias = tb_ref[0, 1] if n == c - 1 else b_far
            piece = s[:, n * blk:(n + 1) * blk] + bias
            if c > MOBA_TOPK:
                gn = gate[:, n:n + 1]
                beats = (gate > gn) | ((gate == gn) & (lane < n))
                rank = jnp.sum(beats.astype(F32), axis=1, keepdims=True)
                piece = jnp.where(rank < MOBA_TOPK, piece, -jnp.inf)
            pieces.append(piece)
        pieces.append(s[:, c * blk:(c + 1) * blk] + tb_ref[0, 0])
        logits = jnp.concatenate(pieces, axis=1) if c > 0 else pieces[0]
        m = jnp.max(logits, axis=1, keepdims=True)
        p = jnp.exp(logits - m)
        denom = jnp.sum(p, axis=1, keepdims=True)
        o = jnp.dot(p.astype(BF16), vb_s[0:n_keys, :], preferred_element_type=F32)
        o_ref[0, c * blk:(c + 1) * blk, :] = o / denom


def _moba_attention(qkv, rel_bias, tables):
    B, S, _ = qkv.shape
    H, dh = A_HEADS, A_DH
    assert S % MOBA_BLOCK == 0 and S // MOBA_BLOCK <= SUBLANES
    return pl.pallas_call(
        functools.partial(_moba_body, seq=S),
        grid=(B, H),
        in_specs=[pl.BlockSpec(memory_space=pltpu.SMEM),
                  pl.BlockSpec((1, S, dh), lambda b, h: (b, 0, h)),
                  pl.BlockSpec((1, S, dh), lambda b, h: (b, 0, H + h)),
                  pl.BlockSpec((1, S, dh), lambda b, h: (b, 0, 2 * H + h)),
                  pl.BlockSpec((1, 2, MOBA_BLOCK, MOBA_BLOCK), lambda b, h: (h, 0, 0, 0))],
        out_specs=pl.BlockSpec((1, S, dh), lambda b, h: (b, 0, h)),
        out_shape=jax.ShapeDtypeStruct((B, S, H * dh), F32),
        scratch_shapes=[pltpu.VMEM((S, dh), BF16), pltpu.VMEM((S, dh), BF16)],
        compiler_params=_cparams("arbitrary", "arbitrary"),
        name="moba_attention",
    )(rel_bias, qkv, qkv, qkv, tables)


def _gmlp_body(u_ref, v_ref, h_ref, vg_ref, vb_ref, ws_ref, bs_ref, wo_ref, g_ref, b_ref, o_ref,
               wst_s, wob_s, *, tm):
    @pl.when(pl.program_id(0) == 0)
    def _():
        row = lax.broadcasted_iota(I32, (B_CHUNK, B_CHUNK), 0)
        col = lax.broadcasted_iota(I32, (B_CHUNK, B_CHUNK), 1)
        for g in range(B_GROUPS):
            wst_s[g] = jnp.where(row >= col, ws_ref[g], 0.0).astype(BF16)
        wob_s[...] = wo_ref[...].astype(BF16)

    vn = _ln(v_ref[...], vg_ref[...], vb_ref[...]).astype(BF16)
    u = u_ref[...]
    rows = []
    for c in range(tm // B_CHUNK):
        r0 = c * B_CHUNK
        cols = []
        for g in range(B_GROUPS):
            c0 = g * B_GW
            mixed = jnp.dot(wst_s[g], vn[r0:r0 + B_CHUNK, c0:c0 + B_GW],
                            preferred_element_type=F32) + bs_ref[g]
            cols.append(u[r0:r0 + B_CHUNK, c0:c0 + B_GW] * mixed)
        rows.append(jnp.concatenate(cols, axis=1))
    y = jnp.concatenate(rows, axis=0).astype(BF16)
    t = jnp.dot(y, wob_s[...], preferred_element_type=F32)
    o_ref[...] = _ln(ALPHA * h_ref[...] + t, g_ref[...], b_ref[...])


def _gmlp_gate_out(z, h, v_g, v_b, w_s, b_s, w_out, g, b, *, tm=256):
    T = z.shape[0]
    D = D_MODEL
    return pl.pallas_call(
        functools.partial(_gmlp_body, tm=tm),
        grid=(T // tm,),
        in_specs=[pl.BlockSpec((tm, B_WIDTH), lambda i: (i, 0)),
                  pl.BlockSpec((tm, B_WIDTH), lambda i: (i, 1)),
                  pl.BlockSpec((tm, D), lambda i: (i, 0)),
                  pl.BlockSpec((1, B_WIDTH), lambda i: (0, 0)),
                  pl.BlockSpec((1, B_WIDTH), lambda i: (0, 0)),
                  pl.BlockSpec((B_GROUPS, B_CHUNK, B_CHUNK), lambda i: (0, 0, 0)),
                  pl.BlockSpec((B_GROUPS, B_CHUNK, 1), lambda i: (0, 0, 0)),
                  pl.BlockSpec((B_WIDTH, D), lambda i: (0, 0)),
                  pl.BlockSpec((1, D), lambda i: (0, 0)),
                  pl.BlockSpec((1, D), lambda i: (0, 0))],
        out_specs=pl.BlockSpec((tm, D), lambda i: (i, 0)),
        out_shape=jax.ShapeDtypeStruct((T, D), F32),
        scratch_shapes=[pltpu.VMEM((B_GROUPS, B_CHUNK, B_CHUNK), BF16), pltpu.VMEM((B_WIDTH, D), BF16)],
        compiler_params=_cparams("arbitrary"),
        name="gmlp_gate_out",
    )(z, z, h, v_g.reshape(1, B_WIDTH), v_b.reshape(1, B_WIDTH), w_s, b_s.reshape(B_GROUPS, B_CHUNK, 1),
      w_out, g.reshape(1, D), b.reshape(1, D))


def _gla_body(q_ref, k_ref, v_ref, g_ref, gl_ref, wup_ref, bgk_ref, ng_ref, o_ref, la_s, st_s, raw_s, *, seq):
    ch, sub, dk, dv = GLA_CHUNK, GLA_SUB, C_DK, C_DV
    x = jnp.dot(gl_ref[0], wup_ref[...], precision=HI, preferred_element_type=F32) + bgk_ref[...]
    la_s[...] = (jnp.minimum(x, 0.0) - jnp.log1p(jnp.exp(-jnp.abs(x)))) * (1.0 / C_GATE_NORMALIZER)
    st_s[...] = jnp.zeros((dk, dv), F32)

    row_c = lax.broadcasted_iota(I32, (ch, ch), 0)
    col_c = lax.broadcasted_iota(I32, (ch, ch), 1)
    tri = (row_c >= col_c).astype(F32)
    eye = (row_c == col_c).astype(F32)
    sub_i = lax.broadcasted_iota(I32, (sub, ch), 0)
    lane_j = lax.broadcasted_iota(I32, (sub, ch), 1)

    def chunk(c, carry):
        s0 = pl.multiple_of(c * ch, ch)
        a = la_s[pl.ds(s0, ch), :]
        b = jnp.dot(tri, a, precision=HI, preferred_element_type=F32)
        q = q_ref[0, pl.ds(s0, ch), :] * (dk ** -0.5)
        k = k_ref[0, pl.ds(s0, ch), :]
        v = v_ref[0, pl.ds(s0, ch), :].astype(BF16)
        st = st_s[...]
        o = jnp.dot((q * jnp.exp(b)).astype(BF16), st.astype(BF16), preferred_element_type=F32)

        blocks = []
        for sb in range(ch // sub):
            r0 = sb * sub
            q_i = q[r0:r0 + sub]
            b_i = b[r0:r0 + sub]
            k_i = k[r0:r0 + sub]
            att = jnp.zeros((sub, ch), F32)
            for j in range(sub):
                e = jnp.exp(jnp.minimum(b_i - b_i[j:j + 1, :], 0.0))
                colv = jnp.sum(q_i * k_i[j:j + 1, :] * e, axis=1, keepdims=True)
                att = jnp.where((lane_j == r0 + j) & (sub_i >= j), colv, att)
            if sb > 0:
                ref_b = b[r0 - 1:r0, :]
                q_t = q_i * jnp.exp(b_i - ref_b)
                k_t = k * jnp.exp(jnp.minimum(ref_b - b, 0.0))
                off = lax.dot_general(q_t, k_t, (((1,), (1,)), ((), ())),
                                      precision=HI, preferred_element_type=F32)
                att = jnp.where(lane_j < r0, off, att)
            blocks.append(att)
        att_full = jnp.concatenate(blocks, axis=0)
        o = o + jnp.dot(att_full.astype(BF16), v, preferred_element_type=F32)
        raw_s[pl.ds(s0, ch), :] = o

        b_last = b[ch - 1:ch, :]
        k_d = k * jnp.exp(b_last - b)
        upd = jnp.dot(k_d.T.astype(BF16), v, preferred_element_type=F32)
        decay_col = lax.dot_general(eye, jnp.broadcast_to(jnp.exp(b_last), (SUBLANES, dk)),
                                    (((1,), (1,)), ((), ())), precision=HI,
                                    preferred_element_type=F32)[:, 0:1]
        st_s[...] = decay_col * st + upd
        return carry

    lax.fori_loop(0, seq // ch, chunk, 0)

    o = raw_s[...]
    rms = o * lax.rsqrt(jnp.mean(o * o, axis=-1, keepdims=True) + LN_EPS) * ng_ref[...]
    gg = g_ref[0]
    o_ref[0] = rms * (gg * jax.nn.sigmoid(gg))


def _gla_core(proj, w_up_pad, b_gk, norm_g):
    B, S, _ = proj.shape
    H, dk, dv = C_HEADS, C_DK, C_DV
    assert dk == LANES and S % GLA_CHUNK == 0
    k_off = C_KEY_DIM // dk
    v_off = 2 * C_KEY_DIM // dv
    g_off = (2 * C_KEY_DIM + C_VAL_DIM) // dv
    gl_off = (2 * C_KEY_DIM + 2 * C_VAL_DIM) // LANES
    return pl.pallas_call(
        functools.partial(_gla_body, seq=S),
        grid=(B, H),
        in_specs=[pl.BlockSpec((1, S, dk), lambda b, h: (b, 0, h)),
                  pl.BlockSpec((1, S, dk), lambda b, h: (b, 0, k_off + h)),
                  pl.BlockSpec((1, S, dv), lambda b, h: (b, 0, v_off + h)),
                  pl.BlockSpec((1, S, dv), lambda b, h: (b, 0, g_off + h)),
                  pl.BlockSpec((1, S, LANES), lambda b, h: (b, 0, gl_off)),
                  pl.BlockSpec((LANES, dk), lambda b, h: (0, h)),
                  pl.BlockSpec((1, dk), lambda b, h: (0, h)),
                  pl.BlockSpec((1, dv), lambda b, h: (0, 0))],
        out_specs=pl.BlockSpec((1, S, dv), lambda b, h: (b, 0, h)),
        out_shape=jax.ShapeDtypeStruct((B, S, H * dv), F32),
        scratch_shapes=[pltpu.VMEM((S, dk), F32), pltpu.VMEM((dk, dv), F32), pltpu.VMEM((S, dv), F32)],
        compiler_params=_cparams("arbitrary", "arbitrary"),
        name="gla_core",
    )(proj, proj, proj, proj, proj, w_up_pad, b_gk.reshape(1, C_KEY_DIM), norm_g.reshape(1, dv))


def _router_body(h_ref, w_ref, b_ref, slot_ref, gate_ref, cnt_ref, lbase_ref, gbase_ref, run_s, *, tm):
    E = N_EXPERTS

    @pl.when(pl.program_id(0) == 0)
    def _():
        run_s[...] = jnp.zeros((1, E), F32)

    logits = jnp.dot(h_ref[...], w_ref[...], precision=HI, preferred_element_type=F32) + b_ref[...]
    lane = lax.broadcasted_iota(I32, (tm, E), 1)
    rest = logits
    top_val, onehot = [], []
    for _ in range(TOP_K):
        m = jnp.max(rest, axis=1, keepdims=True)
        idx = jnp.min(jnp.where(rest == m, lane, E), axis=1, keepdims=True)
        oh = lane == idx
        rest = jnp.where(oh, -jnp.inf, rest)
        top_val.append(m)
        onehot.append(oh)
    ex = [jnp.exp(v - top_val[0]) for v in top_val]
    den = ex[0] + ex[1] + ex[2] + ex[3]
    chosen = sum(oh.astype(F32) for oh in onehot)

    row = lax.broadcasted_iota(I32, (tm, tm), 0)
    col = lax.broadcasted_iota(I32, (tm, tm), 1)
    before = (row > col).astype(BF16)
    seen = jnp.dot(before, chosen.astype(BF16), preferred_element_type=F32)
    cnt = jnp.sum(chosen, axis=0, keepdims=True)
    cnt_pad = jnp.floor((cnt + (MOE_SEG - 1)) * (1.0 / MOE_SEG)) * MOE_SEG
    er = lax.broadcasted_iota(I32, (E, E), 0)
    ec = lax.broadcasted_iota(I32, (E, E), 1)
    cnt_rows = jnp.broadcast_to(cnt_pad, (SUBLANES, E))
    lbase_rows = jnp.dot(cnt_rows, (er < ec).astype(F32), precision=HI, preferred_element_type=F32)
    pos = seen + lbase_rows[0:1, :]

    lane_k = lax.broadcasted_iota(I32, (tm, TOP_K), 1)
    s_out = jnp.zeros((tm, TOP_K), I32)
    g_out = jnp.zeros((tm, TOP_K), F32)
    for kk in range(TOP_K):
        sk = jnp.sum(jnp.where(onehot[kk], pos, 0.0), axis=1, keepdims=True)
        s_out = jnp.where(lane_k == kk, sk.astype(I32), s_out)
        g_out = jnp.where(lane_k == kk, ex[kk] / den, g_out)
    slot_ref[...] = s_out
    gate_ref[...] = g_out
    cnt_ref[...] = cnt_rows.astype(I32)
    lbase_ref[...] = lbase_rows.astype(I32)
    gbase_ref[...] = jnp.broadcast_to(run_s[...], (SUBLANES, E)).astype(I32)
    run_s[...] = run_s[...] + cnt_pad


def _router(h, w, b, *, tm=MOE_TM):
    T, D = h.shape
    E = N_EXPERTS
    nt = T // tm
    tab = jax.ShapeDtypeStruct((nt * SUBLANES, E), I32)
    tab_spec = pl.BlockSpec((SUBLANES, E), lambda i: (i, 0))
    return pl.pallas_call(
        functools.partial(_router_body, tm=tm),
        grid=(nt,),
        in_specs=[pl.BlockSpec((tm, D), lambda i: (i, 0)),
                  pl.BlockSpec((D, E), lambda i: (0, 0)),
                  pl.BlockSpec((1, E), lambda i: (0, 0))],
        out_specs=[pl.BlockSpec((tm, TOP_K), lambda i: (i, 0)),
                   pl.BlockSpec((tm, TOP_K), lambda i: (i, 0)),
                   tab_spec, tab_spec, tab_spec],
        out_shape=[jax.ShapeDtypeStruct((T, TOP_K), I32),
                   jax.ShapeDtypeStruct((T, TOP_K), F32),
                   tab, tab, tab],
        scratch_shapes=[pltpu.VMEM((1, E), F32)],
        compiler_params=_cparams("arbitrary"),
        name="moe_router",
    )(h, w, b.reshape(1, E))


def _segment_copies(seg_g, seg_l, seg_n, tile, make_copy, start):
    def per_expert(e, carry):
        idx = tile * N_EXPERTS + e
        n = seg_n[idx]
        g = seg_g[idx]
        l = seg_l[idx]
        for bit in range(MOE_SEG_BITS - 1, -1, -1):
            size = MOE_SEG << bit

            @pl.when((n & size) != 0)
            def _():
                off = n & ~(2 * size - 1)
                cp = make_copy(pl.multiple_of(g + off, MOE_SEG), pl.multiple_of(l + off, MOE_SEG), size)
                if start:
                    cp.start()
                else:
                    cp.wait()
        return carry

    lax.fori_loop(0, N_EXPERTS, per_expert, 0)


def _dispatch_body(segg_ref, segl_ref, segn_ref, zf_ref, slot_t_ref, h_ref, xs_ref, stage, zbuf, sem, *, tm, bm):
    i = pl.program_id(0)

    @pl.when(i == 0)
    def _():
        zbuf[...] = jnp.zeros(zbuf.shape, F32)

        def zero_copy(j):
            return pltpu.make_async_copy(zbuf, xs_ref.at[pl.ds(pl.multiple_of(zf_ref[j], MOE_SEG), bm), :], sem)

        def start(j, carry):
            @pl.when(zf_ref[j] >= 0)
            def _():
                zero_copy(j).start()
            return carry

        def wait(j, carry):
            @pl.when(zf_ref[j] >= 0)
            def _():
                zero_copy(j).wait()
            return carry

        lax.fori_loop(0, zf_ref.shape[0], start, 0)
        lax.fori_loop(0, zf_ref.shape[0], wait, 0)

    hb = h_ref[...].astype(BF16)
    for c in range(MOE_STAGE // MOE_PERM_CHUNK):
        r0 = c * MOE_PERM_CHUNK
        rows = lax.broadcasted_iota(I32, (MOE_PERM_CHUNK, tm), 0) + r0
        hit = rows == slot_t_ref[0:1, :]
        for kk in range(1, TOP_K):
            hit = hit | (rows == slot_t_ref[kk:kk + 1, :])
        perm = jnp.where(hit, 1.0, 0.0).astype(BF16)
        stage[r0:r0 + MOE_PERM_CHUNK, :] = jnp.dot(perm, hb, preferred_element_type=F32)

    def make_copy(g, l, size):
        return pltpu.make_async_copy(stage.at[pl.ds(l, size), :], xs_ref.at[pl.ds(g, size), :], sem)

    _segment_copies(segg_ref, segl_ref, segn_ref, i, make_copy, True)
    _segment_copies(segg_ref, segl_ref, segn_ref, i, make_copy, False)


def _dispatch(seg_g, seg_l, seg_n, zero_rows, slot_t, h, n_rows, *, tm=MOE_TM, bm=MOE_BM):
    T, D = h.shape
    grid_spec = pltpu.PrefetchScalarGridSpec(
        num_scalar_prefetch=4,
        grid=(T // tm,),
        in_specs=[pl.BlockSpec((TOP_K, tm), lambda i, *_: (0, i)),
                  pl.BlockSpec((tm, D), lambda i, *_: (i, 0))],
        out_specs=pl.BlockSpec(memory_space=pl.ANY),
        scratch_shapes=[pltpu.VMEM((MOE_STAGE, D), F32), pltpu.VMEM((bm, D), F32), pltpu.SemaphoreType.DMA],
    )
    return pl.pallas_call(
        functools.partial(_dispatch_body, tm=tm, bm=bm),
        grid_spec=grid_spec,
        out_shape=jax.ShapeDtypeStruct((n_rows, D), F32),
        compiler_params=pltpu.CompilerParams(dimension_semantics=("arbitrary",), vmem_limit_bytes=VMEM_LIMIT,
                                             has_side_effects=True),
        name="moe_dispatch",
    )(seg_g, seg_l, seg_n, zero_rows, slot_t, h)


def _expert_body(be_ref, nb_ref, xs_ref, wgu_ref, bgu_ref, wd_ref, bd_ref, ys_ref, wgu_s, wd_s):
    i = pl.program_id(0)
    prev = be_ref[jnp.maximum(i - 1, 0)]
    live = i < nb_ref[0]

    @pl.when(live & ((i == 0) | (be_ref[i] != prev)))
    def _():
        wgu_s[...] = wgu_ref[...].astype(BF16)
        wd_s[...] = wd_ref[...].astype(BF16)

    @pl.when(live)
    def _():
        hh = jnp.dot(xs_ref[...].astype(BF16), wgu_s[...], preferred_element_type=F32) + bgu_ref[...]
        d_e = hh.shape[1] // 2
        h_gate = jnp.minimum(hh[:, :d_e], SWIGLU_LIMIT)
        h_up = jnp.clip(hh[:, d_e:], -SWIGLU_LIMIT, SWIGLU_LIMIT)
        act = h_gate * jax.nn.sigmoid(SWIGLU_ALPHA * h_gate) * (h_up + 1.0)
        ys_ref[...] = jnp.dot(act.astype(BF16), wd_s[...], preferred_element_type=F32) + bd_ref[...]

    @pl.when(jnp.logical_not(live))
    def _():
        ys_ref[...] = jnp.zeros(ys_ref.shape, F32)


def _experts(block_e, n_live, xs, layer, w_gu, b_gu, w_down, b_down, *, bm=MOE_BM):
    L, E, D, D2 = w_gu.shape
    n_blocks = xs.shape[0] // bm

    def blk(i, nb):
        return jnp.maximum(jnp.minimum(i, nb[0] - 1), 0)

    grid_spec = pltpu.PrefetchScalarGridSpec(
        num_scalar_prefetch=2,
        grid=(n_blocks,),
        in_specs=[pl.BlockSpec((bm, D), lambda i, be, nb: (blk(i, nb), 0)),
                  pl.BlockSpec((None, None, D, D2), lambda i, be, nb: (layer, be[blk(i, nb)], 0, 0)),
                  pl.BlockSpec((None, None, 1, D2), lambda i, be, nb: (layer, be[blk(i, nb)], 0, 0)),
                  pl.BlockSpec((None, None, D2 // 2, D), lambda i, be, nb: (layer, be[blk(i, nb)], 0, 0)),
                  pl.BlockSpec((None, None, 1, D), lambda i, be, nb: (layer, be[blk(i, nb)], 0, 0))],
        out_specs=pl.BlockSpec((bm, D), lambda i, be, nb: (i, 0)),
        scratch_shapes=[pltpu.VMEM((D, D2), BF16), pltpu.VMEM((D2 // 2, D), BF16)],
    )
    return pl.pallas_call(
        _expert_body,
        grid_spec=grid_spec,
        out_shape=jax.ShapeDtypeStruct(xs.shape, F32),
        compiler_params=_cparams("arbitrary"),
        name="moe_experts",
    )(block_e, n_live, xs, w_gu, b_gu.reshape(L, E, 1, D2), w_down, b_down.reshape(L, E, 1, D))


def _combine_body(segg_ref, segl_ref, segn_ref, ys_ref, slot_ref, gate_ref, h_ref, g_ref, b_ref, o_ref,
                  stage, mix_s, sem, *, tm):
    i = pl.program_id(0)

    @pl.when(i == 0)
    def _():
        stage[...] = jnp.zeros(stage.shape, F32)

    def make_copy(g, l, size):
        return pltpu.make_async_copy(ys_ref.at[pl.ds(g, size), :], stage.at[pl.ds(l, size), :], sem)

    _segment_copies(segg_ref, segl_ref, segn_ref, i, make_copy, True)

    slot = slot_ref[...]
    gate = gate_ref[...]
    for c in range(MOE_STAGE // MOE_PERM_CHUNK):
        c0 = c * MOE_PERM_CHUNK
        cols = lax.broadcasted_iota(I32, (tm, MOE_PERM_CHUNK), 1) + c0
        w = jnp.zeros((tm, MOE_PERM_CHUNK), F32)
        for kk in range(TOP_K):
            w = jnp.where(cols == slot[:, kk:kk + 1], gate[:, kk:kk + 1], w)
        mix_s[:, c0:c0 + MOE_PERM_CHUNK] = w.astype(BF16)

    _segment_copies(segg_ref, segl_ref, segn_ref, i, make_copy, False)
    y = jnp.dot(mix_s[...], stage[...].astype(BF16), preferred_element_type=F32)
    o_ref[...] = _ln(ALPHA * h_ref[...] + y, g_ref[...], b_ref[...])


def _combine(seg_g, seg_l, seg_n, ys, slot, gate, h, g, b, *, tm=MOE_TM):
    T, D = h.shape
    grid_spec = pltpu.PrefetchScalarGridSpec(
        num_scalar_prefetch=3,
        grid=(T // tm,),
        in_specs=[pl.BlockSpec(memory_space=pl.ANY),
                  pl.BlockSpec((tm, TOP_K), lambda i, *_: (i, 0)),
                  pl.BlockSpec((tm, TOP_K), lambda i, *_: (i, 0)),
                  pl.BlockSpec((tm, D), lambda i, *_: (i, 0)),
                  pl.BlockSpec((1, D), lambda i, *_: (0, 0)),
                  pl.BlockSpec((1, D), lambda i, *_: (0, 0))],
        out_specs=pl.BlockSpec((tm, D), lambda i, *_: (i, 0)),
        scratch_shapes=[pltpu.VMEM((MOE_STAGE, D), F32), pltpu.VMEM((tm, MOE_STAGE), BF16),
                        pltpu.SemaphoreType.DMA],
    )
    return pl.pallas_call(
        functools.partial(_combine_body, tm=tm),
        grid_spec=grid_spec,
        out_shape=jax.ShapeDtypeStruct((T, D), F32),
        compiler_params=_cparams("arbitrary"),
        name="moe_combine",
    )(seg_g, seg_l, seg_n, ys, slot, gate, h, g.reshape(1, D), b.reshape(1, D))


def _moe_block(h, layer, w_router, b_router, w_gu, b_gu, w_down, b_down, g, b):
    T = h.shape[0]
    bm, tm, E = MOE_BM, MOE_TM, N_EXPERTS
    nt = T // tm
    slot, gate, cnt, lbase, gbase = _router(h, w_router, b_router)
    cnt, lbase, gbase = cnt[::SUBLANES], lbase[::SUBLANES], gbase[::SUBLANES]
    total = gbase[-1] + cnt[-1]
    padded = (total + bm - 1) // bm * bm
    pend = jnp.cumsum(padded)
    pstart = pend - padded
    seg_g = (pstart[None, :] + gbase).reshape(-1).astype(I32)
    seg_l = lbase.reshape(-1)
    seg_n = cnt.reshape(-1)
    n_blocks = -(-(T * TOP_K + nt * E * (MOE_SEG - 1)) // bm) + E
    block_start = jnp.arange(n_blocks, dtype=I32) * bm
    block_e = jnp.minimum(jnp.sum((pend[None, :] <= block_start[:, None]).astype(I32), axis=1), E - 1)
    n_live = (pend[-1:] // bm).astype(I32)
    last_blk = jnp.where(padded > 0, pend - bm, -1)
    tail_blk = n_live[0] + jnp.arange(n_blocks - T * TOP_K // bm, dtype=I32)
    tail_blk = jnp.where(tail_blk < n_blocks, tail_blk * bm, -1)
    zero_rows = jnp.concatenate([last_blk, tail_blk]).astype(I32)
    xs = _dispatch(seg_g, seg_l, seg_n, zero_rows, slot.T, h, n_blocks * bm)
    ys = _experts(block_e, n_live, xs, layer, w_gu, b_gu, w_down, b_down)
    return _combine(seg_g, seg_l, seg_n, ys, slot, gate, h, g, b)


def kernel(x, rel_bias, a_w_in, a_w_out, b_w_in, b_ln_g, b_ln_b, b_w_s, b_b_s, b_w_out,
           c_w_in, c_w_gk_up, c_b_gk, c_norm_g, c_w_out, ln_g, ln_b,
           moe_w_router, moe_b_router, moe_w_gate_up, moe_b_gate_up, moe_w_down, moe_b_down):
    B, S, D = x.shape
    T = B * S
    h = x.reshape(T, D)
    tables = _bias_tables(rel_bias)
    for i in range(DEPTH):
        j = i // N_MIXERS
        mixer = i % N_MIXERS
        if mixer == 0:
            qkv = _matmul(h, a_w_in[j], tn=1024)
            o = _moba_attention(qkv.reshape(B, S, 3 * D), rel_bias, tables)
            h = _matmul_res_ln(o.reshape(T, D), a_w_out[j], h, ln_g[i, 0], ln_b[i, 0])
        elif mixer == 1:
            z = _matmul(h, b_w_in[j], act="gelu", tn=1024)
            h = _gmlp_gate_out(z, h, b_ln_g[j], b_ln_b[j], b_w_s[j], b_b_s[j], b_w_out[j],
                               ln_g[i, 0], ln_b[i, 0])
        else:
            pad = LANES - C_GATE_RANK
            w_in = jnp.pad(c_w_in[j], ((0, 0), (0, pad)))
            w_up = jnp.pad(c_w_gk_up[j], ((0, pad), (0, 0)))
            proj = _matmul(h, w_in, tn=640)
            o = _gla_core(proj.reshape(B, S, C_IN_WIDTH + pad), w_up, c_b_gk[j], c_norm_g[j])
            h = _matmul_res_ln(o.reshape(T, D), c_w_out[j], h, ln_g[i, 0], ln_b[i, 0])
        h = _moe_block(h, i, moe_w_router[i], moe_b_router[i], moe_w_gate_up, moe_b_gate_up,
                       moe_w_down, moe_b_down, ln_g[i, 1], ln_b[i, 1])
    return h.reshape(B, S, D)
```

```python
import functools
import math

import numpy as np
import jax
import jax.numpy as jnp
from jax import lax
from jax.experimental import pallas as pl
from jax.experimental.pallas import tpu as pltpu

F32 = jnp.float32
BF16 = jnp.bfloat16
I32 = jnp.int32
HI = lax.Precision.HIGHEST

D_MODEL = 1024
DEPTH = 4
N_MIXERS = 3
ALPHA = (2.0 * DEPTH) ** 0.25
LN_EPS = 1e-5

A_HEADS = 8
A_DH = D_MODEL // A_HEADS
MOBA_BLOCK = 256
MOBA_TOPK = 3
REL_BUCKETS = 32
REL_MAX_DIST = 128

B_GROUPS = 8
B_WIDTH = 2 * D_MODEL
B_CHUNK = 128
B_GW = B_WIDTH // B_GROUPS

C_HEADS = 4
C_KEY_DIM = D_MODEL // 2
C_VAL_DIM = D_MODEL
C_DK = C_KEY_DIM // C_HEADS
C_DV = C_VAL_DIM // C_HEADS
C_GATE_RANK = 16
C_GATE_NORMALIZER = 16.0
C_IN_WIDTH = 2 * C_KEY_DIM + 2 * C_VAL_DIM + C_GATE_RANK

N_EXPERTS = 32
TOP_K = 4
SWIGLU_LIMIT = 7.0
SWIGLU_ALPHA = 1.702

LANES = 128
SUBLANES = 8
VMEM_LIMIT = 56 * 1024 * 1024

GLA_CHUNK = 128
GLA_SUB = 16
MOE_BM = 512
MOE_TM = 512
MOE_SEG = SUBLANES
MOE_SEG_BITS = (MOE_TM // MOE_SEG).bit_length()
MOE_STAGE = MOE_TM * TOP_K + N_EXPERTS * MOE_SEG
MOE_PERM_CHUNK = 256


def _cparams(*sem):
    return pltpu.CompilerParams(dimension_semantics=sem, vmem_limit_bytes=VMEM_LIMIT)


def _ln(x, g, b):
    mu = jnp.mean(x, axis=-1, keepdims=True)
    xc = x - mu
    var = jnp.mean(xc * xc, axis=-1, keepdims=True)
    return xc * lax.rsqrt(var + LN_EPS) * g + b


def _mm_body(x_ref, w_ref, o_ref, *, act, tn):
    x = x_ref[...].astype(BF16)
    for n0 in range(0, w_ref.shape[1], tn):
        acc = jnp.dot(x, w_ref[:, n0:n0 + tn], preferred_element_type=F32)
        if act == "gelu":
            acc = 0.5 * acc * (1.0 + lax.erf(acc * (2.0 ** -0.5)))
        o_ref[:, n0:n0 + tn] = acc.astype(o_ref.dtype)


def _matmul(x, w, *, act=None, tm=512, tn=512, out_dtype=F32):
    M, K = x.shape
    N = w.shape[1]
    assert M % tm == 0 and N % tn == 0
    return pl.pallas_call(
        functools.partial(_mm_body, act=act, tn=tn),
        grid=(M // tm,),
        in_specs=[pl.BlockSpec((tm, K), lambda i: (i, 0)),
                  pl.BlockSpec((K, N), lambda i: (0, 0))],
        out_specs=pl.BlockSpec((tm, N), lambda i: (i, 0)),
        out_shape=jax.ShapeDtypeStruct((M, N), out_dtype),
        compiler_params=_cparams("arbitrary"),
        name="proj_matmul",
    )(x, w)


def _mm_res_ln_body(x_ref, w_ref, h_ref, g_ref, b_ref, o_ref, wb_ref):
    @pl.when(pl.program_id(0) == 0)
    def _():
        wb_ref[...] = w_ref[...].astype(BF16)

    t = jnp.dot(x_ref[...].astype(BF16), wb_ref[...], preferred_element_type=F32)
    o_ref[...] = _ln(ALPHA * h_ref[...] + t, g_ref[...], b_ref[...])


def _matmul_res_ln(x, w, h, g, b, *, tm=512):
    M, K = x.shape
    N = w.shape[1]
    return pl.pallas_call(
        _mm_res_ln_body,
        grid=(M // tm,),
        in_specs=[pl.BlockSpec((tm, K), lambda i: (i, 0)),
                  pl.BlockSpec((K, N), lambda i: (0, 0)),
                  pl.BlockSpec((tm, N), lambda i: (i, 0)),
                  pl.BlockSpec((1, N), lambda i: (0, 0)),
                  pl.BlockSpec((1, N), lambda i: (0, 0))],
        out_specs=pl.BlockSpec((tm, N), lambda i: (i, 0)),
        out_shape=jax.ShapeDtypeStruct((M, N), F32),
        scratch_shapes=[pltpu.VMEM((K, N), BF16)],
        compiler_params=_cparams("arbitrary"),
        name="outproj_res_ln",
    )(x, w, h, g.reshape(1, N), b.reshape(1, N))


def _t5_bucket_lower_bounds():
    n = np.arange(0, 4 * REL_MAX_DIST, dtype=np.int64)
    max_exact = REL_BUCKETS // 2
    nf = np.maximum(n, 1).astype(np.float32)
    large = max_exact + (np.log(nf / np.float32(max_exact)) / np.float32(math.log(REL_MAX_DIST / max_exact))
                         * np.float32(REL_BUCKETS - max_exact)).astype(np.int32)
    large = np.minimum(large, REL_BUCKETS - 1)
    bucket = np.where(n < max_exact, n, large)
    assert np.all(np.diff(bucket) >= 0) and bucket[-1] == REL_BUCKETS - 1
    return [int(np.argmax(bucket >= b)) for b in range(REL_BUCKETS)]


_BUCKET_LO = _t5_bucket_lower_bounds()


def _bias_table_body(rel_ref, o_ref):
    blk = MOBA_BLOCK
    row = lax.broadcasted_iota(I32, (blk, blk), 0)
    col = lax.broadcasted_iota(I32, (blk, blk), 1)
    for t in range(2):
        d = row - col + t * blk
        for h in range(A_HEADS):
            val = jnp.full((blk, blk), rel_ref[REL_BUCKETS - 1, h], F32)
            for b in range(REL_BUCKETS - 2, -1, -1):
                val = jnp.where(d < _BUCKET_LO[b + 1], rel_ref[b, h], val)
            if t == 0:
                val = jnp.where(d >= 0, val, -jnp.inf)
            o_ref[h, t] = val


def _bias_tables(rel_bias):
    return pl.pallas_call(
        _bias_table_body,
        in_specs=[pl.BlockSpec(memory_space=pltpu.SMEM)],
        out_shape=jax.ShapeDtypeStruct((A_HEADS, 2, MOBA_BLOCK, MOBA_BLOCK), F32),
        compiler_params=pltpu.CompilerParams(vmem_limit_bytes=VMEM_LIMIT),
        name="rel_bias_tables",
    )(rel_bias)


def _moba_body(rel_ref, q_ref, k_ref, v_ref, tb_ref, o_ref, *, seq):
    blk, dh = MOBA_BLOCK, A_DH
    nb = seq // blk
    h = pl.program_id(1)
    scale = dh ** -0.5
    kmean = jnp.mean(k_ref[0].astype(F32).reshape(nb, blk, dh), axis=1)
    kmean_pad = jnp.concatenate([kmean, jnp.zeros((LANES - nb, dh), F32)], axis=0)
    b_far = rel_ref[REL_BUCKETS - 1, h]
    lane = lax.broadcasted_iota(I32, (blk, LANES), 1)
    for c in range(nb):
        q = q_ref[0, c * blk:(c + 1) * blk, :]
        n_keys = (c + 1) * blk
        s = lax.dot_general(q, k_ref[0, 0:n_keys, :], (((1,), (1,)), ((), ())),
                            preferred_element_type=F32) * scale
        if c > MOBA_TOPK:
            gate = lax.dot_general(q.astype(F32), kmean_pad, (((1,), (1,)), ((), ())),
                                   precision=HI, preferred_element_type=F32)
            gate = jnp.where(lane < c, gate, -jnp.inf)
        pieces = []
        for n in range(c):
            bias = tb_ref[0, 1] if n == c - 1 else b_far
            piece = s[:, n * blk:(n + 1) * blk] + bias
            if c > MOBA_TOPK:
                gn = gate[:, n:n + 1]
                beats = (gate > gn) | ((gate == gn) & (lane < n))
                rank = jnp.sum(beats.astype(F32), axis=1, keepdims=True)
                piece = jnp.where(rank < MOBA_TOPK, piece, -jnp.inf)
            pieces.append(piece)
        pieces.append(s[:, c * blk:(c + 1) * blk] + tb_ref[0, 0])
        logits = jnp.concatenate(pieces, axis=1) if c > 0 else pieces[0]
        m = jnp.max(logits, axis=1, keepdims=True)
        p = jnp.exp(logits - m)
        denom = jnp.sum(p, axis=1, keepdims=True)
        o = jnp.dot(p.astype(BF16), v_ref[0, 0:n_keys, :], preferred_element_type=F32)
        o_ref[0, c * blk:(c + 1) * blk, :] = o / denom


def _moba_attention(qkv, rel_bias, tables):
    B, S, _ = qkv.shape
    H, dh = A_HEADS, A_DH
    assert S % MOBA_BLOCK == 0 and S // MOBA_BLOCK <= SUBLANES
    return pl.pallas_call(
        functools.partial(_moba_body, seq=S),
        grid=(B, H),
        in_specs=[pl.BlockSpec(memory_space=pltpu.SMEM),
                  pl.BlockSpec((1, S, dh), lambda b, h: (b, 0, h)),
                  pl.BlockSpec((1, S, dh), lambda b, h: (b, 0, H + h)),
                  pl.BlockSpec((1, S, dh), lambda b, h: (b, 0, 2 * H + h)),
                  pl.BlockSpec((1, 2, MOBA_BLOCK, MOBA_BLOCK), lambda b, h: (h, 0, 0, 0))],
        out_specs=pl.BlockSpec((1, S, dh), lambda b, h: (b, 0, h)),
        out_shape=jax.ShapeDtypeStruct((B, S, H * dh), F32),
        compiler_params=_cparams("arbitrary", "arbitrary"),
        name="moba_attention",
    )(rel_bias, qkv, qkv, qkv, tables)


def _gmlp_body(u_ref, v_ref, h_ref, vg_ref, vb_ref, ws_ref, bs_ref, wo_ref, g_ref, b_ref, o_ref,
               wst_s, wob_s, *, tm):
    @pl.when(pl.program_id(0) == 0)
    def _():
        row = lax.broadcasted_iota(I32, (B_CHUNK, B_CHUNK), 0)
        col = lax.broadcasted_iota(I32, (B_CHUNK, B_CHUNK), 1)
        for g in range(B_GROUPS):
            wst_s[g] = jnp.where(row >= col, ws_ref[g], 0.0).astype(BF16)
        wob_s[...] = wo_ref[...].astype(BF16)

    vn = _ln(v_ref[...].astype(F32), vg_ref[...], vb_ref[...]).astype(BF16)
    u = u_ref[...].astype(F32)
    rows = []
    for c in range(tm // B_CHUNK):
        r0 = c * B_CHUNK
        cols = []
        for g in range(B_GROUPS):
            c0 = g * B_GW
            mixed = jnp.dot(wst_s[g], vn[r0:r0 + B_CHUNK, c0:c0 + B_GW],
                            preferred_element_type=F32) + bs_ref[g]
            cols.append(u[r0:r0 + B_CHUNK, c0:c0 + B_GW] * mixed)
        rows.append(jnp.concatenate(cols, axis=1))
    y = jnp.concatenate(rows, axis=0).astype(BF16)
    t = jnp.dot(y, wob_s[...], preferred_element_type=F32)
    o_ref[...] = _ln(ALPHA * h_ref[...] + t, g_ref[...], b_ref[...])


def _gmlp_gate_out(z, h, v_g, v_b, w_s, b_s, w_out, g, b, *, tm=256):
    T = z.shape[0]
    D = D_MODEL
    return pl.pallas_call(
        functools.partial(_gmlp_body, tm=tm),
        grid=(T // tm,),
        in_specs=[pl.BlockSpec((tm, B_WIDTH), lambda i: (i, 0)),
                  pl.BlockSpec((tm, B_WIDTH), lambda i: (i, 1)),
                  pl.BlockSpec((tm, D), lambda i: (i, 0)),
                  pl.BlockSpec((1, B_WIDTH), lambda i: (0, 0)),
                  pl.BlockSpec((1, B_WIDTH), lambda i: (0, 0)),
                  pl.BlockSpec((B_GROUPS, B_CHUNK, B_CHUNK), lambda i: (0, 0, 0)),
                  pl.BlockSpec((B_GROUPS, B_CHUNK, 1), lambda i: (0, 0, 0)),
                  pl.BlockSpec((B_WIDTH, D), lambda i: (0, 0)),
                  pl.BlockSpec((1, D), lambda i: (0, 0)),
                  pl.BlockSpec((1, D), lambda i: (0, 0))],
        out_specs=pl.BlockSpec((tm, D), lambda i: (i, 0)),
        out_shape=jax.ShapeDtypeStruct((T, D), F32),
        scratch_shapes=[pltpu.VMEM((B_GROUPS, B_CHUNK, B_CHUNK), BF16), pltpu.VMEM((B_WIDTH, D), BF16)],
        compiler_params=_cparams("arbitrary"),
        name="gmlp_gate_out",
    )(z, z, h, v_g.reshape(1, B_WIDTH), v_b.reshape(1, B_WIDTH), w_s, b_s.reshape(B_GROUPS, B_CHUNK, 1),
      w_out, g.reshape(1, D), b.reshape(1, D))


def _gla_body(q_ref, k_ref, v_ref, g_ref, gl_ref, wup_ref, bgk_ref, ng_ref, o_ref, la_s, st_s, raw_s, *, seq):
    ch, sub, dk, dv = GLA_CHUNK, GLA_SUB, C_DK, C_DV
    x = jnp.dot(gl_ref[0], wup_ref[...], precision=HI, preferred_element_type=F32) + bgk_ref[...]
    la_s[...] = (jnp.minimum(x, 0.0) - jnp.log1p(jnp.exp(-jnp.abs(x)))) * (1.0 / C_GATE_NORMALIZER)
    st_s[...] = jnp.zeros((dk, dv), F32)

    row_c = lax.broadcasted_iota(I32, (ch, ch), 0)
    col_c = lax.broadcasted_iota(I32, (ch, ch), 1)
    tri = (row_c >= col_c).astype(F32)
    eye = (row_c == col_c).astype(F32)
    sub_i = lax.broadcasted_iota(I32, (sub, ch), 0)
    lane_j = lax.broadcasted_iota(I32, (sub, ch), 1)

    def chunk(c, carry):
        s0 = pl.multiple_of(c * ch, ch)
        a = la_s[pl.ds(s0, ch), :]
        b = jnp.dot(tri, a, precision=HI, preferred_element_type=F32)
        q = q_ref[0, pl.ds(s0, ch), :] * (dk ** -0.5)
        k = k_ref[0, pl.ds(s0, ch), :]
        v = v_ref[0, pl.ds(s0, ch), :].astype(BF16)
        st = st_s[...]
        o = jnp.dot((q * jnp.exp(b)).astype(BF16), st.astype(BF16), preferred_element_type=F32)

        blocks = []
        for sb in range(ch // sub):
            r0 = sb * sub
            q_i = q[r0:r0 + sub]
            b_i = b[r0:r0 + sub]
            k_i = k[r0:r0 + sub]
            att = jnp.zeros((sub, ch), F32)
            for j in range(sub):
                e = jnp.exp(jnp.minimum(b_i - b_i[j:j + 1, :], 0.0))
                colv = jnp.sum(q_i * k_i[j:j + 1, :] * e, axis=1, keepdims=True)
                att = jnp.where((lane_j == r0 + j) & (sub_i >= j), colv, att)
            if sb > 0:
                ref_b = b[r0 - 1:r0, :]
                q_t = q_i * jnp.exp(b_i - ref_b)
                k_t = k * jnp.exp(jnp.minimum(ref_b - b, 0.0))
                off = lax.dot_general(q_t, k_t, (((1,), (1,)), ((), ())),
                                      precision=HI, preferred_element_type=F32)
                att = jnp.where(lane_j < r0, off, att)
            blocks.append(att)
        att_full = jnp.concatenate(blocks, axis=0)
        o = o + jnp.dot(att_full.astype(BF16), v, preferred_element_type=F32)
        raw_s[pl.ds(s0, ch), :] = o

        b_last = b[ch - 1:ch, :]
        k_d = k * jnp.exp(b_last - b)
        upd = jnp.dot(k_d.T.astype(BF16), v, preferred_element_type=F32)
        decay_col = lax.dot_general(eye, jnp.broadcast_to(jnp.exp(b_last), (SUBLANES, dk)),
                                    (((1,), (1,)), ((), ())), precision=HI,
                                    preferred_element_type=F32)[:, 0:1]
        st_s[...] = decay_col * st + upd
        return carry

    lax.fori_loop(0, seq // ch, chunk, 0)

    o = raw_s[...]
    rms = o * lax.rsqrt(jnp.mean(o * o, axis=-1, keepdims=True) + LN_EPS) * ng_ref[...]
    gg = g_ref[0]
    o_ref[0] = rms * (gg * jax.nn.sigmoid(gg))


def _gla_core(proj, w_up_pad, b_gk, norm_g):
    B, S, _ = proj.shape
    H, dk, dv = C_HEADS, C_DK, C_DV
    assert dk == LANES and S % GLA_CHUNK == 0
    k_off = C_KEY_DIM // dk
    v_off = 2 * C_KEY_DIM // dv
    g_off = (2 * C_KEY_DIM + C_VAL_DIM) // dv
    gl_off = (2 * C_KEY_DIM + 2 * C_VAL_DIM) // LANES
    return pl.pallas_call(
        functools.partial(_gla_body, seq=S),
        grid=(B, H),
        in_specs=[pl.BlockSpec((1, S, dk), lambda b, h: (b, 0, h)),
                  pl.BlockSpec((1, S, dk), lambda b, h: (b, 0, k_off + h)),
                  pl.BlockSpec((1, S, dv), lambda b, h: (b, 0, v_off + h)),
                  pl.BlockSpec((1, S, dv), lambda b, h: (b, 0, g_off + h)),
                  pl.BlockSpec((1, S, LANES), lambda b, h: (b, 0, gl_off)),
                  pl.BlockSpec((LANES, dk), lambda b, h: (0, h)),
                  pl.BlockSpec((1, dk), lambda b, h: (0, h)),
                  pl.BlockSpec((1, dv), lambda b, h: (0, 0))],
        out_specs=pl.BlockSpec((1, S, dv), lambda b, h: (b, 0, h)),
        out_shape=jax.ShapeDtypeStruct((B, S, H * dv), F32),
        scratch_shapes=[pltpu.VMEM((S, dk), F32), pltpu.VMEM((dk, dv), F32), pltpu.VMEM((S, dv), F32)],
        compiler_params=_cparams("arbitrary", "arbitrary"),
        name="gla_core",
    )(proj, proj, proj, proj, proj, w_up_pad, b_gk.reshape(1, C_KEY_DIM), norm_g.reshape(1, dv))


def _router_body(h_ref, w_ref, b_ref, slot_ref, gate_ref, cnt_ref, lbase_ref, gbase_ref, run_s, w1_s, w2_s, *, tm):
    E = N_EXPERTS

    @pl.when(pl.program_id(0) == 0)
    def _():
        run_s[...] = jnp.zeros((1, E), F32)
        w = w_ref[...]
        w1 = w.astype(BF16)
        w1_s[...] = w1
        w2_s[...] = (w - w1.astype(F32)).astype(BF16)

    h = h_ref[...]
    h1 = h.astype(BF16)
    h2 = (h - h1.astype(F32)).astype(BF16)
    logits = (jnp.dot(h1, w1_s[...], preferred_element_type=F32)
              + (jnp.dot(h1, w2_s[...], preferred_element_type=F32)
                 + jnp.dot(h2, w1_s[...], preferred_element_type=F32))) + b_ref[...]
    lane = lax.broadcasted_iota(I32, (tm, E), 1)
    rest = logits
    top_val, onehot = [], []
    for _ in range(TOP_K):
        m = jnp.max(rest, axis=1, keepdims=True)
        idx = jnp.min(jnp.where(rest == m, lane, E), axis=1, keepdims=True)
        oh = lane == idx
        rest = jnp.where(oh, -jnp.inf, rest)
        top_val.append(m)
        onehot.append(oh)
    ex = [jnp.exp(v - top_val[0]) for v in top_val]
    den = ex[0] + ex[1] + ex[2] + ex[3]
    chosen = sum(oh.astype(F32) for oh in onehot)

    row = lax.broadcasted_iota(I32, (tm, tm), 0)
    col = lax.broadcasted_iota(I32, (tm, tm), 1)
    before = (row > col).astype(BF16)
    seen = jnp.dot(before, chosen.astype(BF16), preferred_element_type=F32)
    cnt = jnp.sum(chosen, axis=0, keepdims=True)
    cnt_pad = jnp.floor((cnt + (MOE_SEG - 1)) * (1.0 / MOE_SEG)) * MOE_SEG
    er = lax.broadcasted_iota(I32, (E, E), 0)
    ec = lax.broadcasted_iota(I32, (E, E), 1)
    cnt_rows = jnp.broadcast_to(cnt_pad, (SUBLANES, E))
    lbase_rows = jnp.dot(cnt_rows, (er < ec).astype(F32), precision=HI, preferred_element_type=F32)
    pos = seen + lbase_rows[0:1, :]

    lane_k = lax.broadcasted_iota(I32, (tm, TOP_K), 1)
    s_out = jnp.zeros((tm, TOP_K), I32)
    g_out = jnp.zeros((tm, TOP_K), F32)
    for kk in range(TOP_K):
        sk = jnp.sum(jnp.where(onehot[kk], pos, 0.0), axis=1, keepdims=True)
        s_out = jnp.where(lane_k == kk, sk.astype(I32), s_out)
        g_out = jnp.where(lane_k == kk, ex[kk] / den, g_out)
    slot_ref[...] = s_out
    gate_ref[...] = g_out
    cnt_ref[...] = cnt_rows.astype(I32)
    lbase_ref[...] = lbase_rows.astype(I32)
    gbase_ref[...] = jnp.broadcast_to(run_s[...], (SUBLANES, E)).astype(I32)
    run_s[...] = run_s[...] + cnt_pad


def _router(h, w, b, *, tm=MOE_TM):
    T, D = h.shape
    E = N_EXPERTS
    nt = T // tm
    tab = jax.ShapeDtypeStruct((nt * SUBLANES, E), I32)
    tab_spec = pl.BlockSpec((SUBLANES, E), lambda i: (i, 0))
    return pl.pallas_call(
        functools.partial(_router_body, tm=tm),
        grid=(nt,),
        in_specs=[pl.BlockSpec((tm, D), lambda i: (i, 0)),
                  pl.BlockSpec((D, E), lambda i: (0, 0)),
                  pl.BlockSpec((1, E), lambda i: (0, 0))],
        out_specs=[pl.BlockSpec((tm, TOP_K), lambda i: (i, 0)),
                   pl.BlockSpec((tm, TOP_K), lambda i: (i, 0)),
                   tab_spec, tab_spec, tab_spec],
        out_shape=[jax.ShapeDtypeStruct((T, TOP_K), I32),
                   jax.ShapeDtypeStruct((T, TOP_K), F32),
                   tab, tab, tab],
        scratch_shapes=[pltpu.VMEM((1, E), F32), pltpu.VMEM((D, E), BF16), pltpu.VMEM((D, E), BF16)],
        compiler_params=_cparams("arbitrary"),
        name="moe_router",
    )(h, w, b.reshape(1, E))


def _segment_copies(seg_g, seg_l, seg_n, tile, make_copy, start):
    def per_expert(e, carry):
        idx = tile * N_EXPERTS + e
        n = seg_n[idx]
        g = seg_g[idx]
        l = seg_l[idx]
        for bit in range(MOE_SEG_BITS - 1, -1, -1):
            size = MOE_SEG << bit

            @pl.when((n & size) != 0)
            def _():
                off = n & ~(2 * size - 1)
                cp = make_copy(pl.multiple_of(g + off, MOE_SEG), pl.multiple_of(l + off, MOE_SEG), size)
                if start:
                    cp.start()
                else:
                    cp.wait()
        return carry

    lax.fori_loop(0, N_EXPERTS, per_expert, 0)


def _dispatch_body(segg_ref, segl_ref, segn_ref, zf_ref, slot_t_ref, h_ref, xs_ref, stage, zbuf, sem, *, tm, bm):
    i = pl.program_id(0)

    @pl.when(i == 0)
    def _():
        zbuf[...] = jnp.zeros(zbuf.shape, F32)

        def zero_copy(j):
            return pltpu.make_async_copy(zbuf, xs_ref.at[pl.ds(pl.multiple_of(zf_ref[j], MOE_SEG), bm), :], sem)

        def start(j, carry):
            @pl.when(zf_ref[j] >= 0)
            def _():
                zero_copy(j).start()
            return carry

        def wait(j, carry):
            @pl.when(zf_ref[j] >= 0)
            def _():
                zero_copy(j).wait()
            return carry

        lax.fori_loop(0, zf_ref.shape[0], start, 0)
        lax.fori_loop(0, zf_ref.shape[0], wait, 0)

    hb = h_ref[...].astype(BF16)
    for c in range(MOE_STAGE // MOE_PERM_CHUNK):
        r0 = c * MOE_PERM_CHUNK
        rows = lax.broadcasted_iota(I32, (MOE_PERM_CHUNK, tm), 0) + r0
        hit = rows == slot_t_ref[0:1, :]
        for kk in range(1, TOP_K):
            hit = hit | (rows == slot_t_ref[kk:kk + 1, :])
        perm = jnp.where(hit, 1.0, 0.0).astype(BF16)
        stage[r0:r0 + MOE_PERM_CHUNK, :] = jnp.dot(perm, hb, preferred_element_type=F32)

    def make_copy(g, l, size):
        return pltpu.make_async_copy(stage.at[pl.ds(l, size), :], xs_ref.at[pl.ds(g, size), :], sem)

    _segment_copies(segg_ref, segl_ref, segn_ref, i, make_copy, True)
    _segment_copies(segg_ref, segl_ref, segn_ref, i, make_copy, False)


def _dispatch(seg_g, seg_l, seg_n, zero_rows, slot_t, h, n_rows, *, tm=MOE_TM, bm=MOE_BM):
    T, D = h.shape
    grid_spec = pltpu.PrefetchScalarGridSpec(
        num_scalar_prefetch=4,
        grid=(T // tm,),
        in_specs=[pl.BlockSpec((TOP_K, tm), lambda i, *_: (0, i)),
                  pl.BlockSpec((tm, D), lambda i, *_: (i, 0))],
        out_specs=pl.BlockSpec(memory_space=pl.ANY),
        scratch_shapes=[pltpu.VMEM((MOE_STAGE, D), F32), pltpu.VMEM((bm, D), F32), pltpu.SemaphoreType.DMA],
    )
    return pl.pallas_call(
        functools.partial(_dispatch_body, tm=tm, bm=bm),
        grid_spec=grid_spec,
        out_shape=jax.ShapeDtypeStruct((n_rows, D), F32),
        compiler_params=pltpu.CompilerParams(dimension_semantics=("arbitrary",), vmem_limit_bytes=VMEM_LIMIT,
                                             has_side_effects=True),
        name="moe_dispatch",
    )(seg_g, seg_l, seg_n, zero_rows, slot_t, h)


def _expert_body(be_ref, nb_ref, nv_ref, xs_ref, wgu_ref, bgu_ref, wd_ref, bd_ref, ys_ref, wgu_s, wd_s, *, bm):
    i = pl.program_id(0)
    prev = be_ref[jnp.maximum(i - 1, 0)]
    live = i < nb_ref[0]
    half = bm // 2
    full = nv_ref[i] > half

    @pl.when(live & ((i == 0) | (be_ref[i] != prev)))
    def _():
        wgu_s[...] = wgu_ref[...].astype(BF16)
        wd_s[...] = wd_ref[...].astype(BF16)

    def mlp(r0, rows):
        hh = jnp.dot(xs_ref[r0:r0 + rows, :].astype(BF16), wgu_s[...], preferred_element_type=F32) + bgu_ref[...]
        d_e = hh.shape[1] // 2
        h_gate = jnp.minimum(hh[:, :d_e], SWIGLU_LIMIT)
        h_up = jnp.clip(hh[:, d_e:], -SWIGLU_LIMIT, SWIGLU_LIMIT)
        act = h_gate * jax.nn.sigmoid(SWIGLU_ALPHA * h_gate) * (h_up + 1.0)
        ys_ref[r0:r0 + rows, :] = jnp.dot(act.astype(BF16), wd_s[...], preferred_element_type=F32) + bd_ref[...]

    @pl.when(live & full)
    def _():
        mlp(0, bm)

    @pl.when(live & jnp.logical_not(full))
    def _():
        mlp(0, half)
        ys_ref[half:bm, :] = jnp.zeros((bm - half, ys_ref.shape[1]), F32)

    @pl.when(jnp.logical_not(live))
    def _():
        ys_ref[...] = jnp.zeros(ys_ref.shape, F32)


def _experts(block_e, n_live, n_valid, xs, layer, w_gu, b_gu, w_down, b_down, *, bm=MOE_BM):
    L, E, D, D2 = w_gu.shape
    n_blocks = xs.shape[0] // bm

    def blk(i, nb):
        return jnp.maximum(jnp.minimum(i, nb[0] - 1), 0)

    grid_spec = pltpu.PrefetchScalarGridSpec(
        num_scalar_prefetch=3,
        grid=(n_blocks,),
        in_specs=[pl.BlockSpec((bm, D), lambda i, be, nb, nv: (blk(i, nb), 0)),
                  pl.BlockSpec((None, None, D, D2), lambda i, be, nb, nv: (layer, be[blk(i, nb)], 0, 0)),
                  pl.BlockSpec((None, None, 1, D2), lambda i, be, nb, nv: (layer, be[blk(i, nb)], 0, 0)),
                  pl.BlockSpec((None, None, D2 // 2, D), lambda i, be, nb, nv: (layer, be[blk(i, nb)], 0, 0)),
                  pl.BlockSpec((None, None, 1, D), lambda i, be, nb, nv: (layer, be[blk(i, nb)], 0, 0))],
        out_specs=pl.BlockSpec((bm, D), lambda i, be, nb, nv: (i, 0)),
        scratch_shapes=[pltpu.VMEM((D, D2), BF16), pltpu.VMEM((D2 // 2, D), BF16)],
    )
    return pl.pallas_call(
        functools.partial(_expert_body, bm=bm),
        grid_spec=grid_spec,
        out_shape=jax.ShapeDtypeStruct(xs.shape, F32),
        compiler_params=_cparams("arbitrary"),
        name="moe_experts",
    )(block_e, n_live, n_valid, xs, w_gu, b_gu.reshape(L, E, 1, D2), w_down, b_down.reshape(L, E, 1, D))


def _combine_body(segg_ref, segl_ref, segn_ref, ys_ref, slot_ref, gate_ref, h_ref, g_ref, b_ref, o_ref,
                  stage, mix_s, sem, *, tm):
    i = pl.program_id(0)

    @pl.when(i == 0)
    def _():
        stage[...] = jnp.zeros(stage.shape, F32)

    def make_copy(g, l, size):
        return pltpu.make_async_copy(ys_ref.at[pl.ds(g, size), :], stage.at[pl.ds(l, size), :], sem)

    _segment_copies(segg_ref, segl_ref, segn_ref, i, make_copy, True)

    slot = slot_ref[...]
    gate = gate_ref[...]
    for c in range(MOE_STAGE // MOE_PERM_CHUNK):
        c0 = c * MOE_PERM_CHUNK
        cols = lax.broadcasted_iota(I32, (tm, MOE_PERM_CHUNK), 1) + c0
        w = jnp.zeros((tm, MOE_PERM_CHUNK), F32)
        for kk in range(TOP_K):
            w = jnp.where(cols == slot[:, kk:kk + 1], gate[:, kk:kk + 1], w)
        mix_s[:, c0:c0 + MOE_PERM_CHUNK] = w.astype(BF16)

    _segment_copies(segg_ref, segl_ref, segn_ref, i, make_copy, False)
    y = jnp.dot(mix_s[...], stage[...].astype(BF16), preferred_element_type=F32)
    o_ref[...] = _ln(ALPHA * h_ref[...] + y, g_ref[...], b_ref[...])


def _combine(seg_g, seg_l, seg_n, ys, slot, gate, h, g, b, *, tm=MOE_TM):
    T, D = h.shape
    grid_spec = pltpu.PrefetchScalarGridSpec(
        num_scalar_prefetch=3,
        grid=(T // tm,),
        in_specs=[pl.BlockSpec(memory_space=pl.ANY),
                  pl.BlockSpec((tm, TOP_K), lambda i, *_: (i, 0)),
                  pl.BlockSpec((tm, TOP_K), lambda i, *_: (i, 0)),
                  pl.BlockSpec((tm, D), lambda i, *_: (i, 0)),
                  pl.BlockSpec((1, D), lambda i, *_: (0, 0)),
                  pl.BlockSpec((1, D), lambda i, *_: (0, 0))],
        out_specs=pl.BlockSpec((tm, D), lambda i, *_: (i, 0)),
        scratch_shapes=[pltpu.VMEM((MOE_STAGE, D), F32), pltpu.VMEM((tm, MOE_STAGE), BF16),
                        pltpu.SemaphoreType.DMA],
    )
    return pl.pallas_call(
        functools.partial(_combine_body, tm=tm),
        grid_spec=grid_spec,
        out_shape=jax.ShapeDtypeStruct((T, D), F32),
        compiler_params=_cparams("arbitrary"),
        name="moe_combine",
    )(seg_g, seg_l, seg_n, ys, slot, gate, h, g.reshape(1, D), b.reshape(1, D))


def _moe_block(h, layer, w_router, b_router, w_gu, b_gu, w_down, b_down, g, b):
    T = h.shape[0]
    bm, tm, E = MOE_BM, MOE_TM, N_EXPERTS
    nt = T // tm
    slot, gate, cnt, lbase, gbase = _router(h, w_router, b_router)
    cnt, lbase, gbase = cnt[::SUBLANES], lbase[::SUBLANES], gbase[::SUBLANES]
    total = gbase[-1] + cnt[-1]
    padded = (total + bm - 1) // bm * bm
    pend = jnp.cumsum(padded)
    pstart = pend - padded
    seg_g = (pstart[None, :] + gbase).reshape(-1).astype(I32)
    seg_l = lbase.reshape(-1)
    seg_n = cnt.reshape(-1)
    n_blocks = -(-(T * TOP_K + nt * E * (MOE_SEG - 1)) // bm) + E
    block_start = jnp.arange(n_blocks, dtype=I32) * bm
    block_e = jnp.minimum(jnp.sum((pend[None, :] <= block_start[:, None]).astype(I32), axis=1), E - 1)
    n_live = (pend[-1:] // bm).astype(I32)
    n_valid = jnp.clip((pstart + total)[block_e] - block_start, 0, bm).astype(I32)
    last_blk = jnp.where(padded > 0, pend - bm, -1)
    tail_blk = n_live[0] + jnp.arange(n_blocks - T * TOP_K // bm, dtype=I32)
    tail_blk = jnp.where(tail_blk < n_blocks, tail_blk * bm, -1)
    zero_rows = jnp.concatenate([last_blk, tail_blk]).astype(I32)
    xs = _dispatch(seg_g, seg_l, seg_n, zero_rows, slot.T, h, n_blocks * bm)
    ys = _experts(block_e, n_live, n_valid, xs, layer, w_gu, b_gu, w_down, b_down)
    return _combine(seg_g, seg_l, seg_n, ys, slot, gate, h, g, b)


def kernel(x, rel_bias, a_w_in, a_w_out, b_w_in, b_ln_g, b_ln_b, b_w_s, b_b_s, b_w_out,
           c_w_in, c_w_gk_up, c_b_gk, c_norm_g, c_w_out, ln_g, ln_b,
           moe_w_router, moe_b_router, moe_w_gate_up, moe_b_gate_up, moe_w_down, moe_b_down):
    B, S, D = x.shape
    T = B * S
    h = x.reshape(T, D)
    tables = _bias_tables(rel_bias)
    for i in range(DEPTH):
        j = i // N_MIXERS
        mixer = i % N_MIXERS
        if mixer == 0:
            qkv = _matmul(h, a_w_in[j].astype(BF16), out_dtype=BF16)
            o = _moba_attention(qkv.reshape(B, S, 3 * D), rel_bias, tables)
            h = _matmul_res_ln(o.reshape(T, D), a_w_out[j], h, ln_g[i, 0], ln_b[i, 0])
        elif mixer == 1:
            z = _matmul(h, b_w_in[j].astype(BF16), act="gelu", out_dtype=BF16)
            h = _gmlp_gate_out(z, h, b_ln_g[j], b_ln_b[j], b_w_s[j], b_b_s[j], b_w_out[j],
                               ln_g[i, 0], ln_b[i, 0])
        else:
            pad = LANES - C_GATE_RANK
            w_in = jnp.pad(c_w_in[j], ((0, 0), (0, pad)))
            w_up = jnp.pad(c_w_gk_up[j], ((0, pad), (0, 0)))
            proj = _matmul(h, w_in.astype(BF16), tn=640)
            o = _gla_core(proj.reshape(B, S, C_IN_WIDTH + pad), w_up, c_b_gk[j], c_norm_g[j])
            h = _matmul_res_ln(o.reshape(T, D), c_w_out[j], h, ln_g[i, 0], ln_b[i, 0])
        h = _moe_block(h, i, moe_w_router[i], moe_b_router[i], moe_w_gate_up, moe_b_gate_up,
                       moe_w_down, moe_b_down, ln_g[i, 1], ln_b[i, 1])
    return h.reshape(B, S, D)
```

```python
import functools
import math

import numpy as np
import jax
import jax.numpy as jnp
from jax import lax
from jax.experimental import pallas as pl
from jax.experimental.pallas import tpu as pltpu

F32 = jnp.float32
BF16 = jnp.bfloat16
I32 = jnp.int32
HI = lax.Precision.HIGHEST

D_MODEL = 1024
DEPTH = 4
N_MIXERS = 3
ALPHA = (2.0 * DEPTH) ** 0.25
LN_EPS = 1e-5

A_HEADS = 8
A_DH = D_MODEL // A_HEADS
MOBA_BLOCK = 256
MOBA_TOPK = 3
REL_BUCKETS = 32
REL_MAX_DIST = 128

B_GROUPS = 8
B_WIDTH = 2 * D_MODEL
B_CHUNK = 128
B_GW = B_WIDTH // B_GROUPS

C_HEADS = 4
C_KEY_DIM = D_MODEL // 2
C_VAL_DIM = D_MODEL
C_DK = C_KEY_DIM // C_HEADS
C_DV = C_VAL_DIM // C_HEADS
C_GATE_RANK = 16
C_GATE_NORMALIZER = 16.0
C_IN_WIDTH = 2 * C_KEY_DIM + 2 * C_VAL_DIM + C_GATE_RANK

N_EXPERTS = 32
TOP_K = 4
SWIGLU_LIMIT = 7.0
SWIGLU_ALPHA = 1.702

LANES = 128
SUBLANES = 8
VMEM_LIMIT = 56 * 1024 * 1024

GLA_CHUNK = 128
GLA_SUB = 16
MOE_BM = 512
MOE_TM = 512
MOE_SEG = SUBLANES
MOE_SEG_BITS = (MOE_TM // MOE_SEG).bit_length()
MOE_STAGE = MOE_TM * TOP_K + N_EXPERTS * MOE_SEG
MOE_PERM_CHUNK = 256


def _cparams(*sem):
    return pltpu.CompilerParams(dimension_semantics=sem, vmem_limit_bytes=VMEM_LIMIT)


def _ln(x, g, b):
    mu = jnp.mean(x, axis=-1, keepdims=True)
    xc = x - mu
    var = jnp.mean(xc * xc, axis=-1, keepdims=True)
    return xc * lax.rsqrt(var + LN_EPS) * g + b


def _mm_body(x_ref, w_ref, o_ref, *, act, tn):
    x = x_ref[...].astype(BF16)
    for n0 in range(0, w_ref.shape[1], tn):
        acc = jnp.dot(x, w_ref[:, n0:n0 + tn], preferred_element_type=F32)
        if act == "gelu":
            acc = 0.5 * acc * (1.0 + lax.erf(acc * (2.0 ** -0.5)))
        o_ref[:, n0:n0 + tn] = acc.astype(o_ref.dtype)


def _matmul(x, w, *, act=None, tm=512, tn=512, out_dtype=F32):
    M, K = x.shape
    N = w.shape[1]
    assert M % tm == 0 and N % tn == 0
    return pl.pallas_call(
        functools.partial(_mm_body, act=act, tn=tn),
        grid=(M // tm,),
        in_specs=[pl.BlockSpec((tm, K), lambda i: (i, 0)),
                  pl.BlockSpec((K, N), lambda i: (0, 0))],
        out_specs=pl.BlockSpec((tm, N), lambda i: (i, 0)),
        out_shape=jax.ShapeDtypeStruct((M, N), out_dtype),
        compiler_params=_cparams("arbitrary"),
        name="proj_matmul",
    )(x, w)


def _mm_res_ln_body(x_ref, w_ref, h_ref, g_ref, b_ref, o_ref, wb_ref):
    @pl.when(pl.program_id(0) == 0)
    def _():
        wb_ref[...] = w_ref[...].astype(BF16)

    t = jnp.dot(x_ref[...].astype(BF16), wb_ref[...], preferred_element_type=F32)
    o_ref[...] = _ln(ALPHA * h_ref[...] + t, g_ref[...], b_ref[...])


def _matmul_res_ln(x, w, h, g, b, *, tm=512):
    M, K = x.shape
    N = w.shape[1]
    return pl.pallas_call(
        _mm_res_ln_body,
        grid=(M // tm,),
        in_specs=[pl.BlockSpec((tm, K), lambda i: (i, 0)),
                  pl.BlockSpec((K, N), lambda i: (0, 0)),
                  pl.BlockSpec((tm, N), lambda i: (i, 0)),
                  pl.BlockSpec((1, N), lambda i: (0, 0)),
                  pl.BlockSpec((1, N), lambda i: (0, 0))],
        out_specs=pl.BlockSpec((tm, N), lambda i: (i, 0)),
        out_shape=jax.ShapeDtypeStruct((M, N), F32),
        scratch_shapes=[pltpu.VMEM((K, N), BF16)],
        compiler_params=_cparams("arbitrary"),
        name="outproj_res_ln",
    )(x, w, h, g.reshape(1, N), b.reshape(1, N))


def _t5_bucket_lower_bounds():
    n = np.arange(0, 4 * REL_MAX_DIST, dtype=np.int64)
    max_exact = REL_BUCKETS // 2
    nf = np.maximum(n, 1).astype(np.float32)
    large = max_exact + (np.log(nf / np.float32(max_exact)) / np.float32(math.log(REL_MAX_DIST / max_exact))
                         * np.float32(REL_BUCKETS - max_exact)).astype(np.int32)
    large = np.minimum(large, REL_BUCKETS - 1)
    bucket = np.where(n < max_exact, n, large)
    assert np.all(np.diff(bucket) >= 0) and bucket[-1] == REL_BUCKETS - 1
    return [int(np.argmax(bucket >= b)) for b in range(REL_BUCKETS)]


_BUCKET_LO = _t5_bucket_lower_bounds()


def _bias_table_body(rel_ref, o_ref):
    blk = MOBA_BLOCK
    row = lax.broadcasted_iota(I32, (blk, blk), 0)
    col = lax.broadcasted_iota(I32, (blk, blk), 1)
    for t in range(2):
        d = row - col + t * blk
        for h in range(A_HEADS):
            val = jnp.full((blk, blk), rel_ref[REL_BUCKETS - 1, h], F32)
            for b in range(REL_BUCKETS - 2, -1, -1):
                val = jnp.where(d < _BUCKET_LO[b + 1], rel_ref[b, h], val)
            if t == 0:
                val = jnp.where(d >= 0, val, -jnp.inf)
            o_ref[h, t] = val


def _bias_tables(rel_bias):
    return pl.pallas_call(
        _bias_table_body,
        in_specs=[pl.BlockSpec(memory_space=pltpu.SMEM)],
        out_shape=jax.ShapeDtypeStruct((A_HEADS, 2, MOBA_BLOCK, MOBA_BLOCK), F32),
        compiler_params=pltpu.CompilerParams(vmem_limit_bytes=VMEM_LIMIT),
        name="rel_bias_tables",
    )(rel_bias)


def _moba_body(rel_ref, q_ref, k_ref, v_ref, tb_ref, o_ref, *, seq):
    blk, dh = MOBA_BLOCK, A_DH
    nb = seq // blk
    h = pl.program_id(1)
    scale = dh ** -0.5
    kmean = jnp.mean(k_ref[0].astype(F32).reshape(nb, blk, dh), axis=1)
    kmean_pad = jnp.concatenate([kmean, jnp.zeros((LANES - nb, dh), F32)], axis=0)
    b_far = rel_ref[REL_BUCKETS - 1, h]
    lane = lax.broadcasted_iota(I32, (blk, LANES), 1)
    for c in range(nb):
        q = q_ref[0, c * blk:(c + 1) * blk, :]
        n_keys = (c + 1) * blk
        s = lax.dot_general(q, k_ref[0, 0:n_keys, :], (((1,), (1,)), ((), ())),
                            preferred_element_type=F32) * scale
        if c > MOBA_TOPK:
            gate = lax.dot_general(q.astype(F32), kmean_pad, (((1,), (1,)), ((), ())),
                                   precision=HI, preferred_element_type=F32)
            gate = jnp.where(lane < c, gate, -jnp.inf)
        pieces = []
        for n in range(c):
            bias = tb_ref[0, 1] if n == c - 1 else b_far
            piece = s[:, n * blk:(n + 1) * blk] + bias
            if c > MOBA_TOPK:
                gn = gate[:, n:n + 1]
                beats = (gate > gn) | ((gate == gn) & (lane < n))
                rank = jnp.sum(beats.astype(F32), axis=1, keepdims=True)
                piece = jnp.where(rank < MOBA_TOPK, piece, -jnp.inf)
            pieces.append(piece)
        pieces.append(s[:, c * blk:(c + 1) * blk] + tb_ref[0, 0])
        logits = jnp.concatenate(pieces, axis=1) if c > 0 else pieces[0]
        m = jnp.max(logits, axis=1, keepdims=True)
        p = jnp.exp(logits - m)
        denom = jnp.sum(p, axis=1, keepdims=True)
        o = jnp.dot(p.astype(BF16), v_ref[0, 0:n_keys, :], preferred_element_type=F32)
        o_ref[0, c * blk:(c + 1) * blk, :] = o / denom


def _moba_attention(qkv, rel_bias, tables):
    B, S, _ = qkv.shape
    H, dh = A_HEADS, A_DH
    assert S % MOBA_BLOCK == 0 and S // MOBA_BLOCK <= SUBLANES
    return pl.pallas_call(
        functools.partial(_moba_body, seq=S),
        grid=(B, H),
        in_specs=[pl.BlockSpec(memory_space=pltpu.SMEM),
                  pl.BlockSpec((1, S, dh), lambda b, h: (b, 0, h)),
                  pl.BlockSpec((1, S, dh), lambda b, h: (b, 0, H + h)),
                  pl.BlockSpec((1, S, dh), lambda b, h: (b, 0, 2 * H + h)),
                  pl.BlockSpec((1, 2, MOBA_BLOCK, MOBA_BLOCK), lambda b, h: (h, 0, 0, 0))],
        out_specs=pl.BlockSpec((1, S, dh), lambda b, h: (b, 0, h)),
        out_shape=jax.ShapeDtypeStruct((B, S, H * dh), F32),
        compiler_params=_cparams("arbitrary", "arbitrary"),
        name="moba_attention",
    )(rel_bias, qkv, qkv, qkv, tables)


def _gmlp_body(u_ref, v_ref, h_ref, vg_ref, vb_ref, ws_ref, bs_ref, wo_ref, g_ref, b_ref, o_ref,
               wst_s, wob_s, *, tm):
    @pl.when(pl.program_id(0) == 0)
    def _():
        row = lax.broadcasted_iota(I32, (B_CHUNK, B_CHUNK), 0)
        col = lax.broadcasted_iota(I32, (B_CHUNK, B_CHUNK), 1)
        for g in range(B_GROUPS):
            wst_s[g] = jnp.where(row >= col, ws_ref[g], 0.0).astype(BF16)
        wob_s[...] = wo_ref[...].astype(BF16)

    vn = _ln(v_ref[...].astype(F32), vg_ref[...], vb_ref[...]).astype(BF16)
    u = u_ref[...].astype(F32)
    rows = []
    for c in range(tm // B_CHUNK):
        r0 = c * B_CHUNK
        cols = []
        for g in range(B_GROUPS):
            c0 = g * B_GW
            mixed = jnp.dot(wst_s[g], vn[r0:r0 + B_CHUNK, c0:c0 + B_GW],
                            preferred_element_type=F32) + bs_ref[g]
            cols.append(u[r0:r0 + B_CHUNK, c0:c0 + B_GW] * mixed)
        rows.append(jnp.concatenate(cols, axis=1))
    y = jnp.concatenate(rows, axis=0).astype(BF16)
    t = jnp.dot(y, wob_s[...], preferred_element_type=F32)
    o_ref[...] = _ln(ALPHA * h_ref[...] + t, g_ref[...], b_ref[...])


def _gmlp_gate_out(z, h, v_g, v_b, w_s, b_s, w_out, g, b, *, tm=256):
    T = z.shape[0]
    D = D_MODEL
    return pl.pallas_call(
        functools.partial(_gmlp_body, tm=tm),
        grid=(T // tm,),
        in_specs=[pl.BlockSpec((tm, B_WIDTH), lambda i: (i, 0)),
                  pl.BlockSpec((tm, B_WIDTH), lambda i: (i, 1)),
                  pl.BlockSpec((tm, D), lambda i: (i, 0)),
                  pl.BlockSpec((1, B_WIDTH), lambda i: (0, 0)),
                  pl.BlockSpec((1, B_WIDTH), lambda i: (0, 0)),
                  pl.BlockSpec((B_GROUPS, B_CHUNK, B_CHUNK), lambda i: (0, 0, 0)),
                  pl.BlockSpec((B_GROUPS, B_CHUNK, 1), lambda i: (0, 0, 0)),
                  pl.BlockSpec((B_WIDTH, D), lambda i: (0, 0)),
                  pl.BlockSpec((1, D), lambda i: (0, 0)),
                  pl.BlockSpec((1, D), lambda i: (0, 0))],
        out_specs=pl.BlockSpec((tm, D), lambda i: (i, 0)),
        out_shape=jax.ShapeDtypeStruct((T, D), F32),
        scratch_shapes=[pltpu.VMEM((B_GROUPS, B_CHUNK, B_CHUNK), BF16), pltpu.VMEM((B_WIDTH, D), BF16)],
        compiler_params=_cparams("arbitrary"),
        name="gmlp_gate_out",
    )(z, z, h, v_g.reshape(1, B_WIDTH), v_b.reshape(1, B_WIDTH), w_s, b_s.reshape(B_GROUPS, B_CHUNK, 1),
      w_out, g.reshape(1, D), b.reshape(1, D))


def _gla_body(q_ref, k_ref, v_ref, g_ref, gl_ref, wup_ref, bgk_ref, ng_ref, o_ref, la_s, st_s, raw_s, *, seq):
    ch, sub, dk, dv = GLA_CHUNK, GLA_SUB, C_DK, C_DV
    x = jnp.dot(gl_ref[0], wup_ref[...], precision=HI, preferred_element_type=F32) + bgk_ref[...]
    la_s[...] = (jnp.minimum(x, 0.0) - jnp.log1p(jnp.exp(-jnp.abs(x)))) * (1.0 / C_GATE_NORMALIZER)
    st_s[...] = jnp.zeros((dk, dv), F32)

    row_c = lax.broadcasted_iota(I32, (ch, ch), 0)
    col_c = lax.broadcasted_iota(I32, (ch, ch), 1)
    tri = (row_c >= col_c).astype(F32)
    eye = (row_c == col_c).astype(F32)
    sub_i = lax.broadcasted_iota(I32, (sub, ch), 0)
    lane_j = lax.broadcasted_iota(I32, (sub, ch), 1)

    def chunk(c, carry):
        s0 = pl.multiple_of(c * ch, ch)
        a = la_s[pl.ds(s0, ch), :]
        b = jnp.dot(tri, a, precision=HI, preferred_element_type=F32)
        q = q_ref[0, pl.ds(s0, ch), :] * (dk ** -0.5)
        k = k_ref[0, pl.ds(s0, ch), :]
        v = v_ref[0, pl.ds(s0, ch), :].astype(BF16)
        st = st_s[...]
        o = jnp.dot((q * jnp.exp(b)).astype(BF16), st.astype(BF16), preferred_element_type=F32)

        blocks = []
        for sb in range(ch // sub):
            r0 = sb * sub
            q_i = q[r0:r0 + sub]
            b_i = b[r0:r0 + sub]
            k_i = k[r0:r0 + sub]
            att = jnp.zeros((sub, ch), F32)
            for j in range(sub):
                e = jnp.exp(jnp.minimum(b_i - b_i[j:j + 1, :], 0.0))
                colv = jnp.sum(q_i * k_i[j:j + 1, :] * e, axis=1, keepdims=True)
                att = jnp.where((lane_j == r0 + j) & (sub_i >= j), colv, att)
            if sb > 0:
                ref_b = b[r0 - 1:r0, :]
                q_t = q_i * jnp.exp(b_i - ref_b)
                k_t = k * jnp.exp(jnp.minimum(ref_b - b, 0.0))
                off = lax.dot_general(q_t, k_t, (((1,), (1,)), ((), ())),
                                      precision=HI, preferred_element_type=F32)
                att = jnp.where(lane_j < r0, off, att)
            blocks.append(att)
        att_full = jnp.concatenate(blocks, axis=0)
        o = o + jnp.dot(att_full.astype(BF16), v, preferred_element_type=F32)
        raw_s[pl.ds(s0, ch), :] = o

        b_last = b[ch - 1:ch, :]
        k_d = k * jnp.exp(b_last - b)
        upd = jnp.dot(k_d.T.astype(BF16), v, preferred_element_type=F32)
        decay_col = lax.dot_general(eye, jnp.broadcast_to(jnp.exp(b_last), (SUBLANES, dk)),
                                    (((1,), (1,)), ((), ())), precision=HI,
                                    preferred_element_type=F32)[:, 0:1]
        st_s[...] = decay_col * st + upd
        return carry

    lax.fori_loop(0, seq // ch, chunk, 0)

    o = raw_s[...]
    rms = o * lax.rsqrt(jnp.mean(o * o, axis=-1, keepdims=True) + LN_EPS) * ng_ref[...]
    gg = g_ref[0]
    o_ref[0] = rms * (gg * jax.nn.sigmoid(gg))


def _gla_core(proj, w_up_pad, b_gk, norm_g):
    B, S, _ = proj.shape
    H, dk, dv = C_HEADS, C_DK, C_DV
    assert dk == LANES and S % GLA_CHUNK == 0
    k_off = C_KEY_DIM // dk
    v_off = 2 * C_KEY_DIM // dv
    g_off = (2 * C_KEY_DIM + C_VAL_DIM) // dv
    gl_off = (2 * C_KEY_DIM + 2 * C_VAL_DIM) // LANES
    return pl.pallas_call(
        functools.partial(_gla_body, seq=S),
        grid=(B, H),
        in_specs=[pl.BlockSpec((1, S, dk), lambda b, h: (b, 0, h)),
                  pl.BlockSpec((1, S, dk), lambda b, h: (b, 0, k_off + h)),
                  pl.BlockSpec((1, S, dv), lambda b, h: (b, 0, v_off + h)),
                  pl.BlockSpec((1, S, dv), lambda b, h: (b, 0, g_off + h)),
                  pl.BlockSpec((1, S, LANES), lambda b, h: (b, 0, gl_off)),
                  pl.BlockSpec((LANES, dk), lambda b, h: (0, h)),
                  pl.BlockSpec((1, dk), lambda b, h: (0, h)),
                  pl.BlockSpec((1, dv), lambda b, h: (0, 0))],
        out_specs=pl.BlockSpec((1, S, dv), lambda b, h: (b, 0, h)),
        out_shape=jax.ShapeDtypeStruct((B, S, H * dv), F32),
        scratch_shapes=[pltpu.VMEM((S, dk), F32), pltpu.VMEM((dk, dv), F32), pltpu.VMEM((S, dv), F32)],
        compiler_params=_cparams("arbitrary", "arbitrary"),
        name="gla_core",
    )(proj, proj, proj, proj, proj, w_up_pad, b_gk.reshape(1, C_KEY_DIM), norm_g.reshape(1, dv))


def _router_body(h_ref, w_ref, b_ref, slot_ref, gate_ref, cnt_ref, lbase_ref, gbase_ref, run_s, w1_s, w2_s, *, tm):
    E = N_EXPERTS

    @pl.when(pl.program_id(0) == 0)
    def _():
        run_s[...] = jnp.zeros((1, E), F32)
        w = w_ref[...]
        w1 = w.astype(BF16)
        w1_s[...] = w1
        w2_s[...] = (w - w1.astype(F32)).astype(BF16)

    h = h_ref[...]
    h1 = h.astype(BF16)
    h2 = (h - h1.astype(F32)).astype(BF16)
    logits = (jnp.dot(h1, w1_s[...], preferred_element_type=F32)
              + (jnp.dot(h1, w2_s[...], preferred_element_type=F32)
                 + jnp.dot(h2, w1_s[...], preferred_element_type=F32))) + b_ref[...]
    lane = lax.broadcasted_iota(I32, (tm, E), 1)
    rest = logits
    top_val, onehot = [], []
    for _ in range(TOP_K):
        m = jnp.max(rest, axis=1, keepdims=True)
        idx = jnp.min(jnp.where(rest == m, lane, E), axis=1, keepdims=True)
        oh = lane == idx
        rest = jnp.where(oh, -jnp.inf, rest)
        top_val.append(m)
        onehot.append(oh)
    ex = [jnp.exp(v - top_val[0]) for v in top_val]
    den = ex[0] + ex[1] + ex[2] + ex[3]
    chosen = sum(oh.astype(F32) for oh in onehot)

    row = lax.broadcasted_iota(I32, (tm, tm), 0)
    col = lax.broadcasted_iota(I32, (tm, tm), 1)
    before = (row > col).astype(BF16)
    seen = jnp.dot(before, chosen.astype(BF16), preferred_element_type=F32)
    cnt = jnp.sum(chosen, axis=0, keepdims=True)
    cnt_pad = jnp.floor((cnt + (MOE_SEG - 1)) * (1.0 / MOE_SEG)) * MOE_SEG
    er = lax.broadcasted_iota(I32, (E, E), 0)
    ec = lax.broadcasted_iota(I32, (E, E), 1)
    cnt_rows = jnp.broadcast_to(cnt_pad, (SUBLANES, E))
    lbase_rows = jnp.dot(cnt_rows, (er < ec).astype(F32), precision=HI, preferred_element_type=F32)
    pos = seen + lbase_rows[0:1, :]

    lane_k = lax.broadcasted_iota(I32, (tm, TOP_K), 1)
    s_out = jnp.zeros((tm, TOP_K), I32)
    g_out = jnp.zeros((tm, TOP_K), F32)
    for kk in range(TOP_K):
        sk = jnp.sum(jnp.where(onehot[kk], pos, 0.0), axis=1, keepdims=True)
        s_out = jnp.where(lane_k == kk, sk.astype(I32), s_out)
        g_out = jnp.where(lane_k == kk, ex[kk] / den, g_out)
    slot_ref[...] = s_out
    gate_ref[...] = g_out
    cnt_ref[...] = cnt_rows.astype(I32)
    lbase_ref[...] = lbase_rows.astype(I32)
    gbase_ref[...] = jnp.broadcast_to(run_s[...], (SUBLANES, E)).astype(I32)
    run_s[...] = run_s[...] + cnt_pad


def _router(h, w, b, *, tm=MOE_TM):
    T, D = h.shape
    E = N_EXPERTS
    nt = T // tm
    tab = jax.ShapeDtypeStruct((nt * SUBLANES, E), I32)
    tab_spec = pl.BlockSpec((SUBLANES, E), lambda i: (i, 0))
    return pl.pallas_call(
        functools.partial(_router_body, tm=tm),
        grid=(nt,),
        in_specs=[pl.BlockSpec((tm, D), lambda i: (i, 0)),
                  pl.BlockSpec((D, E), lambda i: (0, 0)),
                  pl.BlockSpec((1, E), lambda i: (0, 0))],
        out_specs=[pl.BlockSpec((tm, TOP_K), lambda i: (i, 0)),
                   pl.BlockSpec((tm, TOP_K), lambda i: (i, 0)),
                   tab_spec, tab_spec, tab_spec],
        out_shape=[jax.ShapeDtypeStruct((T, TOP_K), I32),
                   jax.ShapeDtypeStruct((T, TOP_K), F32),
                   tab, tab, tab],
        scratch_shapes=[pltpu.VMEM((1, E), F32), pltpu.VMEM((D, E), BF16), pltpu.VMEM((D, E), BF16)],
        compiler_params=_cparams("arbitrary"),
        name="moe_router",
    )(h, w, b.reshape(1, E))


def _segment_copies(seg_g, seg_l, seg_n, tile, make_copy, start):
    def per_expert(e, carry):
        idx = tile * N_EXPERTS + e
        n = seg_n[idx]
        g = seg_g[idx]
        l = seg_l[idx]
        for bit in range(MOE_SEG_BITS - 1, -1, -1):
            size = MOE_SEG << bit

            @pl.when((n & size) != 0)
            def _():
                off = n & ~(2 * size - 1)
                cp = make_copy(pl.multiple_of(g + off, MOE_SEG), pl.multiple_of(l + off, MOE_SEG), size)
                if start:
                    cp.start()
                else:
                    cp.wait()
        return carry

    lax.fori_loop(0, N_EXPERTS, per_expert, 0)


def _dispatch_body(segg_ref, segl_ref, segn_ref, zf_ref, slot_t_ref, h_ref, xs_ref, stage, zbuf, sem, *, tm, bm):
    i = pl.program_id(0)

    @pl.when(i == 0)
    def _():
        zbuf[...] = jnp.zeros(zbuf.shape, F32)

        def zero_copy(j):
            return pltpu.make_async_copy(zbuf, xs_ref.at[pl.ds(pl.multiple_of(zf_ref[j], MOE_SEG), bm), :],
                                         sem.at[0])

        def start(j, carry):
            @pl.when(zf_ref[j] >= 0)
            def _():
                zero_copy(j).start()
            return carry

        def wait(j, carry):
            @pl.when(zf_ref[j] >= 0)
            def _():
                zero_copy(j).wait()
            return carry

        lax.fori_loop(0, zf_ref.shape[0], start, 0)
        lax.fori_loop(0, zf_ref.shape[0], wait, 0)

    slot_i = i % 2
    hb = h_ref[...].astype(BF16)
    for c in range(MOE_STAGE // MOE_PERM_CHUNK):
        r0 = c * MOE_PERM_CHUNK
        rows = lax.broadcasted_iota(I32, (MOE_PERM_CHUNK, tm), 0) + r0
        hit = rows == slot_t_ref[0:1, :]
        for kk in range(1, TOP_K):
            hit = hit | (rows == slot_t_ref[kk:kk + 1, :])
        perm = jnp.where(hit, 1.0, 0.0).astype(BF16)
        stage[slot_i, r0:r0 + MOE_PERM_CHUNK, :] = jnp.dot(perm, hb, preferred_element_type=F32)

    def copier(buf):
        def make_copy(g, l, size):
            return pltpu.make_async_copy(stage.at[buf, pl.ds(l, size), :], xs_ref.at[pl.ds(g, size), :],
                                         sem.at[buf])
        return make_copy

    _segment_copies(segg_ref, segl_ref, segn_ref, i, copier(slot_i), True)

    @pl.when(i > 0)
    def _():
        _segment_copies(segg_ref, segl_ref, segn_ref, i - 1, copier(1 - slot_i), False)

    @pl.when(i == pl.num_programs(0) - 1)
    def _():
        _segment_copies(segg_ref, segl_ref, segn_ref, i, copier(slot_i), False)


def _dispatch(seg_g, seg_l, seg_n, zero_rows, slot_t, h, n_rows, *, tm=MOE_TM, bm=MOE_BM):
    T, D = h.shape
    grid_spec = pltpu.PrefetchScalarGridSpec(
        num_scalar_prefetch=4,
        grid=(T // tm,),
        in_specs=[pl.BlockSpec((TOP_K, tm), lambda i, *_: (0, i)),
                  pl.BlockSpec((tm, D), lambda i, *_: (i, 0))],
        out_specs=pl.BlockSpec(memory_space=pl.ANY),
        scratch_shapes=[pltpu.VMEM((2, MOE_STAGE, D), F32), pltpu.VMEM((bm, D), F32),
                        pltpu.SemaphoreType.DMA((2,))],
    )
    return pl.pallas_call(
        functools.partial(_dispatch_body, tm=tm, bm=bm),
        grid_spec=grid_spec,
        out_shape=jax.ShapeDtypeStruct((n_rows, D), F32),
        compiler_params=pltpu.CompilerParams(dimension_semantics=("arbitrary",), vmem_limit_bytes=VMEM_LIMIT,
                                             has_side_effects=True),
        name="moe_dispatch",
    )(seg_g, seg_l, seg_n, zero_rows, slot_t, h)


def _expert_body(be_ref, nb_ref, nv_ref, xs_ref, wgu_ref, bgu_ref, wd_ref, bd_ref, ys_ref, wgu_s, wd_s, *, bm):
    i = pl.program_id(0)
    prev = be_ref[jnp.maximum(i - 1, 0)]
    live = i < nb_ref[0]
    half = bm // 2
    full = nv_ref[i] > half

    @pl.when(live & ((i == 0) | (be_ref[i] != prev)))
    def _():
        wgu_s[...] = wgu_ref[...].astype(BF16)
        wd_s[...] = wd_ref[...].astype(BF16)

    def mlp(r0, rows):
        hh = jnp.dot(xs_ref[r0:r0 + rows, :].astype(BF16), wgu_s[...], preferred_element_type=F32) + bgu_ref[...]
        d_e = hh.shape[1] // 2
        h_gate = jnp.minimum(hh[:, :d_e], SWIGLU_LIMIT)
        h_up = jnp.clip(hh[:, d_e:], -SWIGLU_LIMIT, SWIGLU_LIMIT)
        act = h_gate * jax.nn.sigmoid(SWIGLU_ALPHA * h_gate) * (h_up + 1.0)
        ys_ref[r0:r0 + rows, :] = jnp.dot(act.astype(BF16), wd_s[...], preferred_element_type=F32) + bd_ref[...]

    @pl.when(live & full)
    def _():
        mlp(0, bm)

    @pl.when(live & jnp.logical_not(full))
    def _():
        mlp(0, half)
        ys_ref[half:bm, :] = jnp.zeros((bm - half, ys_ref.shape[1]), F32)

    @pl.when(jnp.logical_not(live))
    def _():
        ys_ref[...] = jnp.zeros(ys_ref.shape, F32)


def _experts(block_e, n_live, n_valid, xs, layer, w_gu, b_gu, w_down, b_down, *, bm=MOE_BM):
    L, E, D, D2 = w_gu.shape
    n_blocks = xs.shape[0] // bm

    def blk(i, nb):
        return jnp.maximum(jnp.minimum(i, nb[0] - 1), 0)

    grid_spec = pltpu.PrefetchScalarGridSpec(
        num_scalar_prefetch=3,
        grid=(n_blocks,),
        in_specs=[pl.BlockSpec((bm, D), lambda i, be, nb, nv: (blk(i, nb), 0)),
                  pl.BlockSpec((None, None, D, D2), lambda i, be, nb, nv: (layer, be[blk(i, nb)], 0, 0)),
                  pl.BlockSpec((None, None, 1, D2), lambda i, be, nb, nv: (layer, be[blk(i, nb)], 0, 0)),
                  pl.BlockSpec((None, None, D2 // 2, D), lambda i, be, nb, nv: (layer, be[blk(i, nb)], 0, 0)),
                  pl.BlockSpec((None, None, 1, D), lambda i, be, nb, nv: (layer, be[blk(i, nb)], 0, 0))],
        out_specs=pl.BlockSpec((bm, D), lambda i, be, nb, nv: (i, 0)),
        scratch_shapes=[pltpu.VMEM((D, D2), BF16), pltpu.VMEM((D2 // 2, D), BF16)],
    )
    return pl.pallas_call(
        functools.partial(_expert_body, bm=bm),
        grid_spec=grid_spec,
        out_shape=jax.ShapeDtypeStruct(xs.shape, F32),
        compiler_params=_cparams("arbitrary"),
        name="moe_experts",
    )(block_e, n_live, n_valid, xs, w_gu, b_gu.reshape(L, E, 1, D2), w_down, b_down.reshape(L, E, 1, D))


def _combine_body(segg_ref, segl_ref, segn_ref, ys_ref, slot_ref, gate_ref, h_ref, g_ref, b_ref, o_ref,
                  stage, mix_s, sem, *, tm):
    i = pl.program_id(0)

    slot_i = i % 2

    def copier(buf):
        def make_copy(g, l, size):
            return pltpu.make_async_copy(ys_ref.at[pl.ds(g, size), :], stage.at[buf, pl.ds(l, size), :],
                                         sem.at[buf])
        return make_copy

    @pl.when(i == 0)
    def _():
        stage[...] = jnp.zeros(stage.shape, F32)
        _segment_copies(segg_ref, segl_ref, segn_ref, i, copier(slot_i), True)

    @pl.when(i + 1 < pl.num_programs(0))
    def _():
        _segment_copies(segg_ref, segl_ref, segn_ref, i + 1, copier(1 - slot_i), True)

    slot = slot_ref[...]
    gate = gate_ref[...]
    for c in range(MOE_STAGE // MOE_PERM_CHUNK):
        c0 = c * MOE_PERM_CHUNK
        cols = lax.broadcasted_iota(I32, (tm, MOE_PERM_CHUNK), 1) + c0
        w = jnp.zeros((tm, MOE_PERM_CHUNK), F32)
        for kk in range(TOP_K):
            w = jnp.where(cols == slot[:, kk:kk + 1], gate[:, kk:kk + 1], w)
        mix_s[:, c0:c0 + MOE_PERM_CHUNK] = w.astype(BF16)

    _segment_copies(segg_ref, segl_ref, segn_ref, i, copier(slot_i), False)
    y = jnp.dot(mix_s[...], stage[slot_i].astype(BF16), preferred_element_type=F32)
    o_ref[...] = _ln(ALPHA * h_ref[...] + y, g_ref[...], b_ref[...])


def _combine(seg_g, seg_l, seg_n, ys, slot, gate, h, g, b, *, tm=MOE_TM):
    T, D = h.shape
    grid_spec = pltpu.PrefetchScalarGridSpec(
        num_scalar_prefetch=3,
        grid=(T // tm,),
        in_specs=[pl.BlockSpec(memory_space=pl.ANY),
                  pl.BlockSpec((tm, TOP_K), lambda i, *_: (i, 0)),
                  pl.BlockSpec((tm, TOP_K), lambda i, *_: (i, 0)),
                  pl.BlockSpec((tm, D), lambda i, *_: (i, 0)),
                  pl.BlockSpec((1, D), lambda i, *_: (0, 0)),
                  pl.BlockSpec((1, D), lambda i, *_: (0, 0))],
        out_specs=pl.BlockSpec((tm, D), lambda i, *_: (i, 0)),
        scratch_shapes=[pltpu.VMEM((2, MOE_STAGE, D), F32), pltpu.VMEM((tm, MOE_STAGE), BF16),
                        pltpu.SemaphoreType.DMA((2,))],
    )
    return pl.pallas_call(
        functools.partial(_combine_body, tm=tm),
        grid_spec=grid_spec,
        out_shape=jax.ShapeDtypeStruct((T, D), F32),
        compiler_params=_cparams("arbitrary"),
        name="moe_combine",
    )(seg_g, seg_l, seg_n, ys, slot, gate, h, g.reshape(1, D), b.reshape(1, D))


def _moe_block(h, layer, w_router, b_router, w_gu, b_gu, w_down, b_down, g, b):
    T = h.shape[0]
    bm, tm, E = MOE_BM, MOE_TM, N_EXPERTS
    nt = T // tm
    slot, gate, cnt, lbase, gbase = _router(h, w_router, b_router)
    cnt, lbase, gbase = cnt[::SUBLANES], lbase[::SUBLANES], gbase[::SUBLANES]
    total = gbase[-1] + cnt[-1]
    padded = (total + bm - 1) // bm * bm
    pend = jnp.cumsum(padded)
    pstart = pend - padded
    seg_g = (pstart[None, :] + gbase).reshape(-1).astype(I32)
    seg_l = lbase.reshape(-1)
    seg_n = cnt.reshape(-1)
    n_blocks = -(-(T * TOP_K + nt * E * (MOE_SEG - 1)) // bm) + E
    block_start = jnp.arange(n_blocks, dtype=I32) * bm
    block_e = jnp.minimum(jnp.sum((pend[None, :] <= block_start[:, None]).astype(I32), axis=1), E - 1)
    n_live = (pend[-1:] // bm).astype(I32)
    n_valid = jnp.clip((pstart + total)[block_e] - block_start, 0, bm).astype(I32)
    last_blk = jnp.where(padded > 0, pend - bm, -1)
    tail_blk = n_live[0] + jnp.arange(n_blocks - T * TOP_K // bm, dtype=I32)
    tail_blk = jnp.where(tail_blk < n_blocks, tail_blk * bm, -1)
    zero_rows = jnp.concatenate([last_blk, tail_blk]).astype(I32)
    xs = _dispatch(seg_g, seg_l, seg_n, zero_rows, slot.T, h, n_blocks * bm)
    ys = _experts(block_e, n_live, n_valid, xs, layer, w_gu, b_gu, w_down, b_down)
    return _combine(seg_g, seg_l, seg_n, ys, slot, gate, h, g, b)


def kernel(x, rel_bias, a_w_in, a_w_out, b_w_in, b_ln_g, b_ln_b, b_w_s, b_b_s, b_w_out,
           c_w_in, c_w_gk_up, c_b_gk, c_norm_g, c_w_out, ln_g, ln_b,
           moe_w_router, moe_b_router, moe_w_gate_up, moe_b_gate_up, moe_w_down, moe_b_down):
    B, S, D = x.shape
    T = B * S
    h = x.reshape(T, D)
    tables = _bias_tables(rel_bias)
    for i in range(DEPTH):
        j = i // N_MIXERS
        mixer = i % N_MIXERS
        if mixer == 0:
            qkv = _matmul(h, a_w_in[j].astype(BF16), out_dtype=BF16)
            o = _moba_attention(qkv.reshape(B, S, 3 * D), rel_bias, tables)
            h = _matmul_res_ln(o.reshape(T, D), a_w_out[j], h, ln_g[i, 0], ln_b[i, 0])
        elif mixer == 1:
            z = _matmul(h, b_w_in[j].astype(BF16), act="gelu", out_dtype=BF16)
            h = _gmlp_gate_out(z, h, b_ln_g[j], b_ln_b[j], b_w_s[j], b_b_s[j], b_w_out[j],
                               ln_g[i, 0], ln_b[i, 0])
        else:
            pad = LANES - C_GATE_RANK
            w_in = jnp.pad(c_w_in[j], ((0, 0), (0, pad)))
            w_up = jnp.pad(c_w_gk_up[j], ((0, pad), (0, 0)))
            proj = _matmul(h, w_in.astype(BF16), tn=640)
            o = _gla_core(proj.reshape(B, S, C_IN_WIDTH + pad), w_up, c_b_gk[j], c_norm_g[j])
            h = _matmul_res_ln(o.reshape(T, D), c_w_out[j], h, ln_g[i, 0], ln_b[i, 0])
        h = _moe_block(h, i, moe_w_router[i], moe_b_router[i], moe_w_gate_up, moe_b_gate_up,
                       moe_w_down, moe_b_down, ln_g[i, 1], ln_b[i, 1])
    return h.reshape(B, S, D)
```

```python
import functools
import math

import numpy as np
import jax
import jax.numpy as jnp
from jax import lax
from jax.experimental import pallas as pl
from jax.experimental.pallas import tpu as pltpu

F32 = jnp.float32
BF16 = jnp.bfloat16
I32 = jnp.int32
HI = lax.Precision.HIGHEST

D_MODEL = 1024
DEPTH = 4
N_MIXERS = 3
ALPHA = (2.0 * DEPTH) ** 0.25
LN_EPS = 1e-5
LOG2E = math.log2(math.e)

A_HEADS = 8
A_DH = D_MODEL // A_HEADS
MOBA_BLOCK = 256
MOBA_TOPK = 3
REL_BUCKETS = 32
REL_MAX_DIST = 128

B_GROUPS = 8
B_WIDTH = 2 * D_MODEL
B_CHUNK = 128
B_GW = B_WIDTH // B_GROUPS

C_HEADS = 4
C_KEY_DIM = D_MODEL // 2
C_VAL_DIM = D_MODEL
C_DK = C_KEY_DIM // C_HEADS
C_DV = C_VAL_DIM // C_HEADS
C_GATE_RANK = 16
C_GATE_NORMALIZER = 16.0
C_IN_WIDTH = 2 * C_KEY_DIM + 2 * C_VAL_DIM + C_GATE_RANK

N_EXPERTS = 32
TOP_K = 4
SWIGLU_LIMIT = 7.0
SWIGLU_ALPHA = 1.702

LANES = 128
SUBLANES = 8
VMEM_LIMIT = 56 * 1024 * 1024

GLA_CHUNK = 128
GLA_SUB = 16
GLA_HEADS_PER_STEP = 2
MOE_BM = 512
MOE_TM = 512
MOE_SEG = SUBLANES
MOE_SEG_BITS = (MOE_TM // MOE_SEG).bit_length()
MOE_STAGE = MOE_TM * TOP_K + N_EXPERTS * MOE_SEG
MOE_PERM_CHUNK = 256


def _cparams(*sem):
    return pltpu.CompilerParams(dimension_semantics=sem, vmem_limit_bytes=VMEM_LIMIT)


def _ln(x, g, b):
    mu = jnp.mean(x, axis=-1, keepdims=True)
    xc = x - mu
    var = jnp.mean(xc * xc, axis=-1, keepdims=True)
    return xc * lax.rsqrt(var + LN_EPS) * g + b


def _mm_body(x_ref, w_ref, o_ref, *, act, tn):
    x = x_ref[...].astype(BF16)
    for n0 in range(0, w_ref.shape[1], tn):
        acc = jnp.dot(x, w_ref[:, n0:n0 + tn], preferred_element_type=F32)
        if act == "gelu":
            acc = 0.5 * acc * (1.0 + lax.erf(acc * (2.0 ** -0.5)))
        o_ref[:, n0:n0 + tn] = acc.astype(o_ref.dtype)


def _matmul(x, w, *, act=None, tm=512, tn=512, out_dtype=F32):
    M, K = x.shape
    N = w.shape[1]
    assert M % tm == 0 and N % tn == 0
    return pl.pallas_call(
        functools.partial(_mm_body, act=act, tn=tn),
        grid=(M // tm,),
        in_specs=[pl.BlockSpec((tm, K), lambda i: (i, 0)),
                  pl.BlockSpec((K, N), lambda i: (0, 0))],
        out_specs=pl.BlockSpec((tm, N), lambda i: (i, 0)),
        out_shape=jax.ShapeDtypeStruct((M, N), out_dtype),
        compiler_params=_cparams("arbitrary"),
        name="proj_matmul",
    )(x, w)


def _mm_res_ln_body(x_ref, w_ref, h_ref, g_ref, b_ref, o_ref, wb_ref):
    @pl.when(pl.program_id(0) == 0)
    def _():
        wb_ref[...] = w_ref[...].astype(BF16)

    t = jnp.dot(x_ref[...].astype(BF16), wb_ref[...], preferred_element_type=F32)
    o_ref[...] = _ln(ALPHA * h_ref[...] + t, g_ref[...], b_ref[...])


def _matmul_res_ln(x, w, h, g, b, *, tm=512):
    M, K = x.shape
    N = w.shape[1]
    return pl.pallas_call(
        _mm_res_ln_body,
        grid=(M // tm,),
        in_specs=[pl.BlockSpec((tm, K), lambda i: (i, 0)),
                  pl.BlockSpec((K, N), lambda i: (0, 0)),
                  pl.BlockSpec((tm, N), lambda i: (i, 0)),
                  pl.BlockSpec((1, N), lambda i: (0, 0)),
                  pl.BlockSpec((1, N), lambda i: (0, 0))],
        out_specs=pl.BlockSpec((tm, N), lambda i: (i, 0)),
        out_shape=jax.ShapeDtypeStruct((M, N), F32),
        scratch_shapes=[pltpu.VMEM((K, N), BF16)],
        compiler_params=_cparams("arbitrary"),
        name="outproj_res_ln",
    )(x, w, h, g.reshape(1, N), b.reshape(1, N))


def _t5_bucket_lower_bounds():
    n = np.arange(0, 4 * REL_MAX_DIST, dtype=np.int64)
    max_exact = REL_BUCKETS // 2
    nf = np.maximum(n, 1).astype(np.float32)
    large = max_exact + (np.log(nf / np.float32(max_exact)) / np.float32(math.log(REL_MAX_DIST / max_exact))
                         * np.float32(REL_BUCKETS - max_exact)).astype(np.int32)
    large = np.minimum(large, REL_BUCKETS - 1)
    bucket = np.where(n < max_exact, n, large)
    assert np.all(np.diff(bucket) >= 0) and bucket[-1] == REL_BUCKETS - 1
    return [int(np.argmax(bucket >= b)) for b in range(REL_BUCKETS)]


_BUCKET_LO = _t5_bucket_lower_bounds()


def _bias_table_body(rel_ref, o_ref):
    blk = MOBA_BLOCK
    row = lax.broadcasted_iota(I32, (blk, blk), 0)
    col = lax.broadcasted_iota(I32, (blk, blk), 1)
    for t in range(2):
        d = row - col + t * blk
        for h in range(A_HEADS):
            val = jnp.full((blk, blk), rel_ref[REL_BUCKETS - 1, h], F32)
            for b in range(REL_BUCKETS - 2, -1, -1):
                val = jnp.where(d < _BUCKET_LO[b + 1], rel_ref[b, h], val)
            val = val * LOG2E
            if t == 0:
                val = jnp.where(d >= 0, val, -jnp.inf)
            o_ref[h, t] = val


def _bias_tables(rel_bias):
    return pl.pallas_call(
        _bias_table_body,
        in_specs=[pl.BlockSpec(memory_space=pltpu.SMEM)],
        out_shape=jax.ShapeDtypeStruct((A_HEADS, 2, MOBA_BLOCK, MOBA_BLOCK), F32),
        compiler_params=pltpu.CompilerParams(vmem_limit_bytes=VMEM_LIMIT),
        name="rel_bias_tables",
    )(rel_bias)


def _moba_body(rel_ref, q_ref, k_ref, v_ref, tb_ref, o_ref, *, seq):
    blk, dh = MOBA_BLOCK, A_DH
    nb = seq // blk
    h = pl.program_id(1)
    kmean = jnp.mean(k_ref[0].astype(F32).reshape(nb, blk, dh), axis=1)
    kmean_pad = jnp.concatenate([kmean, jnp.zeros((LANES - nb, dh), F32)], axis=0)
    b_far = rel_ref[REL_BUCKETS - 1, h] * LOG2E
    lane = lax.broadcasted_iota(I32, (blk, LANES), 1)
    for c in range(nb):
        q = q_ref[0, c * blk:(c + 1) * blk, :]
        n_keys = (c + 1) * blk
        s = lax.dot_general(q, k_ref[0, 0:n_keys, :], (((1,), (1,)), ((), ())),
                            preferred_element_type=F32)
        if c > MOBA_TOPK:
            gate = lax.dot_general(q.astype(F32), kmean_pad, (((1,), (1,)), ((), ())),
                                   precision=HI, preferred_element_type=F32)
            gate = jnp.where(lane < c, gate, -jnp.inf)
        pieces = []
        for n in range(c):
            shift = b_far
            if c > MOBA_TOPK:
                gn = gate[:, n:n + 1]
                beats = (gate > gn) | ((gate == gn) & (lane < n))
                rank = jnp.sum(beats.astype(F32), axis=1, keepdims=True)
                shift = jnp.where(rank < MOBA_TOPK, b_far, -jnp.inf)
            if n == c - 1:
                shift = tb_ref[0, 1] + (shift - b_far)
            pieces.append(s[:, n * blk:(n + 1) * blk] + shift)
        pieces.append(s[:, c * blk:(c + 1) * blk] + tb_ref[0, 0])
        logits = jnp.concatenate(pieces, axis=1) if c > 0 else pieces[0]
        m = jnp.max(logits, axis=1, keepdims=True)
        p = jnp.exp2(logits - m)
        denom = jnp.sum(p, axis=1, keepdims=True)
        o = jnp.dot(p.astype(BF16), v_ref[0, 0:n_keys, :], preferred_element_type=F32)
        o_ref[0, c * blk:(c + 1) * blk, :] = o / denom


def _moba_attention(qkv, rel_bias, tables):
    B, S, _ = qkv.shape
    H, dh = A_HEADS, A_DH
    assert S % MOBA_BLOCK == 0 and S // MOBA_BLOCK <= SUBLANES
    return pl.pallas_call(
        functools.partial(_moba_body, seq=S),
        grid=(B, H),
        in_specs=[pl.BlockSpec(memory_space=pltpu.SMEM),
                  pl.BlockSpec((1, S, dh), lambda b, h: (b, 0, h)),
                  pl.BlockSpec((1, S, dh), lambda b, h: (b, 0, H + h)),
                  pl.BlockSpec((1, S, dh), lambda b, h: (b, 0, 2 * H + h)),
                  pl.BlockSpec((1, 2, MOBA_BLOCK, MOBA_BLOCK), lambda b, h: (h, 0, 0, 0))],
        out_specs=pl.BlockSpec((1, S, dh), lambda b, h: (b, 0, h)),
        out_shape=jax.ShapeDtypeStruct((B, S, H * dh), F32),
        compiler_params=_cparams("arbitrary", "arbitrary"),
        name="moba_attention",
    )(rel_bias, qkv, qkv, qkv, tables)


def _gmlp_body(u_ref, v_ref, h_ref, vg_ref, vb_ref, ws_ref, bs_ref, wo_ref, g_ref, b_ref, o_ref,
               wst_s, wob_s, *, tm):
    @pl.when(pl.program_id(0) == 0)
    def _():
        row = lax.broadcasted_iota(I32, (B_CHUNK, B_CHUNK), 0)
        col = lax.broadcasted_iota(I32, (B_CHUNK, B_CHUNK), 1)
        for g in range(B_GROUPS):
            wst_s[g] = jnp.where(row >= col, ws_ref[g], 0.0).astype(BF16)
        wob_s[...] = wo_ref[...].astype(BF16)

    vn = _ln(v_ref[...].astype(F32), vg_ref[...], vb_ref[...]).astype(BF16)
    u = u_ref[...].astype(F32)
    rows = []
    for c in range(tm // B_CHUNK):
        r0 = c * B_CHUNK
        cols = []
        for g in range(B_GROUPS):
            c0 = g * B_GW
            mixed = jnp.dot(wst_s[g], vn[r0:r0 + B_CHUNK, c0:c0 + B_GW],
                            preferred_element_type=F32) + bs_ref[g]
            cols.append(u[r0:r0 + B_CHUNK, c0:c0 + B_GW] * mixed)
        rows.append(jnp.concatenate(cols, axis=1))
    y = jnp.concatenate(rows, axis=0).astype(BF16)
    t = jnp.dot(y, wob_s[...], preferred_element_type=F32)
    o_ref[...] = _ln(ALPHA * h_ref[...] + t, g_ref[...], b_ref[...])


def _gmlp_gate_out(z, h, v_g, v_b, w_s, b_s, w_out, g, b, *, tm=256):
    T = z.shape[0]
    D = D_MODEL
    return pl.pallas_call(
        functools.partial(_gmlp_body, tm=tm),
        grid=(T // tm,),
        in_specs=[pl.BlockSpec((tm, B_WIDTH), lambda i: (i, 0)),
                  pl.BlockSpec((tm, B_WIDTH), lambda i: (i, 1)),
                  pl.BlockSpec((tm, D), lambda i: (i, 0)),
                  pl.BlockSpec((1, B_WIDTH), lambda i: (0, 0)),
                  pl.BlockSpec((1, B_WIDTH), lambda i: (0, 0)),
                  pl.BlockSpec((B_GROUPS, B_CHUNK, B_CHUNK), lambda i: (0, 0, 0)),
                  pl.BlockSpec((B_GROUPS, B_CHUNK, 1), lambda i: (0, 0, 0)),
                  pl.BlockSpec((B_WIDTH, D), lambda i: (0, 0)),
                  pl.BlockSpec((1, D), lambda i: (0, 0)),
                  pl.BlockSpec((1, D), lambda i: (0, 0))],
        out_specs=pl.BlockSpec((tm, D), lambda i: (i, 0)),
        out_shape=jax.ShapeDtypeStruct((T, D), F32),
        scratch_shapes=[pltpu.VMEM((B_GROUPS, B_CHUNK, B_CHUNK), BF16), pltpu.VMEM((B_WIDTH, D), BF16)],
        compiler_params=_cparams("arbitrary"),
        name="gmlp_gate_out",
    )(z, z, h, v_g.reshape(1, B_WIDTH), v_b.reshape(1, B_WIDTH), w_s, b_s.reshape(B_GROUPS, B_CHUNK, 1),
      w_out, g.reshape(1, D), b.reshape(1, D))


def _gla_body(q_ref, k_ref, v_ref, g_ref, gl_ref, wup_ref, bgk_ref, ng_ref, o_ref, la_s, st_s, raw_s, *, seq):
    ch, sub, dk, dv = GLA_CHUNK, GLA_SUB, C_DK, C_DV
    x = jnp.dot(gl_ref[0], wup_ref[...], precision=HI, preferred_element_type=F32) + bgk_ref[...]
    la_s[...] = (jnp.minimum(x, 0.0) - jnp.log1p(jnp.exp(-jnp.abs(x)))) * (1.0 / C_GATE_NORMALIZER)
    st_s[...] = jnp.zeros(st_s.shape, F32)

    row_c = lax.broadcasted_iota(I32, (ch, ch), 0)
    col_c = lax.broadcasted_iota(I32, (ch, ch), 1)
    tri = (row_c >= col_c).astype(F32)
    eye = (row_c == col_c).astype(F32)
    sub_i = lax.broadcasted_iota(I32, (sub, ch), 0)
    lane_j = lax.broadcasted_iota(I32, (sub, ch), 1)

    def chunk(c, carry):
        for hh in range(GLA_HEADS_PER_STEP):
            chunk_head(c, hh)
        return carry

    def chunk_head(c, hh):
        s0 = pl.multiple_of(c * ch, ch)
        a = la_s[pl.ds(s0, ch), hh * dk:(hh + 1) * dk]
        b = jnp.dot(tri, a, precision=HI, preferred_element_type=F32)
        q = q_ref[0, pl.ds(s0, ch), hh * dk:(hh + 1) * dk] * (dk ** -0.5)
        k = k_ref[0, pl.ds(s0, ch), hh * dk:(hh + 1) * dk]
        v = v_ref[0, pl.ds(s0, ch), hh * dv:(hh + 1) * dv].astype(BF16)
        st = st_s[hh]
        o = jnp.dot((q * jnp.exp(b)).astype(BF16), st.astype(BF16), preferred_element_type=F32)

        blocks = []
        for sb in range(ch // sub):
            r0 = sb * sub
            q_i = q[r0:r0 + sub]
            b_i = b[r0:r0 + sub]
            k_i = k[r0:r0 + sub]
            att = jnp.zeros((sub, ch), F32)
            for j in range(sub):
                e = jnp.exp(jnp.minimum(b_i - b_i[j:j + 1, :], 0.0))
                colv = jnp.sum(q_i * k_i[j:j + 1, :] * e, axis=1, keepdims=True)
                att = jnp.where((lane_j == r0 + j) & (sub_i >= j), colv, att)
            if sb > 0:
                ref_b = b[r0 - 1:r0, :]
                q_t = q_i * jnp.exp(b_i - ref_b)
                k_t = k * jnp.exp(jnp.minimum(ref_b - b, 0.0))
                off = lax.dot_general(q_t.astype(BF16), k_t.astype(BF16), (((1,), (1,)), ((), ())),
                                      preferred_element_type=F32)
                att = jnp.where(lane_j < r0, off, att)
            blocks.append(att)
        att_full = jnp.concatenate(blocks, axis=0)
        o = o + jnp.dot(att_full.astype(BF16), v, preferred_element_type=F32)
        raw_s[pl.ds(s0, ch), hh * dv:(hh + 1) * dv] = o

        b_last = b[ch - 1:ch, :]
        k_d = k * jnp.exp(b_last - b)
        upd = jnp.dot(k_d.T.astype(BF16), v, preferred_element_type=F32)
        decay_col = lax.dot_general(eye, jnp.broadcast_to(jnp.exp(b_last), (SUBLANES, dk)),
                                    (((1,), (1,)), ((), ())), precision=HI,
                                    preferred_element_type=F32)[:, 0:1]
        st_s[hh] = decay_col * st + upd

    lax.fori_loop(0, seq // ch, chunk, 0)

    for hh in range(GLA_HEADS_PER_STEP):
        o = raw_s[:, hh * dv:(hh + 1) * dv]
        rms = o * lax.rsqrt(jnp.mean(o * o, axis=-1, keepdims=True) + LN_EPS) * ng_ref[...]
        gg = g_ref[0, :, hh * dv:(hh + 1) * dv]
        o_ref[0, :, hh * dv:(hh + 1) * dv] = rms * (gg * jax.nn.sigmoid(gg))


def _gla_core(proj, w_up_pad, b_gk, norm_g):
    B, S, _ = proj.shape
    H, hps = C_HEADS, GLA_HEADS_PER_STEP
    dk, dv = C_DK * hps, C_DV * hps
    assert C_DK == LANES and S % GLA_CHUNK == 0 and H % hps == 0
    k_off = C_KEY_DIM // dk
    v_off = 2 * C_KEY_DIM // dv
    g_off = (2 * C_KEY_DIM + C_VAL_DIM) // dv
    gl_off = (2 * C_KEY_DIM + 2 * C_VAL_DIM) // LANES
    return pl.pallas_call(
        functools.partial(_gla_body, seq=S),
        grid=(B, H // hps),
        in_specs=[pl.BlockSpec((1, S, dk), lambda b, h: (b, 0, h)),
                  pl.BlockSpec((1, S, dk), lambda b, h: (b, 0, k_off + h)),
                  pl.BlockSpec((1, S, dv), lambda b, h: (b, 0, v_off + h)),
                  pl.BlockSpec((1, S, dv), lambda b, h: (b, 0, g_off + h)),
                  pl.BlockSpec((1, S, LANES), lambda b, h: (b, 0, gl_off)),
                  pl.BlockSpec((LANES, dk), lambda b, h: (0, h)),
                  pl.BlockSpec((1, dk), lambda b, h: (0, h)),
                  pl.BlockSpec((1, C_DV), lambda b, h: (0, 0))],
        out_specs=pl.BlockSpec((1, S, dv), lambda b, h: (b, 0, h)),
        out_shape=jax.ShapeDtypeStruct((B, S, C_VAL_DIM), F32),
        scratch_shapes=[pltpu.VMEM((S, dk), F32), pltpu.VMEM((hps, C_DK, C_DV), F32), pltpu.VMEM((S, dv), F32)],
        compiler_params=_cparams("arbitrary", "arbitrary"),
        name="gla_core",
    )(proj, proj, proj, proj, proj, w_up_pad, b_gk.reshape(1, C_KEY_DIM), norm_g.reshape(1, C_DV))


def _router_body(h_ref, w_ref, b_ref, slot_ref, gate_ref, cnt_ref, lbase_ref, gbase_ref, run_s, w1_s, w2_s, *, tm):
    E = N_EXPERTS

    @pl.when(pl.program_id(0) == 0)
    def _():
        run_s[...] = jnp.zeros((1, E), F32)
        w = w_ref[...]
        w1 = w.astype(BF16)
        w1_s[...] = w1
        w2_s[...] = (w - w1.astype(F32)).astype(BF16)

    h = h_ref[...]
    h1 = h.astype(BF16)
    h2 = (h - h1.astype(F32)).astype(BF16)
    logits = (jnp.dot(h1, w1_s[...], preferred_element_type=F32)
              + (jnp.dot(h1, w2_s[...], preferred_element_type=F32)
                 + jnp.dot(h2, w1_s[...], preferred_element_type=F32))) + b_ref[...]
    lane = lax.broadcasted_iota(I32, (tm, E), 1)
    rest = logits
    top_val, onehot = [], []
    for _ in range(TOP_K):
        m = jnp.max(rest, axis=1, keepdims=True)
        idx = jnp.min(jnp.where(rest == m, lane, E), axis=1, keepdims=True)
        oh = lane == idx
        rest = jnp.where(oh, -jnp.inf, rest)
        top_val.append(m)
        onehot.append(oh)
    ex = [jnp.exp(v - top_val[0]) for v in top_val]
    den = ex[0] + ex[1] + ex[2] + ex[3]
    chosen = sum(oh.astype(F32) for oh in onehot)

    row = lax.broadcasted_iota(I32, (tm, tm), 0)
    col = lax.broadcasted_iota(I32, (tm, tm), 1)
    before = (row > col).astype(BF16)
    seen = jnp.dot(before, chosen.astype(BF16), preferred_element_type=F32)
    cnt = jnp.sum(chosen, axis=0, keepdims=True)
    cnt_pad = jnp.floor((cnt + (MOE_SEG - 1)) * (1.0 / MOE_SEG)) * MOE_SEG
    er = lax.broadcasted_iota(I32, (E, E), 0)
    ec = lax.broadcasted_iota(I32, (E, E), 1)
    cnt_rows = jnp.broadcast_to(cnt_pad, (SUBLANES, E))
    lbase_rows = jnp.dot(cnt_rows, (er < ec).astype(F32), precision=HI, preferred_element_type=F32)
    pos = seen + lbase_rows[0:1, :]

    lane_k = lax.broadcasted_iota(I32, (tm, TOP_K), 1)
    s_out = jnp.zeros((tm, TOP_K), I32)
    g_out = jnp.zeros((tm, TOP_K), F32)
    for kk in range(TOP_K):
        sk = jnp.sum(jnp.where(onehot[kk], pos, 0.0), axis=1, keepdims=True)
        s_out = jnp.where(lane_k == kk, sk.astype(I32), s_out)
        g_out = jnp.where(lane_k == kk, ex[kk] / den, g_out)
    slot_ref[...] = s_out
    gate_ref[...] = g_out
    cnt_ref[...] = cnt_rows.astype(I32)
    lbase_ref[...] = lbase_rows.astype(I32)
    gbase_ref[...] = jnp.broadcast_to(run_s[...], (SUBLANES, E)).astype(I32)
    run_s[...] = run_s[...] + cnt_pad


def _router(h, w, b, *, tm=MOE_TM):
    T, D = h.shape
    E = N_EXPERTS
    nt = T // tm
    tab = jax.ShapeDtypeStruct((nt * SUBLANES, E), I32)
    tab_spec = pl.BlockSpec((SUBLANES, E), lambda i: (i, 0))
    return pl.pallas_call(
        functools.partial(_router_body, tm=tm),
        grid=(nt,),
        in_specs=[pl.BlockSpec((tm, D), lambda i: (i, 0)),
                  pl.BlockSpec((D, E), lambda i: (0, 0)),
                  pl.BlockSpec((1, E), lambda i: (0, 0))],
        out_specs=[pl.BlockSpec((tm, TOP_K), lambda i: (i, 0)),
                   pl.BlockSpec((tm, TOP_K), lambda i: (i, 0)),
                   tab_spec, tab_spec, tab_spec],
        out_shape=[jax.ShapeDtypeStruct((T, TOP_K), I32),
                   jax.ShapeDtypeStruct((T, TOP_K), F32),
                   tab, tab, tab],
        scratch_shapes=[pltpu.VMEM((1, E), F32), pltpu.VMEM((D, E), BF16), pltpu.VMEM((D, E), BF16)],
        compiler_params=_cparams("arbitrary"),
        name="moe_router",
    )(h, w, b.reshape(1, E))


def _segment_copies(seg_g, seg_l, seg_n, tile, make_copy, start):
    def per_expert(e, carry):
        idx = tile * N_EXPERTS + e
        n = seg_n[idx]
        g = seg_g[idx]
        l = seg_l[idx]
        for bit in range(MOE_SEG_BITS - 1, -1, -1):
            size = MOE_SEG << bit

            @pl.when((n & size) != 0)
            def _():
                off = n & ~(2 * size - 1)
                cp = make_copy(pl.multiple_of(g + off, MOE_SEG), pl.multiple_of(l + off, MOE_SEG), size)
                if start:
                    cp.start()
                else:
                    cp.wait()
        return carry

    lax.fori_loop(0, N_EXPERTS, per_expert, 0)


def _dispatch_body(segg_ref, segl_ref, segn_ref, zf_ref, slot_t_ref, h_ref, xs_ref, stage, zbuf, sem, *, tm, bm):
    i = pl.program_id(0)

    @pl.when(i == 0)
    def _():
        zbuf[...] = jnp.zeros(zbuf.shape, F32)

        def zero_copy(j):
            return pltpu.make_async_copy(zbuf, xs_ref.at[pl.ds(pl.multiple_of(zf_ref[j], MOE_SEG), bm), :],
                                         sem.at[0])

        def start(j, carry):
            @pl.when(zf_ref[j] >= 0)
            def _():
                zero_copy(j).start()
            return carry

        def wait(j, carry):
            @pl.when(zf_ref[j] >= 0)
            def _():
                zero_copy(j).wait()
            return carry

        lax.fori_loop(0, zf_ref.shape[0], start, 0)
        lax.fori_loop(0, zf_ref.shape[0], wait, 0)

    slot_i = i % 2
    hb = h_ref[...].astype(BF16)
    for c in range(MOE_STAGE // MOE_PERM_CHUNK):
        r0 = c * MOE_PERM_CHUNK
        rows = lax.broadcasted_iota(I32, (MOE_PERM_CHUNK, tm), 0) + r0
        hit = rows == slot_t_ref[0:1, :]
        for kk in range(1, TOP_K):
            hit = hit | (rows == slot_t_ref[kk:kk + 1, :])
        perm = jnp.where(hit, 1.0, 0.0).astype(BF16)
        stage[slot_i, r0:r0 + MOE_PERM_CHUNK, :] = jnp.dot(perm, hb, preferred_element_type=F32)

    def copier(buf):
        def make_copy(g, l, size):
            return pltpu.make_async_copy(stage.at[buf, pl.ds(l, size), :], xs_ref.at[pl.ds(g, size), :],
                                         sem.at[buf])
        return make_copy

    _segment_copies(segg_ref, segl_ref, segn_ref, i, copier(slot_i), True)

    @pl.when(i > 0)
    def _():
        _segment_copies(segg_ref, segl_ref, segn_ref, i - 1, copier(1 - slot_i), False)

    @pl.when(i == pl.num_programs(0) - 1)
    def _():
        _segment_copies(segg_ref, segl_ref, segn_ref, i, copier(slot_i), False)


def _dispatch(seg_g, seg_l, seg_n, zero_rows, slot_t, h, n_rows, *, tm=MOE_TM, bm=MOE_BM):
    T, D = h.shape
    grid_spec = pltpu.PrefetchScalarGridSpec(
        num_scalar_prefetch=4,
        grid=(T // tm,),
        in_specs=[pl.BlockSpec((TOP_K, tm), lambda i, *_: (0, i)),
                  pl.BlockSpec((tm, D), lambda i, *_: (i, 0))],
        out_specs=pl.BlockSpec(memory_space=pl.ANY),
        scratch_shapes=[pltpu.VMEM((2, MOE_STAGE, D), F32), pltpu.VMEM((bm, D), F32),
                        pltpu.SemaphoreType.DMA((2,))],
    )
    return pl.pallas_call(
        functools.partial(_dispatch_body, tm=tm, bm=bm),
        grid_spec=grid_spec,
        out_shape=jax.ShapeDtypeStruct((n_rows, D), F32),
        compiler_params=pltpu.CompilerParams(dimension_semantics=("arbitrary",), vmem_limit_bytes=VMEM_LIMIT,
                                             has_side_effects=True),
        name="moe_dispatch",
    )(seg_g, seg_l, seg_n, zero_rows, slot_t, h)


def _expert_body(be_ref, nb_ref, nv_ref, xs_ref, wgu_ref, bgu_ref, wd_ref, bd_ref, ys_ref, wgu_s, wd_s, *, bm):
    i = pl.program_id(0)
    prev = be_ref[jnp.maximum(i - 1, 0)]
    live = i < nb_ref[0]
    half = bm // 2
    full = nv_ref[i] > half

    @pl.when(live & ((i == 0) | (be_ref[i] != prev)))
    def _():
        wgu_s[...] = wgu_ref[...].astype(BF16)
        wd_s[...] = wd_ref[...].astype(BF16)

    def mlp(r0, rows):
        hh = jnp.dot(xs_ref[r0:r0 + rows, :].astype(BF16), wgu_s[...], preferred_element_type=F32) + bgu_ref[...]
        d_e = hh.shape[1] // 2
        h_gate = jnp.minimum(hh[:, :d_e], SWIGLU_LIMIT)
        h_up = jnp.clip(hh[:, d_e:], -SWIGLU_LIMIT, SWIGLU_LIMIT)
        act = h_gate * jax.nn.sigmoid(SWIGLU_ALPHA * h_gate) * (h_up + 1.0)
        ys_ref[r0:r0 + rows, :] = jnp.dot(act.astype(BF16), wd_s[...], preferred_element_type=F32) + bd_ref[...]

    @pl.when(live & full)
    def _():
        mlp(0, bm)

    @pl.when(live & jnp.logical_not(full))
    def _():
        mlp(0, half)
        ys_ref[half:bm, :] = jnp.zeros((bm - half, ys_ref.shape[1]), F32)

    @pl.when(jnp.logical_not(live))
    def _():
        ys_ref[...] = jnp.zeros(ys_ref.shape, F32)


def _experts(block_e, n_live, n_valid, xs, layer, w_gu, b_gu, w_down, b_down, *, bm=MOE_BM):
    L, E, D, D2 = w_gu.shape
    n_blocks = xs.shape[0] // bm

    def blk(i, nb):
        return jnp.maximum(jnp.minimum(i, nb[0] - 1), 0)

    grid_spec = pltpu.PrefetchScalarGridSpec(
        num_scalar_prefetch=3,
        grid=(n_blocks,),
        in_specs=[pl.BlockSpec((bm, D), lambda i, be, nb, nv: (blk(i, nb), 0)),
                  pl.BlockSpec((None, None, D, D2), lambda i, be, nb, nv: (layer, be[blk(i, nb)], 0, 0)),
                  pl.BlockSpec((None, None, 1, D2), lambda i, be, nb, nv: (layer, be[blk(i, nb)], 0, 0)),
                  pl.BlockSpec((None, None, D2 // 2, D), lambda i, be, nb, nv: (layer, be[blk(i, nb)], 0, 0)),
                  pl.BlockSpec((None, None, 1, D), lambda i, be, nb, nv: (layer, be[blk(i, nb)], 0, 0))],
        out_specs=pl.BlockSpec((bm, D), lambda i, be, nb, nv: (i, 0)),
        scratch_shapes=[pltpu.VMEM((D, D2), BF16), pltpu.VMEM((D2 // 2, D), BF16)],
    )
    return pl.pallas_call(
        functools.partial(_expert_body, bm=bm),
        grid_spec=grid_spec,
        out_shape=jax.ShapeDtypeStruct(xs.shape, F32),
        compiler_params=_cparams("arbitrary"),
        name="moe_experts",
    )(block_e, n_live, n_valid, xs, w_gu, b_gu.reshape(L, E, 1, D2), w_down, b_down.reshape(L, E, 1, D))


def _combine_body(segg_ref, segl_ref, segn_ref, ys_ref, slot_ref, gate_ref, h_ref, g_ref, b_ref, o_ref,
                  stage, mix_s, sem, *, tm):
    i = pl.program_id(0)

    slot_i = i % 2

    def copier(buf):
        def make_copy(g, l, size):
            return pltpu.make_async_copy(ys_ref.at[pl.ds(g, size), :], stage.at[buf, pl.ds(l, size), :],
                                         sem.at[buf])
        return make_copy

    @pl.when(i == 0)
    def _():
        stage[...] = jnp.zeros(stage.shape, F32)
        _segment_copies(segg_ref, segl_ref, segn_ref, i, copier(slot_i), True)

    @pl.when(i + 1 < pl.num_programs(0))
    def _():
        _segment_copies(segg_ref, segl_ref, segn_ref, i + 1, copier(1 - slot_i), True)

    slot = slot_ref[...]
    gate = gate_ref[...]
    for c in range(MOE_STAGE // MOE_PERM_CHUNK):
        c0 = c * MOE_PERM_CHUNK
        cols = lax.broadcasted_iota(I32, (tm, MOE_PERM_CHUNK), 1) + c0
        w = jnp.zeros((tm, MOE_PERM_CHUNK), F32)
        for kk in range(TOP_K):
            w = jnp.where(cols == slot[:, kk:kk + 1], gate[:, kk:kk + 1], w)
        mix_s[:, c0:c0 + MOE_PERM_CHUNK] = w.astype(BF16)

    _segment_copies(segg_ref, segl_ref, segn_ref, i, copier(slot_i), False)
    y = jnp.dot(mix_s[...], stage[slot_i].astype(BF16), preferred_element_type=F32)
    o_ref[...] = _ln(ALPHA * h_ref[...] + y, g_ref[...], b_ref[...])


def _combine(seg_g, seg_l, seg_n, ys, slot, gate, h, g, b, *, tm=MOE_TM):
    T, D = h.shape
    grid_spec = pltpu.PrefetchScalarGridSpec(
        num_scalar_prefetch=3,
        grid=(T // tm,),
        in_specs=[pl.BlockSpec(memory_space=pl.ANY),
                  pl.BlockSpec((tm, TOP_K), lambda i, *_: (i, 0)),
                  pl.BlockSpec((tm, TOP_K), lambda i, *_: (i, 0)),
                  pl.BlockSpec((tm, D), lambda i, *_: (i, 0)),
                  pl.BlockSpec((1, D), lambda i, *_: (0, 0)),
                  pl.BlockSpec((1, D), lambda i, *_: (0, 0))],
        out_specs=pl.BlockSpec((tm, D), lambda i, *_: (i, 0)),
        scratch_shapes=[pltpu.VMEM((2, MOE_STAGE, D), F32), pltpu.VMEM((tm, MOE_STAGE), BF16),
                        pltpu.SemaphoreType.DMA((2,))],
    )
    return pl.pallas_call(
        functools.partial(_combine_body, tm=tm),
        grid_spec=grid_spec,
        out_shape=jax.ShapeDtypeStruct((T, D), F32),
        compiler_params=_cparams("arbitrary"),
        name="moe_combine",
    )(seg_g, seg_l, seg_n, ys, slot, gate, h, g.reshape(1, D), b.reshape(1, D))


def _moe_block(h, layer, w_router, b_router, w_gu, b_gu, w_down, b_down, g, b):
    T = h.shape[0]
    bm, tm, E = MOE_BM, MOE_TM, N_EXPERTS
    nt = T // tm
    slot, gate, cnt, lbase, gbase = _router(h, w_router, b_router)
    cnt, lbase, gbase = cnt[::SUBLANES], lbase[::SUBLANES], gbase[::SUBLANES]
    total = gbase[-1] + cnt[-1]
    padded = (total + bm - 1) // bm * bm
    pend = jnp.cumsum(padded)
    pstart = pend - padded
    seg_g = (pstart[None, :] + gbase).reshape(-1).astype(I32)
    seg_l = lbase.reshape(-1)
    seg_n = cnt.reshape(-1)
    n_blocks = -(-(T * TOP_K + nt * E * (MOE_SEG - 1)) // bm) + E
    block_start = jnp.arange(n_blocks, dtype=I32) * bm
    block_e = jnp.minimum(jnp.sum((pend[None, :] <= block_start[:, None]).astype(I32), axis=1), E - 1)
    n_live = (pend[-1:] // bm).astype(I32)
    n_valid = jnp.clip((pstart + total)[block_e] - block_start, 0, bm).astype(I32)
    last_blk = jnp.where(padded > 0, pend - bm, -1)
    tail_blk = n_live[0] + jnp.arange(n_blocks - T * TOP_K // bm, dtype=I32)
    tail_blk = jnp.where(tail_blk < n_blocks, tail_blk * bm, -1)
    zero_rows = jnp.concatenate([last_blk, tail_blk]).astype(I32)
    xs = _dispatch(seg_g, seg_l, seg_n, zero_rows, slot.T, h, n_blocks * bm)
    ys = _experts(block_e, n_live, n_valid, xs, layer, w_gu, b_gu, w_down, b_down)
    return _combine(seg_g, seg_l, seg_n, ys, slot, gate, h, g, b)


def kernel(x, rel_bias, a_w_in, a_w_out, b_w_in, b_ln_g, b_ln_b, b_w_s, b_b_s, b_w_out,
           c_w_in, c_w_gk_up, c_b_gk, c_norm_g, c_w_out, ln_g, ln_b,
           moe_w_router, moe_b_router, moe_w_gate_up, moe_b_gate_up, moe_w_down, moe_b_down):
    B, S, D = x.shape
    T = B * S
    h = x.reshape(T, D)
    tables = _bias_tables(rel_bias)
    for i in range(DEPTH):
        j = i // N_MIXERS
        mixer = i % N_MIXERS
        if mixer == 0:
            q_fold = jnp.where(jnp.arange(3 * D) < D, LOG2E * A_DH ** -0.5, 1.0).astype(F32)
            qkv = _matmul(h, (a_w_in[j] * q_fold).astype(BF16), out_dtype=BF16)
            o = _moba_attention(qkv.reshape(B, S, 3 * D), rel_bias, tables)
            h = _matmul_res_ln(o.reshape(T, D), a_w_out[j], h, ln_g[i, 0], ln_b[i, 0])
        elif mixer == 1:
            z = _matmul(h, b_w_in[j].astype(BF16), act="gelu", out_dtype=BF16)
            h = _gmlp_gate_out(z, h, b_ln_g[j], b_ln_b[j], b_w_s[j], b_b_s[j], b_w_out[j],
                               ln_g[i, 0], ln_b[i, 0])
        else:
            pad = LANES - C_GATE_RANK
            w_in = jnp.pad(c_w_in[j], ((0, 0), (0, pad)))
            w_up = jnp.pad(c_w_gk_up[j], ((0, pad), (0, 0)))
            proj = _matmul(h, w_in.astype(BF16), tn=640)
            o = _gla_core(proj.reshape(B, S, C_IN_WIDTH + pad), w_up, c_b_gk[j], c_norm_g[j])
            h = _matmul_res_ln(o.reshape(T, D), c_w_out[j], h, ln_g[i, 0], ln_b[i, 0])
        h = _moe_block(h, i, moe_w_router[i], moe_b_router[i], moe_w_gate_up, moe_b_gate_up,
                       moe_w_down, moe_b_down, ln_g[i, 1], ln_b[i, 1])
    return h.reshape(B, S, D)
```

```python
import functools
import math

import numpy as np
import jax
import jax.numpy as jnp
from jax import lax
from jax.experimental import pallas as pl
from jax.experimental.pallas import tpu as pltpu

F32 = jnp.float32
BF16 = jnp.bfloat16
I32 = jnp.int32
HI = lax.Precision.HIGHEST

D_MODEL = 1024
DEPTH = 4
N_MIXERS = 3
ALPHA = (2.0 * DEPTH) ** 0.25
LN_EPS = 1e-5
LOG2E = math.log2(math.e)

A_HEADS = 8
A_DH = D_MODEL // A_HEADS
MOBA_BLOCK = 256
MOBA_TOPK = 3
REL_BUCKETS = 32
REL_MAX_DIST = 128

B_GROUPS = 8
B_WIDTH = 2 * D_MODEL
B_CHUNK = 128
B_GW = B_WIDTH // B_GROUPS

C_HEADS = 4
C_KEY_DIM = D_MODEL // 2
C_VAL_DIM = D_MODEL
C_DK = C_KEY_DIM // C_HEADS
C_DV = C_VAL_DIM // C_HEADS
C_GATE_RANK = 16
C_GATE_NORMALIZER = 16.0
C_IN_WIDTH = 2 * C_KEY_DIM + 2 * C_VAL_DIM + C_GATE_RANK

N_EXPERTS = 32
TOP_K = 4
SWIGLU_LIMIT = 7.0
SWIGLU_ALPHA = 1.702

LANES = 128
SUBLANES = 8
VMEM_LIMIT = 56 * 1024 * 1024

GLA_CHUNK = 128
GLA_SUB = 16
GLA_HEADS_PER_STEP = 2
MOE_BM = 512
MOE_TM = 512
MOE_SEG = SUBLANES
MOE_SEG_BITS = (MOE_TM // MOE_SEG).bit_length()
MOE_STAGE = MOE_TM * TOP_K + N_EXPERTS * MOE_SEG
MOE_PERM_CHUNK = 256
MOE_PACK_W = D_MODEL // 2


def _cparams(*sem):
    return pltpu.CompilerParams(dimension_semantics=sem, vmem_limit_bytes=VMEM_LIMIT)


def _ln(x, g, b):
    mu = jnp.mean(x, axis=-1, keepdims=True)
    xc = x - mu
    var = jnp.mean(xc * xc, axis=-1, keepdims=True)
    return xc * lax.rsqrt(var + LN_EPS) * g + b


def _mm_body(x_ref, w_ref, o_ref, *, act, tn):
    x = x_ref[...].astype(BF16)
    for n0 in range(0, w_ref.shape[1], tn):
        acc = jnp.dot(x, w_ref[:, n0:n0 + tn], preferred_element_type=F32)
        if act == "gelu":
            acc = 0.5 * acc * (1.0 + lax.erf(acc * (2.0 ** -0.5)))
        o_ref[:, n0:n0 + tn] = acc.astype(o_ref.dtype)


def _matmul(x, w, *, act=None, tm=512, tn=512, out_dtype=F32):
    M, K = x.shape
    N = w.shape[1]
    assert M % tm == 0 and N % tn == 0
    return pl.pallas_call(
        functools.partial(_mm_body, act=act, tn=tn),
        grid=(M // tm,),
        in_specs=[pl.BlockSpec((tm, K), lambda i: (i, 0)),
                  pl.BlockSpec((K, N), lambda i: (0, 0))],
        out_specs=pl.BlockSpec((tm, N), lambda i: (i, 0)),
        out_shape=jax.ShapeDtypeStruct((M, N), out_dtype),
        compiler_params=_cparams("arbitrary"),
        name="proj_matmul",
    )(x, w)


def _mm_res_ln_body(x_ref, w_ref, h_ref, g_ref, b_ref, o_ref, wb_ref):
    @pl.when(pl.program_id(0) == 0)
    def _():
        wb_ref[...] = w_ref[...].astype(BF16)

    t = jnp.dot(x_ref[...].astype(BF16), wb_ref[...], preferred_element_type=F32)
    o_ref[...] = _ln(ALPHA * h_ref[...] + t, g_ref[...], b_ref[...])


def _matmul_res_ln(x, w, h, g, b, *, tm=512):
    M, K = x.shape
    N = w.shape[1]
    return pl.pallas_call(
        _mm_res_ln_body,
        grid=(M // tm,),
        in_specs=[pl.BlockSpec((tm, K), lambda i: (i, 0)),
                  pl.BlockSpec((K, N), lambda i: (0, 0)),
                  pl.BlockSpec((tm, N), lambda i: (i, 0)),
                  pl.BlockSpec((1, N), lambda i: (0, 0)),
                  pl.BlockSpec((1, N), lambda i: (0, 0))],
        out_specs=pl.BlockSpec((tm, N), lambda i: (i, 0)),
        out_shape=jax.ShapeDtypeStruct((M, N), F32),
        scratch_shapes=[pltpu.VMEM((K, N), BF16)],
        compiler_params=_cparams("arbitrary"),
        name="outproj_res_ln",
    )(x, w, h, g.reshape(1, N), b.reshape(1, N))


def _t5_bucket_lower_bounds():
    n = np.arange(0, 4 * REL_MAX_DIST, dtype=np.int64)
    max_exact = REL_BUCKETS // 2
    nf = np.maximum(n, 1).astype(np.float32)
    large = max_exact + (np.log(nf / np.float32(max_exact)) / np.float32(math.log(REL_MAX_DIST / max_exact))
                         * np.float32(REL_BUCKETS - max_exact)).astype(np.int32)
    large = np.minimum(large, REL_BUCKETS - 1)
    bucket = np.where(n < max_exact, n, large)
    assert np.all(np.diff(bucket) >= 0) and bucket[-1] == REL_BUCKETS - 1
    return [int(np.argmax(bucket >= b)) for b in range(REL_BUCKETS)]


_BUCKET_LO = _t5_bucket_lower_bounds()


def _bias_table_body(rel_ref, o_ref):
    blk = MOBA_BLOCK
    row = lax.broadcasted_iota(I32, (blk, blk), 0)
    col = lax.broadcasted_iota(I32, (blk, blk), 1)
    for t in range(2):
        d = row - col + t * blk
        for h in range(A_HEADS):
            val = jnp.full((blk, blk), rel_ref[REL_BUCKETS - 1, h], F32)
            for b in range(REL_BUCKETS - 2, -1, -1):
                val = jnp.where(d < _BUCKET_LO[b + 1], rel_ref[b, h], val)
            val = val * LOG2E
            if t == 0:
                val = jnp.where(d >= 0, val, -jnp.inf)
            o_ref[h, t] = val


def _bias_tables(rel_bias):
    return pl.pallas_call(
        _bias_table_body,
        in_specs=[pl.BlockSpec(memory_space=pltpu.SMEM)],
        out_shape=jax.ShapeDtypeStruct((A_HEADS, 2, MOBA_BLOCK, MOBA_BLOCK), F32),
        compiler_params=pltpu.CompilerParams(vmem_limit_bytes=VMEM_LIMIT),
        name="rel_bias_tables",
    )(rel_bias)


def _moba_body(rel_ref, q_ref, k_ref, v_ref, tb_ref, o_ref, *, seq):
    blk, dh = MOBA_BLOCK, A_DH
    nb = seq // blk
    h = pl.program_id(1)
    kmean = jnp.mean(k_ref[0].astype(F32).reshape(nb, blk, dh), axis=1)
    kmean_pad = jnp.concatenate([kmean, jnp.zeros((LANES - nb, dh), F32)], axis=0)
    b_far = rel_ref[REL_BUCKETS - 1, h] * LOG2E
    lane = lax.broadcasted_iota(I32, (blk, LANES), 1)
    for c in range(nb):
        q = q_ref[0, c * blk:(c + 1) * blk, :]
        n_keys = (c + 1) * blk
        s = lax.dot_general(q, k_ref[0, 0:n_keys, :], (((1,), (1,)), ((), ())),
                            preferred_element_type=F32)
        if c > MOBA_TOPK:
            gate = lax.dot_general(q.astype(F32), kmean_pad, (((1,), (1,)), ((), ())),
                                   precision=HI, preferred_element_type=F32)
            gate = jnp.where(lane < c, gate, -jnp.inf)
        pieces = []
        for n in range(c):
            shift = b_far
            if c > MOBA_TOPK:
                gn = gate[:, n:n + 1]
                beats = (gate > gn) | ((gate == gn) & (lane < n))
                rank = jnp.sum(beats.astype(F32), axis=1, keepdims=True)
                shift = jnp.where(rank < MOBA_TOPK, b_far, -jnp.inf)
            if n == c - 1:
                shift = tb_ref[0, 1] + (shift - b_far)
            pieces.append(s[:, n * blk:(n + 1) * blk] + shift)
        pieces.append(s[:, c * blk:(c + 1) * blk] + tb_ref[0, 0])
        logits = jnp.concatenate(pieces, axis=1) if c > 0 else pieces[0]
        m = jnp.max(logits, axis=1, keepdims=True)
        p = jnp.exp2(logits - m)
        denom = jnp.sum(p, axis=1, keepdims=True)
        o = jnp.dot(p.astype(BF16), v_ref[0, 0:n_keys, :], preferred_element_type=F32)
        o_ref[0, c * blk:(c + 1) * blk, :] = o / denom


def _moba_attention(qkv, rel_bias, tables):
    B, S, _ = qkv.shape
    H, dh = A_HEADS, A_DH
    assert S % MOBA_BLOCK == 0 and S // MOBA_BLOCK <= SUBLANES
    return pl.pallas_call(
        functools.partial(_moba_body, seq=S),
        grid=(B, H),
        in_specs=[pl.BlockSpec(memory_space=pltpu.SMEM),
                  pl.BlockSpec((1, S, dh), lambda b, h: (b, 0, h)),
                  pl.BlockSpec((1, S, dh), lambda b, h: (b, 0, H + h)),
                  pl.BlockSpec((1, S, dh), lambda b, h: (b, 0, 2 * H + h)),
                  pl.BlockSpec((1, 2, MOBA_BLOCK, MOBA_BLOCK), lambda b, h: (h, 0, 0, 0))],
        out_specs=pl.BlockSpec((1, S, dh), lambda b, h: (b, 0, h)),
        out_shape=jax.ShapeDtypeStruct((B, S, H * dh), F32),
        compiler_params=_cparams("arbitrary", "arbitrary"),
        name="moba_attention",
    )(rel_bias, qkv, qkv, qkv, tables)


def _gmlp_body(u_ref, v_ref, h_ref, vg_ref, vb_ref, ws_ref, bs_ref, wo_ref, g_ref, b_ref, o_ref,
               wst_s, wob_s, *, tm):
    @pl.when(pl.program_id(0) == 0)
    def _():
        row = lax.broadcasted_iota(I32, (B_CHUNK, B_CHUNK), 0)
        col = lax.broadcasted_iota(I32, (B_CHUNK, B_CHUNK), 1)
        for g in range(B_GROUPS):
            wst_s[g] = jnp.where(row >= col, ws_ref[g], 0.0).astype(BF16)
        wob_s[...] = wo_ref[...].astype(BF16)

    vn = _ln(v_ref[...].astype(F32), vg_ref[...], vb_ref[...]).astype(BF16)
    u = u_ref[...].astype(F32)
    rows = []
    for c in range(tm // B_CHUNK):
        r0 = c * B_CHUNK
        cols = []
        for g in range(B_GROUPS):
            c0 = g * B_GW
            mixed = jnp.dot(wst_s[g], vn[r0:r0 + B_CHUNK, c0:c0 + B_GW],
                            preferred_element_type=F32) + bs_ref[g]
            cols.append(u[r0:r0 + B_CHUNK, c0:c0 + B_GW] * mixed)
        rows.append(jnp.concatenate(cols, axis=1))
    y = jnp.concatenate(rows, axis=0).astype(BF16)
    t = jnp.dot(y, wob_s[...], preferred_element_type=F32)
    o_ref[...] = _ln(ALPHA * h_ref[...] + t, g_ref[...], b_ref[...])


def _gmlp_gate_out(z, h, v_g, v_b, w_s, b_s, w_out, g, b, *, tm=256):
    T = z.shape[0]
    D = D_MODEL
    return pl.pallas_call(
        functools.partial(_gmlp_body, tm=tm),
        grid=(T // tm,),
        in_specs=[pl.BlockSpec((tm, B_WIDTH), lambda i: (i, 0)),
                  pl.BlockSpec((tm, B_WIDTH), lambda i: (i, 1)),
                  pl.BlockSpec((tm, D), lambda i: (i, 0)),
                  pl.BlockSpec((1, B_WIDTH), lambda i: (0, 0)),
                  pl.BlockSpec((1, B_WIDTH), lambda i: (0, 0)),
                  pl.BlockSpec((B_GROUPS, B_CHUNK, B_CHUNK), lambda i: (0, 0, 0)),
                  pl.BlockSpec((B_GROUPS, B_CHUNK, 1), lambda i: (0, 0, 0)),
                  pl.BlockSpec((B_WIDTH, D), lambda i: (0, 0)),
                  pl.BlockSpec((1, D), lambda i: (0, 0)),
                  pl.BlockSpec((1, D), lambda i: (0, 0))],
        out_specs=pl.BlockSpec((tm, D), lambda i: (i, 0)),
        out_shape=jax.ShapeDtypeStruct((T, D), F32),
        scratch_shapes=[pltpu.VMEM((B_GROUPS, B_CHUNK, B_CHUNK), BF16), pltpu.VMEM((B_WIDTH, D), BF16)],
        compiler_params=_cparams("arbitrary"),
        name="gmlp_gate_out",
    )(z, z, h, v_g.reshape(1, B_WIDTH), v_b.reshape(1, B_WIDTH), w_s, b_s.reshape(B_GROUPS, B_CHUNK, 1),
      w_out, g.reshape(1, D), b.reshape(1, D))


def _gla_body(q_ref, k_ref, v_ref, g_ref, gl_ref, wup_ref, bgk_ref, ng_ref, o_ref, la_s, st_s, raw_s, *, seq):
    ch, sub, dk, dv = GLA_CHUNK, GLA_SUB, C_DK, C_DV
    x = jnp.dot(gl_ref[0], wup_ref[...], precision=HI, preferred_element_type=F32) + bgk_ref[...]
    la_s[...] = (jnp.minimum(x, 0.0) - jnp.log1p(jnp.exp(-jnp.abs(x)))) * (1.0 / C_GATE_NORMALIZER)
    st_s[...] = jnp.zeros(st_s.shape, F32)

    row_c = lax.broadcasted_iota(I32, (ch, ch), 0)
    col_c = lax.broadcasted_iota(I32, (ch, ch), 1)
    tri = (row_c >= col_c).astype(F32)
    eye = (row_c == col_c).astype(F32)
    sub_i = lax.broadcasted_iota(I32, (sub, ch), 0)
    lane_j = lax.broadcasted_iota(I32, (sub, ch), 1)

    def chunk(c, carry):
        for hh in range(GLA_HEADS_PER_STEP):
            chunk_head(c, hh)
        return carry

    def chunk_head(c, hh):
        s0 = pl.multiple_of(c * ch, ch)
        a = la_s[pl.ds(s0, ch), hh * dk:(hh + 1) * dk]
        b = jnp.dot(tri, a, precision=HI, preferred_element_type=F32)
        q = q_ref[0, pl.ds(s0, ch), hh * dk:(hh + 1) * dk] * (dk ** -0.5)
        k = k_ref[0, pl.ds(s0, ch), hh * dk:(hh + 1) * dk]
        v = v_ref[0, pl.ds(s0, ch), hh * dv:(hh + 1) * dv].astype(BF16)
        st = st_s[hh]
        o = jnp.dot((q * jnp.exp(b)).astype(BF16), st.astype(BF16), preferred_element_type=F32)

        blocks = []
        for sb in range(ch // sub):
            r0 = sb * sub
            q_i = q[r0:r0 + sub]
            b_i = b[r0:r0 + sub]
            k_i = k[r0:r0 + sub]
            att = jnp.zeros((sub, ch), F32)
            for j in range(sub):
                e = jnp.exp(jnp.minimum(b_i - b_i[j:j + 1, :], 0.0))
                colv = jnp.sum(q_i * k_i[j:j + 1, :] * e, axis=1, keepdims=True)
                att = jnp.where((lane_j == r0 + j) & (sub_i >= j), colv, att)
            if sb > 0:
                ref_b = b[r0 - 1:r0, :]
                q_t = q_i * jnp.exp(b_i - ref_b)
                k_t = k * jnp.exp(jnp.minimum(ref_b - b, 0.0))
                off = lax.dot_general(q_t.astype(BF16), k_t.astype(BF16), (((1,), (1,)), ((), ())),
                                      preferred_element_type=F32)
                att = jnp.where(lane_j < r0, off, att)
            blocks.append(att)
        att_full = jnp.concatenate(blocks, axis=0)
        o = o + jnp.dot(att_full.astype(BF16), v, preferred_element_type=F32)
        raw_s[pl.ds(s0, ch), hh * dv:(hh + 1) * dv] = o

        b_last = b[ch - 1:ch, :]
        k_d = k * jnp.exp(b_last - b)
        upd = jnp.dot(k_d.T.astype(BF16), v, preferred_element_type=F32)
        decay_col = lax.dot_general(eye, jnp.broadcast_to(jnp.exp(b_last), (SUBLANES, dk)),
                                    (((1,), (1,)), ((), ())), precision=HI,
                                    preferred_element_type=F32)[:, 0:1]
        st_s[hh] = decay_col * st + upd

    lax.fori_loop(0, seq // ch, chunk, 0)

    for hh in range(GLA_HEADS_PER_STEP):
        o = raw_s[:, hh * dv:(hh + 1) * dv]
        rms = o * lax.rsqrt(jnp.mean(o * o, axis=-1, keepdims=True) + LN_EPS) * ng_ref[...]
        gg = g_ref[0, :, hh * dv:(hh + 1) * dv]
        o_ref[0, :, hh * dv:(hh + 1) * dv] = rms * (gg * jax.nn.sigmoid(gg))


def _gla_core(proj, w_up_pad, b_gk, norm_g):
    B, S, _ = proj.shape
    H, hps = C_HEADS, GLA_HEADS_PER_STEP
    dk, dv = C_DK * hps, C_DV * hps
    assert C_DK == LANES and S % GLA_CHUNK == 0 and H % hps == 0
    k_off = C_KEY_DIM // dk
    v_off = 2 * C_KEY_DIM // dv
    g_off = (2 * C_KEY_DIM + C_VAL_DIM) // dv
    gl_off = (2 * C_KEY_DIM + 2 * C_VAL_DIM) // LANES
    return pl.pallas_call(
        functools.partial(_gla_body, seq=S),
        grid=(B, H // hps),
        in_specs=[pl.BlockSpec((1, S, dk), lambda b, h: (b, 0, h)),
                  pl.BlockSpec((1, S, dk), lambda b, h: (b, 0, k_off + h)),
                  pl.BlockSpec((1, S, dv), lambda b, h: (b, 0, v_off + h)),
                  pl.BlockSpec((1, S, dv), lambda b, h: (b, 0, g_off + h)),
                  pl.BlockSpec((1, S, LANES), lambda b, h: (b, 0, gl_off)),
                  pl.BlockSpec((LANES, dk), lambda b, h: (0, h)),
                  pl.BlockSpec((1, dk), lambda b, h: (0, h)),
                  pl.BlockSpec((1, C_DV), lambda b, h: (0, 0))],
        out_specs=pl.BlockSpec((1, S, dv), lambda b, h: (b, 0, h)),
        out_shape=jax.ShapeDtypeStruct((B, S, C_VAL_DIM), F32),
        scratch_shapes=[pltpu.VMEM((S, dk), F32), pltpu.VMEM((hps, C_DK, C_DV), F32), pltpu.VMEM((S, dv), F32)],
        compiler_params=_cparams("arbitrary", "arbitrary"),
        name="gla_core",
    )(proj, proj, proj, proj, proj, w_up_pad, b_gk.reshape(1, C_KEY_DIM), norm_g.reshape(1, C_DV))


def _router_body(h_ref, w_ref, b_ref, slot_ref, gate_ref, cnt_ref, lbase_ref, gbase_ref, run_s, w1_s, w2_s, *, tm):
    E = N_EXPERTS

    @pl.when(pl.program_id(0) == 0)
    def _():
        run_s[...] = jnp.zeros((1, E), F32)
        w = w_ref[...]
        w1 = w.astype(BF16)
        w1_s[...] = w1
        w2_s[...] = (w - w1.astype(F32)).astype(BF16)

    h = h_ref[...]
    h1 = h.astype(BF16)
    h2 = (h - h1.astype(F32)).astype(BF16)
    logits = (jnp.dot(h1, w1_s[...], preferred_element_type=F32)
              + (jnp.dot(h1, w2_s[...], preferred_element_type=F32)
                 + jnp.dot(h2, w1_s[...], preferred_element_type=F32))) + b_ref[...]
    lane = lax.broadcasted_iota(I32, (tm, E), 1)
    rest = logits
    top_val, onehot = [], []
    for _ in range(TOP_K):
        m = jnp.max(rest, axis=1, keepdims=True)
        idx = jnp.min(jnp.where(rest == m, lane, E), axis=1, keepdims=True)
        oh = lane == idx
        rest = jnp.where(oh, -jnp.inf, rest)
        top_val.append(m)
        onehot.append(oh)
    ex = [jnp.exp(v - top_val[0]) for v in top_val]
    den = ex[0] + ex[1] + ex[2] + ex[3]
    chosen = sum(oh.astype(F32) for oh in onehot)

    row = lax.broadcasted_iota(I32, (tm, tm), 0)
    col = lax.broadcasted_iota(I32, (tm, tm), 1)
    before = (row > col).astype(BF16)
    seen = jnp.dot(before, chosen.astype(BF16), preferred_element_type=F32)
    cnt = jnp.sum(chosen, axis=0, keepdims=True)
    cnt_pad = jnp.floor((cnt + (MOE_SEG - 1)) * (1.0 / MOE_SEG)) * MOE_SEG
    er = lax.broadcasted_iota(I32, (E, E), 0)
    ec = lax.broadcasted_iota(I32, (E, E), 1)
    cnt_rows = jnp.broadcast_to(cnt_pad, (SUBLANES, E))
    lbase_rows = jnp.dot(cnt_rows, (er < ec).astype(F32), precision=HI, preferred_element_type=F32)
    pos = seen + lbase_rows[0:1, :]

    lane_k = lax.broadcasted_iota(I32, (tm, TOP_K), 1)
    s_out = jnp.zeros((tm, TOP_K), I32)
    g_out = jnp.zeros((tm, TOP_K), F32)
    for kk in range(TOP_K):
        sk = jnp.sum(jnp.where(onehot[kk], pos, 0.0), axis=1, keepdims=True)
        s_out = jnp.where(lane_k == kk, sk.astype(I32), s_out)
        g_out = jnp.where(lane_k == kk, ex[kk] / den, g_out)
    slot_ref[...] = s_out
    gate_ref[...] = g_out
    cnt_ref[...] = cnt_rows.astype(I32)
    lbase_ref[...] = lbase_rows.astype(I32)
    gbase_ref[...] = jnp.broadcast_to(run_s[...], (SUBLANES, E)).astype(I32)
    run_s[...] = run_s[...] + cnt_pad


def _router(h, w, b, *, tm=MOE_TM):
    T, D = h.shape
    E = N_EXPERTS
    nt = T // tm
    tab = jax.ShapeDtypeStruct((nt * SUBLANES, E), I32)
    tab_spec = pl.BlockSpec((SUBLANES, E), lambda i: (i, 0))
    return pl.pallas_call(
        functools.partial(_router_body, tm=tm),
        grid=(nt,),
        in_specs=[pl.BlockSpec((tm, D), lambda i: (i, 0)),
                  pl.BlockSpec((D, E), lambda i: (0, 0)),
                  pl.BlockSpec((1, E), lambda i: (0, 0))],
        out_specs=[pl.BlockSpec((tm, TOP_K), lambda i: (i, 0)),
                   pl.BlockSpec((tm, TOP_K), lambda i: (i, 0)),
                   tab_spec, tab_spec, tab_spec],
        out_shape=[jax.ShapeDtypeStruct((T, TOP_K), I32),
                   jax.ShapeDtypeStruct((T, TOP_K), F32),
                   tab, tab, tab],
        scratch_shapes=[pltpu.VMEM((1, E), F32), pltpu.VMEM((D, E), BF16), pltpu.VMEM((D, E), BF16)],
        compiler_params=_cparams("arbitrary"),
        name="moe_router",
    )(h, w, b.reshape(1, E))


def _pack_pairs(left, right):
    hi = lax.bitcast_convert_type(left, I32) & jnp.int32(-65536)
    lo = lax.shift_right_logical(lax.bitcast_convert_type(right, I32), 16)
    return hi | lo


def _unpack_pairs(word):
    left = lax.bitcast_convert_type(word & jnp.int32(-65536), F32)
    right = lax.bitcast_convert_type(lax.shift_left(word, 16), F32)
    return left.astype(BF16), right.astype(BF16)


def _segment_copies(seg_g, seg_l, seg_n, tile, make_copy, start):
    def per_expert(e, carry):
        idx = tile * N_EXPERTS + e
        n = seg_n[idx]
        g = seg_g[idx]
        l = seg_l[idx]
        for bit in range(MOE_SEG_BITS - 1, -1, -1):
            size = MOE_SEG << bit

            @pl.when((n & size) != 0)
            def _():
                off = n & ~(2 * size - 1)
                cp = make_copy(pl.multiple_of(g + off, MOE_SEG), pl.multiple_of(l + off, MOE_SEG), size)
                if start:
                    cp.start()
                else:
                    cp.wait()
        return carry

    lax.fori_loop(0, N_EXPERTS, per_expert, 0)


def _dispatch_body(segg_ref, segl_ref, segn_ref, zf_ref, slot_t_ref, h_ref, xs_ref, stage, zbuf, sem, *, tm, bm):
    i = pl.program_id(0)

    @pl.when(i == 0)
    def _():
        zbuf[...] = jnp.zeros(zbuf.shape, I32)

        def zero_copy(j):
            return pltpu.make_async_copy(zbuf, xs_ref.at[pl.ds(pl.multiple_of(zf_ref[j], MOE_SEG), bm), :],
                                         sem.at[0])

        def start(j, carry):
            @pl.when(zf_ref[j] >= 0)
            def _():
                zero_copy(j).start()
            return carry

        def wait(j, carry):
            @pl.when(zf_ref[j] >= 0)
            def _():
                zero_copy(j).wait()
            return carry

        lax.fori_loop(0, zf_ref.shape[0], start, 0)
        lax.fori_loop(0, zf_ref.shape[0], wait, 0)

    slot_i = i % 2
    hb = h_ref[...].astype(BF16)
    for c in range(MOE_STAGE // MOE_PERM_CHUNK):
        r0 = c * MOE_PERM_CHUNK
        rows = lax.broadcasted_iota(I32, (MOE_PERM_CHUNK, tm), 0) + r0
        hit = rows == slot_t_ref[0:1, :]
        for kk in range(1, TOP_K):
            hit = hit | (rows == slot_t_ref[kk:kk + 1, :])
        perm = jnp.where(hit, 1.0, 0.0).astype(BF16)
        rows_f = jnp.dot(perm, hb, preferred_element_type=F32)
        stage[slot_i, r0:r0 + MOE_PERM_CHUNK, :] = _pack_pairs(rows_f[:, :MOE_PACK_W], rows_f[:, MOE_PACK_W:])

    def copier(buf):
        def make_copy(g, l, size):
            return pltpu.make_async_copy(stage.at[buf, pl.ds(l, size), :], xs_ref.at[pl.ds(g, size), :],
                                         sem.at[buf])
        return make_copy

    _segment_copies(segg_ref, segl_ref, segn_ref, i, copier(slot_i), True)

    @pl.when(i > 0)
    def _():
        _segment_copies(segg_ref, segl_ref, segn_ref, i - 1, copier(1 - slot_i), False)

    @pl.when(i == pl.num_programs(0) - 1)
    def _():
        _segment_copies(segg_ref, segl_ref, segn_ref, i, copier(slot_i), False)


def _dispatch(seg_g, seg_l, seg_n, zero_rows, slot_t, h, n_rows, *, tm=MOE_TM, bm=MOE_BM):
    T, D = h.shape
    grid_spec = pltpu.PrefetchScalarGridSpec(
        num_scalar_prefetch=4,
        grid=(T // tm,),
        in_specs=[pl.BlockSpec((TOP_K, tm), lambda i, *_: (0, i)),
                  pl.BlockSpec((tm, D), lambda i, *_: (i, 0))],
        out_specs=pl.BlockSpec(memory_space=pl.ANY),
        scratch_shapes=[pltpu.VMEM((2, MOE_STAGE, MOE_PACK_W), I32), pltpu.VMEM((bm, MOE_PACK_W), I32),
                        pltpu.SemaphoreType.DMA((2,))],
    )
    return pl.pallas_call(
        functools.partial(_dispatch_body, tm=tm, bm=bm),
        grid_spec=grid_spec,
        out_shape=jax.ShapeDtypeStruct((n_rows, MOE_PACK_W), I32),
        compiler_params=pltpu.CompilerParams(dimension_semantics=("arbitrary",), vmem_limit_bytes=VMEM_LIMIT,
                                             has_side_effects=True),
        name="moe_dispatch",
    )(seg_g, seg_l, seg_n, zero_rows, slot_t, h)


def _expert_body(be_ref, nb_ref, nv_ref, xs_ref, wgu_ref, bgu_ref, wd_ref, bd_ref, ys_ref, wgu_s, wd_s, *, bm):
    i = pl.program_id(0)
    prev = be_ref[jnp.maximum(i - 1, 0)]
    live = i < nb_ref[0]
    half = bm // 2
    full = nv_ref[i] > half

    @pl.when(live & ((i == 0) | (be_ref[i] != prev)))
    def _():
        wgu_s[...] = wgu_ref[...].astype(BF16)
        wd_s[...] = wd_ref[...].astype(BF16)

    def mlp(r0, rows):
        x = jnp.concatenate(_unpack_pairs(xs_ref[r0:r0 + rows, :]), axis=1)
        hh = jnp.dot(x, wgu_s[...], preferred_element_type=F32) + bgu_ref[...]
        d_e = hh.shape[1] // 2
        h_gate = jnp.minimum(hh[:, :d_e], SWIGLU_LIMIT)
        h_up = jnp.clip(hh[:, d_e:], -SWIGLU_LIMIT, SWIGLU_LIMIT)
        act = h_gate * jax.nn.sigmoid(SWIGLU_ALPHA * h_gate) * (h_up + 1.0)
        y = jnp.dot(act.astype(BF16), wd_s[...], preferred_element_type=F32) + bd_ref[...]
        y = y.astype(BF16).astype(F32)
        ys_ref[r0:r0 + rows, :] = _pack_pairs(y[:, :MOE_PACK_W], y[:, MOE_PACK_W:])

    @pl.when(live & full)
    def _():
        mlp(0, bm)

    @pl.when(live & jnp.logical_not(full))
    def _():
        mlp(0, half)
        ys_ref[half:bm, :] = jnp.zeros((bm - half, ys_ref.shape[1]), I32)

    @pl.when(jnp.logical_not(live))
    def _():
        ys_ref[...] = jnp.zeros(ys_ref.shape, I32)


def _experts(block_e, n_live, n_valid, xs, layer, w_gu, b_gu, w_down, b_down, *, bm=MOE_BM):
    L, E, D, D2 = w_gu.shape
    n_blocks = xs.shape[0] // bm

    def blk(i, nb):
        return jnp.maximum(jnp.minimum(i, nb[0] - 1), 0)

    grid_spec = pltpu.PrefetchScalarGridSpec(
        num_scalar_prefetch=3,
        grid=(n_blocks,),
        in_specs=[pl.BlockSpec((bm, MOE_PACK_W), lambda i, be, nb, nv: (blk(i, nb), 0)),
                  pl.BlockSpec((None, None, D, D2), lambda i, be, nb, nv: (layer, be[blk(i, nb)], 0, 0)),
                  pl.BlockSpec((None, None, 1, D2), lambda i, be, nb, nv: (layer, be[blk(i, nb)], 0, 0)),
                  pl.BlockSpec((None, None, D2 // 2, D), lambda i, be, nb, nv: (layer, be[blk(i, nb)], 0, 0)),
                  pl.BlockSpec((None, None, 1, D), lambda i, be, nb, nv: (layer, be[blk(i, nb)], 0, 0))],
        out_specs=pl.BlockSpec((bm, MOE_PACK_W), lambda i, be, nb, nv: (i, 0)),
        scratch_shapes=[pltpu.VMEM((D, D2), BF16), pltpu.VMEM((D2 // 2, D), BF16)],
    )
    return pl.pallas_call(
        functools.partial(_expert_body, bm=bm),
        grid_spec=grid_spec,
        out_shape=jax.ShapeDtypeStruct(xs.shape, I32),
        compiler_params=_cparams("arbitrary"),
        name="moe_experts",
    )(block_e, n_live, n_valid, xs, w_gu, b_gu.reshape(L, E, 1, D2), w_down, b_down.reshape(L, E, 1, D))


def _combine_body(segg_ref, segl_ref, segn_ref, ys_ref, slot_ref, gate_ref, h_ref, g_ref, b_ref, o_ref,
                  stage, mix_s, sem, *, tm):
    i = pl.program_id(0)

    slot_i = i % 2

    def copier(buf):
        def make_copy(g, l, size):
            return pltpu.make_async_copy(ys_ref.at[pl.ds(g, size), :], stage.at[buf, pl.ds(l, size), :],
                                         sem.at[buf])
        return make_copy

    @pl.when(i == 0)
    def _():
        stage[...] = jnp.zeros(stage.shape, I32)
        _segment_copies(segg_ref, segl_ref, segn_ref, i, copier(slot_i), True)

    @pl.when(i + 1 < pl.num_programs(0))
    def _():
        _segment_copies(segg_ref, segl_ref, segn_ref, i + 1, copier(1 - slot_i), True)

    slot = slot_ref[...]
    gate = gate_ref[...]
    for c in range(MOE_STAGE // MOE_PERM_CHUNK):
        c0 = c * MOE_PERM_CHUNK
        cols = lax.broadcasted_iota(I32, (tm, MOE_PERM_CHUNK), 1) + c0
        w = jnp.zeros((tm, MOE_PERM_CHUNK), F32)
        for kk in range(TOP_K):
            w = jnp.where(cols == slot[:, kk:kk + 1], gate[:, kk:kk + 1], w)
        mix_s[:, c0:c0 + MOE_PERM_CHUNK] = w.astype(BF16)

    _segment_copies(segg_ref, segl_ref, segn_ref, i, copier(slot_i), False)
    left, right = _unpack_pairs(stage[slot_i])
    y = jnp.concatenate([jnp.dot(mix_s[...], left, preferred_element_type=F32),
                         jnp.dot(mix_s[...], right, preferred_element_type=F32)], axis=1)
    o_ref[...] = _ln(ALPHA * h_ref[...] + y, g_ref[...], b_ref[...])


def _combine(seg_g, seg_l, seg_n, ys, slot, gate, h, g, b, *, tm=MOE_TM):
    T, D = h.shape
    grid_spec = pltpu.PrefetchScalarGridSpec(
        num_scalar_prefetch=3,
        grid=(T // tm,),
        in_specs=[pl.BlockSpec(memory_space=pl.ANY),
                  pl.BlockSpec((tm, TOP_K), lambda i, *_: (i, 0)),
                  pl.BlockSpec((tm, TOP_K), lambda i, *_: (i, 0)),
                  pl.BlockSpec((tm, D), lambda i, *_: (i, 0)),
                  pl.BlockSpec((1, D), lambda i, *_: (0, 0)),
                  pl.BlockSpec((1, D), lambda i, *_: (0, 0))],
        out_specs=pl.BlockSpec((tm, D), lambda i, *_: (i, 0)),
        scratch_shapes=[pltpu.VMEM((2, MOE_STAGE, MOE_PACK_W), I32), pltpu.VMEM((tm, MOE_STAGE), BF16),
                        pltpu.SemaphoreType.DMA((2,))],
    )
    return pl.pallas_call(
        functools.partial(_combine_body, tm=tm),
        grid_spec=grid_spec,
        out_shape=jax.ShapeDtypeStruct((T, D), F32),
        compiler_params=_cparams("arbitrary"),
        name="moe_combine",
    )(seg_g, seg_l, seg_n, ys, slot, gate, h, g.reshape(1, D), b.reshape(1, D))


def _moe_block(h, layer, w_router, b_router, w_gu, b_gu, w_down, b_down, g, b):
    T = h.shape[0]
    bm, tm, E = MOE_BM, MOE_TM, N_EXPERTS
    nt = T // tm
    slot, gate, cnt, lbase, gbase = _router(h, w_router, b_router)
    cnt, lbase, gbase = cnt[::SUBLANES], lbase[::SUBLANES], gbase[::SUBLANES]
    total = gbase[-1] + cnt[-1]
    padded = (total + bm - 1) // bm * bm
    pend = jnp.cumsum(padded)
    pstart = pend - padded
    seg_g = (pstart[None, :] + gbase).reshape(-1).astype(I32)
    seg_l = lbase.reshape(-1)
    seg_n = cnt.reshape(-1)
    n_blocks = -(-(T * TOP_K + nt * E * (MOE_SEG - 1)) // bm) + E
    block_start = jnp.arange(n_blocks, dtype=I32) * bm
    block_e = jnp.minimum(jnp.sum((pend[None, :] <= block_start[:, None]).astype(I32), axis=1), E - 1)
    n_live = (pend[-1:] // bm).astype(I32)
    n_valid = jnp.clip((pstart + total)[block_e] - block_start, 0, bm).astype(I32)
    last_blk = jnp.where(padded > 0, pend - bm, -1)
    tail_blk = n_live[0] + jnp.arange(n_blocks - T * TOP_K // bm, dtype=I32)
    tail_blk = jnp.where(tail_blk < n_blocks, tail_blk * bm, -1)
    zero_rows = jnp.concatenate([last_blk, tail_blk]).astype(I32)
    xs = _dispatch(seg_g, seg_l, seg_n, zero_rows, slot.T, h, n_blocks * bm)
    ys = _experts(block_e, n_live, n_valid, xs, layer, w_gu, b_gu, w_down, b_down)
    return _combine(seg_g, seg_l, seg_n, ys, slot, gate, h, g, b)


def kernel(x, rel_bias, a_w_in, a_w_out, b_w_in, b_ln_g, b_ln_b, b_w_s, b_b_s, b_w_out,
           c_w_in, c_w_gk_up, c_b_gk, c_norm_g, c_w_out, ln_g, ln_b,
           moe_w_router, moe_b_router, moe_w_gate_up, moe_b_gate_up, moe_w_down, moe_b_down):
    B, S, D = x.shape
    T = B * S
    h = x.reshape(T, D)
    tables = _bias_tables(rel_bias)
    for i in range(DEPTH):
        j = i // N_MIXERS
        mixer = i % N_MIXERS
        if mixer == 0:
            q_fold = jnp.where(jnp.arange(3 * D) < D, LOG2E * A_DH ** -0.5, 1.0).astype(F32)
            qkv = _matmul(h, (a_w_in[j] * q_fold).astype(BF16), out_dtype=BF16)
            o = _moba_attention(qkv.reshape(B, S, 3 * D), rel_bias, tables)
            h = _matmul_res_ln(o.reshape(T, D), a_w_out[j], h, ln_g[i, 0], ln_b[i, 0])
        elif mixer == 1:
            z = _matmul(h, b_w_in[j].astype(BF16), act="gelu", out_dtype=BF16)
            h = _gmlp_gate_out(z, h, b_ln_g[j], b_ln_b[j], b_w_s[j], b_b_s[j], b_w_out[j],
                               ln_g[i, 0], ln_b[i, 0])
        else:
            pad = LANES - C_GATE_RANK
            w_in = jnp.pad(c_w_in[j], ((0, 0), (0, pad)))
            w_up = jnp.pad(c_w_gk_up[j], ((0, pad), (0, 0)))
            proj = _matmul(h, w_in.astype(BF16), tn=640)
            o = _gla_core(proj.reshape(B, S, C_IN_WIDTH + pad), w_up, c_b_gk[j], c_norm_g[j])
            h = _matmul_res_ln(o.reshape(T, D), c_w_out[j], h, ln_g[i, 0], ln_b[i, 0])
        h = _moe_block(h, i, moe_w_router[i], moe_b_router[i], moe_w_gate_up, moe_b_gate_up,
                       moe_w_down, moe_b_down, ln_g[i, 1], ln_b[i, 1])
    return h.reshape(B, S, D)
```

```python
import functools
import math

import numpy as np
import jax
import jax.numpy as jnp
from jax import lax
from jax.experimental import pallas as pl
from jax.experimental.pallas import tpu as pltpu

F32 = jnp.float32
BF16 = jnp.bfloat16
I32 = jnp.int32
HI = lax.Precision.HIGHEST

D_MODEL = 1024
DEPTH = 4
N_MIXERS = 3
ALPHA = (2.0 * DEPTH) ** 0.25
LN_EPS = 1e-5
LOG2E = math.log2(math.e)

A_HEADS = 8
A_DH = D_MODEL // A_HEADS
MOBA_BLOCK = 256
MOBA_TOPK = 3
REL_BUCKETS = 32
REL_MAX_DIST = 128

B_GROUPS = 8
B_WIDTH = 2 * D_MODEL
B_CHUNK = 128
B_GW = B_WIDTH // B_GROUPS

C_HEADS = 4
C_KEY_DIM = D_MODEL // 2
C_VAL_DIM = D_MODEL
C_DK = C_KEY_DIM // C_HEADS
C_DV = C_VAL_DIM // C_HEADS
C_GATE_RANK = 16
C_GATE_NORMALIZER = 16.0
C_IN_WIDTH = 2 * C_KEY_DIM + 2 * C_VAL_DIM + C_GATE_RANK

N_EXPERTS = 32
TOP_K = 4
SWIGLU_LIMIT = 7.0
SWIGLU_ALPHA = 1.702

LANES = 128
SUBLANES = 8
VMEM_LIMIT = 56 * 1024 * 1024

GLA_CHUNK = 128
GLA_SUB = 16
GLA_HEADS_PER_STEP = 2
MOE_BM = 1024
MOE_SUB = 256
MOE_TM = 512
MOE_SEG = SUBLANES
MOE_SEG_BITS = (MOE_TM // MOE_SEG).bit_length()
MOE_STAGE = MOE_TM * TOP_K + N_EXPERTS * MOE_SEG
MOE_PERM_CHUNK = 256
MOE_PACK_W = D_MODEL // 2


def _cparams(*sem):
    return pltpu.CompilerParams(dimension_semantics=sem, vmem_limit_bytes=VMEM_LIMIT)


def _ln(x, g, b):
    mu = jnp.mean(x, axis=-1, keepdims=True)
    xc = x - mu
    var = jnp.mean(xc * xc, axis=-1, keepdims=True)
    return xc * lax.rsqrt(var + LN_EPS) * g + b


def _mm_body(x_ref, w_ref, o_ref, *, act, tn):
    x = x_ref[...].astype(BF16)
    for n0 in range(0, w_ref.shape[1], tn):
        acc = jnp.dot(x, w_ref[:, n0:n0 + tn], preferred_element_type=F32)
        if act == "gelu":
            acc = 0.5 * acc * (1.0 + lax.erf(acc * (2.0 ** -0.5)))
        o_ref[:, n0:n0 + tn] = acc.astype(o_ref.dtype)


def _matmul(x, w, *, act=None, tm=512, tn=512, out_dtype=F32):
    M, K = x.shape
    N = w.shape[1]
    assert M % tm == 0 and N % tn == 0
    return pl.pallas_call(
        functools.partial(_mm_body, act=act, tn=tn),
        grid=(M // tm,),
        in_specs=[pl.BlockSpec((tm, K), lambda i: (i, 0)),
                  pl.BlockSpec((K, N), lambda i: (0, 0))],
        out_specs=pl.BlockSpec((tm, N), lambda i: (i, 0)),
        out_shape=jax.ShapeDtypeStruct((M, N), out_dtype),
        compiler_params=_cparams("arbitrary"),
        name="proj_matmul",
    )(x, w)


def _mm_res_ln_body(x_ref, w_ref, h_ref, g_ref, b_ref, o_ref, wb_ref):
    @pl.when(pl.program_id(0) == 0)
    def _():
        wb_ref[...] = w_ref[...].astype(BF16)

    t = jnp.dot(x_ref[...].astype(BF16), wb_ref[...], preferred_element_type=F32)
    o_ref[...] = _ln(ALPHA * h_ref[...] + t, g_ref[...], b_ref[...])


def _matmul_res_ln(x, w, h, g, b, *, tm=512):
    M, K = x.shape
    N = w.shape[1]
    return pl.pallas_call(
        _mm_res_ln_body,
        grid=(M // tm,),
        in_specs=[pl.BlockSpec((tm, K), lambda i: (i, 0)),
                  pl.BlockSpec((K, N), lambda i: (0, 0)),
                  pl.BlockSpec((tm, N), lambda i: (i, 0)),
                  pl.BlockSpec((1, N), lambda i: (0, 0)),
                  pl.BlockSpec((1, N), lambda i: (0, 0))],
        out_specs=pl.BlockSpec((tm, N), lambda i: (i, 0)),
        out_shape=jax.ShapeDtypeStruct((M, N), F32),
        scratch_shapes=[pltpu.VMEM((K, N), BF16)],
        compiler_params=_cparams("arbitrary"),
        name="outproj_res_ln",
    )(x, w, h, g.reshape(1, N), b.reshape(1, N))


def _t5_bucket_lower_bounds():
    n = np.arange(0, 4 * REL_MAX_DIST, dtype=np.int64)
    max_exact = REL_BUCKETS // 2
    nf = np.maximum(n, 1).astype(np.float32)
    large = max_exact + (np.log(nf / np.float32(max_exact)) / np.float32(math.log(REL_MAX_DIST / max_exact))
                         * np.float32(REL_BUCKETS - max_exact)).astype(np.int32)
    large = np.minimum(large, REL_BUCKETS - 1)
    bucket = np.where(n < max_exact, n, large)
    assert np.all(np.diff(bucket) >= 0) and bucket[-1] == REL_BUCKETS - 1
    return [int(np.argmax(bucket >= b)) for b in range(REL_BUCKETS)]


_BUCKET_LO = _t5_bucket_lower_bounds()


def _bias_table_body(rel_ref, o_ref):
    blk = MOBA_BLOCK
    row = lax.broadcasted_iota(I32, (blk, blk), 0)
    col = lax.broadcasted_iota(I32, (blk, blk), 1)
    for t in range(2):
        d = row - col + t * blk
        for h in range(A_HEADS):
            val = jnp.full((blk, blk), rel_ref[REL_BUCKETS - 1, h], F32)
            for b in range(REL_BUCKETS - 2, -1, -1):
                val = jnp.where(d < _BUCKET_LO[b + 1], rel_ref[b, h], val)
            val = val * LOG2E
            if t == 0:
                val = jnp.where(d >= 0, val, -jnp.inf)
            o_ref[h, t] = val


def _bias_tables(rel_bias):
    return pl.pallas_call(
        _bias_table_body,
        in_specs=[pl.BlockSpec(memory_space=pltpu.SMEM)],
        out_shape=jax.ShapeDtypeStruct((A_HEADS, 2, MOBA_BLOCK, MOBA_BLOCK), F32),
        compiler_params=pltpu.CompilerParams(vmem_limit_bytes=VMEM_LIMIT),
        name="rel_bias_tables",
    )(rel_bias)


def _moba_body(rel_ref, q_ref, k_ref, v_ref, tb_ref, o_ref, s_s, p_s, acc_s, rden_s, *, seq):
    blk, dh = MOBA_BLOCK, A_DH
    nb = seq // blk
    h = pl.program_id(1)
    kmean = jnp.mean(k_ref[0].astype(F32).reshape(nb, blk, dh), axis=1)
    kmean_pad = jnp.concatenate([kmean, jnp.zeros((LANES - nb, dh), F32)], axis=0)
    b_far = rel_ref[REL_BUCKETS - 1, h] * LOG2E
    lane = lax.broadcasted_iota(I32, (blk, LANES), 1)
    ones_bf = jnp.ones((LANES, LANES), BF16)

    for n in range(nb):
        r0 = n * blk
        s_s[r0:seq, r0:r0 + blk] = lax.dot_general(q_ref[0, r0:seq, :], k_ref[0, r0:r0 + blk, :],
                                                   (((1,), (1,)), ((), ())), preferred_element_type=F32)

    for c in range(nb):
        q = q_ref[0, c * blk:(c + 1) * blk, :]
        n_keys = (c + 1) * blk
        s = s_s[c * blk:(c + 1) * blk, 0:n_keys]
        if c > MOBA_TOPK:
            gate = lax.dot_general(q.astype(F32), kmean_pad, (((1,), (1,)), ((), ())),
                                   precision=HI, preferred_element_type=F32)
            gate = jnp.where(lane < c, gate, -jnp.inf)
        pieces = []
        for n in range(c):
            shift = b_far
            if c > MOBA_TOPK:
                gn = gate[:, n:n + 1]
                beats = (gate > gn) | ((gate == gn) & (lane < n))
                rank = jnp.dot(jnp.where(beats, 1.0, 0.0).astype(BF16), ones_bf, preferred_element_type=F32)
                shift = jnp.where(rank < MOBA_TOPK, b_far, -jnp.inf)
                shift = jnp.concatenate([shift] * (blk // LANES), axis=1)
            if n == c - 1:
                shift = tb_ref[0, 1] + (shift - b_far)
            pieces.append(s[:, n * blk:(n + 1) * blk] + shift)
        pieces.append(s[:, c * blk:(c + 1) * blk] + tb_ref[0, 0])
        logits = jnp.concatenate(pieces, axis=1) if c > 0 else pieces[0]
        m = jnp.max(logits, axis=1, keepdims=True)
        p = jnp.exp2(logits - m)
        denom = jnp.sum(p, axis=1, keepdims=True)
        p_s[c * blk:(c + 1) * blk, 0:n_keys] = p.astype(BF16)
        rden_s[c * blk:(c + 1) * blk, :] = jnp.broadcast_to(1.0 / denom, (blk, dh))

    for n in range(nb):
        r0 = n * blk
        pv = jnp.dot(p_s[r0:seq, r0:r0 + blk], v_ref[0, r0:r0 + blk, :], preferred_element_type=F32)
        if n == 0:
            acc_s[...] = pv
        else:
            acc_s[r0:seq, :] = acc_s[r0:seq, :] + pv
    o_ref[0] = acc_s[...] * rden_s[...]


def _moba_attention(qkv, rel_bias, tables):
    B, S, _ = qkv.shape
    H, dh = A_HEADS, A_DH
    assert S % MOBA_BLOCK == 0 and S // MOBA_BLOCK <= SUBLANES
    return pl.pallas_call(
        functools.partial(_moba_body, seq=S),
        grid=(B, H),
        in_specs=[pl.BlockSpec(memory_space=pltpu.SMEM),
                  pl.BlockSpec((1, S, dh), lambda b, h: (b, 0, h)),
                  pl.BlockSpec((1, S, dh), lambda b, h: (b, 0, H + h)),
                  pl.BlockSpec((1, S, dh), lambda b, h: (b, 0, 2 * H + h)),
                  pl.BlockSpec((1, 2, MOBA_BLOCK, MOBA_BLOCK), lambda b, h: (h, 0, 0, 0))],
        out_specs=pl.BlockSpec((1, S, dh), lambda b, h: (b, 0, h)),
        out_shape=jax.ShapeDtypeStruct((B, S, H * dh), F32),
        scratch_shapes=[pltpu.VMEM((S, S), F32), pltpu.VMEM((S, S), BF16),
                        pltpu.VMEM((S, dh), F32), pltpu.VMEM((S, dh), F32)],
        compiler_params=_cparams("arbitrary", "arbitrary"),
        name="moba_attention",
    )(rel_bias, qkv, qkv, qkv, tables)


def _gmlp_body(u_ref, v_ref, h_ref, vg_ref, vb_ref, ws_ref, bs_ref, wo_ref, g_ref, b_ref, o_ref,
               wst_s, wob_s, *, tm):
    @pl.when(pl.program_id(0) == 0)
    def _():
        row = lax.broadcasted_iota(I32, (B_CHUNK, B_CHUNK), 0)
        col = lax.broadcasted_iota(I32, (B_CHUNK, B_CHUNK), 1)
        for g in range(B_GROUPS):
            wst_s[g] = jnp.where(row >= col, ws_ref[g], 0.0).astype(BF16)
        wob_s[...] = wo_ref[...].astype(BF16)

    vn = _ln(v_ref[...].astype(F32), vg_ref[...], vb_ref[...]).astype(BF16)
    u = u_ref[...].astype(F32)
    rows = []
    for c in range(tm // B_CHUNK):
        r0 = c * B_CHUNK
        cols = []
        for g in range(B_GROUPS):
            c0 = g * B_GW
            mixed = jnp.dot(wst_s[g], vn[r0:r0 + B_CHUNK, c0:c0 + B_GW],
                            preferred_element_type=F32) + bs_ref[g]
            cols.append(u[r0:r0 + B_CHUNK, c0:c0 + B_GW] * mixed)
        rows.append(jnp.concatenate(cols, axis=1))
    y = jnp.concatenate(rows, axis=0).astype(BF16)
    t = jnp.dot(y, wob_s[...], preferred_element_type=F32)
    o_ref[...] = _ln(ALPHA * h_ref[...] + t, g_ref[...], b_ref[...])


def _gmlp_gate_out(z, h, v_g, v_b, w_s, b_s, w_out, g, b, *, tm=256):
    T = z.shape[0]
    D = D_MODEL
    return pl.pallas_call(
        functools.partial(_gmlp_body, tm=tm),
        grid=(T // tm,),
        in_specs=[pl.BlockSpec((tm, B_WIDTH), lambda i: (i, 0)),
                  pl.BlockSpec((tm, B_WIDTH), lambda i: (i, 1)),
                  pl.BlockSpec((tm, D), lambda i: (i, 0)),
                  pl.BlockSpec((1, B_WIDTH), lambda i: (0, 0)),
                  pl.BlockSpec((1, B_WIDTH), lambda i: (0, 0)),
                  pl.BlockSpec((B_GROUPS, B_CHUNK, B_CHUNK), lambda i: (0, 0, 0)),
                  pl.BlockSpec((B_GROUPS, B_CHUNK, 1), lambda i: (0, 0, 0)),
                  pl.BlockSpec((B_WIDTH, D), lambda i: (0, 0)),
                  pl.BlockSpec((1, D), lambda i: (0, 0)),
                  pl.BlockSpec((1, D), lambda i: (0, 0))],
        out_specs=pl.BlockSpec((tm, D), lambda i: (i, 0)),
        out_shape=jax.ShapeDtypeStruct((T, D), F32),
        scratch_shapes=[pltpu.VMEM((B_GROUPS, B_CHUNK, B_CHUNK), BF16), pltpu.VMEM((B_WIDTH, D), BF16)],
        compiler_params=_cparams("arbitrary"),
        name="gmlp_gate_out",
    )(z, z, h, v_g.reshape(1, B_WIDTH), v_b.reshape(1, B_WIDTH), w_s, b_s.reshape(B_GROUPS, B_CHUNK, 1),
      w_out, g.reshape(1, D), b.reshape(1, D))


def _gla_body(q_ref, k_ref, v_ref, g_ref, gl_ref, wup_ref, bgk_ref, ng_ref, o_ref, la_s, st_s, raw_s, *, seq):
    ch, sub, dk, dv = GLA_CHUNK, GLA_SUB, C_DK, C_DV
    x = jnp.dot(gl_ref[0], wup_ref[...], precision=HI, preferred_element_type=F32) + bgk_ref[...]
    la_s[...] = (jnp.minimum(x, 0.0) - jnp.log1p(jnp.exp(-jnp.abs(x)))) * (1.0 / C_GATE_NORMALIZER)
    st_s[...] = jnp.zeros(st_s.shape, F32)

    row_c = lax.broadcasted_iota(I32, (ch, ch), 0)
    col_c = lax.broadcasted_iota(I32, (ch, ch), 1)
    tri = (row_c >= col_c).astype(F32)
    eye = (row_c == col_c).astype(F32)
    sub_i = lax.broadcasted_iota(I32, (sub, ch), 0)
    lane_j = lax.broadcasted_iota(I32, (sub, ch), 1)

    def chunk(c, carry):
        for hh in range(GLA_HEADS_PER_STEP):
            chunk_head(c, hh)
        return carry

    def chunk_head(c, hh):
        s0 = pl.multiple_of(c * ch, ch)
        a = la_s[pl.ds(s0, ch), hh * dk:(hh + 1) * dk]
        b = jnp.dot(tri, a, precision=HI, preferred_element_type=F32)
        q = q_ref[0, pl.ds(s0, ch), hh * dk:(hh + 1) * dk] * (dk ** -0.5)
        k = k_ref[0, pl.ds(s0, ch), hh * dk:(hh + 1) * dk]
        v = v_ref[0, pl.ds(s0, ch), hh * dv:(hh + 1) * dv].astype(BF16)
        st = st_s[hh]
        o = jnp.dot((q * jnp.exp(b)).astype(BF16), st.astype(BF16), preferred_element_type=F32)

        blocks = []
        for sb in range(ch // sub):
            r0 = sb * sub
            q_i = q[r0:r0 + sub]
            b_i = b[r0:r0 + sub]
            k_i = k[r0:r0 + sub]
            att = jnp.zeros((sub, ch), F32)
            for j in range(sub):
                e = jnp.exp(jnp.minimum(b_i - b_i[j:j + 1, :], 0.0))
                colv = jnp.sum(q_i * k_i[j:j + 1, :] * e, axis=1, keepdims=True)
                att = jnp.where((lane_j == r0 + j) & (sub_i >= j), colv, att)
            if sb > 0:
                ref_b = b[r0 - 1:r0, :]
                q_t = q_i * jnp.exp(b_i - ref_b)
                k_t = k * jnp.exp(jnp.minimum(ref_b - b, 0.0))
                off = lax.dot_general(q_t.astype(BF16), k_t.astype(BF16), (((1,), (1,)), ((), ())),
                                      preferred_element_type=F32)
                att = jnp.where(lane_j < r0, off, att)
            blocks.append(att)
        att_full = jnp.concatenate(blocks, axis=0)
        o = o + jnp.dot(att_full.astype(BF16), v, preferred_element_type=F32)
        raw_s[pl.ds(s0, ch), hh * dv:(hh + 1) * dv] = o

        b_last = b[ch - 1:ch, :]
        k_d = k * jnp.exp(b_last - b)
        upd = jnp.dot(k_d.T.astype(BF16), v, preferred_element_type=F32)
        decay_col = lax.dot_general(eye, jnp.broadcast_to(jnp.exp(b_last), (SUBLANES, dk)),
                                    (((1,), (1,)), ((), ())), precision=HI,
                                    preferred_element_type=F32)[:, 0:1]
        st_s[hh] = decay_col * st + upd

    lax.fori_loop(0, seq // ch, chunk, 0)

    for hh in range(GLA_HEADS_PER_STEP):
        o = raw_s[:, hh * dv:(hh + 1) * dv]
        rms = o * lax.rsqrt(jnp.mean(o * o, axis=-1, keepdims=True) + LN_EPS) * ng_ref[...]
        gg = g_ref[0, :, hh * dv:(hh + 1) * dv]
        o_ref[0, :, hh * dv:(hh + 1) * dv] = rms * (gg * jax.nn.sigmoid(gg))


def _gla_core(proj, w_up_pad, b_gk, norm_g):
    B, S, _ = proj.shape
    H, hps = C_HEADS, GLA_HEADS_PER_STEP
    dk, dv = C_DK * hps, C_DV * hps
    assert C_DK == LANES and S % GLA_CHUNK == 0 and H % hps == 0
    k_off = C_KEY_DIM // dk
    v_off = 2 * C_KEY_DIM // dv
    g_off = (2 * C_KEY_DIM + C_VAL_DIM) // dv
    gl_off = (2 * C_KEY_DIM + 2 * C_VAL_DIM) // LANES
    return pl.pallas_call(
        functools.partial(_gla_body, seq=S),
        grid=(B, H // hps),
        in_specs=[pl.BlockSpec((1, S, dk), lambda b, h: (b, 0, h)),
                  pl.BlockSpec((1, S, dk), lambda b, h: (b, 0, k_off + h)),
                  pl.BlockSpec((1, S, dv), lambda b, h: (b, 0, v_off + h)),
                  pl.BlockSpec((1, S, dv), lambda b, h: (b, 0, g_off + h)),
                  pl.BlockSpec((1, S, LANES), lambda b, h: (b, 0, gl_off)),
                  pl.BlockSpec((LANES, dk), lambda b, h: (0, h)),
                  pl.BlockSpec((1, dk), lambda b, h: (0, h)),
                  pl.BlockSpec((1, C_DV), lambda b, h: (0, 0))],
        out_specs=pl.BlockSpec((1, S, dv), lambda b, h: (b, 0, h)),
        out_shape=jax.ShapeDtypeStruct((B, S, C_VAL_DIM), F32),
        scratch_shapes=[pltpu.VMEM((S, dk), F32), pltpu.VMEM((hps, C_DK, C_DV), F32), pltpu.VMEM((S, dv), F32)],
        compiler_params=_cparams("arbitrary", "arbitrary"),
        name="gla_core",
    )(proj, proj, proj, proj, proj, w_up_pad, b_gk.reshape(1, C_KEY_DIM), norm_g.reshape(1, C_DV))


def _router_body(h_ref, w_ref, b_ref, slot_ref, gate_ref, cnt_ref, lbase_ref, gbase_ref, run_s, w1_s, w2_s, *, tm):
    E = N_EXPERTS

    @pl.when(pl.program_id(0) == 0)
    def _():
        run_s[...] = jnp.zeros((1, E), F32)
        w = w_ref[...]
        w1 = w.astype(BF16)
        w1_s[...] = w1
        w2_s[...] = (w - w1.astype(F32)).astype(BF16)

    h = h_ref[...]
    h1 = h.astype(BF16)
    h2 = (h - h1.astype(F32)).astype(BF16)
    logits = (jnp.dot(h1, w1_s[...], preferred_element_type=F32)
              + (jnp.dot(h1, w2_s[...], preferred_element_type=F32)
                 + jnp.dot(h2, w1_s[...], preferred_element_type=F32))) + b_ref[...]
    lane = lax.broadcasted_iota(I32, (tm, E), 1)
    rest = logits
    top_val, onehot = [], []
    for _ in range(TOP_K):
        m = jnp.max(rest, axis=1, keepdims=True)
        idx = jnp.min(jnp.where(rest == m, lane, E), axis=1, keepdims=True)
        oh = lane == idx
        rest = jnp.where(oh, -jnp.inf, rest)
        top_val.append(m)
        onehot.append(oh)
    ex = [jnp.exp(v - top_val[0]) for v in top_val]
    den = ex[0] + ex[1] + ex[2] + ex[3]
    chosen = sum(oh.astype(F32) for oh in onehot)

    row = lax.broadcasted_iota(I32, (tm, tm), 0)
    col = lax.broadcasted_iota(I32, (tm, tm), 1)
    before = (row > col).astype(BF16)
    seen = jnp.dot(before, chosen.astype(BF16), preferred_element_type=F32)
    cnt = jnp.sum(chosen, axis=0, keepdims=True)
    cnt_pad = jnp.floor((cnt + (MOE_SEG - 1)) * (1.0 / MOE_SEG)) * MOE_SEG
    er = lax.broadcasted_iota(I32, (E, E), 0)
    ec = lax.broadcasted_iota(I32, (E, E), 1)
    cnt_rows = jnp.broadcast_to(cnt_pad, (SUBLANES, E))
    lbase_rows = jnp.dot(cnt_rows, (er < ec).astype(F32), precision=HI, preferred_element_type=F32)
    pos = seen + lbase_rows[0:1, :]

    lane_k = lax.broadcasted_iota(I32, (tm, TOP_K), 1)
    s_out = jnp.zeros((tm, TOP_K), I32)
    g_out = jnp.zeros((tm, TOP_K), F32)
    for kk in range(TOP_K):
        sk = jnp.sum(jnp.where(onehot[kk], pos, 0.0), axis=1, keepdims=True)
        s_out = jnp.where(lane_k == kk, sk.astype(I32), s_out)
        g_out = jnp.where(lane_k == kk, ex[kk] / den, g_out)
    slot_ref[...] = s_out
    gate_ref[...] = g_out
    cnt_ref[...] = cnt_rows.astype(I32)
    lbase_ref[...] = lbase_rows.astype(I32)
    gbase_ref[...] = jnp.broadcast_to(run_s[...], (SUBLANES, E)).astype(I32)
    run_s[...] = run_s[...] + cnt_pad


def _router(h, w, b, *, tm=MOE_TM):
    T, D = h.shape
    E = N_EXPERTS
    nt = T // tm
    tab = jax.ShapeDtypeStruct((nt * SUBLANES, E), I32)
    tab_spec = pl.BlockSpec((SUBLANES, E), lambda i: (i, 0))
    return pl.pallas_call(
        functools.partial(_router_body, tm=tm),
        grid=(nt,),
        in_specs=[pl.BlockSpec((tm, D), lambda i: (i, 0)),
                  pl.BlockSpec((D, E), lambda i: (0, 0)),
                  pl.BlockSpec((1, E), lambda i: (0, 0))],
        out_specs=[pl.BlockSpec((tm, TOP_K), lambda i: (i, 0)),
                   pl.BlockSpec((tm, TOP_K), lambda i: (i, 0)),
                   tab_spec, tab_spec, tab_spec],
        out_shape=[jax.ShapeDtypeStruct((T, TOP_K), I32),
                   jax.ShapeDtypeStruct((T, TOP_K), F32),
                   tab, tab, tab],
        scratch_shapes=[pltpu.VMEM((1, E), F32), pltpu.VMEM((D, E), BF16), pltpu.VMEM((D, E), BF16)],
        compiler_params=_cparams("arbitrary"),
        name="moe_router",
    )(h, w, b.reshape(1, E))


def _pack_pairs(left, right):
    hi = lax.bitcast_convert_type(left, I32) & jnp.int32(-65536)
    lo = lax.shift_right_logical(lax.bitcast_convert_type(right, I32), 16)
    return hi | lo


def _unpack_pairs(word):
    left = lax.bitcast_convert_type(word & jnp.int32(-65536), F32)
    right = lax.bitcast_convert_type(lax.shift_left(word, 16), F32)
    return left.astype(BF16), right.astype(BF16)


def _segment_copies(seg_g, seg_l, seg_n, tile, make_copy, start):
    def per_expert(e, carry):
        idx = tile * N_EXPERTS + e
        n = seg_n[idx]
        g = seg_g[idx]
        l = seg_l[idx]
        for bit in range(MOE_SEG_BITS - 1, -1, -1):
            size = MOE_SEG << bit

            @pl.when((n & size) != 0)
            def _():
                off = n & ~(2 * size - 1)
                cp = make_copy(pl.multiple_of(g + off, MOE_SEG), pl.multiple_of(l + off, MOE_SEG), size)
                if start:
                    cp.start()
                else:
                    cp.wait()
        return carry

    lax.fori_loop(0, N_EXPERTS, per_expert, 0)


def _dispatch_body(segg_ref, segl_ref, segn_ref, zf_ref, slot_t_ref, h_ref, xs_ref, stage, zbuf, sem, *, tm, bm):
    i = pl.program_id(0)

    @pl.when(i == 0)
    def _():
        zbuf[...] = jnp.zeros(zbuf.shape, I32)

        def zero_copy(j):
            return pltpu.make_async_copy(zbuf, xs_ref.at[pl.ds(pl.multiple_of(zf_ref[j], MOE_SEG), bm), :],
                                         sem.at[0])

        def start(j, carry):
            @pl.when(zf_ref[j] >= 0)
            def _():
                zero_copy(j).start()
            return carry

        def wait(j, carry):
            @pl.when(zf_ref[j] >= 0)
            def _():
                zero_copy(j).wait()
            return carry

        lax.fori_loop(0, zf_ref.shape[0], start, 0)
        lax.fori_loop(0, zf_ref.shape[0], wait, 0)

    slot_i = i % 2
    hb = h_ref[...].astype(BF16)
    for c in range(MOE_STAGE // MOE_PERM_CHUNK):
        r0 = c * MOE_PERM_CHUNK
        rows = lax.broadcasted_iota(I32, (MOE_PERM_CHUNK, tm), 0) + r0
        hit = rows == slot_t_ref[0:1, :]
        for kk in range(1, TOP_K):
            hit = hit | (rows == slot_t_ref[kk:kk + 1, :])
        perm = jnp.where(hit, 1.0, 0.0).astype(BF16)
        rows_f = jnp.dot(perm, hb, preferred_element_type=F32)
        stage[slot_i, r0:r0 + MOE_PERM_CHUNK, :] = _pack_pairs(rows_f[:, :MOE_PACK_W], rows_f[:, MOE_PACK_W:])

    def copier(buf):
        def make_copy(g, l, size):
            return pltpu.make_async_copy(stage.at[buf, pl.ds(l, size), :], xs_ref.at[pl.ds(g, size), :],
                                         sem.at[buf])
        return make_copy

    _segment_copies(segg_ref, segl_ref, segn_ref, i, copier(slot_i), True)

    @pl.when(i > 0)
    def _():
        _segment_copies(segg_ref, segl_ref, segn_ref, i - 1, copier(1 - slot_i), False)

    @pl.when(i == pl.num_programs(0) - 1)
    def _():
        _segment_copies(segg_ref, segl_ref, segn_ref, i, copier(slot_i), False)


def _dispatch(seg_g, seg_l, seg_n, zero_rows, slot_t, h, n_rows, *, tm=MOE_TM, bm=MOE_BM):
    T, D = h.shape
    grid_spec = pltpu.PrefetchScalarGridSpec(
        num_scalar_prefetch=4,
        grid=(T // tm,),
        in_specs=[pl.BlockSpec((TOP_K, tm), lambda i, *_: (0, i)),
                  pl.BlockSpec((tm, D), lambda i, *_: (i, 0))],
        out_specs=pl.BlockSpec(memory_space=pl.ANY),
        scratch_shapes=[pltpu.VMEM((2, MOE_STAGE, MOE_PACK_W), I32), pltpu.VMEM((bm, MOE_PACK_W), I32),
                        pltpu.SemaphoreType.DMA((2,))],
    )
    return pl.pallas_call(
        functools.partial(_dispatch_body, tm=tm, bm=bm),
        grid_spec=grid_spec,
        out_shape=jax.ShapeDtypeStruct((n_rows, MOE_PACK_W), I32),
        compiler_params=pltpu.CompilerParams(dimension_semantics=("arbitrary",), vmem_limit_bytes=VMEM_LIMIT,
                                             has_side_effects=True),
        name="moe_dispatch",
    )(seg_g, seg_l, seg_n, zero_rows, slot_t, h)


def _expert_body(be_ref, nb_ref, nv_ref, xs_ref, wgu_ref, bgu_ref, wd_ref, bd_ref, ys_ref, wgu_s, wd_s, *, bm):
    i = pl.program_id(0)
    prev = be_ref[jnp.maximum(i - 1, 0)]
    live = i < nb_ref[0]
    n_valid = jnp.where(live, nv_ref[i], 0)

    @pl.when(live & ((i == 0) | (be_ref[i] != prev)))
    def _():
        wgu_s[...] = wgu_ref[...].astype(BF16)
        wd_s[...] = wd_ref[...].astype(BF16)

    def mlp(r0, rows):
        x = jnp.concatenate(_unpack_pairs(xs_ref[r0:r0 + rows, :]), axis=1)
        hh = jnp.dot(x, wgu_s[...], preferred_element_type=F32) + bgu_ref[...]
        d_e = hh.shape[1] // 2
        h_gate = jnp.minimum(hh[:, :d_e], SWIGLU_LIMIT)
        h_up = jnp.clip(hh[:, d_e:], -SWIGLU_LIMIT, SWIGLU_LIMIT)
        act = h_gate * jax.nn.sigmoid(SWIGLU_ALPHA * h_gate) * (h_up + 1.0)
        y = jnp.dot(act.astype(BF16), wd_s[...], preferred_element_type=F32) + bd_ref[...]
        y = y.astype(BF16).astype(F32)
        ys_ref[r0:r0 + rows, :] = _pack_pairs(y[:, :MOE_PACK_W], y[:, MOE_PACK_W:])

    for r0 in range(0, bm, MOE_SUB):
        @pl.when(n_valid > r0)
        def _(r0=r0):
            mlp(r0, MOE_SUB)

        @pl.when(n_valid <= r0)
        def _(r0=r0):
            ys_ref[r0:r0 + MOE_SUB, :] = jnp.zeros((MOE_SUB, ys_ref.shape[1]), I32)


def _experts(block_e, n_live, n_valid, xs, layer, w_gu, b_gu, w_down, b_down, *, bm=MOE_BM):
    L, E, D, D2 = w_gu.shape
    n_blocks = xs.shape[0] // bm

    def blk(i, nb):
        return jnp.maximum(jnp.minimum(i, nb[0] - 1), 0)

    grid_spec = pltpu.PrefetchScalarGridSpec(
        num_scalar_prefetch=3,
        grid=(n_blocks,),
        in_specs=[pl.BlockSpec((bm, MOE_PACK_W), lambda i, be, nb, nv: (blk(i, nb), 0)),
                  pl.BlockSpec((None, None, D, D2), lambda i, be, nb, nv: (layer, be[blk(i, nb)], 0, 0)),
                  pl.BlockSpec((None, None, 1, D2), lambda i, be, nb, nv: (layer, be[blk(i, nb)], 0, 0)),
                  pl.BlockSpec((None, None, D2 // 2, D), lambda i, be, nb, nv: (layer, be[blk(i, nb)], 0, 0)),
                  pl.BlockSpec((None, None, 1, D), lambda i, be, nb, nv: (layer, be[blk(i, nb)], 0, 0))],
        out_specs=pl.BlockSpec((bm, MOE_PACK_W), lambda i, be, nb, nv: (i, 0)),
        scratch_shapes=[pltpu.VMEM((D, D2), BF16), pltpu.VMEM((D2 // 2, D), BF16)],
    )
    return pl.pallas_call(
        functools.partial(_expert_body, bm=bm),
        grid_spec=grid_spec,
        out_shape=jax.ShapeDtypeStruct(xs.shape, I32),
        compiler_params=_cparams("arbitrary"),
        name="moe_experts",
    )(block_e, n_live, n_valid, xs, w_gu, b_gu.reshape(L, E, 1, D2), w_down, b_down.reshape(L, E, 1, D))


def _combine_body(segg_ref, segl_ref, segn_ref, ys_ref, slot_ref, gate_ref, h_ref, g_ref, b_ref, o_ref,
                  stage, mix_s, sem, *, tm):
    i = pl.program_id(0)

    slot_i = i % 2

    def copier(buf):
        def make_copy(g, l, size):
            return pltpu.make_async_copy(ys_ref.at[pl.ds(g, size), :], stage.at[buf, pl.ds(l, size), :],
                                         sem.at[buf])
        return make_copy

    @pl.when(i == 0)
    def _():
        stage[...] = jnp.zeros(stage.shape, I32)
        _segment_copies(segg_ref, segl_ref, segn_ref, i, copier(slot_i), True)

    @pl.when(i + 1 < pl.num_programs(0))
    def _():
        _segment_copies(segg_ref, segl_ref, segn_ref, i + 1, copier(1 - slot_i), True)

    slot = slot_ref[...]
    gate = gate_ref[...]
    for c in range(MOE_STAGE // MOE_PERM_CHUNK):
        c0 = c * MOE_PERM_CHUNK
        cols = lax.broadcasted_iota(I32, (tm, MOE_PERM_CHUNK), 1) + c0
        w = jnp.zeros((tm, MOE_PERM_CHUNK), F32)
        for kk in range(TOP_K):
            w = jnp.where(cols == slot[:, kk:kk + 1], gate[:, kk:kk + 1], w)
        mix_s[:, c0:c0 + MOE_PERM_CHUNK] = w.astype(BF16)

    _segment_copies(segg_ref, segl_ref, segn_ref, i, copier(slot_i), False)
    left, right = _unpack_pairs(stage[slot_i])
    y = jnp.concatenate([jnp.dot(mix_s[...], left, preferred_element_type=F32),
                         jnp.dot(mix_s[...], right, preferred_element_type=F32)], axis=1)
    o_ref[...] = _ln(ALPHA * h_ref[...] + y, g_ref[...], b_ref[...])


def _combine(seg_g, seg_l, seg_n, ys, slot, gate, h, g, b, *, tm=MOE_TM):
    T, D = h.shape
    grid_spec = pltpu.PrefetchScalarGridSpec(
        num_scalar_prefetch=3,
        grid=(T // tm,),
        in_specs=[pl.BlockSpec(memory_space=pl.ANY),
                  pl.BlockSpec((tm, TOP_K), lambda i, *_: (i, 0)),
                  pl.BlockSpec((tm, TOP_K), lambda i, *_: (i, 0)),
                  pl.BlockSpec((tm, D), lambda i, *_: (i, 0)),
                  pl.BlockSpec((1, D), lambda i, *_: (0, 0)),
                  pl.BlockSpec((1, D), lambda i, *_: (0, 0))],
        out_specs=pl.BlockSpec((tm, D), lambda i, *_: (i, 0)),
        scratch_shapes=[pltpu.VMEM((2, MOE_STAGE, MOE_PACK_W), I32), pltpu.VMEM((tm, MOE_STAGE), BF16),
                        pltpu.SemaphoreType.DMA((2,))],
    )
    return pl.pallas_call(
        functools.partial(_combine_body, tm=tm),
        grid_spec=grid_spec,
        out_shape=jax.ShapeDtypeStruct((T, D), F32),
        compiler_params=_cparams("arbitrary"),
        name="moe_combine",
    )(seg_g, seg_l, seg_n, ys, slot, gate, h, g.reshape(1, D), b.reshape(1, D))


def _moe_block(h, layer, w_router, b_router, w_gu, b_gu, w_down, b_down, g, b):
    T = h.shape[0]
    bm, tm, E = MOE_BM, MOE_TM, N_EXPERTS
    nt = T // tm
    slot, gate, cnt, lbase, gbase = _router(h, w_router, b_router)
    cnt, lbase, gbase = cnt[::SUBLANES], lbase[::SUBLANES], gbase[::SUBLANES]
    total = gbase[-1] + cnt[-1]
    padded = (total + bm - 1) // bm * bm
    pend = jnp.cumsum(padded)
    pstart = pend - padded
    seg_g = (pstart[None, :] + gbase).reshape(-1).astype(I32)
    seg_l = lbase.reshape(-1)
    seg_n = cnt.reshape(-1)
    n_blocks = -(-(T * TOP_K + nt * E * (MOE_SEG - 1)) // bm) + E
    block_start = jnp.arange(n_blocks, dtype=I32) * bm
    block_e = jnp.minimum(jnp.sum((pend[None, :] <= block_start[:, None]).astype(I32), axis=1), E - 1)
    n_live = (pend[-1:] // bm).astype(I32)
    n_valid = jnp.clip((pstart + total)[block_e] - block_start, 0, bm).astype(I32)
    last_blk = jnp.where(padded > 0, pend - bm, -1)
    tail_blk = n_live[0] + jnp.arange(n_blocks - T * TOP_K // bm, dtype=I32)
    tail_blk = jnp.where(tail_blk < n_blocks, tail_blk * bm, -1)
    zero_rows = jnp.concatenate([last_blk, tail_blk]).astype(I32)
    xs = _dispatch(seg_g, seg_l, seg_n, zero_rows, slot.T, h, n_blocks * bm)
    ys = _experts(block_e, n_live, n_valid, xs, layer, w_gu, b_gu, w_down, b_down)
    return _combine(seg_g, seg_l, seg_n, ys, slot, gate, h, g, b)


def kernel(x, rel_bias, a_w_in, a_w_out, b_w_in, b_ln_g, b_ln_b, b_w_s, b_b_s, b_w_out,
           c_w_in, c_w_gk_up, c_b_gk, c_norm_g, c_w_out, ln_g, ln_b,
           moe_w_router, moe_b_router, moe_w_gate_up, moe_b_gate_up, moe_w_down, moe_b_down):
    B, S, D = x.shape
    T = B * S
    h = x.reshape(T, D)
    tables = _bias_tables(rel_bias)
    for i in range(DEPTH):
        j = i // N_MIXERS
        mixer = i % N_MIXERS
        if mixer == 0:
            q_fold = jnp.where(jnp.arange(3 * D) < D, LOG2E * A_DH ** -0.5, 1.0).astype(F32)
            qkv = _matmul(h, (a_w_in[j] * q_fold).astype(BF16), out_dtype=BF16)
            o = _moba_attention(qkv.reshape(B, S, 3 * D), rel_bias, tables)
            h = _matmul_res_ln(o.reshape(T, D), a_w_out[j], h, ln_g[i, 0], ln_b[i, 0])
        elif mixer == 1:
            z = _matmul(h, b_w_in[j].astype(BF16), act="gelu", out_dtype=BF16)
            h = _gmlp_gate_out(z, h, b_ln_g[j], b_ln_b[j], b_w_s[j], b_b_s[j], b_w_out[j],
                               ln_g[i, 0], ln_b[i, 0])
        else:
            pad = LANES - C_GATE_RANK
            w_in = jnp.pad(c_w_in[j], ((0, 0), (0, pad)))
            w_up = jnp.pad(c_w_gk_up[j], ((0, pad), (0, 0)))
            proj = _matmul(h, w_in.astype(BF16), tn=640)
            o = _gla_core(proj.reshape(B, S, C_IN_WIDTH + pad), w_up, c_b_gk[j], c_norm_g[j])
            h = _matmul_res_ln(o.reshape(T, D), c_w_out[j], h, ln_g[i, 0], ln_b[i, 0])
        h = _moe_block(h, i, moe_w_router[i], moe_b_router[i], moe_w_gate_up, moe_b_gate_up,
                       moe_w_down, moe_b_down, ln_g[i, 1], ln_b[i, 1])
    return h.reshape(B, S, D)
```

```python
import functools
import math

import numpy as np
import jax
import jax.numpy as jnp
from jax import lax
from jax.experimental import pallas as pl
from jax.experimental.pallas import tpu as pltpu

F32 = jnp.float32
BF16 = jnp.bfloat16
I32 = jnp.int32
HI = lax.Precision.HIGHEST

D_MODEL = 1024
DEPTH = 4
N_MIXERS = 3
ALPHA = (2.0 * DEPTH) ** 0.25
LN_EPS = 1e-5
LOG2E = math.log2(math.e)

A_HEADS = 8
A_DH = D_MODEL // A_HEADS
MOBA_BLOCK = 256
MOBA_TOPK = 3
REL_BUCKETS = 32
REL_MAX_DIST = 128

B_GROUPS = 8
B_WIDTH = 2 * D_MODEL
B_CHUNK = 128
B_GW = B_WIDTH // B_GROUPS

C_HEADS = 4
C_KEY_DIM = D_MODEL // 2
C_VAL_DIM = D_MODEL
C_DK = C_KEY_DIM // C_HEADS
C_DV = C_VAL_DIM // C_HEADS
C_GATE_RANK = 16
C_GATE_NORMALIZER = 16.0
C_IN_WIDTH = 2 * C_KEY_DIM + 2 * C_VAL_DIM + C_GATE_RANK

N_EXPERTS = 32
TOP_K = 4
SWIGLU_LIMIT = 7.0
SWIGLU_ALPHA = 1.702

LANES = 128
SUBLANES = 8
VMEM_LIMIT = 56 * 1024 * 1024

GLA_CHUNK = 128
GLA_SUB = 16
GLA_HEADS_PER_STEP = 2
MOE_BM = 1024
MOE_SUB = 256
MOE_TM = 512
MOE_SEG = SUBLANES
MOE_SEG_BITS = (MOE_TM // MOE_SEG).bit_length()
MOE_STAGE = MOE_TM * TOP_K + N_EXPERTS * MOE_SEG
MOE_PERM_CHUNK = 256
MOE_PACK_W = D_MODEL // 2


def _cparams(*sem):
    return pltpu.CompilerParams(dimension_semantics=sem, vmem_limit_bytes=VMEM_LIMIT)


def _ln(x, g, b):
    mu = jnp.mean(x, axis=-1, keepdims=True)
    xc = x - mu
    var = jnp.mean(xc * xc, axis=-1, keepdims=True)
    return xc * lax.rsqrt(var + LN_EPS) * g + b


def _mm_body(x_ref, w_ref, o_ref, *, act, tn):
    x = x_ref[...].astype(BF16)
    for n0 in range(0, w_ref.shape[1], tn):
        acc = jnp.dot(x, w_ref[:, n0:n0 + tn], preferred_element_type=F32)
        if act == "gelu":
            acc = 0.5 * acc * (1.0 + lax.erf(acc * (2.0 ** -0.5)))
        o_ref[:, n0:n0 + tn] = acc.astype(o_ref.dtype)


def _matmul(x, w, *, act=None, tm=512, tn=512, out_dtype=F32):
    M, K = x.shape
    N = w.shape[1]
    assert M % tm == 0 and N % tn == 0
    return pl.pallas_call(
        functools.partial(_mm_body, act=act, tn=tn),
        grid=(M // tm,),
        in_specs=[pl.BlockSpec((tm, K), lambda i: (i, 0)),
                  pl.BlockSpec((K, N), lambda i: (0, 0))],
        out_specs=pl.BlockSpec((tm, N), lambda i: (i, 0)),
        out_shape=jax.ShapeDtypeStruct((M, N), out_dtype),
        compiler_params=_cparams("arbitrary"),
        name="proj_matmul",
    )(x, w)


def _mm_res_ln_body(x_ref, w_ref, h_ref, g_ref, b_ref, o_ref, wb_ref):
    @pl.when(pl.program_id(0) == 0)
    def _():
        wb_ref[...] = w_ref[...].astype(BF16)

    t = jnp.dot(x_ref[...].astype(BF16), wb_ref[...], preferred_element_type=F32)
    o_ref[...] = _ln(ALPHA * h_ref[...] + t, g_ref[...], b_ref[...])


def _matmul_res_ln(x, w, h, g, b, *, tm=512):
    M, K = x.shape
    N = w.shape[1]
    return pl.pallas_call(
        _mm_res_ln_body,
        grid=(M // tm,),
        in_specs=[pl.BlockSpec((tm, K), lambda i: (i, 0)),
                  pl.BlockSpec((K, N), lambda i: (0, 0)),
                  pl.BlockSpec((tm, N), lambda i: (i, 0)),
                  pl.BlockSpec((1, N), lambda i: (0, 0)),
                  pl.BlockSpec((1, N), lambda i: (0, 0))],
        out_specs=pl.BlockSpec((tm, N), lambda i: (i, 0)),
        out_shape=jax.ShapeDtypeStruct((M, N), F32),
        scratch_shapes=[pltpu.VMEM((K, N), BF16)],
        compiler_params=_cparams("arbitrary"),
        name="outproj_res_ln",
    )(x, w, h, g.reshape(1, N), b.reshape(1, N))


def _t5_bucket_lower_bounds():
    n = np.arange(0, 4 * REL_MAX_DIST, dtype=np.int64)
    max_exact = REL_BUCKETS // 2
    nf = np.maximum(n, 1).astype(np.float32)
    large = max_exact + (np.log(nf / np.float32(max_exact)) / np.float32(math.log(REL_MAX_DIST / max_exact))
                         * np.float32(REL_BUCKETS - max_exact)).astype(np.int32)
    large = np.minimum(large, REL_BUCKETS - 1)
    bucket = np.where(n < max_exact, n, large)
    assert np.all(np.diff(bucket) >= 0) and bucket[-1] == REL_BUCKETS - 1
    return [int(np.argmax(bucket >= b)) for b in range(REL_BUCKETS)]


_BUCKET_LO = _t5_bucket_lower_bounds()


def _bias_table_body(rel_ref, o_ref):
    blk = MOBA_BLOCK
    row = lax.broadcasted_iota(I32, (blk, blk), 0)
    col = lax.broadcasted_iota(I32, (blk, blk), 1)
    for t in range(2):
        d = row - col + t * blk
        for h in range(A_HEADS):
            val = jnp.full((blk, blk), rel_ref[REL_BUCKETS - 1, h], F32)
            for b in range(REL_BUCKETS - 2, -1, -1):
                val = jnp.where(d < _BUCKET_LO[b + 1], rel_ref[b, h], val)
            val = val * LOG2E
            if t == 0:
                val = jnp.where(d >= 0, val, -jnp.inf)
            o_ref[h, t] = val


def _bias_tables(rel_bias):
    return pl.pallas_call(
        _bias_table_body,
        in_specs=[pl.BlockSpec(memory_space=pltpu.SMEM)],
        out_shape=jax.ShapeDtypeStruct((A_HEADS, 2, MOBA_BLOCK, MOBA_BLOCK), F32),
        compiler_params=pltpu.CompilerParams(vmem_limit_bytes=VMEM_LIMIT),
        name="rel_bias_tables",
    )(rel_bias)


def _moba_body(rel_ref, q_ref, k_ref, v_ref, tb_ref, o_ref, s_s, p_s, acc_s, rden_s, *, seq):
    blk, dh = MOBA_BLOCK, A_DH
    nb = seq // blk
    h = pl.program_id(1)
    kmean = jnp.mean(k_ref[0].astype(F32).reshape(nb, blk, dh), axis=1)
    kmean_pad = jnp.concatenate([kmean, jnp.zeros((LANES - nb, dh), F32)], axis=0)
    b_far = rel_ref[REL_BUCKETS - 1, h] * LOG2E
    lane = lax.broadcasted_iota(I32, (blk, LANES), 1)
    ones_bf = jnp.ones((LANES, LANES), BF16)

    for n in range(nb):
        r0 = n * blk
        s_s[r0:seq, r0:r0 + blk] = lax.dot_general(q_ref[0, r0:seq, :], k_ref[0, r0:r0 + blk, :],
                                                   (((1,), (1,)), ((), ())), preferred_element_type=F32)

    for c in range(nb):
        q = q_ref[0, c * blk:(c + 1) * blk, :]
        n_keys = (c + 1) * blk
        s = s_s[c * blk:(c + 1) * blk, 0:n_keys]
        if c > MOBA_TOPK:
            gate = lax.dot_general(q.astype(F32), kmean_pad, (((1,), (1,)), ((), ())),
                                   precision=HI, preferred_element_type=F32)
            gate = jnp.where(lane < c, gate, -jnp.inf)
        pieces = []
        for n in range(c):
            shift = b_far
            if c > MOBA_TOPK:
                gn = gate[:, n:n + 1]
                beats = (gate > gn) | ((gate == gn) & (lane < n))
                rank = jnp.dot(jnp.where(beats, 1.0, 0.0).astype(BF16), ones_bf, preferred_element_type=F32)
                shift = jnp.where(rank < MOBA_TOPK, b_far, -jnp.inf)
                shift = jnp.concatenate([shift] * (blk // LANES), axis=1)
            if n == c - 1:
                shift = tb_ref[0, 1] + (shift - b_far)
            pieces.append(s[:, n * blk:(n + 1) * blk] + shift)
        pieces.append(s[:, c * blk:(c + 1) * blk] + tb_ref[0, 0])
        logits = jnp.concatenate(pieces, axis=1) if c > 0 else pieces[0]
        m = jnp.max(logits, axis=1, keepdims=True)
        p = jnp.exp2(logits - m)
        denom = jnp.sum(p, axis=1, keepdims=True)
        p_s[c * blk:(c + 1) * blk, 0:n_keys] = p.astype(BF16)
        rden_s[c * blk:(c + 1) * blk, :] = jnp.broadcast_to(1.0 / denom, (blk, dh))

    for n in range(nb):
        r0 = n * blk
        pv = jnp.dot(p_s[r0:seq, r0:r0 + blk], v_ref[0, r0:r0 + blk, :], preferred_element_type=F32)
        if n == 0:
            acc_s[...] = pv
        else:
            acc_s[r0:seq, :] = acc_s[r0:seq, :] + pv
    o_ref[0] = acc_s[...] * rden_s[...]


def _moba_attention(qkv, rel_bias, tables):
    B, S, _ = qkv.shape
    H, dh = A_HEADS, A_DH
    assert S % MOBA_BLOCK == 0 and S // MOBA_BLOCK <= SUBLANES
    return pl.pallas_call(
        functools.partial(_moba_body, seq=S),
        grid=(B, H),
        in_specs=[pl.BlockSpec(memory_space=pltpu.SMEM),
                  pl.BlockSpec((1, S, dh), lambda b, h: (b, 0, h)),
                  pl.BlockSpec((1, S, dh), lambda b, h: (b, 0, H + h)),
                  pl.BlockSpec((1, S, dh), lambda b, h: (b, 0, 2 * H + h)),
                  pl.BlockSpec((1, 2, MOBA_BLOCK, MOBA_BLOCK), lambda b, h: (h, 0, 0, 0))],
        out_specs=pl.BlockSpec((1, S, dh), lambda b, h: (b, 0, h)),
        out_shape=jax.ShapeDtypeStruct((B, S, H * dh), F32),
        scratch_shapes=[pltpu.VMEM((S, S), F32), pltpu.VMEM((S, S), BF16),
                        pltpu.VMEM((S, dh), F32), pltpu.VMEM((S, dh), F32)],
        compiler_params=_cparams("arbitrary", "arbitrary"),
        name="moba_attention",
    )(rel_bias, qkv, qkv, qkv, tables)


def _gmlp_body(u_ref, v_ref, h_ref, vg_ref, vb_ref, ws_ref, bs_ref, wo_ref, g_ref, b_ref, o_ref,
               wst_s, wob_s, *, tm):
    @pl.when(pl.program_id(0) == 0)
    def _():
        row = lax.broadcasted_iota(I32, (B_CHUNK, B_CHUNK), 0)
        col = lax.broadcasted_iota(I32, (B_CHUNK, B_CHUNK), 1)
        for g in range(B_GROUPS):
            wst_s[g] = jnp.where(row >= col, ws_ref[g], 0.0).astype(BF16)
        wob_s[...] = wo_ref[...].astype(BF16)

    vn = _ln(v_ref[...].astype(F32), vg_ref[...], vb_ref[...]).astype(BF16)
    u = u_ref[...].astype(F32)
    rows = []
    for c in range(tm // B_CHUNK):
        r0 = c * B_CHUNK
        cols = []
        for g in range(B_GROUPS):
            c0 = g * B_GW
            mixed = jnp.dot(wst_s[g], vn[r0:r0 + B_CHUNK, c0:c0 + B_GW],
                            preferred_element_type=F32) + bs_ref[g]
            cols.append(u[r0:r0 + B_CHUNK, c0:c0 + B_GW] * mixed)
        rows.append(jnp.concatenate(cols, axis=1))
    y = jnp.concatenate(rows, axis=0).astype(BF16)
    t = jnp.dot(y, wob_s[...], preferred_element_type=F32)
    o_ref[...] = _ln(ALPHA * h_ref[...] + t, g_ref[...], b_ref[...])


def _gmlp_gate_out(z, h, v_g, v_b, w_s, b_s, w_out, g, b, *, tm=256):
    T = z.shape[0]
    D = D_MODEL
    return pl.pallas_call(
        functools.partial(_gmlp_body, tm=tm),
        grid=(T // tm,),
        in_specs=[pl.BlockSpec((tm, B_WIDTH), lambda i: (i, 0)),
                  pl.BlockSpec((tm, B_WIDTH), lambda i: (i, 1)),
                  pl.BlockSpec((tm, D), lambda i: (i, 0)),
                  pl.BlockSpec((1, B_WIDTH), lambda i: (0, 0)),
                  pl.BlockSpec((1, B_WIDTH), lambda i: (0, 0)),
                  pl.BlockSpec((B_GROUPS, B_CHUNK, B_CHUNK), lambda i: (0, 0, 0)),
                  pl.BlockSpec((B_GROUPS, B_CHUNK, 1), lambda i: (0, 0, 0)),
                  pl.BlockSpec((B_WIDTH, D), lambda i: (0, 0)),
                  pl.BlockSpec((1, D), lambda i: (0, 0)),
                  pl.BlockSpec((1, D), lambda i: (0, 0))],
        out_specs=pl.BlockSpec((tm, D), lambda i: (i, 0)),
        out_shape=jax.ShapeDtypeStruct((T, D), F32),
        scratch_shapes=[pltpu.VMEM((B_GROUPS, B_CHUNK, B_CHUNK), BF16), pltpu.VMEM((B_WIDTH, D), BF16)],
        compiler_params=_cparams("arbitrary"),
        name="gmlp_gate_out",
    )(z, z, h, v_g.reshape(1, B_WIDTH), v_b.reshape(1, B_WIDTH), w_s, b_s.reshape(B_GROUPS, B_CHUNK, 1),
      w_out, g.reshape(1, D), b.reshape(1, D))


def _gla_body(q_ref, k_ref, v_ref, g_ref, gl_ref, wup_ref, bgk_ref, ng_ref, o_ref, la_s, st_s, raw_s, *, seq):
    ch, sub, dk, dv = GLA_CHUNK, GLA_SUB, C_DK, C_DV
    x = jnp.dot(gl_ref[0], wup_ref[...], precision=HI, preferred_element_type=F32) + bgk_ref[...]
    la_s[...] = (jnp.minimum(x, 0.0) - jnp.log1p(jnp.exp(-jnp.abs(x)))) * (1.0 / C_GATE_NORMALIZER)
    st_s[...] = jnp.zeros(st_s.shape, F32)

    row_c = lax.broadcasted_iota(I32, (ch, ch), 0)
    col_c = lax.broadcasted_iota(I32, (ch, ch), 1)
    tri = (row_c >= col_c).astype(F32)
    eye = (row_c == col_c).astype(F32)
    sub_i = lax.broadcasted_iota(I32, (sub, ch), 0)
    lane_j = lax.broadcasted_iota(I32, (sub, ch), 1)

    def chunk(c, carry):
        for hh in range(GLA_HEADS_PER_STEP):
            chunk_head(c, hh)
        return carry

    def chunk_head(c, hh):
        s0 = pl.multiple_of(c * ch, ch)
        a = la_s[pl.ds(s0, ch), hh * dk:(hh + 1) * dk]
        b = jnp.dot(tri, a, precision=HI, preferred_element_type=F32)
        q = q_ref[0, pl.ds(s0, ch), hh * dk:(hh + 1) * dk] * (dk ** -0.5)
        k = k_ref[0, pl.ds(s0, ch), hh * dk:(hh + 1) * dk]
        v = v_ref[0, pl.ds(s0, ch), hh * dv:(hh + 1) * dv].astype(BF16)
        st = st_s[hh]
        o = jnp.dot((q * jnp.exp(b)).astype(BF16), st.astype(BF16), preferred_element_type=F32)

        blocks = []
        for sb in range(ch // sub):
            r0 = sb * sub
            q_i = q[r0:r0 + sub]
            b_i = b[r0:r0 + sub]
            k_i = k[r0:r0 + sub]
            att = jnp.zeros((sub, ch), F32)
            for j in range(sub):
                e = jnp.exp(jnp.minimum(b_i - b_i[j:j + 1, :], 0.0))
                colv = jnp.sum(q_i * k_i[j:j + 1, :] * e, axis=1, keepdims=True)
                att = jnp.where((lane_j == r0 + j) & (sub_i >= j), colv, att)
            if sb > 0:
                ref_b = b[r0 - 1:r0, :]
                q_t = q_i * jnp.exp(b_i - ref_b)
                k_t = k * jnp.exp(jnp.minimum(ref_b - b, 0.0))
                off = lax.dot_general(q_t.astype(BF16), k_t.astype(BF16), (((1,), (1,)), ((), ())),
                                      preferred_element_type=F32)
                att = jnp.where(lane_j < r0, off, att)
            blocks.append(att)
        att_full = jnp.concatenate(blocks, axis=0)
        o = o + jnp.dot(att_full.astype(BF16), v, preferred_element_type=F32)
        raw_s[pl.ds(s0, ch), hh * dv:(hh + 1) * dv] = o

        b_last = b[ch - 1:ch, :]
        k_d = k * jnp.exp(b_last - b)
        upd = jnp.dot(k_d.T.astype(BF16), v, preferred_element_type=F32)
        decay_col = lax.dot_general(eye, jnp.broadcast_to(jnp.exp(b_last), (SUBLANES, dk)),
                                    (((1,), (1,)), ((), ())), precision=HI,
                                    preferred_element_type=F32)[:, 0:1]
        st_s[hh] = decay_col * st + upd

    lax.fori_loop(0, seq // ch, chunk, 0)

    for hh in range(GLA_HEADS_PER_STEP):
        o = raw_s[:, hh * dv:(hh + 1) * dv]
        rms = o * lax.rsqrt(jnp.mean(o * o, axis=-1, keepdims=True) + LN_EPS) * ng_ref[...]
        gg = g_ref[0, :, hh * dv:(hh + 1) * dv]
        o_ref[0, :, hh * dv:(hh + 1) * dv] = rms * (gg * jax.nn.sigmoid(gg))


def _gla_core(proj, w_up_pad, b_gk, norm_g):
    B, S, _ = proj.shape
    H, hps = C_HEADS, GLA_HEADS_PER_STEP
    dk, dv = C_DK * hps, C_DV * hps
    assert C_DK == LANES and S % GLA_CHUNK == 0 and H % hps == 0
    k_off = C_KEY_DIM // dk
    v_off = 2 * C_KEY_DIM // dv
    g_off = (2 * C_KEY_DIM + C_VAL_DIM) // dv
    gl_off = (2 * C_KEY_DIM + 2 * C_VAL_DIM) // LANES
    return pl.pallas_call(
        functools.partial(_gla_body, seq=S),
        grid=(B, H // hps),
        in_specs=[pl.BlockSpec((1, S, dk), lambda b, h: (b, 0, h)),
                  pl.BlockSpec((1, S, dk), lambda b, h: (b, 0, k_off + h)),
                  pl.BlockSpec((1, S, dv), lambda b, h: (b, 0, v_off + h)),
                  pl.BlockSpec((1, S, dv), lambda b, h: (b, 0, g_off + h)),
                  pl.BlockSpec((1, S, LANES), lambda b, h: (b, 0, gl_off)),
                  pl.BlockSpec((LANES, dk), lambda b, h: (0, h)),
                  pl.BlockSpec((1, dk), lambda b, h: (0, h)),
                  pl.BlockSpec((1, C_DV), lambda b, h: (0, 0))],
        out_specs=pl.BlockSpec((1, S, dv), lambda b, h: (b, 0, h)),
        out_shape=jax.ShapeDtypeStruct((B, S, C_VAL_DIM), F32),
        scratch_shapes=[pltpu.VMEM((S, dk), F32), pltpu.VMEM((hps, C_DK, C_DV), F32), pltpu.VMEM((S, dv), F32)],
        compiler_params=_cparams("arbitrary", "arbitrary"),
        name="gla_core",
    )(proj, proj, proj, proj, proj, w_up_pad, b_gk.reshape(1, C_KEY_DIM), norm_g.reshape(1, C_DV))


def _router_body(h_ref, w_ref, b_ref, slot_ref, gate_ref, cnt_ref, lbase_ref, gbase_ref, run_s, w1_s, w2_s, *, tm):
    E = N_EXPERTS

    @pl.when(pl.program_id(0) == 0)
    def _():
        run_s[...] = jnp.zeros((1, E), F32)
        w = w_ref[...]
        w1 = w.astype(BF16)
        w1_s[...] = w1
        w2_s[...] = (w - w1.astype(F32)).astype(BF16)

    h = h_ref[...]
    h1 = h.astype(BF16)
    h2 = (h - h1.astype(F32)).astype(BF16)
    logits = (jnp.dot(h1, w1_s[...], preferred_element_type=F32)
              + (jnp.dot(h1, w2_s[...], preferred_element_type=F32)
                 + jnp.dot(h2, w1_s[...], preferred_element_type=F32))) + b_ref[...]
    lane = lax.broadcasted_iota(I32, (tm, E), 1)
    rest = logits
    top_val, onehot = [], []
    for _ in range(TOP_K):
        m = jnp.max(rest, axis=1, keepdims=True)
        idx = jnp.min(jnp.where(rest == m, lane, E), axis=1, keepdims=True)
        oh = lane == idx
        rest = jnp.where(oh, -jnp.inf, rest)
        top_val.append(m)
        onehot.append(oh)
    ex = [jnp.exp(v - top_val[0]) for v in top_val]
    den = ex[0] + ex[1] + ex[2] + ex[3]
    chosen = sum(oh.astype(F32) for oh in onehot)

    row = lax.broadcasted_iota(I32, (tm, tm), 0)
    col = lax.broadcasted_iota(I32, (tm, tm), 1)
    before = (row > col).astype(BF16)
    seen = jnp.dot(before, chosen.astype(BF16), preferred_element_type=F32)
    cnt = jnp.sum(chosen, axis=0, keepdims=True)
    cnt_pad = jnp.floor((cnt + (MOE_SEG - 1)) * (1.0 / MOE_SEG)) * MOE_SEG
    er = lax.broadcasted_iota(I32, (E, E), 0)
    ec = lax.broadcasted_iota(I32, (E, E), 1)
    cnt_rows = jnp.broadcast_to(cnt_pad, (SUBLANES, E))
    lbase_rows = jnp.dot(cnt_rows, (er < ec).astype(F32), precision=HI, preferred_element_type=F32)
    pos = seen + lbase_rows[0:1, :]

    lane_k = lax.broadcasted_iota(I32, (tm, TOP_K), 1)
    s_out = jnp.zeros((tm, TOP_K), I32)
    g_out = jnp.zeros((tm, TOP_K), F32)
    for kk in range(TOP_K):
        sk = jnp.sum(jnp.where(onehot[kk], pos, 0.0), axis=1, keepdims=True)
        s_out = jnp.where(lane_k == kk, sk.astype(I32), s_out)
        g_out = jnp.where(lane_k == kk, ex[kk] / den, g_out)
    slot_ref[...] = s_out
    gate_ref[...] = g_out
    cnt_ref[...] = cnt_rows.astype(I32)
    lbase_ref[...] = lbase_rows.astype(I32)
    gbase_ref[...] = jnp.broadcast_to(run_s[...], (SUBLANES, E)).astype(I32)
    run_s[...] = run_s[...] + cnt_pad


def _router(h, w, b, *, tm=MOE_TM):
    T, D = h.shape
    E = N_EXPERTS
    nt = T // tm
    tab = jax.ShapeDtypeStruct((nt * SUBLANES, E), I32)
    tab_spec = pl.BlockSpec((SUBLANES, E), lambda i: (i, 0))
    return pl.pallas_call(
        functools.partial(_router_body, tm=tm),
        grid=(nt,),
        in_specs=[pl.BlockSpec((tm, D), lambda i: (i, 0)),
                  pl.BlockSpec((D, E), lambda i: (0, 0)),
                  pl.BlockSpec((1, E), lambda i: (0, 0))],
        out_specs=[pl.BlockSpec((tm, TOP_K), lambda i: (i, 0)),
                   pl.BlockSpec((tm, TOP_K), lambda i: (i, 0)),
                   tab_spec, tab_spec, tab_spec],
        out_shape=[jax.ShapeDtypeStruct((T, TOP_K), I32),
                   jax.ShapeDtypeStruct((T, TOP_K), F32),
                   tab, tab, tab],
        scratch_shapes=[pltpu.VMEM((1, E), F32), pltpu.VMEM((D, E), BF16), pltpu.VMEM((D, E), BF16)],
        compiler_params=_cparams("arbitrary"),
        name="moe_router",
    )(h, w, b.reshape(1, E))


def _pack_pairs(left, right):
    hi = lax.bitcast_convert_type(left, I32) & jnp.int32(-65536)
    lo = lax.shift_right_logical(lax.bitcast_convert_type(right, I32), 16)
    return hi | lo


def _unpack_pairs(word):
    left = lax.bitcast_convert_type(word & jnp.int32(-65536), F32)
    right = lax.bitcast_convert_type(lax.shift_left(word, 16), F32)
    return left.astype(BF16), right.astype(BF16)


def _segment_copies(seg_g, seg_l, seg_n, tile, make_copy, start):
    def per_expert(e, carry):
        idx = tile * N_EXPERTS + e
        n = seg_n[idx]
        g = seg_g[idx]
        l = seg_l[idx]
        for bit in range(MOE_SEG_BITS - 1, -1, -1):
            size = MOE_SEG << bit

            @pl.when((n & size) != 0)
            def _():
                off = n & ~(2 * size - 1)
                cp = make_copy(pl.multiple_of(g + off, MOE_SEG), pl.multiple_of(l + off, MOE_SEG), size)
                if start:
                    cp.start()
                else:
                    cp.wait()
        return carry

    lax.fori_loop(0, N_EXPERTS, per_expert, 0)


def _segment_waits(seg_l, seg_n, tile, make_copy):
    last = tile * N_EXPERTS + N_EXPERTS - 1
    total = seg_l[last] + seg_n[last]
    for bit in range((MOE_STAGE // MOE_SEG).bit_length() - 1, -1, -1):
        size = MOE_SEG << bit

        @pl.when((total & size) != 0)
        def _():
            make_copy(0, 0, size).wait()


def _dispatch_body(segg_ref, segl_ref, segn_ref, zf_ref, slot_t_ref, h_ref, xs_ref, stage, zbuf, sem, *, tm, bm):
    i = pl.program_id(0)

    def zero_fill(start):
        def act(cp):
            if start:
                cp.start()
            else:
                cp.wait()

        def per_expert(e, carry):
            g = zf_ref[e]
            n = zf_ref[N_EXPERTS + e]
            for bit in range((bm // MOE_SEG).bit_length() - 2, -1, -1):
                size = MOE_SEG << bit

                @pl.when((n & size) != 0)
                def _():
                    off = n & ~(2 * size - 1)
                    act(pltpu.make_async_copy(
                        zbuf.at[pl.ds(0, size), :],
                        xs_ref.at[pl.ds(pl.multiple_of(g + off, MOE_SEG), size), :], sem.at[2]))
            return carry

        def per_tail(j, carry):
            row = zf_ref[2 * N_EXPERTS + j]

            @pl.when(row >= 0)
            def _():
                act(pltpu.make_async_copy(zbuf, xs_ref.at[pl.ds(pl.multiple_of(row, MOE_SEG), bm), :], sem.at[2]))
            return carry

        lax.fori_loop(0, N_EXPERTS, per_expert, 0)
        lax.fori_loop(0, zf_ref.shape[0] - 2 * N_EXPERTS, per_tail, 0)

    @pl.when(i == 0)
    def _():
        zbuf[...] = jnp.zeros(zbuf.shape, I32)
        zero_fill(True)

    @pl.when(i == pl.num_programs(0) - 1)
    def _():
        zero_fill(False)

    slot_i = i % 2
    hb = h_ref[...].astype(BF16)
    for c in range(MOE_STAGE // MOE_PERM_CHUNK):
        r0 = c * MOE_PERM_CHUNK
        rows = lax.broadcasted_iota(I32, (MOE_PERM_CHUNK, tm), 0) + r0
        hit = rows == slot_t_ref[0:1, :]
        for kk in range(1, TOP_K):
            hit = hit | (rows == slot_t_ref[kk:kk + 1, :])
        perm = jnp.where(hit, 1.0, 0.0).astype(BF16)
        rows_f = jnp.dot(perm, hb, preferred_element_type=F32)
        stage[slot_i, r0:r0 + MOE_PERM_CHUNK, :] = _pack_pairs(rows_f[:, :MOE_PACK_W], rows_f[:, MOE_PACK_W:])

    def copier(buf):
        def make_copy(g, l, size):
            return pltpu.make_async_copy(stage.at[buf, pl.ds(l, size), :], xs_ref.at[pl.ds(g, size), :],
                                         sem.at[buf])
        return make_copy

    _segment_copies(segg_ref, segl_ref, segn_ref, i, copier(slot_i), True)

    @pl.when(i > 0)
    def _():
        _segment_waits(segl_ref, segn_ref, i - 1, copier(1 - slot_i))

    @pl.when(i == pl.num_programs(0) - 1)
    def _():
        _segment_waits(segl_ref, segn_ref, i, copier(slot_i))


def _dispatch(seg_g, seg_l, seg_n, zero_rows, slot_t, h, n_rows, *, tm=MOE_TM, bm=MOE_BM):
    T, D = h.shape
    grid_spec = pltpu.PrefetchScalarGridSpec(
        num_scalar_prefetch=4,
        grid=(T // tm,),
        in_specs=[pl.BlockSpec((TOP_K, tm), lambda i, *_: (0, i)),
                  pl.BlockSpec((tm, D), lambda i, *_: (i, 0))],
        out_specs=pl.BlockSpec(memory_space=pl.ANY),
        scratch_shapes=[pltpu.VMEM((2, MOE_STAGE, MOE_PACK_W), I32), pltpu.VMEM((bm, MOE_PACK_W), I32),
                        pltpu.SemaphoreType.DMA((3,))],
    )
    return pl.pallas_call(
        functools.partial(_dispatch_body, tm=tm, bm=bm),
        grid_spec=grid_spec,
        out_shape=jax.ShapeDtypeStruct((n_rows, MOE_PACK_W), I32),
        compiler_params=pltpu.CompilerParams(dimension_semantics=("arbitrary",), vmem_limit_bytes=VMEM_LIMIT,
                                             has_side_effects=True),
        name="moe_dispatch",
    )(seg_g, seg_l, seg_n, zero_rows, slot_t, h)


def _expert_body(be_ref, nb_ref, nv_ref, xs_ref, wgu_ref, bgu_ref, wd_ref, bd_ref, ys_ref, wgu_s, wd_s, *, bm):
    i = pl.program_id(0)
    prev = be_ref[jnp.maximum(i - 1, 0)]
    live = i < nb_ref[0]
    n_valid = jnp.where(live, nv_ref[i], 0)

    @pl.when(live & ((i == 0) | (be_ref[i] != prev)))
    def _():
        wgu_s[...] = wgu_ref[...].astype(BF16)
        wd_s[...] = wd_ref[...].astype(BF16)

    def mlp(r0, rows):
        x = jnp.concatenate(_unpack_pairs(xs_ref[r0:r0 + rows, :]), axis=1)
        hh = jnp.dot(x, wgu_s[...], preferred_element_type=F32) + bgu_ref[...]
        d_e = hh.shape[1] // 2
        h_gate = jnp.minimum(hh[:, :d_e], SWIGLU_LIMIT)
        h_up = jnp.clip(hh[:, d_e:], -SWIGLU_LIMIT, SWIGLU_LIMIT)
        act = h_gate * jax.nn.sigmoid(SWIGLU_ALPHA * h_gate) * (h_up + 1.0)
        y = jnp.dot(act.astype(BF16), wd_s[...], preferred_element_type=F32) + bd_ref[...]
        y = y.astype(BF16).astype(F32)
        ys_ref[r0:r0 + rows, :] = _pack_pairs(y[:, :MOE_PACK_W], y[:, MOE_PACK_W:])

    full = n_valid > bm - MOE_SUB

    @pl.when(full)
    def _():
        mlp(0, bm)

    for r0 in range(0, bm, MOE_SUB):
        @pl.when(jnp.logical_not(full) & (n_valid > r0))
        def _(r0=r0):
            mlp(r0, MOE_SUB)

        @pl.when(n_valid <= r0)
        def _(r0=r0):
            ys_ref[r0:r0 + MOE_SUB, :] = jnp.zeros((MOE_SUB, ys_ref.shape[1]), I32)


def _experts(block_e, n_live, n_valid, xs, layer, w_gu, b_gu, w_down, b_down, *, bm=MOE_BM):
    L, E, D, D2 = w_gu.shape
    n_blocks = xs.shape[0] // bm

    def blk(i, nb):
        return jnp.maximum(jnp.minimum(i, nb[0] - 1), 0)

    grid_spec = pltpu.PrefetchScalarGridSpec(
        num_scalar_prefetch=3,
        grid=(n_blocks,),
        in_specs=[pl.BlockSpec((bm, MOE_PACK_W), lambda i, be, nb, nv: (blk(i, nb), 0)),
                  pl.BlockSpec((None, None, D, D2), lambda i, be, nb, nv: (layer, be[blk(i, nb)], 0, 0)),
                  pl.BlockSpec((None, None, 1, D2), lambda i, be, nb, nv: (layer, be[blk(i, nb)], 0, 0)),
                  pl.BlockSpec((None, None, D2 // 2, D), lambda i, be, nb, nv: (layer, be[blk(i, nb)], 0, 0)),
                  pl.BlockSpec((None, None, 1, D), lambda i, be, nb, nv: (layer, be[blk(i, nb)], 0, 0))],
        out_specs=pl.BlockSpec((bm, MOE_PACK_W), lambda i, be, nb, nv: (i, 0)),
        scratch_shapes=[pltpu.VMEM((D, D2), BF16), pltpu.VMEM((D2 // 2, D), BF16)],
    )
    return pl.pallas_call(
        functools.partial(_expert_body, bm=bm),
        grid_spec=grid_spec,
        out_shape=jax.ShapeDtypeStruct(xs.shape, I32),
        compiler_params=_cparams("arbitrary"),
        name="moe_experts",
    )(block_e, n_live, n_valid, xs, w_gu, b_gu.reshape(L, E, 1, D2), w_down, b_down.reshape(L, E, 1, D))


def _combine_body(segg_ref, segl_ref, segn_ref, ys_ref, slot_ref, gate_ref, h_ref, g_ref, b_ref, o_ref,
                  stage, mix_s, sem, *, tm):
    i = pl.program_id(0)

    slot_i = i % 2

    def copier(buf):
        def make_copy(g, l, size):
            return pltpu.make_async_copy(ys_ref.at[pl.ds(g, size), :], stage.at[buf, pl.ds(l, size), :],
                                         sem.at[buf])
        return make_copy

    @pl.when(i == 0)
    def _():
        stage[...] = jnp.zeros(stage.shape, I32)
        _segment_copies(segg_ref, segl_ref, segn_ref, i, copier(slot_i), True)

    @pl.when(i + 1 < pl.num_programs(0))
    def _():
        _segment_copies(segg_ref, segl_ref, segn_ref, i + 1, copier(1 - slot_i), True)

    slot = slot_ref[...]
    gate = gate_ref[...]
    for c in range(MOE_STAGE // MOE_PERM_CHUNK):
        c0 = c * MOE_PERM_CHUNK
        cols = lax.broadcasted_iota(I32, (tm, MOE_PERM_CHUNK), 1) + c0
        w = jnp.zeros((tm, MOE_PERM_CHUNK), F32)
        for kk in range(TOP_K):
            w = jnp.where(cols == slot[:, kk:kk + 1], gate[:, kk:kk + 1], w)
        mix_s[:, c0:c0 + MOE_PERM_CHUNK] = w.astype(BF16)

    _segment_waits(segl_ref, segn_ref, i, copier(slot_i))
    left, right = _unpack_pairs(stage[slot_i])
    y = jnp.concatenate([jnp.dot(mix_s[...], left, preferred_element_type=F32),
                         jnp.dot(mix_s[...], right, preferred_element_type=F32)], axis=1)
    o_ref[...] = _ln(ALPHA * h_ref[...] + y, g_ref[...], b_ref[...])


def _combine(seg_g, seg_l, seg_n, ys, slot, gate, h, g, b, *, tm=MOE_TM):
    T, D = h.shape
    grid_spec = pltpu.PrefetchScalarGridSpec(
        num_scalar_prefetch=3,
        grid=(T // tm,),
        in_specs=[pl.BlockSpec(memory_space=pl.ANY),
                  pl.BlockSpec((tm, TOP_K), lambda i, *_: (i, 0)),
                  pl.BlockSpec((tm, TOP_K), lambda i, *_: (i, 0)),
                  pl.BlockSpec((tm, D), lambda i, *_: (i, 0)),
                  pl.BlockSpec((1, D), lambda i, *_: (0, 0)),
                  pl.BlockSpec((1, D), lambda i, *_: (0, 0))],
        out_specs=pl.BlockSpec((tm, D), lambda i, *_: (i, 0)),
        scratch_shapes=[pltpu.VMEM((2, MOE_STAGE, MOE_PACK_W), I32), pltpu.VMEM((tm, MOE_STAGE), BF16),
                        pltpu.SemaphoreType.DMA((2,))],
    )
    return pl.pallas_call(
        functools.partial(_combine_body, tm=tm),
        grid_spec=grid_spec,
        out_shape=jax.ShapeDtypeStruct((T, D), F32),
        compiler_params=_cparams("arbitrary"),
        name="moe_combine",
    )(seg_g, seg_l, seg_n, ys, slot, gate, h, g.reshape(1, D), b.reshape(1, D))


def _moe_block(h, layer, w_router, b_router, w_gu, b_gu, w_down, b_down, g, b):
    T = h.shape[0]
    bm, tm, E = MOE_BM, MOE_TM, N_EXPERTS
    nt = T // tm
    slot, gate, cnt, lbase, gbase = _router(h, w_router, b_router)
    cnt, lbase, gbase = cnt[::SUBLANES], lbase[::SUBLANES], gbase[::SUBLANES]
    total = gbase[-1] + cnt[-1]
    padded = (total + bm - 1) // bm * bm
    pend = jnp.cumsum(padded)
    pstart = pend - padded
    seg_g = (pstart[None, :] + gbase).reshape(-1).astype(I32)
    seg_l = lbase.reshape(-1)
    seg_n = cnt.reshape(-1)
    n_blocks = -(-(T * TOP_K + nt * E * (MOE_SEG - 1)) // bm) + E
    block_start = jnp.arange(n_blocks, dtype=I32) * bm
    block_e = jnp.minimum(jnp.sum((pend[None, :] <= block_start[:, None]).astype(I32), axis=1), E - 1)
    n_live = (pend[-1:] // bm).astype(I32)
    n_valid = jnp.clip((pstart + total)[block_e] - block_start, 0, bm).astype(I32)
    tail_blk = n_live[0] + jnp.arange(n_blocks - T * TOP_K // bm, dtype=I32)
    tail_blk = jnp.where(tail_blk < n_blocks, tail_blk * bm, -1)
    zero_rows = jnp.concatenate([pstart + total, padded - total, tail_blk]).astype(I32)
    xs = _dispatch(seg_g, seg_l, seg_n, zero_rows, slot.T, h, n_blocks * bm)
    ys = _experts(block_e, n_live, n_valid, xs, layer, w_gu, b_gu, w_down, b_down)
    return _combine(seg_g, seg_l, seg_n, ys, slot, gate, h, g, b)


def kernel(x, rel_bias, a_w_in, a_w_out, b_w_in, b_ln_g, b_ln_b, b_w_s, b_b_s, b_w_out,
           c_w_in, c_w_gk_up, c_b_gk, c_norm_g, c_w_out, ln_g, ln_b,
           moe_w_router, moe_b_router, moe_w_gate_up, moe_b_gate_up, moe_w_down, moe_b_down):
    B, S, D = x.shape
    T = B * S
    h = x.reshape(T, D)
    tables = _bias_tables(rel_bias)
    for i in range(DEPTH):
        j = i // N_MIXERS
        mixer = i % N_MIXERS
        if mixer == 0:
            q_fold = jnp.where(jnp.arange(3 * D) < D, LOG2E * A_DH ** -0.5, 1.0).astype(F32)
            qkv = _matmul(h, (a_w_in[j] * q_fold).astype(BF16), out_dtype=BF16)
            o = _moba_attention(qkv.reshape(B, S, 3 * D), rel_bias, tables)
            h = _matmul_res_ln(o.reshape(T, D), a_w_out[j], h, ln_g[i, 0], ln_b[i, 0])
        elif mixer == 1:
            z = _matmul(h, b_w_in[j].astype(BF16), act="gelu", out_dtype=BF16)
            h = _gmlp_gate_out(z, h, b_ln_g[j], b_ln_b[j], b_w_s[j], b_b_s[j], b_w_out[j],
                               ln_g[i, 0], ln_b[i, 0])
        else:
            pad = LANES - C_GATE_RANK
            w_in = jnp.pad(c_w_in[j], ((0, 0), (0, pad)))
            w_up = jnp.pad(c_w_gk_up[j], ((0, pad), (0, 0)))
            proj = _matmul(h, w_in.astype(BF16), tn=640)
            o = _gla_core(proj.reshape(B, S, C_IN_WIDTH + pad), w_up, c_b_gk[j], c_norm_g[j])
            h = _matmul_res_ln(o.reshape(T, D), c_w_out[j], h, ln_g[i, 0], ln_b[i, 0])
        h = _moe_block(h, i, moe_w_router[i], moe_b_router[i], moe_w_gate_up, moe_b_gate_up,
                       moe_w_down, moe_b_down, ln_g[i, 1], ln_b[i, 1])
    return h.reshape(B, S, D)
```

```python
import functools
import math

import numpy as np
import jax
import jax.numpy as jnp
from jax import lax
from jax.experimental import pallas as pl
from jax.experimental.pallas import tpu as pltpu

F32 = jnp.float32
BF16 = jnp.bfloat16
I32 = jnp.int32
HI = lax.Precision.HIGHEST

D_MODEL = 1024
DEPTH = 4
N_MIXERS = 3
ALPHA = (2.0 * DEPTH) ** 0.25
LN_EPS = 1e-5
LOG2E = math.log2(math.e)

A_HEADS = 8
A_DH = D_MODEL // A_HEADS
MOBA_BLOCK = 256
MOBA_TOPK = 3
REL_BUCKETS = 32
REL_MAX_DIST = 128

B_GROUPS = 8
B_WIDTH = 2 * D_MODEL
B_CHUNK = 128
B_GW = B_WIDTH // B_GROUPS

C_HEADS = 4
C_KEY_DIM = D_MODEL // 2
C_VAL_DIM = D_MODEL
C_DK = C_KEY_DIM // C_HEADS
C_DV = C_VAL_DIM // C_HEADS
C_GATE_RANK = 16
C_GATE_NORMALIZER = 16.0
C_IN_WIDTH = 2 * C_KEY_DIM + 2 * C_VAL_DIM + C_GATE_RANK

N_EXPERTS = 32
TOP_K = 4
SWIGLU_LIMIT = 7.0
SWIGLU_ALPHA = 1.702

LANES = 128
SUBLANES = 8
VMEM_LIMIT = 56 * 1024 * 1024

GLA_CHUNK = 128
GLA_SUB = 16
GLA_HEADS_PER_STEP = 2
MOE_BM = 1024
MOE_SUB = 256
MOE_TM = 512
MOE_SEG = SUBLANES
MOE_SEG_BITS = (MOE_TM // MOE_SEG).bit_length()
MOE_STAGE = MOE_TM * TOP_K + N_EXPERTS * MOE_SEG
MOE_PERM_CHUNK = 256
MOE_PACK_W = D_MODEL // 2


def _cparams(*sem):
    return pltpu.CompilerParams(dimension_semantics=sem, vmem_limit_bytes=VMEM_LIMIT)


def _ln(x, g, b):
    mu = jnp.mean(x, axis=-1, keepdims=True)
    xc = x - mu
    var = jnp.mean(xc * xc, axis=-1, keepdims=True)
    return xc * lax.rsqrt(var + LN_EPS) * g + b


def _mm_body(x_ref, w_ref, o_ref, *, act, tn):
    x = x_ref[...].astype(BF16)
    for n0 in range(0, w_ref.shape[1], tn):
        acc = jnp.dot(x, w_ref[:, n0:n0 + tn], preferred_element_type=F32)
        if act == "gelu":
            acc = 0.5 * acc * (1.0 + lax.erf(acc * (2.0 ** -0.5)))
        o_ref[:, n0:n0 + tn] = acc.astype(o_ref.dtype)


def _matmul(x, w, *, act=None, tm=512, tn=512, out_dtype=F32):
    M, K = x.shape
    N = w.shape[1]
    assert M % tm == 0 and N % tn == 0
    return pl.pallas_call(
        functools.partial(_mm_body, act=act, tn=tn),
        grid=(M // tm,),
        in_specs=[pl.BlockSpec((tm, K), lambda i: (i, 0)),
                  pl.BlockSpec((K, N), lambda i: (0, 0))],
        out_specs=pl.BlockSpec((tm, N), lambda i: (i, 0)),
        out_shape=jax.ShapeDtypeStruct((M, N), out_dtype),
        compiler_params=_cparams("arbitrary"),
        name="proj_matmul",
    )(x, w)


def _mm_res_ln_body(x_ref, w_ref, h_ref, g_ref, b_ref, o_ref, wb_ref):
    @pl.when(pl.program_id(0) == 0)
    def _():
        wb_ref[...] = w_ref[...].astype(BF16)

    t = jnp.dot(x_ref[...].astype(BF16), wb_ref[...], preferred_element_type=F32)
    o_ref[...] = _ln(ALPHA * h_ref[...] + t, g_ref[...], b_ref[...])


def _matmul_res_ln(x, w, h, g, b, *, tm=512):
    M, K = x.shape
    N = w.shape[1]
    return pl.pallas_call(
        _mm_res_ln_body,
        grid=(M // tm,),
        in_specs=[pl.BlockSpec((tm, K), lambda i: (i, 0)),
                  pl.BlockSpec((K, N), lambda i: (0, 0)),
                  pl.BlockSpec((tm, N), lambda i: (i, 0)),
                  pl.BlockSpec((1, N), lambda i: (0, 0)),
                  pl.BlockSpec((1, N), lambda i: (0, 0))],
        out_specs=pl.BlockSpec((tm, N), lambda i: (i, 0)),
        out_shape=jax.ShapeDtypeStruct((M, N), F32),
        scratch_shapes=[pltpu.VMEM((K, N), BF16)],
        compiler_params=_cparams("arbitrary"),
        name="outproj_res_ln",
    )(x, w, h, g.reshape(1, N), b.reshape(1, N))


def _t5_bucket_lower_bounds():
    n = np.arange(0, 4 * REL_MAX_DIST, dtype=np.int64)
    max_exact = REL_BUCKETS // 2
    nf = np.maximum(n, 1).astype(np.float32)
    large = max_exact + (np.log(nf / np.float32(max_exact)) / np.float32(math.log(REL_MAX_DIST / max_exact))
                         * np.float32(REL_BUCKETS - max_exact)).astype(np.int32)
    large = np.minimum(large, REL_BUCKETS - 1)
    bucket = np.where(n < max_exact, n, large)
    assert np.all(np.diff(bucket) >= 0) and bucket[-1] == REL_BUCKETS - 1
    return [int(np.argmax(bucket >= b)) for b in range(REL_BUCKETS)]


_BUCKET_LO = _t5_bucket_lower_bounds()


def _bias_table_body(rel_ref, o_ref):
    blk = MOBA_BLOCK
    row = lax.broadcasted_iota(I32, (blk, blk), 0)
    col = lax.broadcasted_iota(I32, (blk, blk), 1)
    for t in range(2):
        d = row - col + t * blk
        for h in range(A_HEADS):
            val = jnp.full((blk, blk), rel_ref[REL_BUCKETS - 1, h], F32)
            for b in range(REL_BUCKETS - 2, -1, -1):
                val = jnp.where(d < _BUCKET_LO[b + 1], rel_ref[b, h], val)
            val = val * LOG2E
            if t == 0:
                val = jnp.where(d >= 0, val, -jnp.inf)
            o_ref[h, t] = val


def _bias_tables(rel_bias):
    return pl.pallas_call(
        _bias_table_body,
        in_specs=[pl.BlockSpec(memory_space=pltpu.SMEM)],
        out_shape=jax.ShapeDtypeStruct((A_HEADS, 2, MOBA_BLOCK, MOBA_BLOCK), F32),
        compiler_params=pltpu.CompilerParams(vmem_limit_bytes=VMEM_LIMIT),
        name="rel_bias_tables",
    )(rel_bias)


def _moba_body(rel_ref, q_ref, k_ref, v_ref, tb_ref, o_ref, s_s, p_s, acc_s, rden_s, *, seq):
    blk, dh = MOBA_BLOCK, A_DH
    nb = seq // blk
    h = pl.program_id(1)
    kmean = jnp.mean(k_ref[0].astype(F32).reshape(nb, blk, dh), axis=1)
    kmean_pad = jnp.concatenate([kmean, jnp.zeros((LANES - nb, dh), F32)], axis=0)
    b_far = rel_ref[REL_BUCKETS - 1, h] * LOG2E
    lane = lax.broadcasted_iota(I32, (blk, LANES), 1)
    ones_bf = jnp.ones((LANES, LANES), BF16)

    for n in range(nb):
        r0 = n * blk
        s_s[r0:seq, r0:r0 + blk] = lax.dot_general(q_ref[0, r0:seq, :], k_ref[0, r0:r0 + blk, :],
                                                   (((1,), (1,)), ((), ())), preferred_element_type=F32)

    for c in range(nb):
        q = q_ref[0, c * blk:(c + 1) * blk, :]
        n_keys = (c + 1) * blk
        s = s_s[c * blk:(c + 1) * blk, 0:n_keys]
        if c > MOBA_TOPK:
            gate = lax.dot_general(q.astype(F32), kmean_pad, (((1,), (1,)), ((), ())),
                                   precision=HI, preferred_element_type=F32)
            gate = jnp.where(lane < c, gate, -jnp.inf)
        pieces = []
        for n in range(c):
            shift = b_far
            if c > MOBA_TOPK:
                gn = gate[:, n:n + 1]
                beats = (gate > gn) | ((gate == gn) & (lane < n))
                rank = jnp.dot(jnp.where(beats, 1.0, 0.0).astype(BF16), ones_bf, preferred_element_type=F32)
                shift = jnp.where(rank < MOBA_TOPK, b_far, -jnp.inf)
                shift = jnp.concatenate([shift] * (blk // LANES), axis=1)
            if n == c - 1:
                shift = tb_ref[0, 1] + (shift - b_far)
            pieces.append(s[:, n * blk:(n + 1) * blk] + shift)
        pieces.append(s[:, c * blk:(c + 1) * blk] + tb_ref[0, 0])
        logits = jnp.concatenate(pieces, axis=1) if c > 0 else pieces[0]
        m = jnp.max(logits, axis=1, keepdims=True)
        p = jnp.exp2(logits - m)
        denom = jnp.sum(p, axis=1, keepdims=True)
        p_s[c * blk:(c + 1) * blk, 0:n_keys] = p.astype(BF16)
        rden_s[c * blk:(c + 1) * blk, :] = jnp.broadcast_to(1.0 / denom, (blk, dh))

    for n in range(nb):
        r0 = n * blk
        pv = jnp.dot(p_s[r0:seq, r0:r0 + blk], v_ref[0, r0:r0 + blk, :], preferred_element_type=F32)
        if n == 0:
            acc_s[...] = pv
        else:
            acc_s[r0:seq, :] = acc_s[r0:seq, :] + pv
    o_ref[0] = acc_s[...] * rden_s[...]


def _moba_attention(qkv, rel_bias, tables):
    B, S, _ = qkv.shape
    H, dh = A_HEADS, A_DH
    assert S % MOBA_BLOCK == 0 and S // MOBA_BLOCK <= SUBLANES
    return pl.pallas_call(
        functools.partial(_moba_body, seq=S),
        grid=(B, H),
        in_specs=[pl.BlockSpec(memory_space=pltpu.SMEM),
                  pl.BlockSpec((1, S, dh), lambda b, h: (b, 0, h)),
                  pl.BlockSpec((1, S, dh), lambda b, h: (b, 0, H + h)),
                  pl.BlockSpec((1, S, dh), lambda b, h: (b, 0, 2 * H + h)),
                  pl.BlockSpec((1, 2, MOBA_BLOCK, MOBA_BLOCK), lambda b, h: (h, 0, 0, 0))],
        out_specs=pl.BlockSpec((1, S, dh), lambda b, h: (b, 0, h)),
        out_shape=jax.ShapeDtypeStruct((B, S, H * dh), F32),
        scratch_shapes=[pltpu.VMEM((S, S), F32), pltpu.VMEM((S, S), BF16),
                        pltpu.VMEM((S, dh), F32), pltpu.VMEM((S, dh), F32)],
        compiler_params=_cparams("arbitrary", "arbitrary"),
        name="moba_attention",
    )(rel_bias, qkv, qkv, qkv, tables)


def _gmlp_body(u_ref, v_ref, h_ref, vg_ref, vb_ref, ws_ref, bs_ref, wo_ref, g_ref, b_ref, o_ref,
               wst_s, wob_s, *, tm):
    @pl.when(pl.program_id(0) == 0)
    def _():
        row = lax.broadcasted_iota(I32, (B_CHUNK, B_CHUNK), 0)
        col = lax.broadcasted_iota(I32, (B_CHUNK, B_CHUNK), 1)
        for g in range(B_GROUPS):
            wst_s[g] = jnp.where(row >= col, ws_ref[g], 0.0).astype(BF16)
        wob_s[...] = wo_ref[...].astype(BF16)

    vn = _ln(v_ref[...].astype(F32), vg_ref[...], vb_ref[...]).astype(BF16)
    u = u_ref[...].astype(F32)
    rows = []
    for c in range(tm // B_CHUNK):
        r0 = c * B_CHUNK
        cols = []
        for g in range(B_GROUPS):
            c0 = g * B_GW
            mixed = jnp.dot(wst_s[g], vn[r0:r0 + B_CHUNK, c0:c0 + B_GW],
                            preferred_element_type=F32) + bs_ref[g]
            cols.append(u[r0:r0 + B_CHUNK, c0:c0 + B_GW] * mixed)
        rows.append(jnp.concatenate(cols, axis=1))
    y = jnp.concatenate(rows, axis=0).astype(BF16)
    t = jnp.dot(y, wob_s[...], preferred_element_type=F32)
    o_ref[...] = _ln(ALPHA * h_ref[...] + t, g_ref[...], b_ref[...])


def _gmlp_gate_out(z, h, v_g, v_b, w_s, b_s, w_out, g, b, *, tm=256):
    T = z.shape[0]
    D = D_MODEL
    return pl.pallas_call(
        functools.partial(_gmlp_body, tm=tm),
        grid=(T // tm,),
        in_specs=[pl.BlockSpec((tm, B_WIDTH), lambda i: (i, 0)),
                  pl.BlockSpec((tm, B_WIDTH), lambda i: (i, 1)),
                  pl.BlockSpec((tm, D), lambda i: (i, 0)),
                  pl.BlockSpec((1, B_WIDTH), lambda i: (0, 0)),
                  pl.BlockSpec((1, B_WIDTH), lambda i: (0, 0)),
                  pl.BlockSpec((B_GROUPS, B_CHUNK, B_CHUNK), lambda i: (0, 0, 0)),
                  pl.BlockSpec((B_GROUPS, B_CHUNK, 1), lambda i: (0, 0, 0)),
                  pl.BlockSpec((B_WIDTH, D), lambda i: (0, 0)),
                  pl.BlockSpec((1, D), lambda i: (0, 0)),
                  pl.BlockSpec((1, D), lambda i: (0, 0))],
        out_specs=pl.BlockSpec((tm, D), lambda i: (i, 0)),
        out_shape=jax.ShapeDtypeStruct((T, D), F32),
        scratch_shapes=[pltpu.VMEM((B_GROUPS, B_CHUNK, B_CHUNK), BF16), pltpu.VMEM((B_WIDTH, D), BF16)],
        compiler_params=_cparams("arbitrary"),
        name="gmlp_gate_out",
    )(z, z, h, v_g.reshape(1, B_WIDTH), v_b.reshape(1, B_WIDTH), w_s, b_s.reshape(B_GROUPS, B_CHUNK, 1),
      w_out, g.reshape(1, D), b.reshape(1, D))


def _gla_body(q_ref, k_ref, v_ref, g_ref, gl_ref, wup_ref, bgk_ref, ng_ref, o_ref, la_s, st_s, raw_s, *, seq):
    ch, sub, dk, dv = GLA_CHUNK, GLA_SUB, C_DK, C_DV
    x = jnp.dot(gl_ref[0], wup_ref[...], precision=HI, preferred_element_type=F32) + bgk_ref[...]
    la_s[...] = (jnp.minimum(x, 0.0) - jnp.log1p(jnp.exp(-jnp.abs(x)))) * (1.0 / C_GATE_NORMALIZER)
    st_s[...] = jnp.zeros(st_s.shape, F32)

    row_c = lax.broadcasted_iota(I32, (ch, ch), 0)
    col_c = lax.broadcasted_iota(I32, (ch, ch), 1)
    tri = (row_c >= col_c).astype(F32)
    eye = (row_c == col_c).astype(F32)
    sub_i = lax.broadcasted_iota(I32, (sub, ch), 0)
    lane_j = lax.broadcasted_iota(I32, (sub, ch), 1)

    def chunk(c, carry):
        for hh in range(GLA_HEADS_PER_STEP):
            chunk_head(c, hh)
        return carry

    def chunk_head(c, hh):
        s0 = pl.multiple_of(c * ch, ch)
        a = la_s[pl.ds(s0, ch), hh * dk:(hh + 1) * dk]
        b = jnp.dot(tri, a, precision=HI, preferred_element_type=F32)
        q = q_ref[0, pl.ds(s0, ch), hh * dk:(hh + 1) * dk] * (dk ** -0.5)
        k = k_ref[0, pl.ds(s0, ch), hh * dk:(hh + 1) * dk]
        v = v_ref[0, pl.ds(s0, ch), hh * dv:(hh + 1) * dv].astype(BF16)
        st = st_s[hh]
        o = jnp.dot((q * jnp.exp(b)).astype(BF16), st.astype(BF16), preferred_element_type=F32)

        blocks = []
        for sb in range(ch // sub):
            r0 = sb * sub
            q_i = q[r0:r0 + sub]
            b_i = b[r0:r0 + sub]
            k_i = k[r0:r0 + sub]
            att = jnp.zeros((sub, ch), F32)
            for j in range(sub):
                e = jnp.exp(jnp.minimum(b_i - b_i[j:j + 1, :], 0.0))
                colv = jnp.sum(q_i * k_i[j:j + 1, :] * e, axis=1, keepdims=True)
                att = jnp.where((lane_j == r0 + j) & (sub_i >= j), colv, att)
            if sb > 0:
                ref_b = b[r0 - 1:r0, :]
                q_t = q_i * jnp.exp(b_i - ref_b)
                k_t = k * jnp.exp(jnp.minimum(ref_b - b, 0.0))
                off = lax.dot_general(q_t.astype(BF16), k_t.astype(BF16), (((1,), (1,)), ((), ())),
                                      preferred_element_type=F32)
                att = jnp.where(lane_j < r0, off, att)
            blocks.append(att)
        att_full = jnp.concatenate(blocks, axis=0)
        o = o + jnp.dot(att_full.astype(BF16), v, preferred_element_type=F32)
        raw_s[pl.ds(s0, ch), hh * dv:(hh + 1) * dv] = o

        b_last = b[ch - 1:ch, :]
        k_d = k * jnp.exp(b_last - b)
        upd = jnp.dot(k_d.T.astype(BF16), v, preferred_element_type=F32)
        decay_col = lax.dot_general(eye, jnp.broadcast_to(jnp.exp(b_last), (SUBLANES, dk)),
                                    (((1,), (1,)), ((), ())), precision=HI,
                                    preferred_element_type=F32)[:, 0:1]
        st_s[hh] = decay_col * st + upd

    lax.fori_loop(0, seq // ch, chunk, 0)

    for hh in range(GLA_HEADS_PER_STEP):
        o = raw_s[:, hh * dv:(hh + 1) * dv]
        rms = o * lax.rsqrt(jnp.mean(o * o, axis=-1, keepdims=True) + LN_EPS) * ng_ref[...]
        gg = g_ref[0, :, hh * dv:(hh + 1) * dv]
        o_ref[0, :, hh * dv:(hh + 1) * dv] = rms * (gg * jax.nn.sigmoid(gg))


def _gla_core(proj, w_up_pad, b_gk, norm_g):
    B, S, _ = proj.shape
    H, hps = C_HEADS, GLA_HEADS_PER_STEP
    dk, dv = C_DK * hps, C_DV * hps
    assert C_DK == LANES and S % GLA_CHUNK == 0 and H % hps == 0
    k_off = C_KEY_DIM // dk
    v_off = 2 * C_KEY_DIM // dv
    g_off = (2 * C_KEY_DIM + C_VAL_DIM) // dv
    gl_off = (2 * C_KEY_DIM + 2 * C_VAL_DIM) // LANES
    return pl.pallas_call(
        functools.partial(_gla_body, seq=S),
        grid=(B, H // hps),
        in_specs=[pl.BlockSpec((1, S, dk), lambda b, h: (b, 0, h)),
                  pl.BlockSpec((1, S, dk), lambda b, h: (b, 0, k_off + h)),
                  pl.BlockSpec((1, S, dv), lambda b, h: (b, 0, v_off + h)),
                  pl.BlockSpec((1, S, dv), lambda b, h: (b, 0, g_off + h)),
                  pl.BlockSpec((1, S, LANES), lambda b, h: (b, 0, gl_off)),
                  pl.BlockSpec((LANES, dk), lambda b, h: (0, h)),
                  pl.BlockSpec((1, dk), lambda b, h: (0, h)),
                  pl.BlockSpec((1, C_DV), lambda b, h: (0, 0))],
        out_specs=pl.BlockSpec((1, S, dv), lambda b, h: (b, 0, h)),
        out_shape=jax.ShapeDtypeStruct((B, S, C_VAL_DIM), F32),
        scratch_shapes=[pltpu.VMEM((S, dk), F32), pltpu.VMEM((hps, C_DK, C_DV), F32), pltpu.VMEM((S, dv), F32)],
        compiler_params=_cparams("arbitrary", "arbitrary"),
        name="gla_core",
    )(proj, proj, proj, proj, proj, w_up_pad, b_gk.reshape(1, C_KEY_DIM), norm_g.reshape(1, C_DV))


def _router_body(h_ref, wt_ref, b_ref, slot_ref, gate_ref, cnt_ref, lbase_ref, gbase_ref, run_s, ws_s, *, tm):
    E = N_EXPERTS

    @pl.when(pl.program_id(0) == 0)
    def _():
        run_s[...] = jnp.zeros(run_s.shape, F32)
        w = wt_ref[...]
        w1 = w.astype(BF16)
        ws_s[0:E, :] = w1
        ws_s[E:2 * E, :] = (w - w1.astype(F32)).astype(BF16)

    h = h_ref[...]
    h1 = h.astype(BF16)
    h2 = (h - h1.astype(F32)).astype(BF16)
    nt_dims = (((1,), (1,)), ((), ()))
    both = lax.dot_general(ws_s[...], h1, nt_dims, preferred_element_type=F32)
    logits = (both[0:E] + (both[E:2 * E] + lax.dot_general(ws_s[0:E, :], h2, nt_dims, preferred_element_type=F32))
              + b_ref[...])
    row = lax.broadcasted_iota(I32, (E, tm), 0)
    rest = logits
    top_val, onehot = [], []
    for _ in range(TOP_K):
        m = jnp.max(rest, axis=0, keepdims=True)
        idx = jnp.min(jnp.where(rest == m, row, E), axis=0, keepdims=True)
        oh = row == idx
        rest = jnp.where(oh, -jnp.inf, rest)
        top_val.append(m)
        onehot.append(oh)
    ex = [jnp.exp(v - top_val[0]) for v in top_val]
    den = ex[0] + ex[1] + ex[2] + ex[3]
    chosen = sum(oh.astype(F32) for oh in onehot)

    tr = lax.broadcasted_iota(I32, (tm, tm), 0)
    tc = lax.broadcasted_iota(I32, (tm, tm), 1)
    earlier = (tr < tc).astype(BF16)
    seen = jnp.dot(chosen.astype(BF16), earlier, preferred_element_type=F32)
    cnt = jnp.sum(chosen, axis=1, keepdims=True)
    cnt_pad = jnp.floor((cnt + (MOE_SEG - 1)) * (1.0 / MOE_SEG)) * MOE_SEG
    cnt_cols = jnp.broadcast_to(cnt_pad, (E, LANES))
    er = lax.broadcasted_iota(I32, (E, E), 0)
    ec = lax.broadcasted_iota(I32, (E, E), 1)
    lbase_cols = jnp.dot((ec < er).astype(F32), cnt_cols, precision=HI, preferred_element_type=F32)
    pos = seen + lbase_cols[:, 0:1]

    row_k = lax.broadcasted_iota(I32, (TOP_K, tm), 0)
    s_out = jnp.zeros((TOP_K, tm), I32)
    g_out = jnp.zeros((TOP_K, tm), F32)
    for kk in range(TOP_K):
        sk = jnp.sum(jnp.where(onehot[kk], pos, 0.0), axis=0, keepdims=True)
        s_out = jnp.where(row_k == kk, sk.astype(I32), s_out)
        g_out = jnp.where(row_k == kk, ex[kk] / den, g_out)
    slot_ref[...] = s_out
    gate_ref[...] = g_out
    cnt_ref[...] = cnt_cols.astype(I32)
    lbase_ref[...] = lbase_cols.astype(I32)
    gbase_ref[...] = run_s[...].astype(I32)
    run_s[...] = run_s[...] + cnt_cols


def _router(h, w, b, *, tm=MOE_TM):
    T, D = h.shape
    E = N_EXPERTS
    nt = T // tm
    tab = jax.ShapeDtypeStruct((nt * E, LANES), I32)
    tab_spec = pl.BlockSpec((E, LANES), lambda i: (i, 0))
    return pl.pallas_call(
        functools.partial(_router_body, tm=tm),
        grid=(nt,),
        in_specs=[pl.BlockSpec((tm, D), lambda i: (i, 0)),
                  pl.BlockSpec((E, D), lambda i: (0, 0)),
                  pl.BlockSpec((E, 1), lambda i: (0, 0))],
        out_specs=[pl.BlockSpec((TOP_K, tm), lambda i: (0, i)),
                   pl.BlockSpec((TOP_K, tm), lambda i: (0, i)),
                   tab_spec, tab_spec, tab_spec],
        out_shape=[jax.ShapeDtypeStruct((TOP_K, T), I32),
                   jax.ShapeDtypeStruct((TOP_K, T), F32),
                   tab, tab, tab],
        scratch_shapes=[pltpu.VMEM((E, LANES), F32), pltpu.VMEM((2 * E, D), BF16)],
        compiler_params=_cparams("arbitrary"),
        name="moe_router",
    )(h, w.T, b.reshape(E, 1))


def _pack_pairs(left, right):
    hi = lax.bitcast_convert_type(left, I32) & jnp.int32(-65536)
    lo = lax.shift_right_logical(lax.bitcast_convert_type(right, I32), 16)
    return hi | lo


def _unpack_pairs(word):
    left = lax.bitcast_convert_type(word & jnp.int32(-65536), F32)
    right = lax.bitcast_convert_type(lax.shift_left(word, 16), F32)
    return left.astype(BF16), right.astype(BF16)


def _segment_copies(seg_g, seg_l, seg_n, tile, make_copy, start):
    def per_expert(e, carry):
        idx = tile * N_EXPERTS + e
        n = seg_n[idx]
        g = seg_g[idx]
        l = seg_l[idx]
        for bit in range(MOE_SEG_BITS - 1, -1, -1):
            size = MOE_SEG << bit

            @pl.when((n & size) != 0)
            def _():
                off = n & ~(2 * size - 1)
                cp = make_copy(pl.multiple_of(g + off, MOE_SEG), pl.multiple_of(l + off, MOE_SEG), size)
                if start:
                    cp.start()
                else:
                    cp.wait()
        return carry

    lax.fori_loop(0, N_EXPERTS, per_expert, 0)


def _segment_waits(seg_l, seg_n, tile, make_copy):
    last = tile * N_EXPERTS + N_EXPERTS - 1
    total = seg_l[last] + seg_n[last]
    for bit in range((MOE_STAGE // MOE_SEG).bit_length() - 1, -1, -1):
        size = MOE_SEG << bit

        @pl.when((total & size) != 0)
        def _():
            make_copy(0, 0, size).wait()


def _dispatch_body(segg_ref, segl_ref, segn_ref, zf_ref, slot_t_ref, h_ref, xs_ref, stage, zbuf, sem, *, tm, bm):
    i = pl.program_id(0)

    def zero_fill(start):
        def act(cp):
            if start:
                cp.start()
            else:
                cp.wait()

        def per_expert(e, carry):
            g = zf_ref[e]
            n = zf_ref[N_EXPERTS + e]
            for bit in range((bm // MOE_SEG).bit_length() - 2, -1, -1):
                size = MOE_SEG << bit

                @pl.when((n & size) != 0)
                def _():
                    off = n & ~(2 * size - 1)
                    act(pltpu.make_async_copy(
                        zbuf.at[pl.ds(0, size), :],
                        xs_ref.at[pl.ds(pl.multiple_of(g + off, MOE_SEG), size), :], sem.at[2]))
            return carry

        def per_tail(j, carry):
            row = zf_ref[2 * N_EXPERTS + j]

            @pl.when(row >= 0)
            def _():
                act(pltpu.make_async_copy(zbuf, xs_ref.at[pl.ds(pl.multiple_of(row, MOE_SEG), bm), :], sem.at[2]))
            return carry

        lax.fori_loop(0, N_EXPERTS, per_expert, 0)
        lax.fori_loop(0, zf_ref.shape[0] - 2 * N_EXPERTS, per_tail, 0)

    @pl.when(i == 0)
    def _():
        zbuf[...] = jnp.zeros(zbuf.shape, I32)
        zero_fill(True)

    @pl.when(i == pl.num_programs(0) - 1)
    def _():
        zero_fill(False)

    slot_i = i % 2
    hb = h_ref[...].astype(BF16)
    for c in range(MOE_STAGE // MOE_PERM_CHUNK):
        r0 = c * MOE_PERM_CHUNK
        rows = lax.broadcasted_iota(I32, (MOE_PERM_CHUNK, tm), 0) + r0
        hit = rows == slot_t_ref[0:1, :]
        for kk in range(1, TOP_K):
            hit = hit | (rows == slot_t_ref[kk:kk + 1, :])
        perm = jnp.where(hit, 1.0, 0.0).astype(BF16)
        rows_f = jnp.dot(perm, hb, preferred_element_type=F32)
        stage[slot_i, r0:r0 + MOE_PERM_CHUNK, :] = _pack_pairs(rows_f[:, :MOE_PACK_W], rows_f[:, MOE_PACK_W:])

    def copier(buf):
        def make_copy(g, l, size):
            return pltpu.make_async_copy(stage.at[buf, pl.ds(l, size), :], xs_ref.at[pl.ds(g, size), :],
                                         sem.at[buf])
        return make_copy

    _segment_copies(segg_ref, segl_ref, segn_ref, i, copier(slot_i), True)

    @pl.when(i > 0)
    def _():
        _segment_waits(segl_ref, segn_ref, i - 1, copier(1 - slot_i))

    @pl.when(i == pl.num_programs(0) - 1)
    def _():
        _segment_waits(segl_ref, segn_ref, i, copier(slot_i))


def _dispatch(seg_g, seg_l, seg_n, zero_rows, slot_t, h, n_rows, *, tm=MOE_TM, bm=MOE_BM):
    T, D = h.shape
    grid_spec = pltpu.PrefetchScalarGridSpec(
        num_scalar_prefetch=4,
        grid=(T // tm,),
        in_specs=[pl.BlockSpec((TOP_K, tm), lambda i, *_: (0, i)),
                  pl.BlockSpec((tm, D), lambda i, *_: (i, 0))],
        out_specs=pl.BlockSpec(memory_space=pl.ANY),
        scratch_shapes=[pltpu.VMEM((2, MOE_STAGE, MOE_PACK_W), I32), pltpu.VMEM((bm, MOE_PACK_W), I32),
                        pltpu.SemaphoreType.DMA((3,))],
    )
    return pl.pallas_call(
        functools.partial(_dispatch_body, tm=tm, bm=bm),
        grid_spec=grid_spec,
        out_shape=jax.ShapeDtypeStruct((n_rows, MOE_PACK_W), I32),
        compiler_params=pltpu.CompilerParams(dimension_semantics=("arbitrary",), vmem_limit_bytes=VMEM_LIMIT,
                                             has_side_effects=True),
        name="moe_dispatch",
    )(seg_g, seg_l, seg_n, zero_rows, slot_t, h)


def _expert_body(be_ref, nb_ref, nv_ref, xs_ref, wgu_ref, bgu_ref, wd_ref, bd_ref, ys_ref, wgu_s, wd_s, *, bm):
    i = pl.program_id(0)
    prev = be_ref[jnp.maximum(i - 1, 0)]
    live = i < nb_ref[0]
    n_valid = jnp.where(live, nv_ref[i], 0)

    @pl.when(live & ((i == 0) | (be_ref[i] != prev)))
    def _():
        wgu_s[...] = wgu_ref[...].astype(BF16)
        wd_s[...] = wd_ref[...].astype(BF16)

    def mlp(r0, rows):
        x = jnp.concatenate(_unpack_pairs(xs_ref[r0:r0 + rows, :]), axis=1)
        hh = jnp.dot(x, wgu_s[...], preferred_element_type=F32) + bgu_ref[...]
        d_e = hh.shape[1] // 2
        h_gate = jnp.minimum(hh[:, :d_e], SWIGLU_LIMIT)
        h_up = jnp.clip(hh[:, d_e:], -SWIGLU_LIMIT, SWIGLU_LIMIT)
        act = h_gate * jax.nn.sigmoid(SWIGLU_ALPHA * h_gate) * (h_up + 1.0)
        y = jnp.dot(act.astype(BF16), wd_s[...], preferred_element_type=F32) + bd_ref[...]
        y = y.astype(BF16).astype(F32)
        ys_ref[r0:r0 + rows, :] = _pack_pairs(y[:, :MOE_PACK_W], y[:, MOE_PACK_W:])

    full = n_valid > bm - MOE_SUB

    @pl.when(full)
    def _():
        mlp(0, bm)

    for r0 in range(0, bm, MOE_SUB):
        @pl.when(jnp.logical_not(full) & (n_valid > r0))
        def _(r0=r0):
            mlp(r0, MOE_SUB)

        @pl.when(n_valid <= r0)
        def _(r0=r0):
            ys_ref[r0:r0 + MOE_SUB, :] = jnp.zeros((MOE_SUB, ys_ref.shape[1]), I32)


def _experts(block_e, n_live, n_valid, xs, layer, w_gu, b_gu, w_down, b_down, *, bm=MOE_BM):
    L, E, D, D2 = w_gu.shape
    n_blocks = xs.shape[0] // bm

    def blk(i, nb):
        return jnp.maximum(jnp.minimum(i, nb[0] - 1), 0)

    grid_spec = pltpu.PrefetchScalarGridSpec(
        num_scalar_prefetch=3,
        grid=(n_blocks,),
        in_specs=[pl.BlockSpec((bm, MOE_PACK_W), lambda i, be, nb, nv: (blk(i, nb), 0)),
                  pl.BlockSpec((None, None, D, D2), lambda i, be, nb, nv: (layer, be[blk(i, nb)], 0, 0)),
                  pl.BlockSpec((None, None, 1, D2), lambda i, be, nb, nv: (layer, be[blk(i, nb)], 0, 0)),
                  pl.BlockSpec((None, None, D2 // 2, D), lambda i, be, nb, nv: (layer, be[blk(i, nb)], 0, 0)),
                  pl.BlockSpec((None, None, 1, D), lambda i, be, nb, nv: (layer, be[blk(i, nb)], 0, 0))],
        out_specs=pl.BlockSpec((bm, MOE_PACK_W), lambda i, be, nb, nv: (i, 0)),
        scratch_shapes=[pltpu.VMEM((D, D2), BF16), pltpu.VMEM((D2 // 2, D), BF16)],
    )
    return pl.pallas_call(
        functools.partial(_expert_body, bm=bm),
        grid_spec=grid_spec,
        out_shape=jax.ShapeDtypeStruct(xs.shape, I32),
        compiler_params=_cparams("arbitrary"),
        name="moe_experts",
    )(block_e, n_live, n_valid, xs, w_gu, b_gu.reshape(L, E, 1, D2), w_down, b_down.reshape(L, E, 1, D))


def _combine_body(segg_ref, segl_ref, segn_ref, ys_ref, slot_ref, gate_ref, h_ref, g_ref, b_ref, o_ref,
                  stage, mix_s, sem, *, tm):
    i = pl.program_id(0)

    slot_i = i % 2

    def copier(buf):
        def make_copy(g, l, size):
            return pltpu.make_async_copy(ys_ref.at[pl.ds(g, size), :], stage.at[buf, pl.ds(l, size), :],
                                         sem.at[buf])
        return make_copy

    @pl.when(i == 0)
    def _():
        stage[...] = jnp.zeros(stage.shape, I32)
        _segment_copies(segg_ref, segl_ref, segn_ref, i, copier(slot_i), True)

    @pl.when(i + 1 < pl.num_programs(0))
    def _():
        _segment_copies(segg_ref, segl_ref, segn_ref, i + 1, copier(1 - slot_i), True)

    slot = slot_ref[...]
    gate = gate_ref[...]
    for c in range(MOE_STAGE // MOE_PERM_CHUNK):
        c0 = c * MOE_PERM_CHUNK
        cols = lax.broadcasted_iota(I32, (tm, MOE_PERM_CHUNK), 1) + c0
        w = jnp.zeros((tm, MOE_PERM_CHUNK), F32)
        for kk in range(TOP_K):
            w = jnp.where(cols == slot[:, kk:kk + 1], gate[:, kk:kk + 1], w)
        mix_s[:, c0:c0 + MOE_PERM_CHUNK] = w.astype(BF16)

    _segment_waits(segl_ref, segn_ref, i, copier(slot_i))
    left, right = _unpack_pairs(stage[slot_i])
    y = jnp.concatenate([jnp.dot(mix_s[...], left, preferred_element_type=F32),
                         jnp.dot(mix_s[...], right, preferred_element_type=F32)], axis=1)
    o_ref[...] = _ln(ALPHA * h_ref[...] + y, g_ref[...], b_ref[...])


def _combine(seg_g, seg_l, seg_n, ys, slot, gate, h, g, b, *, tm=MOE_TM):
    T, D = h.shape
    grid_spec = pltpu.PrefetchScalarGridSpec(
        num_scalar_prefetch=3,
        grid=(T // tm,),
        in_specs=[pl.BlockSpec(memory_space=pl.ANY),
                  pl.BlockSpec((tm, TOP_K), lambda i, *_: (i, 0)),
                  pl.BlockSpec((tm, TOP_K), lambda i, *_: (i, 0)),
                  pl.BlockSpec((tm, D), lambda i, *_: (i, 0)),
                  pl.BlockSpec((1, D), lambda i, *_: (0, 0)),
                  pl.BlockSpec((1, D), lambda i, *_: (0, 0))],
        out_specs=pl.BlockSpec((tm, D), lambda i, *_: (i, 0)),
        scratch_shapes=[pltpu.VMEM((2, MOE_STAGE, MOE_PACK_W), I32), pltpu.VMEM((tm, MOE_STAGE), BF16),
                        pltpu.SemaphoreType.DMA((2,))],
    )
    return pl.pallas_call(
        functools.partial(_combine_body, tm=tm),
        grid_spec=grid_spec,
        out_shape=jax.ShapeDtypeStruct((T, D), F32),
        compiler_params=_cparams("arbitrary"),
        name="moe_combine",
    )(seg_g, seg_l, seg_n, ys, slot, gate, h, g.reshape(1, D), b.reshape(1, D))


def _moe_block(h, layer, w_router, b_router, w_gu, b_gu, w_down, b_down, g, b):
    T = h.shape[0]
    bm, tm, E = MOE_BM, MOE_TM, N_EXPERTS
    nt = T // tm
    slot_t, gate_t, cnt, lbase, gbase = _router(h, w_router, b_router)
    cnt, lbase, gbase = (t.reshape(nt, E, LANES)[:, :, 0] for t in (cnt, lbase, gbase))
    total = gbase[-1] + cnt[-1]
    padded = (total + bm - 1) // bm * bm
    pend = jnp.cumsum(padded)
    pstart = pend - padded
    seg_g = (pstart[None, :] + gbase).reshape(-1).astype(I32)
    seg_l = lbase.reshape(-1)
    seg_n = cnt.reshape(-1)
    n_blocks = -(-(T * TOP_K + nt * E * (MOE_SEG - 1)) // bm) + E
    block_start = jnp.arange(n_blocks, dtype=I32) * bm
    block_e = jnp.minimum(jnp.sum((pend[None, :] <= block_start[:, None]).astype(I32), axis=1), E - 1)
    n_live = (pend[-1:] // bm).astype(I32)
    n_valid = jnp.clip((pstart + total)[block_e] - block_start, 0, bm).astype(I32)
    tail_blk = n_live[0] + jnp.arange(n_blocks - T * TOP_K // bm, dtype=I32)
    tail_blk = jnp.where(tail_blk < n_blocks, tail_blk * bm, -1)
    zero_rows = jnp.concatenate([pstart + total, padded - total, tail_blk]).astype(I32)
    xs = _dispatch(seg_g, seg_l, seg_n, zero_rows, slot_t, h, n_blocks * bm)
    ys = _experts(block_e, n_live, n_valid, xs, layer, w_gu, b_gu, w_down, b_down)
    return _combine(seg_g, seg_l, seg_n, ys, slot_t.T, gate_t.T, h, g, b)


def kernel(x, rel_bias, a_w_in, a_w_out, b_w_in, b_ln_g, b_ln_b, b_w_s, b_b_s, b_w_out,
           c_w_in, c_w_gk_up, c_b_gk, c_norm_g, c_w_out, ln_g, ln_b,
           moe_w_router, moe_b_router, moe_w_gate_up, moe_b_gate_up, moe_w_down, moe_b_down):
    B, S, D = x.shape
    T = B * S
    h = x.reshape(T, D)
    tables = _bias_tables(rel_bias)
    for i in range(DEPTH):
        j = i // N_MIXERS
        mixer = i % N_MIXERS
        if mixer == 0:
            q_fold = jnp.where(jnp.arange(3 * D) < D, LOG2E * A_DH ** -0.5, 1.0).astype(F32)
            qkv = _matmul(h, (a_w_in[j] * q_fold).astype(BF16), out_dtype=BF16)
            o = _moba_attention(qkv.reshape(B, S, 3 * D), rel_bias, tables)
            h = _matmul_res_ln(o.reshape(T, D), a_w_out[j], h, ln_g[i, 0], ln_b[i, 0])
        elif mixer == 1:
            z = _matmul(h, b_w_in[j].astype(BF16), act="gelu", out_dtype=BF16)
            h = _gmlp_gate_out(z, h, b_ln_g[j], b_ln_b[j], b_w_s[j], b_b_s[j], b_w_out[j],
                               ln_g[i, 0], ln_b[i, 0])
        else:
            pad = LANES - C_GATE_RANK
            w_in = jnp.pad(c_w_in[j], ((0, 0), (0, pad)))
            w_up = jnp.pad(c_w_gk_up[j], ((0, pad), (0, 0)))
            proj = _matmul(h, w_in.astype(BF16), tn=640)
            o = _gla_core(proj.reshape(B, S, C_IN_WIDTH + pad), w_up, c_b_gk[j], c_norm_g[j])
            h = _matmul_res_ln(o.reshape(T, D), c_w_out[j], h, ln_g[i, 0], ln_b[i, 0])
        h = _moe_block(h, i, moe_w_router[i], moe_b_router[i], moe_w_gate_up, moe_b_gate_up,
                       moe_w_down, moe_b_down, ln_g[i, 1], ln_b[i, 1])
    return h.reshape(B, S, D)
```

```python
import functools
import math

import numpy as np
import jax
import jax.numpy as jnp
from jax import lax
from jax.experimental import pallas as pl
from jax.experimental.pallas import tpu as pltpu

F32 = jnp.float32
BF16 = jnp.bfloat16
I32 = jnp.int32
HI = lax.Precision.HIGHEST

D_MODEL = 1024
DEPTH = 4
N_MIXERS = 3
ALPHA = (2.0 * DEPTH) ** 0.25
LN_EPS = 1e-5
LOG2E = math.log2(math.e)

A_HEADS = 8
A_DH = D_MODEL // A_HEADS
MOBA_BLOCK = 256
MOBA_TOPK = 3
REL_BUCKETS = 32
REL_MAX_DIST = 128

B_GROUPS = 8
B_WIDTH = 2 * D_MODEL
B_CHUNK = 128
B_GW = B_WIDTH // B_GROUPS

C_HEADS = 4
C_KEY_DIM = D_MODEL // 2
C_VAL_DIM = D_MODEL
C_DK = C_KEY_DIM // C_HEADS
C_DV = C_VAL_DIM // C_HEADS
C_GATE_RANK = 16
C_GATE_NORMALIZER = 16.0
C_IN_WIDTH = 2 * C_KEY_DIM + 2 * C_VAL_DIM + C_GATE_RANK

N_EXPERTS = 32
TOP_K = 4
SWIGLU_LIMIT = 7.0
SWIGLU_ALPHA = 1.702

LANES = 128
SUBLANES = 8
VMEM_LIMIT = 56 * 1024 * 1024

GLA_CHUNK = 128
GLA_SUB = 16
GLA_HEADS_PER_STEP = 2
MOE_BM = 1024
MOE_SUB = 256
MOE_TM = 512
MOE_SEG = SUBLANES
MOE_SEG_BITS = (MOE_TM // MOE_SEG).bit_length()
MOE_STAGE = MOE_TM * TOP_K + N_EXPERTS * MOE_SEG
MOE_PERM_CHUNK = 256
MOE_PACK_W = D_MODEL // 2


def _cparams(*sem):
    return pltpu.CompilerParams(dimension_semantics=sem, vmem_limit_bytes=VMEM_LIMIT)


def _ln(x, g, b):
    mu = jnp.mean(x, axis=-1, keepdims=True)
    xc = x - mu
    var = jnp.mean(xc * xc, axis=-1, keepdims=True)
    return xc * lax.rsqrt(var + LN_EPS) * g + b


def _mm_body(x_ref, w_ref, o_ref, *, act, tn):
    x = x_ref[...].astype(BF16)
    for n0 in range(0, w_ref.shape[1], tn):
        acc = jnp.dot(x, w_ref[:, n0:n0 + tn], preferred_element_type=F32)
        if act == "gelu":
            acc = 0.5 * acc * (1.0 + lax.erf(acc * (2.0 ** -0.5)))
        o_ref[:, n0:n0 + tn] = acc.astype(o_ref.dtype)


def _matmul(x, w, *, act=None, tm=512, tn=512, out_dtype=F32):
    M, K = x.shape
    N = w.shape[1]
    assert M % tm == 0 and N % tn == 0
    return pl.pallas_call(
        functools.partial(_mm_body, act=act, tn=tn),
        grid=(M // tm,),
        in_specs=[pl.BlockSpec((tm, K), lambda i: (i, 0)),
                  pl.BlockSpec((K, N), lambda i: (0, 0))],
        out_specs=pl.BlockSpec((tm, N), lambda i: (i, 0)),
        out_shape=jax.ShapeDtypeStruct((M, N), out_dtype),
        compiler_params=_cparams("arbitrary"),
        name="proj_matmul",
    )(x, w)


def _mm_res_ln_body(x_ref, w_ref, h_ref, g_ref, b_ref, o_ref, wb_ref):
    @pl.when(pl.program_id(0) == 0)
    def _():
        wb_ref[...] = w_ref[...].astype(BF16)

    t = jnp.dot(x_ref[...].astype(BF16), wb_ref[...], preferred_element_type=F32)
    o_ref[...] = _ln(ALPHA * h_ref[...] + t, g_ref[...], b_ref[...])


def _matmul_res_ln(x, w, h, g, b, *, tm=512):
    M, K = x.shape
    N = w.shape[1]
    return pl.pallas_call(
        _mm_res_ln_body,
        grid=(M // tm,),
        in_specs=[pl.BlockSpec((tm, K), lambda i: (i, 0)),
                  pl.BlockSpec((K, N), lambda i: (0, 0)),
                  pl.BlockSpec((tm, N), lambda i: (i, 0)),
                  pl.BlockSpec((1, N), lambda i: (0, 0)),
                  pl.BlockSpec((1, N), lambda i: (0, 0))],
        out_specs=pl.BlockSpec((tm, N), lambda i: (i, 0)),
        out_shape=jax.ShapeDtypeStruct((M, N), F32),
        scratch_shapes=[pltpu.VMEM((K, N), BF16)],
        compiler_params=_cparams("arbitrary"),
        name="outproj_res_ln",
    )(x, w, h, g.reshape(1, N), b.reshape(1, N))


def _t5_bucket_lower_bounds():
    n = np.arange(0, 4 * REL_MAX_DIST, dtype=np.int64)
    max_exact = REL_BUCKETS // 2
    nf = np.maximum(n, 1).astype(np.float32)
    large = max_exact + (np.log(nf / np.float32(max_exact)) / np.float32(math.log(REL_MAX_DIST / max_exact))
                         * np.float32(REL_BUCKETS - max_exact)).astype(np.int32)
    large = np.minimum(large, REL_BUCKETS - 1)
    bucket = np.where(n < max_exact, n, large)
    assert np.all(np.diff(bucket) >= 0) and bucket[-1] == REL_BUCKETS - 1
    return [int(np.argmax(bucket >= b)) for b in range(REL_BUCKETS)]


_BUCKET_LO = _t5_bucket_lower_bounds()


def _bias_table_body(rel_ref, o_ref):
    blk = MOBA_BLOCK
    row = lax.broadcasted_iota(I32, (blk, blk), 0)
    col = lax.broadcasted_iota(I32, (blk, blk), 1)
    for t in range(2):
        d = row - col + t * blk
        for h in range(A_HEADS):
            val = jnp.full((blk, blk), rel_ref[REL_BUCKETS - 1, h], F32)
            for b in range(REL_BUCKETS - 2, -1, -1):
                val = jnp.where(d < _BUCKET_LO[b + 1], rel_ref[b, h], val)
            val = val * LOG2E
            if t == 0:
                val = jnp.where(d >= 0, val, -jnp.inf)
            o_ref[h, t] = val


def _bias_tables(rel_bias):
    return pl.pallas_call(
        _bias_table_body,
        in_specs=[pl.BlockSpec(memory_space=pltpu.SMEM)],
        out_shape=jax.ShapeDtypeStruct((A_HEADS, 2, MOBA_BLOCK, MOBA_BLOCK), F32),
        compiler_params=pltpu.CompilerParams(vmem_limit_bytes=VMEM_LIMIT),
        name="rel_bias_tables",
    )(rel_bias)


def _moba_body(rel_ref, q_ref, k_ref, v_ref, tb_ref, o_ref, s_s, p_s, acc_s, rden_s, *, seq):
    blk, dh = MOBA_BLOCK, A_DH
    nb = seq // blk
    h = pl.program_id(1)
    kmean = jnp.mean(k_ref[0].astype(F32).reshape(nb, blk, dh), axis=1)
    kmean_pad = jnp.concatenate([kmean, jnp.zeros((LANES - nb, dh), F32)], axis=0)
    b_far = rel_ref[REL_BUCKETS - 1, h] * LOG2E
    lane = lax.broadcasted_iota(I32, (blk, LANES), 1)
    ones_bf = jnp.ones((LANES, LANES), BF16)

    for n in range(nb):
        r0 = n * blk
        s_s[r0:seq, r0:r0 + blk] = lax.dot_general(q_ref[0, r0:seq, :], k_ref[0, r0:r0 + blk, :],
                                                   (((1,), (1,)), ((), ())), preferred_element_type=F32)

    for c in range(nb):
        q = q_ref[0, c * blk:(c + 1) * blk, :]
        n_keys = (c + 1) * blk
        s = s_s[c * blk:(c + 1) * blk, 0:n_keys]
        if c > MOBA_TOPK:
            gate = lax.dot_general(q.astype(F32), kmean_pad, (((1,), (1,)), ((), ())),
                                   precision=HI, preferred_element_type=F32)
            gate = jnp.where(lane < c, gate, -jnp.inf)
        pieces = []
        for n in range(c):
            shift = b_far
            if c > MOBA_TOPK:
                gn = gate[:, n:n + 1]
                beats = (gate > gn) | ((gate == gn) & (lane < n))
                rank = jnp.dot(jnp.where(beats, 1.0, 0.0).astype(BF16), ones_bf, preferred_element_type=F32)
                shift = jnp.where(rank < MOBA_TOPK, b_far, -jnp.inf)
                shift = jnp.concatenate([shift] * (blk // LANES), axis=1)
            if n == c - 1:
                shift = tb_ref[0, 1] + (shift - b_far)
            pieces.append(s[:, n * blk:(n + 1) * blk] + shift)
        pieces.append(s[:, c * blk:(c + 1) * blk] + tb_ref[0, 0])
        logits = jnp.concatenate(pieces, axis=1) if c > 0 else pieces[0]
        m = jnp.max(logits, axis=1, keepdims=True)
        p = jnp.exp2(logits - m)
        denom = jnp.sum(p, axis=1, keepdims=True)
        p_s[c * blk:(c + 1) * blk, 0:n_keys] = p.astype(BF16)
        rden_s[c * blk:(c + 1) * blk, :] = jnp.broadcast_to(1.0 / denom, (blk, dh))

    for n in range(nb):
        r0 = n * blk
        pv = jnp.dot(p_s[r0:seq, r0:r0 + blk], v_ref[0, r0:r0 + blk, :], preferred_element_type=F32)
        if n == 0:
            acc_s[...] = pv
        else:
            acc_s[r0:seq, :] = acc_s[r0:seq, :] + pv
    o_ref[0] = acc_s[...] * rden_s[...]


def _moba_attention(qkv, rel_bias, tables):
    B, S, _ = qkv.shape
    H, dh = A_HEADS, A_DH
    assert S % MOBA_BLOCK == 0 and S // MOBA_BLOCK <= SUBLANES
    return pl.pallas_call(
        functools.partial(_moba_body, seq=S),
        grid=(B, H),
        in_specs=[pl.BlockSpec(memory_space=pltpu.SMEM),
                  pl.BlockSpec((1, S, dh), lambda b, h: (b, 0, h)),
                  pl.BlockSpec((1, S, dh), lambda b, h: (b, 0, H + h)),
                  pl.BlockSpec((1, S, dh), lambda b, h: (b, 0, 2 * H + h)),
                  pl.BlockSpec((1, 2, MOBA_BLOCK, MOBA_BLOCK), lambda b, h: (h, 0, 0, 0))],
        out_specs=pl.BlockSpec((1, S, dh), lambda b, h: (b, 0, h)),
        out_shape=jax.ShapeDtypeStruct((B, S, H * dh), F32),
        scratch_shapes=[pltpu.VMEM((S, S), F32), pltpu.VMEM((S, S), BF16),
                        pltpu.VMEM((S, dh), F32), pltpu.VMEM((S, dh), F32)],
        compiler_params=_cparams("arbitrary", "arbitrary"),
        name="moba_attention",
    )(rel_bias, qkv, qkv, qkv, tables)


def _gmlp_body(u_ref, v_ref, h_ref, vg_ref, vb_ref, ws_ref, bs_ref, wo_ref, g_ref, b_ref, o_ref,
               wst_s, wob_s, *, tm):
    @pl.when(pl.program_id(0) == 0)
    def _():
        row = lax.broadcasted_iota(I32, (B_CHUNK, B_CHUNK), 0)
        col = lax.broadcasted_iota(I32, (B_CHUNK, B_CHUNK), 1)
        for g in range(B_GROUPS):
            wst_s[g] = jnp.where(row >= col, ws_ref[g], 0.0).astype(BF16)
        wob_s[...] = wo_ref[...].astype(BF16)

    vn = _ln(v_ref[...].astype(F32), vg_ref[...], vb_ref[...]).astype(BF16)
    u = u_ref[...].astype(F32)
    rows = []
    for c in range(tm // B_CHUNK):
        r0 = c * B_CHUNK
        cols = []
        for g in range(B_GROUPS):
            c0 = g * B_GW
            mixed = jnp.dot(wst_s[g], vn[r0:r0 + B_CHUNK, c0:c0 + B_GW],
                            preferred_element_type=F32) + bs_ref[g]
            cols.append(u[r0:r0 + B_CHUNK, c0:c0 + B_GW] * mixed)
        rows.append(jnp.concatenate(cols, axis=1))
    y = jnp.concatenate(rows, axis=0).astype(BF16)
    t = jnp.dot(y, wob_s[...], preferred_element_type=F32)
    o_ref[...] = _ln(ALPHA * h_ref[...] + t, g_ref[...], b_ref[...])


def _gmlp_gate_out(z, h, v_g, v_b, w_s, b_s, w_out, g, b, *, tm=256):
    T = z.shape[0]
    D = D_MODEL
    return pl.pallas_call(
        functools.partial(_gmlp_body, tm=tm),
        grid=(T // tm,),
        in_specs=[pl.BlockSpec((tm, B_WIDTH), lambda i: (i, 0)),
                  pl.BlockSpec((tm, B_WIDTH), lambda i: (i, 1)),
                  pl.BlockSpec((tm, D), lambda i: (i, 0)),
                  pl.BlockSpec((1, B_WIDTH), lambda i: (0, 0)),
                  pl.BlockSpec((1, B_WIDTH), lambda i: (0, 0)),
                  pl.BlockSpec((B_GROUPS, B_CHUNK, B_CHUNK), lambda i: (0, 0, 0)),
                  pl.BlockSpec((B_GROUPS, B_CHUNK, 1), lambda i: (0, 0, 0)),
                  pl.BlockSpec((B_WIDTH, D), lambda i: (0, 0)),
                  pl.BlockSpec((1, D), lambda i: (0, 0)),
                  pl.BlockSpec((1, D), lambda i: (0, 0))],
        out_specs=pl.BlockSpec((tm, D), lambda i: (i, 0)),
        out_shape=jax.ShapeDtypeStruct((T, D), F32),
        scratch_shapes=[pltpu.VMEM((B_GROUPS, B_CHUNK, B_CHUNK), BF16), pltpu.VMEM((B_WIDTH, D), BF16)],
        compiler_params=_cparams("arbitrary"),
        name="gmlp_gate_out",
    )(z, z, h, v_g.reshape(1, B_WIDTH), v_b.reshape(1, B_WIDTH), w_s, b_s.reshape(B_GROUPS, B_CHUNK, 1),
      w_out, g.reshape(1, D), b.reshape(1, D))


def _gla_body(q_ref, k_ref, v_ref, g_ref, gl_ref, wup_ref, bgk_ref, ng_ref, o_ref, la_s, st_s, raw_s, *, seq):
    ch, sub, dk, dv = GLA_CHUNK, GLA_SUB, C_DK, C_DV
    x = jnp.dot(gl_ref[0], wup_ref[...], precision=HI, preferred_element_type=F32) + bgk_ref[...]
    la_s[...] = (jnp.minimum(x, 0.0) - jnp.log1p(jnp.exp(-jnp.abs(x)))) * (1.0 / C_GATE_NORMALIZER)
    st_s[...] = jnp.zeros(st_s.shape, F32)

    row_c = lax.broadcasted_iota(I32, (ch, ch), 0)
    col_c = lax.broadcasted_iota(I32, (ch, ch), 1)
    tri = (row_c >= col_c).astype(F32)
    eye = (row_c == col_c).astype(F32)
    sub_i = lax.broadcasted_iota(I32, (sub, ch), 0)
    lane_j = lax.broadcasted_iota(I32, (sub, ch), 1)

    def chunk(c, carry):
        for hh in range(GLA_HEADS_PER_STEP):
            chunk_head(c, hh)
        return carry

    def chunk_head(c, hh):
        s0 = pl.multiple_of(c * ch, ch)
        a = la_s[pl.ds(s0, ch), hh * dk:(hh + 1) * dk]
        b = jnp.dot(tri, a, precision=HI, preferred_element_type=F32)
        q = q_ref[0, pl.ds(s0, ch), hh * dk:(hh + 1) * dk] * (dk ** -0.5)
        k = k_ref[0, pl.ds(s0, ch), hh * dk:(hh + 1) * dk]
        v = v_ref[0, pl.ds(s0, ch), hh * dv:(hh + 1) * dv].astype(BF16)
        st = st_s[hh]
        o = jnp.dot((q * jnp.exp(b)).astype(BF16), st.astype(BF16), preferred_element_type=F32)

        blocks = []
        for sb in range(ch // sub):
            r0 = sb * sub
            q_i = q[r0:r0 + sub]
            b_i = b[r0:r0 + sub]
            k_i = k[r0:r0 + sub]
            att = jnp.zeros((sub, ch), F32)
            for j in range(sub):
                e = jnp.exp(jnp.minimum(b_i - b_i[j:j + 1, :], 0.0))
                colv = jnp.sum(q_i * k_i[j:j + 1, :] * e, axis=1, keepdims=True)
                att = jnp.where((lane_j == r0 + j) & (sub_i >= j), colv, att)
            if sb > 0:
                ref_b = b[r0 - 1:r0, :]
                q_t = q_i * jnp.exp(b_i - ref_b)
                k_t = k * jnp.exp(jnp.minimum(ref_b - b, 0.0))
                off = lax.dot_general(q_t.astype(BF16), k_t.astype(BF16), (((1,), (1,)), ((), ())),
                                      preferred_element_type=F32)
                att = jnp.where(lane_j < r0, off, att)
            blocks.append(att)
        att_full = jnp.concatenate(blocks, axis=0)
        o = o + jnp.dot(att_full.astype(BF16), v, preferred_element_type=F32)
        raw_s[pl.ds(s0, ch), hh * dv:(hh + 1) * dv] = o

        b_last = b[ch - 1:ch, :]
        k_d = k * jnp.exp(b_last - b)
        upd = jnp.dot(k_d.T.astype(BF16), v, preferred_element_type=F32)
        decay_col = lax.dot_general(eye, jnp.broadcast_to(jnp.exp(b_last), (SUBLANES, dk)),
                                    (((1,), (1,)), ((), ())), precision=HI,
                                    preferred_element_type=F32)[:, 0:1]
        st_s[hh] = decay_col * st + upd

    lax.fori_loop(0, seq // ch, chunk, 0)

    for hh in range(GLA_HEADS_PER_STEP):
        o = raw_s[:, hh * dv:(hh + 1) * dv]
        rms = o * lax.rsqrt(jnp.mean(o * o, axis=-1, keepdims=True) + LN_EPS) * ng_ref[...]
        gg = g_ref[0, :, hh * dv:(hh + 1) * dv]
        o_ref[0, :, hh * dv:(hh + 1) * dv] = rms * (gg * jax.nn.sigmoid(gg))


def _gla_core(proj, w_up_pad, b_gk, norm_g):
    B, S, _ = proj.shape
    H, hps = C_HEADS, GLA_HEADS_PER_STEP
    dk, dv = C_DK * hps, C_DV * hps
    assert C_DK == LANES and S % GLA_CHUNK == 0 and H % hps == 0
    k_off = C_KEY_DIM // dk
    v_off = 2 * C_KEY_DIM // dv
    g_off = (2 * C_KEY_DIM + C_VAL_DIM) // dv
    gl_off = (2 * C_KEY_DIM + 2 * C_VAL_DIM) // LANES
    return pl.pallas_call(
        functools.partial(_gla_body, seq=S),
        grid=(B, H // hps),
        in_specs=[pl.BlockSpec((1, S, dk), lambda b, h: (b, 0, h)),
                  pl.BlockSpec((1, S, dk), lambda b, h: (b, 0, k_off + h)),
                  pl.BlockSpec((1, S, dv), lambda b, h: (b, 0, v_off + h)),
                  pl.BlockSpec((1, S, dv), lambda b, h: (b, 0, g_off + h)),
                  pl.BlockSpec((1, S, LANES), lambda b, h: (b, 0, gl_off)),
                  pl.BlockSpec((LANES, dk), lambda b, h: (0, h)),
                  pl.BlockSpec((1, dk), lambda b, h: (0, h)),
                  pl.BlockSpec((1, C_DV), lambda b, h: (0, 0))],
        out_specs=pl.BlockSpec((1, S, dv), lambda b, h: (b, 0, h)),
        out_shape=jax.ShapeDtypeStruct((B, S, C_VAL_DIM), F32),
        scratch_shapes=[pltpu.VMEM((S, dk), F32), pltpu.VMEM((hps, C_DK, C_DV), F32), pltpu.VMEM((S, dv), F32)],
        compiler_params=_cparams("arbitrary", "arbitrary"),
        name="gla_core",
    )(proj, proj, proj, proj, proj, w_up_pad, b_gk.reshape(1, C_KEY_DIM), norm_g.reshape(1, C_DV))


def _router_body(h_ref, wt_ref, b_ref, slot_ref, gate_ref, cnt_ref, lbase_ref, gbase_ref, run_s, ws_s, *, tm):
    E = N_EXPERTS

    @pl.when(pl.program_id(0) == 0)
    def _():
        run_s[...] = jnp.zeros(run_s.shape, F32)
        w = wt_ref[...]
        w1 = w.astype(BF16)
        ws_s[0:E, :] = w1
        ws_s[E:2 * E, :] = (w - w1.astype(F32)).astype(BF16)

    h = h_ref[...]
    h1 = h.astype(BF16)
    h2 = (h - h1.astype(F32)).astype(BF16)
    nt_dims = (((1,), (1,)), ((), ()))
    both = lax.dot_general(ws_s[...], h1, nt_dims, preferred_element_type=F32)
    logits = (both[0:E] + (both[E:2 * E] + lax.dot_general(ws_s[0:E, :], h2, nt_dims, preferred_element_type=F32))
              + b_ref[...])
    row = lax.broadcasted_iota(I32, (E, tm), 0)
    rest = logits
    top_val, onehot = [], []
    for _ in range(TOP_K):
        m = jnp.max(rest, axis=0, keepdims=True)
        idx = jnp.min(jnp.where(rest == m, row, E), axis=0, keepdims=True)
        oh = row == idx
        rest = jnp.where(oh, -jnp.inf, rest)
        top_val.append(m)
        onehot.append(oh)
    ex = [jnp.exp(v - top_val[0]) for v in top_val]
    den = ex[0] + ex[1] + ex[2] + ex[3]
    chosen = sum(oh.astype(F32) for oh in onehot)

    tr = lax.broadcasted_iota(I32, (tm, tm), 0)
    tc = lax.broadcasted_iota(I32, (tm, tm), 1)
    earlier = (tr < tc).astype(BF16)
    seen = jnp.dot(chosen.astype(BF16), earlier, preferred_element_type=F32)
    cnt = jnp.sum(chosen, axis=1, keepdims=True)
    cnt_pad = jnp.floor((cnt + (MOE_SEG - 1)) * (1.0 / MOE_SEG)) * MOE_SEG
    cnt_cols = jnp.broadcast_to(cnt_pad, (E, LANES))
    er = lax.broadcasted_iota(I32, (E, E), 0)
    ec = lax.broadcasted_iota(I32, (E, E), 1)
    lbase_cols = jnp.dot((ec < er).astype(F32), cnt_cols, precision=HI, preferred_element_type=F32)
    pos = seen + lbase_cols[:, 0:1]

    row_k = lax.broadcasted_iota(I32, (TOP_K, tm), 0)
    s_out = jnp.zeros((TOP_K, tm), I32)
    g_out = jnp.zeros((TOP_K, tm), F32)
    for kk in range(TOP_K):
        sk = jnp.sum(jnp.where(onehot[kk], pos, 0.0), axis=0, keepdims=True)
        s_out = jnp.where(row_k == kk, sk.astype(I32), s_out)
        g_out = jnp.where(row_k == kk, ex[kk] / den, g_out)
    slot_ref[...] = s_out
    gate_ref[...] = g_out
    cnt_ref[...] = cnt_cols.astype(I32)
    lbase_ref[...] = lbase_cols.astype(I32)
    gbase_ref[...] = run_s[...].astype(I32)
    run_s[...] = run_s[...] + cnt_cols


def _router(h, w, b, *, tm=MOE_TM):
    T, D = h.shape
    E = N_EXPERTS
    nt = T // tm
    tab = jax.ShapeDtypeStruct((nt * E, LANES), I32)
    tab_spec = pl.BlockSpec((E, LANES), lambda i: (i, 0))
    return pl.pallas_call(
        functools.partial(_router_body, tm=tm),
        grid=(nt,),
        in_specs=[pl.BlockSpec((tm, D), lambda i: (i, 0)),
                  pl.BlockSpec((E, D), lambda i: (0, 0)),
                  pl.BlockSpec((E, 1), lambda i: (0, 0))],
        out_specs=[pl.BlockSpec((TOP_K, tm), lambda i: (0, i)),
                   pl.BlockSpec((TOP_K, tm), lambda i: (0, i)),
                   tab_spec, tab_spec, tab_spec],
        out_shape=[jax.ShapeDtypeStruct((TOP_K, T), I32),
                   jax.ShapeDtypeStruct((TOP_K, T), F32),
                   tab, tab, tab],
        scratch_shapes=[pltpu.VMEM((E, LANES), F32), pltpu.VMEM((2 * E, D), BF16)],
        compiler_params=_cparams("arbitrary"),
        name="moe_router",
    )(h, w.T, b.reshape(E, 1))


def _pack_pairs(left, right):
    hi = lax.bitcast_convert_type(left, I32) & jnp.int32(-65536)
    lo = lax.shift_right_logical(lax.bitcast_convert_type(right, I32), 16)
    return hi | lo


def _unpack_pairs(word):
    left = lax.bitcast_convert_type(word & jnp.int32(-65536), F32)
    right = lax.bitcast_convert_type(lax.shift_left(word, 16), F32)
    return left.astype(BF16), right.astype(BF16)


def _segment_copies(seg_g, seg_l, seg_n, tile, make_copy, start):
    def per_expert(e, carry):
        idx = tile * N_EXPERTS + e
        n = seg_n[idx]
        g = seg_g[idx]
        l = seg_l[idx]
        for bit in range(MOE_SEG_BITS - 1, -1, -1):
            size = MOE_SEG << bit

            @pl.when((n & size) != 0)
            def _():
                off = n & ~(2 * size - 1)
                cp = make_copy(pl.multiple_of(g + off, MOE_SEG), pl.multiple_of(l + off, MOE_SEG), size)
                if start:
                    cp.start()
                else:
                    cp.wait()
        return carry

    lax.fori_loop(0, N_EXPERTS, per_expert, 0)


def _segment_waits(seg_l, seg_n, tile, make_copy):
    last = tile * N_EXPERTS + N_EXPERTS - 1
    total = seg_l[last] + seg_n[last]
    for bit in range((MOE_STAGE // MOE_SEG).bit_length() - 1, -1, -1):
        size = MOE_SEG << bit

        @pl.when((total & size) != 0)
        def _():
            make_copy(0, 0, size).wait()


def _dispatch_body(segg_ref, segl_ref, segn_ref, zf_ref, slot_t_ref, h_ref, xs_ref, stage, zbuf, sem, *, tm, bm):
    i = pl.program_id(0)

    def zero_fill(start):
        def act(cp):
            if start:
                cp.start()
            else:
                cp.wait()

        def per_expert(e, carry):
            g = zf_ref[e]
            n = zf_ref[N_EXPERTS + e]
            for bit in range((bm // MOE_SEG).bit_length() - 2, -1, -1):
                size = MOE_SEG << bit

                @pl.when((n & size) != 0)
                def _():
                    off = n & ~(2 * size - 1)
                    act(pltpu.make_async_copy(
                        zbuf.at[pl.ds(0, size), :],
                        xs_ref.at[pl.ds(pl.multiple_of(g + off, MOE_SEG), size), :], sem.at[2]))
            return carry

        def per_tail(j, carry):
            row = zf_ref[2 * N_EXPERTS + j]

            @pl.when(row >= 0)
            def _():
                act(pltpu.make_async_copy(zbuf, xs_ref.at[pl.ds(pl.multiple_of(row, MOE_SEG), bm), :], sem.at[2]))
            return carry

        lax.fori_loop(0, N_EXPERTS, per_expert, 0)
        lax.fori_loop(0, zf_ref.shape[0] - 2 * N_EXPERTS, per_tail, 0)

    @pl.when(i == 0)
    def _():
        zbuf[...] = jnp.zeros(zbuf.shape, I32)
        zero_fill(True)

    @pl.when(i == pl.num_programs(0) - 1)
    def _():
        zero_fill(False)

    slot_i = i % 2
    hb = h_ref[...].astype(BF16)
    for c in range(MOE_STAGE // MOE_PERM_CHUNK):
        r0 = c * MOE_PERM_CHUNK
        rows = lax.broadcasted_iota(I32, (MOE_PERM_CHUNK, tm), 0) + r0
        hit = rows == slot_t_ref[0:1, :]
        for kk in range(1, TOP_K):
            hit = hit | (rows == slot_t_ref[kk:kk + 1, :])
        perm = jnp.where(hit, 1.0, 0.0).astype(BF16)
        rows_f = jnp.dot(perm, hb, preferred_element_type=F32)
        stage[slot_i, r0:r0 + MOE_PERM_CHUNK, :] = _pack_pairs(rows_f[:, :MOE_PACK_W], rows_f[:, MOE_PACK_W:])

    def copier(buf):
        def make_copy(g, l, size):
            return pltpu.make_async_copy(stage.at[buf, pl.ds(l, size), :], xs_ref.at[pl.ds(g, size), :],
                                         sem.at[buf])
        return make_copy

    _segment_copies(segg_ref, segl_ref, segn_ref, i, copier(slot_i), True)

    @pl.when(i > 0)
    def _():
        _segment_waits(segl_ref, segn_ref, i - 1, copier(1 - slot_i))

    @pl.when(i == pl.num_programs(0) - 1)
    def _():
        _segment_waits(segl_ref, segn_ref, i, copier(slot_i))


def _dispatch(seg_g, seg_l, seg_n, zero_rows, slot_t, h, n_rows, *, tm=MOE_TM, bm=MOE_BM):
    T, D = h.shape
    grid_spec = pltpu.PrefetchScalarGridSpec(
        num_scalar_prefetch=4,
        grid=(T // tm,),
        in_specs=[pl.BlockSpec((TOP_K, tm), lambda i, *_: (0, i)),
                  pl.BlockSpec((tm, D), lambda i, *_: (i, 0))],
        out_specs=pl.BlockSpec(memory_space=pl.ANY),
        scratch_shapes=[pltpu.VMEM((2, MOE_STAGE, MOE_PACK_W), I32), pltpu.VMEM((bm, MOE_PACK_W), I32),
                        pltpu.SemaphoreType.DMA((3,))],
    )
    return pl.pallas_call(
        functools.partial(_dispatch_body, tm=tm, bm=bm),
        grid_spec=grid_spec,
        out_shape=jax.ShapeDtypeStruct((n_rows, MOE_PACK_W), I32),
        compiler_params=pltpu.CompilerParams(dimension_semantics=("arbitrary",), vmem_limit_bytes=VMEM_LIMIT,
                                             has_side_effects=True),
        name="moe_dispatch",
    )(seg_g, seg_l, seg_n, zero_rows, slot_t, h)


def _expert_body(be_ref, nb_ref, nv_ref, nx_ref, xs_ref, wgu_hbm, bgu_ref, wd_hbm, bd_ref, ys_ref,
                 wgu_f, wd_f, wgu_s, wd_s, sem, *, bm, layer):
    i = pl.program_id(0)
    e = be_ref[i]
    prev = be_ref[jnp.maximum(i - 1, 0)]
    live = i < nb_ref[0]
    n_valid = jnp.where(live, nv_ref[i], 0)

    def weight_copies(expert):
        return (pltpu.make_async_copy(wgu_hbm.at[layer, expert], wgu_f, sem.at[0]),
                pltpu.make_async_copy(wd_hbm.at[layer, expert], wd_f, sem.at[1]))

    @pl.when(i == 0)
    def _():
        for cp in weight_copies(e):
            cp.start()

    @pl.when(live & ((i == 0) | (e != prev)))
    def _():
        for cp in weight_copies(e):
            cp.wait()
        wgu_s[...] = wgu_f[...].astype(BF16)
        wd_s[...] = wd_f[...].astype(BF16)
        nxt = nx_ref[e]

        @pl.when(nxt >= 0)
        def _():
            for cp in weight_copies(nxt):
                cp.start()

    def mlp(r0, rows):
        x = jnp.concatenate(_unpack_pairs(xs_ref[r0:r0 + rows, :]), axis=1)
        hh = jnp.dot(x, wgu_s[...], preferred_element_type=F32) + bgu_ref[...]
        d_e = hh.shape[1] // 2
        h_gate = jnp.minimum(hh[:, :d_e], SWIGLU_LIMIT)
        h_up = jnp.clip(hh[:, d_e:], -SWIGLU_LIMIT, SWIGLU_LIMIT)
        act = h_gate * jax.nn.sigmoid(SWIGLU_ALPHA * h_gate) * (h_up + 1.0)
        y = jnp.dot(act.astype(BF16), wd_s[...], preferred_element_type=F32) + bd_ref[...]
        y = y.astype(BF16).astype(F32)
        ys_ref[r0:r0 + rows, :] = _pack_pairs(y[:, :MOE_PACK_W], y[:, MOE_PACK_W:])

    full = n_valid > bm - MOE_SUB

    @pl.when(full)
    def _():
        mlp(0, bm)

    for r0 in range(0, bm, MOE_SUB):
        @pl.when(jnp.logical_not(full) & (n_valid > r0))
        def _(r0=r0):
            mlp(r0, MOE_SUB)

        @pl.when(n_valid <= r0)
        def _(r0=r0):
            ys_ref[r0:r0 + MOE_SUB, :] = jnp.zeros((MOE_SUB, ys_ref.shape[1]), I32)


def _experts(block_e, n_live, n_valid, next_e, xs, layer, w_gu, b_gu, w_down, b_down, *, bm=MOE_BM):
    L, E, D, D2 = w_gu.shape
    n_blocks = xs.shape[0] // bm

    def blk(i, nb):
        return jnp.maximum(jnp.minimum(i, nb[0] - 1), 0)

    grid_spec = pltpu.PrefetchScalarGridSpec(
        num_scalar_prefetch=4,
        grid=(n_blocks,),
        in_specs=[pl.BlockSpec((bm, MOE_PACK_W), lambda i, be, nb, nv, nx: (blk(i, nb), 0)),
                  pl.BlockSpec(memory_space=pl.ANY),
                  pl.BlockSpec((None, None, 1, D2), lambda i, be, nb, nv, nx: (layer, be[blk(i, nb)], 0, 0)),
                  pl.BlockSpec(memory_space=pl.ANY),
                  pl.BlockSpec((None, None, 1, D), lambda i, be, nb, nv, nx: (layer, be[blk(i, nb)], 0, 0))],
        out_specs=pl.BlockSpec((bm, MOE_PACK_W), lambda i, be, nb, nv, nx: (i, 0)),
        scratch_shapes=[pltpu.VMEM((D, D2), F32), pltpu.VMEM((D2 // 2, D), F32),
                        pltpu.VMEM((D, D2), BF16), pltpu.VMEM((D2 // 2, D), BF16),
                        pltpu.SemaphoreType.DMA((2,))],
    )
    return pl.pallas_call(
        functools.partial(_expert_body, bm=bm, layer=layer),
        grid_spec=grid_spec,
        out_shape=jax.ShapeDtypeStruct(xs.shape, I32),
        compiler_params=_cparams("arbitrary"),
        name="moe_experts",
    )(block_e, n_live, n_valid, next_e, xs, w_gu, b_gu.reshape(L, E, 1, D2), w_down, b_down.reshape(L, E, 1, D))


def _combine_body(segg_ref, segl_ref, segn_ref, ys_ref, slot_ref, gate_ref, h_ref, g_ref, b_ref, o_ref,
                  stage, mix_s, sem, *, tm):
    i = pl.program_id(0)

    slot_i = i % 2

    def copier(buf):
        def make_copy(g, l, size):
            return pltpu.make_async_copy(ys_ref.at[pl.ds(g, size), :], stage.at[buf, pl.ds(l, size), :],
                                         sem.at[buf])
        return make_copy

    @pl.when(i == 0)
    def _():
        stage[...] = jnp.zeros(stage.shape, I32)
        _segment_copies(segg_ref, segl_ref, segn_ref, i, copier(slot_i), True)

    @pl.when(i + 1 < pl.num_programs(0))
    def _():
        _segment_copies(segg_ref, segl_ref, segn_ref, i + 1, copier(1 - slot_i), True)

    slot = slot_ref[...]
    gate = gate_ref[...]
    for c in range(MOE_STAGE // MOE_PERM_CHUNK):
        c0 = c * MOE_PERM_CHUNK
        cols = lax.broadcasted_iota(I32, (tm, MOE_PERM_CHUNK), 1) + c0
        w = jnp.zeros((tm, MOE_PERM_CHUNK), F32)
        for kk in range(TOP_K):
            w = jnp.where(cols == slot[:, kk:kk + 1], gate[:, kk:kk + 1], w)
        mix_s[:, c0:c0 + MOE_PERM_CHUNK] = w.astype(BF16)

    _segment_waits(segl_ref, segn_ref, i, copier(slot_i))
    left, right = _unpack_pairs(stage[slot_i])
    y = jnp.concatenate([jnp.dot(mix_s[...], left, preferred_element_type=F32),
                         jnp.dot(mix_s[...], right, preferred_element_type=F32)], axis=1)
    o_ref[...] = _ln(ALPHA * h_ref[...] + y, g_ref[...], b_ref[...])


def _combine(seg_g, seg_l, seg_n, ys, slot, gate, h, g, b, *, tm=MOE_TM):
    T, D = h.shape
    grid_spec = pltpu.PrefetchScalarGridSpec(
        num_scalar_prefetch=3,
        grid=(T // tm,),
        in_specs=[pl.BlockSpec(memory_space=pl.ANY),
                  pl.BlockSpec((tm, TOP_K), lambda i, *_: (i, 0)),
                  pl.BlockSpec((tm, TOP_K), lambda i, *_: (i, 0)),
                  pl.BlockSpec((tm, D), lambda i, *_: (i, 0)),
                  pl.BlockSpec((1, D), lambda i, *_: (0, 0)),
                  pl.BlockSpec((1, D), lambda i, *_: (0, 0))],
        out_specs=pl.BlockSpec((tm, D), lambda i, *_: (i, 0)),
        scratch_shapes=[pltpu.VMEM((2, MOE_STAGE, MOE_PACK_W), I32), pltpu.VMEM((tm, MOE_STAGE), BF16),
                        pltpu.SemaphoreType.DMA((2,))],
    )
    return pl.pallas_call(
        functools.partial(_combine_body, tm=tm),
        grid_spec=grid_spec,
        out_shape=jax.ShapeDtypeStruct((T, D), F32),
        compiler_params=_cparams("arbitrary"),
        name="moe_combine",
    )(seg_g, seg_l, seg_n, ys, slot, gate, h, g.reshape(1, D), b.reshape(1, D))


def _moe_block(h, layer, w_router, b_router, w_gu, b_gu, w_down, b_down, g, b):
    T = h.shape[0]
    bm, tm, E = MOE_BM, MOE_TM, N_EXPERTS
    nt = T // tm
    slot_t, gate_t, cnt, lbase, gbase = _router(h, w_router, b_router)
    cnt, lbase, gbase = (t.reshape(nt, E, LANES)[:, :, 0] for t in (cnt, lbase, gbase))
    total = gbase[-1] + cnt[-1]
    padded = (total + bm - 1) // bm * bm
    pend = jnp.cumsum(padded)
    pstart = pend - padded
    seg_g = (pstart[None, :] + gbase).reshape(-1).astype(I32)
    seg_l = lbase.reshape(-1)
    seg_n = cnt.reshape(-1)
    n_blocks = -(-(T * TOP_K + nt * E * (MOE_SEG - 1)) // bm) + E
    block_start = jnp.arange(n_blocks, dtype=I32) * bm
    block_e = jnp.minimum(jnp.sum((pend[None, :] <= block_start[:, None]).astype(I32), axis=1), E - 1)
    n_live = (pend[-1:] // bm).astype(I32)
    n_valid = jnp.clip((pstart + total)[block_e] - block_start, 0, bm).astype(I32)
    tail_blk = n_live[0] + jnp.arange(n_blocks - T * TOP_K // bm, dtype=I32)
    tail_blk = jnp.where(tail_blk < n_blocks, tail_blk * bm, -1)
    zero_rows = jnp.concatenate([pstart + total, padded - total, tail_blk]).astype(I32)
    xs = _dispatch(seg_g, seg_l, seg_n, zero_rows, slot_t, h, n_blocks * bm)
    owner = jnp.where(padded > 0, jnp.arange(E, dtype=I32), E)
    later = lax.cummin(owner[::-1])[::-1]
    next_e = jnp.concatenate([later[1:], jnp.full((1,), E, I32)])
    next_e = jnp.where(next_e < E, next_e, -1).astype(I32)
    ys = _experts(block_e, n_live, n_valid, next_e, xs, layer, w_gu, b_gu, w_down, b_down)
    return _combine(seg_g, seg_l, seg_n, ys, slot_t.T, gate_t.T, h, g, b)


def kernel(x, rel_bias, a_w_in, a_w_out, b_w_in, b_ln_g, b_ln_b, b_w_s, b_b_s, b_w_out,
           c_w_in, c_w_gk_up, c_b_gk, c_norm_g, c_w_out, ln_g, ln_b,
           moe_w_router, moe_b_router, moe_w_gate_up, moe_b_gate_up, moe_w_down, moe_b_down):
    B, S, D = x.shape
    T = B * S
    h = x.reshape(T, D)
    tables = _bias_tables(rel_bias)
    for i in range(DEPTH):
        j = i // N_MIXERS
        mixer = i % N_MIXERS
        if mixer == 0:
            q_fold = jnp.where(jnp.arange(3 * D) < D, LOG2E * A_DH ** -0.5, 1.0).astype(F32)
            qkv = _matmul(h, (a_w_in[j] * q_fold).astype(BF16), out_dtype=BF16)
            o = _moba_attention(qkv.reshape(B, S, 3 * D), rel_bias, tables)
            h = _matmul_res_ln(o.reshape(T, D), a_w_out[j], h, ln_g[i, 0], ln_b[i, 0])
        elif mixer == 1:
            z = _matmul(h, b_w_in[j].astype(BF16), act="gelu", out_dtype=BF16)
            h = _gmlp_gate_out(z, h, b_ln_g[j], b_ln_b[j], b_w_s[j], b_b_s[j], b_w_out[j],
                               ln_g[i, 0], ln_b[i, 0])
        else:
            pad = LANES - C_GATE_RANK
            w_in = jnp.pad(c_w_in[j], ((0, 0), (0, pad)))
            w_up = jnp.pad(c_w_gk_up[j], ((0, pad), (0, 0)))
            proj = _matmul(h, w_in.astype(BF16), tn=640)
            o = _gla_core(proj.reshape(B, S, C_IN_WIDTH + pad), w_up, c_b_gk[j], c_norm_g[j])
            h = _matmul_res_ln(o.reshape(T, D), c_w_out[j], h, ln_g[i, 0], ln_b[i, 0])
        h = _moe_block(h, i, moe_w_router[i], moe_b_router[i], moe_w_gate_up, moe_b_gate_up,
                       moe_w_down, moe_b_down, ln_g[i, 1], ln_b[i, 1])
    return h.reshape(B, S, D)
```

```python
import functools
import math

import numpy as np
import jax
import jax.numpy as jnp
from jax import lax
from jax.experimental import pallas as pl
from jax.experimental.pallas import tpu as pltpu

F32 = jnp.float32
BF16 = jnp.bfloat16
I32 = jnp.int32
HI = lax.Precision.HIGHEST

D_MODEL = 1024
DEPTH = 4
N_MIXERS = 3
ALPHA = (2.0 * DEPTH) ** 0.25
LN_EPS = 1e-5
LOG2E = math.log2(math.e)

A_HEADS = 8
A_DH = D_MODEL // A_HEADS
MOBA_BLOCK = 256
MOBA_TOPK = 3
REL_BUCKETS = 32
REL_MAX_DIST = 128

B_GROUPS = 8
B_WIDTH = 2 * D_MODEL
B_CHUNK = 128
B_GW = B_WIDTH // B_GROUPS

C_HEADS = 4
C_KEY_DIM = D_MODEL // 2
C_VAL_DIM = D_MODEL
C_DK = C_KEY_DIM // C_HEADS
C_DV = C_VAL_DIM // C_HEADS
C_GATE_RANK = 16
C_GATE_NORMALIZER = 16.0
C_IN_WIDTH = 2 * C_KEY_DIM + 2 * C_VAL_DIM + C_GATE_RANK

N_EXPERTS = 32
TOP_K = 4
SWIGLU_LIMIT = 7.0
SWIGLU_ALPHA = 1.702

LANES = 128
SUBLANES = 8
VMEM_LIMIT = 56 * 1024 * 1024

GLA_CHUNK = 128
GLA_SUB = 16
GLA_HEADS_PER_STEP = 4
GLA_SEQ_TILE = 1024
MOE_BM = 1024
MOE_SUB = 256
MOE_TM = 512
MOE_SEG = SUBLANES
MOE_SEG_BITS = (MOE_TM // MOE_SEG).bit_length()
MOE_STAGE = MOE_TM * TOP_K + N_EXPERTS * MOE_SEG
MOE_PERM_CHUNK = 256
MOE_PACK_W = D_MODEL // 2


def _cparams(*sem):
    return pltpu.CompilerParams(dimension_semantics=sem, vmem_limit_bytes=VMEM_LIMIT)


def _ln(x, g, b):
    mu = jnp.mean(x, axis=-1, keepdims=True)
    xc = x - mu
    var = jnp.mean(xc * xc, axis=-1, keepdims=True)
    return xc * lax.rsqrt(var + LN_EPS) * g + b


def _mm_body(x_ref, w_ref, o_ref, *, act, tn):
    x = x_ref[...].astype(BF16)
    for n0 in range(0, w_ref.shape[1], tn):
        acc = jnp.dot(x, w_ref[:, n0:n0 + tn], preferred_element_type=F32)
        if act == "gelu":
            acc = 0.5 * acc * (1.0 + lax.erf(acc * (2.0 ** -0.5)))
        o_ref[:, n0:n0 + tn] = acc.astype(o_ref.dtype)


def _matmul(x, w, *, act=None, tm=512, tn=512, out_dtype=F32):
    M, K = x.shape
    N = w.shape[1]
    assert M % tm == 0 and N % tn == 0
    return pl.pallas_call(
        functools.partial(_mm_body, act=act, tn=tn),
        grid=(M // tm,),
        in_specs=[pl.BlockSpec((tm, K), lambda i: (i, 0)),
                  pl.BlockSpec((K, N), lambda i: (0, 0))],
        out_specs=pl.BlockSpec((tm, N), lambda i: (i, 0)),
        out_shape=jax.ShapeDtypeStruct((M, N), out_dtype),
        compiler_params=_cparams("arbitrary"),
        name="proj_matmul",
    )(x, w)


def _mm_res_ln_body(x_ref, w_ref, h_ref, g_ref, b_ref, o_ref, wb_ref):
    @pl.when(pl.program_id(0) == 0)
    def _():
        wb_ref[...] = w_ref[...].astype(BF16)

    t = jnp.dot(x_ref[...].astype(BF16), wb_ref[...], preferred_element_type=F32)
    o_ref[...] = _ln(ALPHA * h_ref[...] + t, g_ref[...], b_ref[...])


def _matmul_res_ln(x, w, h, g, b, *, tm=512):
    M, K = x.shape
    N = w.shape[1]
    return pl.pallas_call(
        _mm_res_ln_body,
        grid=(M // tm,),
        in_specs=[pl.BlockSpec((tm, K), lambda i: (i, 0)),
                  pl.BlockSpec((K, N), lambda i: (0, 0)),
                  pl.BlockSpec((tm, N), lambda i: (i, 0)),
                  pl.BlockSpec((1, N), lambda i: (0, 0)),
                  pl.BlockSpec((1, N), lambda i: (0, 0))],
        out_specs=pl.BlockSpec((tm, N), lambda i: (i, 0)),
        out_shape=jax.ShapeDtypeStruct((M, N), F32),
        scratch_shapes=[pltpu.VMEM((K, N), BF16)],
        compiler_params=_cparams("arbitrary"),
        name="outproj_res_ln",
    )(x, w, h, g.reshape(1, N), b.reshape(1, N))


def _t5_bucket_lower_bounds():
    n = np.arange(0, 4 * REL_MAX_DIST, dtype=np.int64)
    max_exact = REL_BUCKETS // 2
    nf = np.maximum(n, 1).astype(np.float32)
    large = max_exact + (np.log(nf / np.float32(max_exact)) / np.float32(math.log(REL_MAX_DIST / max_exact))
                         * np.float32(REL_BUCKETS - max_exact)).astype(np.int32)
    large = np.minimum(large, REL_BUCKETS - 1)
    bucket = np.where(n < max_exact, n, large)
    assert np.all(np.diff(bucket) >= 0) and bucket[-1] == REL_BUCKETS - 1
    return [int(np.argmax(bucket >= b)) for b in range(REL_BUCKETS)]


_BUCKET_LO = _t5_bucket_lower_bounds()


def _bias_table_body(rel_ref, o_ref):
    blk = MOBA_BLOCK
    row = lax.broadcasted_iota(I32, (blk, blk), 0)
    col = lax.broadcasted_iota(I32, (blk, blk), 1)
    for t in range(2):
        d = row - col + t * blk
        for h in range(A_HEADS):
            val = jnp.full((blk, blk), rel_ref[REL_BUCKETS - 1, h], F32)
            for b in range(REL_BUCKETS - 2, -1, -1):
                val = jnp.where(d < _BUCKET_LO[b + 1], rel_ref[b, h], val)
            val = val * LOG2E
            if t == 0:
                val = jnp.where(d >= 0, val, -jnp.inf)
            o_ref[h, t] = val


def _bias_tables(rel_bias):
    return pl.pallas_call(
        _bias_table_body,
        in_specs=[pl.BlockSpec(memory_space=pltpu.SMEM)],
        out_shape=jax.ShapeDtypeStruct((A_HEADS, 2, MOBA_BLOCK, MOBA_BLOCK), F32),
        compiler_params=pltpu.CompilerParams(vmem_limit_bytes=VMEM_LIMIT),
        name="rel_bias_tables",
    )(rel_bias)


def _moba_body(rel_ref, q_ref, k_ref, v_ref, tb_ref, o_ref, s_s, *, seq):
    blk, dh = MOBA_BLOCK, A_DH
    nb = seq // blk
    h = pl.program_id(1)
    kmean = jnp.mean(k_ref[0].astype(F32).reshape(nb, blk, dh), axis=1)
    kmean_pad = jnp.concatenate([kmean, jnp.zeros((LANES - nb, dh), F32)], axis=0)
    b_far = rel_ref[REL_BUCKETS - 1, h] * LOG2E
    lane = lax.broadcasted_iota(I32, (blk, LANES), 1)
    ones_bf = jnp.ones((LANES, LANES), BF16)

    for n in range(nb):
        r0 = n * blk
        s_s[r0:seq, r0:r0 + blk] = lax.dot_general(q_ref[0, r0:seq, :], k_ref[0, r0:r0 + blk, :],
                                                   (((1,), (1,)), ((), ())), preferred_element_type=F32)

    for c in range(nb):
        q = q_ref[0, c * blk:(c + 1) * blk, :]
        n_keys = (c + 1) * blk
        s = s_s[c * blk:(c + 1) * blk, 0:n_keys]
        if c > MOBA_TOPK:
            gate = lax.dot_general(q.astype(F32), kmean_pad, (((1,), (1,)), ((), ())),
                                   precision=HI, preferred_element_type=F32)
            gate = jnp.where(lane < c, gate, -jnp.inf)
        pieces = []
        for n in range(c):
            shift = b_far
            if c > MOBA_TOPK:
                gn = gate[:, n:n + 1]
                beats = (gate > gn) | ((gate == gn) & (lane < n))
                rank = jnp.dot(jnp.where(beats, 1.0, 0.0).astype(BF16), ones_bf, preferred_element_type=F32)
                shift = jnp.where(rank < MOBA_TOPK, b_far, -jnp.inf)
                shift = jnp.concatenate([shift] * (blk // LANES), axis=1)
            if n == c - 1:
                shift = tb_ref[0, 1] + (shift - b_far)
            pieces.append(s[:, n * blk:(n + 1) * blk] + shift)
        pieces.append(s[:, c * blk:(c + 1) * blk] + tb_ref[0, 0])
        logits = jnp.concatenate(pieces, axis=1) if c > 0 else pieces[0]
        m = jnp.max(logits, axis=1, keepdims=True)
        p = jnp.exp2(logits - m)
        denom = jnp.sum(p, axis=1, keepdims=True)
        o = jnp.dot(p.astype(BF16), v_ref[0, 0:n_keys, :], preferred_element_type=F32)
        o_ref[0, c * blk:(c + 1) * blk, :] = o / denom


def _moba_attention(qkv, rel_bias, tables):
    B, S, _ = qkv.shape
    H, dh = A_HEADS, A_DH
    assert S % MOBA_BLOCK == 0 and S // MOBA_BLOCK <= SUBLANES
    return pl.pallas_call(
        functools.partial(_moba_body, seq=S),
        grid=(B, H),
        in_specs=[pl.BlockSpec(memory_space=pltpu.SMEM),
                  pl.BlockSpec((1, S, dh), lambda b, h: (b, 0, h)),
                  pl.BlockSpec((1, S, dh), lambda b, h: (b, 0, H + h)),
                  pl.BlockSpec((1, S, dh), lambda b, h: (b, 0, 2 * H + h)),
                  pl.BlockSpec((1, 2, MOBA_BLOCK, MOBA_BLOCK), lambda b, h: (h, 0, 0, 0))],
        out_specs=pl.BlockSpec((1, S, dh), lambda b, h: (b, 0, h)),
        out_shape=jax.ShapeDtypeStruct((B, S, H * dh), F32),
        scratch_shapes=[pltpu.VMEM((S, S), F32)],
        compiler_params=_cparams("arbitrary", "arbitrary"),
        name="moba_attention",
    )(rel_bias, qkv, qkv, qkv, tables)


def _gmlp_body(u_ref, v_ref, h_ref, vg_ref, vb_ref, ws_ref, bs_ref, wo_ref, g_ref, b_ref, o_ref,
               wst_s, wob_s, *, tm):
    @pl.when(pl.program_id(0) == 0)
    def _():
        row = lax.broadcasted_iota(I32, (B_CHUNK, B_CHUNK), 0)
        col = lax.broadcasted_iota(I32, (B_CHUNK, B_CHUNK), 1)
        for g in range(B_GROUPS):
            wst_s[g] = jnp.where(row >= col, ws_ref[g], 0.0).astype(BF16)
        wob_s[...] = wo_ref[...].astype(BF16)

    vn = _ln(v_ref[...].astype(F32), vg_ref[...], vb_ref[...]).astype(BF16)
    u = u_ref[...].astype(F32)
    rows = []
    for c in range(tm // B_CHUNK):
        r0 = c * B_CHUNK
        cols = []
        for g in range(B_GROUPS):
            c0 = g * B_GW
            mixed = jnp.dot(wst_s[g], vn[r0:r0 + B_CHUNK, c0:c0 + B_GW],
                            preferred_element_type=F32) + bs_ref[g]
            cols.append(u[r0:r0 + B_CHUNK, c0:c0 + B_GW] * mixed)
        rows.append(jnp.concatenate(cols, axis=1))
    y = jnp.concatenate(rows, axis=0).astype(BF16)
    t = jnp.dot(y, wob_s[...], preferred_element_type=F32)
    o_ref[...] = _ln(ALPHA * h_ref[...] + t, g_ref[...], b_ref[...])


def _gmlp_gate_out(z, h, v_g, v_b, w_s, b_s, w_out, g, b, *, tm=256):
    T = z.shape[0]
    D = D_MODEL
    return pl.pallas_call(
        functools.partial(_gmlp_body, tm=tm),
        grid=(T // tm,),
        in_specs=[pl.BlockSpec((tm, B_WIDTH), lambda i: (i, 0)),
                  pl.BlockSpec((tm, B_WIDTH), lambda i: (i, 1)),
                  pl.BlockSpec((tm, D), lambda i: (i, 0)),
                  pl.BlockSpec((1, B_WIDTH), lambda i: (0, 0)),
                  pl.BlockSpec((1, B_WIDTH), lambda i: (0, 0)),
                  pl.BlockSpec((B_GROUPS, B_CHUNK, B_CHUNK), lambda i: (0, 0, 0)),
                  pl.BlockSpec((B_GROUPS, B_CHUNK, 1), lambda i: (0, 0, 0)),
                  pl.BlockSpec((B_WIDTH, D), lambda i: (0, 0)),
                  pl.BlockSpec((1, D), lambda i: (0, 0)),
                  pl.BlockSpec((1, D), lambda i: (0, 0))],
        out_specs=pl.BlockSpec((tm, D), lambda i: (i, 0)),
        out_shape=jax.ShapeDtypeStruct((T, D), F32),
        scratch_shapes=[pltpu.VMEM((B_GROUPS, B_CHUNK, B_CHUNK), BF16), pltpu.VMEM((B_WIDTH, D), BF16)],
        compiler_params=_cparams("arbitrary"),
        name="gmlp_gate_out",
    )(z, z, h, v_g.reshape(1, B_WIDTH), v_b.reshape(1, B_WIDTH), w_s, b_s.reshape(B_GROUPS, B_CHUNK, 1),
      w_out, g.reshape(1, D), b.reshape(1, D))


def _gla_body(q_ref, k_ref, v_ref, g_ref, gl_ref, wup_ref, bgk_ref, ng_ref, o_ref, la_s, st_s, raw_s, *, seq):
    ch, sub, dk, dv = GLA_CHUNK, GLA_SUB, C_DK, C_DV
    x = jnp.dot(gl_ref[0].astype(F32), wup_ref[...], precision=HI, preferred_element_type=F32) + bgk_ref[...]
    la_s[...] = (jnp.minimum(x, 0.0) - jnp.log1p(jnp.exp(-jnp.abs(x)))) * (1.0 / C_GATE_NORMALIZER)

    @pl.when(pl.program_id(2) == 0)
    def _():
        st_s[...] = jnp.zeros(st_s.shape, F32)

    row_c = lax.broadcasted_iota(I32, (ch, ch), 0)
    col_c = lax.broadcasted_iota(I32, (ch, ch), 1)
    tri = (row_c >= col_c).astype(F32)
    eye = (row_c == col_c).astype(F32)
    sub_i = lax.broadcasted_iota(I32, (sub, ch), 0)
    lane_j = lax.broadcasted_iota(I32, (sub, ch), 1)

    def chunk(c, carry):
        for hh in range(GLA_HEADS_PER_STEP):
            chunk_head(c, hh)
        return carry

    def chunk_head(c, hh):
        s0 = pl.multiple_of(c * ch, ch)
        a = la_s[pl.ds(s0, ch), hh * dk:(hh + 1) * dk]
        b = jnp.dot(tri, a, precision=HI, preferred_element_type=F32)
        q = q_ref[0, pl.ds(s0, ch), hh * dk:(hh + 1) * dk].astype(F32) * (dk ** -0.5)
        k = k_ref[0, pl.ds(s0, ch), hh * dk:(hh + 1) * dk].astype(F32)
        v = v_ref[0, pl.ds(s0, ch), hh * dv:(hh + 1) * dv].astype(BF16)
        st = st_s[hh]
        o = jnp.dot((q * jnp.exp(b)).astype(BF16), st.astype(BF16), preferred_element_type=F32)

        blocks = []
        for sb in range(ch // sub):
            r0 = sb * sub
            q_i = q[r0:r0 + sub]
            b_i = b[r0:r0 + sub]
            k_i = k[r0:r0 + sub]
            att = jnp.zeros((sub, ch), F32)
            for j in range(sub):
                e = jnp.exp(jnp.minimum(b_i - b_i[j:j + 1, :], 0.0))
                colv = jnp.sum(q_i * k_i[j:j + 1, :] * e, axis=1, keepdims=True)
                att = jnp.where((lane_j == r0 + j) & (sub_i >= j), colv, att)
            if sb > 0:
                ref_b = b[r0 - 1:r0, :]
                q_t = q_i * jnp.exp(b_i - ref_b)
                k_t = k * jnp.exp(jnp.minimum(ref_b - b, 0.0))
                off = lax.dot_general(q_t.astype(BF16), k_t.astype(BF16), (((1,), (1,)), ((), ())),
                                      preferred_element_type=F32)
                att = jnp.where(lane_j < r0, off, att)
            blocks.append(att)
        att_full = jnp.concatenate(blocks, axis=0)
        o = o + jnp.dot(att_full.astype(BF16), v, preferred_element_type=F32)
        raw_s[pl.ds(s0, ch), hh * dv:(hh + 1) * dv] = o

        b_last = b[ch - 1:ch, :]
        k_d = k * jnp.exp(b_last - b)
        upd = jnp.dot(k_d.T.astype(BF16), v, preferred_element_type=F32)
        decay_col = lax.dot_general(eye, jnp.broadcast_to(jnp.exp(b_last), (SUBLANES, dk)),
                                    (((1,), (1,)), ((), ())), precision=HI,
                                    preferred_element_type=F32)[:, 0:1]
        st_s[hh] = decay_col * st + upd

    lax.fori_loop(0, seq // ch, chunk, 0)

    for hh in range(GLA_HEADS_PER_STEP):
        o = raw_s[:, hh * dv:(hh + 1) * dv]
        rms = o * lax.rsqrt(jnp.mean(o * o, axis=-1, keepdims=True) + LN_EPS) * ng_ref[...]
        gg = g_ref[0, :, hh * dv:(hh + 1) * dv].astype(F32)
        o_ref[0, :, hh * dv:(hh + 1) * dv] = rms * (gg * jax.nn.sigmoid(gg))


def _gla_core(proj, w_up_pad, b_gk, norm_g):
    B, S, _ = proj.shape
    H, hps = C_HEADS, GLA_HEADS_PER_STEP
    dk, dv = C_DK * hps, C_DV * hps
    st = GLA_SEQ_TILE
    assert C_DK == LANES and S % st == 0 and st % GLA_CHUNK == 0 and H % hps == 0
    k_off = C_KEY_DIM // dk
    v_off = 2 * C_KEY_DIM // dv
    g_off = (2 * C_KEY_DIM + C_VAL_DIM) // dv
    gl_off = (2 * C_KEY_DIM + 2 * C_VAL_DIM) // LANES
    return pl.pallas_call(
        functools.partial(_gla_body, seq=st),
        grid=(B, H // hps, S // st),
        in_specs=[pl.BlockSpec((1, st, dk), lambda b, h, s: (b, s, h)),
                  pl.BlockSpec((1, st, dk), lambda b, h, s: (b, s, k_off + h)),
                  pl.BlockSpec((1, st, dv), lambda b, h, s: (b, s, v_off + h)),
                  pl.BlockSpec((1, st, dv), lambda b, h, s: (b, s, g_off + h)),
                  pl.BlockSpec((1, st, LANES), lambda b, h, s: (b, s, gl_off)),
                  pl.BlockSpec((LANES, dk), lambda b, h, s: (0, h)),
                  pl.BlockSpec((1, dk), lambda b, h, s: (0, h)),
                  pl.BlockSpec((1, C_DV), lambda b, h, s: (0, 0))],
        out_specs=pl.BlockSpec((1, st, dv), lambda b, h, s: (b, s, h)),
        out_shape=jax.ShapeDtypeStruct((B, S, C_VAL_DIM), F32),
        scratch_shapes=[pltpu.VMEM((st, dk), F32), pltpu.VMEM((hps, C_DK, C_DV), F32), pltpu.VMEM((st, dv), F32)],
        compiler_params=_cparams("arbitrary", "arbitrary", "arbitrary"),
        name="gla_core",
    )(proj, proj, proj, proj, proj, w_up_pad, b_gk.reshape(1, C_KEY_DIM), norm_g.reshape(1, C_DV))


def _router_body(h_ref, wt_ref, b_ref, slot_ref, gate_ref, cnt_ref, lbase_ref, gbase_ref, run_s, ws_s, *, tm):
    E = N_EXPERTS

    @pl.when(pl.program_id(0) == 0)
    def _():
        run_s[...] = jnp.zeros(run_s.shape, F32)
        w = wt_ref[...]
        w1 = w.astype(BF16)
        ws_s[0:E, :] = w1
        ws_s[E:2 * E, :] = (w - w1.astype(F32)).astype(BF16)

    h = h_ref[...]
    h1 = h.astype(BF16)
    h2 = (h - h1.astype(F32)).astype(BF16)
    nt_dims = (((1,), (1,)), ((), ()))
    both = lax.dot_general(ws_s[...], h1, nt_dims, preferred_element_type=F32)
    logits = (both[0:E] + (both[E:2 * E] + lax.dot_general(ws_s[0:E, :], h2, nt_dims, preferred_element_type=F32))
              + b_ref[...])
    row = lax.broadcasted_iota(I32, (E, tm), 0)
    rest = logits
    top_val, onehot = [], []
    for _ in range(TOP_K):
        m = jnp.max(rest, axis=0, keepdims=True)
        idx = jnp.min(jnp.where(rest == m, row, E), axis=0, keepdims=True)
        oh = row == idx
        rest = jnp.where(oh, -jnp.inf, rest)
        top_val.append(m)
        onehot.append(oh)
    ex = [jnp.exp(v - top_val[0]) for v in top_val]
    den = ex[0] + ex[1] + ex[2] + ex[3]
    chosen = sum(oh.astype(F32) for oh in onehot)

    tr = lax.broadcasted_iota(I32, (tm, tm), 0)
    tc = lax.broadcasted_iota(I32, (tm, tm), 1)
    earlier = (tr < tc).astype(BF16)
    seen = jnp.dot(chosen.astype(BF16), earlier, preferred_element_type=F32)
    cnt = jnp.sum(chosen, axis=1, keepdims=True)
    cnt_pad = jnp.floor((cnt + (MOE_SEG - 1)) * (1.0 / MOE_SEG)) * MOE_SEG
    cnt_cols = jnp.broadcast_to(cnt_pad, (E, LANES))
    er = lax.broadcasted_iota(I32, (E, E), 0)
    ec = lax.broadcasted_iota(I32, (E, E), 1)
    lbase_cols = jnp.dot((ec < er).astype(F32), cnt_cols, precision=HI, preferred_element_type=F32)
    pos = seen + lbase_cols[:, 0:1]

    row_k = lax.broadcasted_iota(I32, (TOP_K, tm), 0)
    s_out = jnp.zeros((TOP_K, tm), I32)
    g_out = jnp.zeros((TOP_K, tm), F32)
    for kk in range(TOP_K):
        sk = jnp.sum(jnp.where(onehot[kk], pos, 0.0), axis=0, keepdims=True)
        s_out = jnp.where(row_k == kk, sk.astype(I32), s_out)
        g_out = jnp.where(row_k == kk, ex[kk] / den, g_out)
    slot_ref[...] = s_out
    gate_ref[...] = g_out
    cnt_ref[...] = cnt_cols.astype(I32)
    lbase_ref[...] = lbase_cols.astype(I32)
    gbase_ref[...] = run_s[...].astype(I32)
    run_s[...] = run_s[...] + cnt_cols


def _router(h, w, b, *, tm=MOE_TM):
    T, D = h.shape
    E = N_EXPERTS
    nt = T // tm
    tab = jax.ShapeDtypeStruct((nt * E, LANES), I32)
    tab_spec = pl.BlockSpec((E, LANES), lambda i: (i, 0))
    return pl.pallas_call(
        functools.partial(_router_body, tm=tm),
        grid=(nt,),
        in_specs=[pl.BlockSpec((tm, D), lambda i: (i, 0)),
                  pl.BlockSpec((E, D), lambda i: (0, 0)),
                  pl.BlockSpec((E, 1), lambda i: (0, 0))],
        out_specs=[pl.BlockSpec((TOP_K, tm), lambda i: (0, i)),
                   pl.BlockSpec((TOP_K, tm), lambda i: (0, i)),
                   tab_spec, tab_spec, tab_spec],
        out_shape=[jax.ShapeDtypeStruct((TOP_K, T), I32),
                   jax.ShapeDtypeStruct((TOP_K, T), F32),
                   tab, tab, tab],
        scratch_shapes=[pltpu.VMEM((E, LANES), F32), pltpu.VMEM((2 * E, D), BF16)],
        compiler_params=_cparams("arbitrary"),
        name="moe_router",
    )(h, w.T, b.reshape(E, 1))


def _pack_pairs(left, right):
    hi = lax.bitcast_convert_type(left, I32) & jnp.int32(-65536)
    lo = lax.shift_right_logical(lax.bitcast_convert_type(right, I32), 16)
    return hi | lo


def _unpack_pairs(word):
    left = lax.bitcast_convert_type(word & jnp.int32(-65536), F32)
    right = lax.bitcast_convert_type(lax.shift_left(word, 16), F32)
    return left.astype(BF16), right.astype(BF16)


def _segment_copies(seg_g, seg_l, seg_n, tile, make_copy, start):
    def per_expert(e, carry):
        idx = tile * N_EXPERTS + e
        n = seg_n[idx]
        g = seg_g[idx]
        l = seg_l[idx]
        for bit in range(MOE_SEG_BITS - 1, -1, -1):
            size = MOE_SEG << bit

            @pl.when((n & size) != 0)
            def _():
                off = n & ~(2 * size - 1)
                cp = make_copy(pl.multiple_of(g + off, MOE_SEG), pl.multiple_of(l + off, MOE_SEG), size)
                if start:
                    cp.start()
                else:
                    cp.wait()
        return carry

    lax.fori_loop(0, N_EXPERTS, per_expert, 0)


def _segment_waits(seg_l, seg_n, tile, make_copy):
    last = tile * N_EXPERTS + N_EXPERTS - 1
    total = seg_l[last] + seg_n[last]
    for bit in range((MOE_STAGE // MOE_SEG).bit_length() - 1, -1, -1):
        size = MOE_SEG << bit

        @pl.when((total & size) != 0)
        def _():
            make_copy(0, 0, size).wait()


def _dispatch_body(segg_ref, segl_ref, segn_ref, zf_ref, slot_t_ref, h_ref, xs_ref, stage, zbuf, sem, *, tm, bm):
    i = pl.program_id(0)

    def zero_fill(start):
        def act(cp):
            if start:
                cp.start()
            else:
                cp.wait()

        def per_expert(e, carry):
            g = zf_ref[e]
            n = zf_ref[N_EXPERTS + e]
            for bit in range((bm // MOE_SEG).bit_length() - 2, -1, -1):
                size = MOE_SEG << bit

                @pl.when((n & size) != 0)
                def _():
                    off = n & ~(2 * size - 1)
                    act(pltpu.make_async_copy(
                        zbuf.at[pl.ds(0, size), :],
                        xs_ref.at[pl.ds(pl.multiple_of(g + off, MOE_SEG), size), :], sem.at[2]))
            return carry

        def per_tail(j, carry):
            row = zf_ref[2 * N_EXPERTS + j]

            @pl.when(row >= 0)
            def _():
                act(pltpu.make_async_copy(zbuf, xs_ref.at[pl.ds(pl.multiple_of(row, MOE_SEG), bm), :], sem.at[2]))
            return carry

        lax.fori_loop(0, N_EXPERTS, per_expert, 0)
        lax.fori_loop(0, zf_ref.shape[0] - 2 * N_EXPERTS, per_tail, 0)

    @pl.when(i == 0)
    def _():
        zbuf[...] = jnp.zeros(zbuf.shape, I32)
        zero_fill(True)

    @pl.when(i == pl.num_programs(0) - 1)
    def _():
        zero_fill(False)

    slot_i = i % 2
    hb = h_ref[...].astype(BF16)
    for c in range(MOE_STAGE // MOE_PERM_CHUNK):
        r0 = c * MOE_PERM_CHUNK
        rows = lax.broadcasted_iota(I32, (MOE_PERM_CHUNK, tm), 0) + r0
        hit = rows == slot_t_ref[0:1, :]
        for kk in range(1, TOP_K):
            hit = hit | (rows == slot_t_ref[kk:kk + 1, :])
        perm = jnp.where(hit, 1.0, 0.0).astype(BF16)
        rows_f = jnp.dot(perm, hb, preferred_element_type=F32)
        stage[slot_i, r0:r0 + MOE_PERM_CHUNK, :] = _pack_pairs(rows_f[:, :MOE_PACK_W], rows_f[:, MOE_PACK_W:])

    def copier(buf):
        def make_copy(g, l, size):
            return pltpu.make_async_copy(stage.at[buf, pl.ds(l, size), :], xs_ref.at[pl.ds(g, size), :],
                                         sem.at[buf])
        return make_copy

    _segment_copies(segg_ref, segl_ref, segn_ref, i, copier(slot_i), True)

    @pl.when(i > 0)
    def _():
        _segment_waits(segl_ref, segn_ref, i - 1, copier(1 - slot_i))

    @pl.when(i == pl.num_programs(0) - 1)
    def _():
        _segment_waits(segl_ref, segn_ref, i, copier(slot_i))


def _dispatch(seg_g, seg_l, seg_n, zero_rows, slot_t, h, n_rows, *, tm=MOE_TM, bm=MOE_BM):
    T, D = h.shape
    grid_spec = pltpu.PrefetchScalarGridSpec(
        num_scalar_prefetch=4,
        grid=(T // tm,),
        in_specs=[pl.BlockSpec((TOP_K, tm), lambda i, *_: (0, i)),
                  pl.BlockSpec((tm, D), lambda i, *_: (i, 0))],
        out_specs=pl.BlockSpec(memory_space=pl.ANY),
        scratch_shapes=[pltpu.VMEM((2, MOE_STAGE, MOE_PACK_W), I32), pltpu.VMEM((bm, MOE_PACK_W), I32),
                        pltpu.SemaphoreType.DMA((3,))],
    )
    return pl.pallas_call(
        functools.partial(_dispatch_body, tm=tm, bm=bm),
        grid_spec=grid_spec,
        out_shape=jax.ShapeDtypeStruct((n_rows, MOE_PACK_W), I32),
        compiler_params=pltpu.CompilerParams(dimension_semantics=("arbitrary",), vmem_limit_bytes=VMEM_LIMIT,
                                             has_side_effects=True),
        name="moe_dispatch",
    )(seg_g, seg_l, seg_n, zero_rows, slot_t, h)


def _expert_body(be_ref, nb_ref, nv_ref, nx_ref, xs_ref, wgu_hbm, bgu_ref, wd_hbm, bd_ref, ys_ref,
                 wgu_f, wd_f, wgu_s, wd_s, sem, *, bm, layer):
    i = pl.program_id(0)
    e = be_ref[i]
    prev = be_ref[jnp.maximum(i - 1, 0)]
    live = i < nb_ref[0]
    n_valid = jnp.where(live, nv_ref[i], 0)

    def weight_copies(expert):
        return (pltpu.make_async_copy(wgu_hbm.at[layer, expert], wgu_f, sem.at[0]),
                pltpu.make_async_copy(wd_hbm.at[layer, expert], wd_f, sem.at[1]))

    @pl.when(i == 0)
    def _():
        for cp in weight_copies(e):
            cp.start()

    @pl.when(live & ((i == 0) | (e != prev)))
    def _():
        for cp in weight_copies(e):
            cp.wait()
        wgu_s[...] = wgu_f[...].astype(BF16)
        wd_s[...] = wd_f[...].astype(BF16)
        nxt = nx_ref[e]

        @pl.when(nxt >= 0)
        def _():
            for cp in weight_copies(nxt):
                cp.start()

    def mlp(r0, rows):
        x = jnp.concatenate(_unpack_pairs(xs_ref[r0:r0 + rows, :]), axis=1)
        hh = jnp.dot(x, wgu_s[...], preferred_element_type=F32) + bgu_ref[...]
        d_e = hh.shape[1] // 2
        h_gate = jnp.minimum(hh[:, :d_e], SWIGLU_LIMIT)
        h_up = jnp.clip(hh[:, d_e:], -SWIGLU_LIMIT, SWIGLU_LIMIT)
        act = h_gate * jax.nn.sigmoid(SWIGLU_ALPHA * h_gate) * (h_up + 1.0)
        y = jnp.dot(act.astype(BF16), wd_s[...], preferred_element_type=F32) + bd_ref[...]
        y = y.astype(BF16).astype(F32)
        ys_ref[r0:r0 + rows, :] = _pack_pairs(y[:, :MOE_PACK_W], y[:, MOE_PACK_W:])

    full = n_valid > bm - MOE_SUB

    @pl.when(full)
    def _():
        mlp(0, bm)

    for r0 in range(0, bm, MOE_SUB):
        @pl.when(jnp.logical_not(full) & (n_valid > r0))
        def _(r0=r0):
            mlp(r0, MOE_SUB)

        @pl.when(n_valid <= r0)
        def _(r0=r0):
            ys_ref[r0:r0 + MOE_SUB, :] = jnp.zeros((MOE_SUB, ys_ref.shape[1]), I32)


def _experts(block_e, n_live, n_valid, next_e, xs, layer, w_gu, b_gu, w_down, b_down, *, bm=MOE_BM):
    L, E, D, D2 = w_gu.shape
    n_blocks = xs.shape[0] // bm

    def blk(i, nb):
        return jnp.maximum(jnp.minimum(i, nb[0] - 1), 0)

    grid_spec = pltpu.PrefetchScalarGridSpec(
        num_scalar_prefetch=4,
        grid=(n_blocks,),
        in_specs=[pl.BlockSpec((bm, MOE_PACK_W), lambda i, be, nb, nv, nx: (blk(i, nb), 0)),
                  pl.BlockSpec(memory_space=pl.ANY),
                  pl.BlockSpec((None, None, 1, D2), lambda i, be, nb, nv, nx: (layer, be[blk(i, nb)], 0, 0)),
                  pl.BlockSpec(memory_space=pl.ANY),
                  pl.BlockSpec((None, None, 1, D), lambda i, be, nb, nv, nx: (layer, be[blk(i, nb)], 0, 0))],
        out_specs=pl.BlockSpec((bm, MOE_PACK_W), lambda i, be, nb, nv, nx: (i, 0)),
        scratch_shapes=[pltpu.VMEM((D, D2), F32), pltpu.VMEM((D2 // 2, D), F32),
                        pltpu.VMEM((D, D2), BF16), pltpu.VMEM((D2 // 2, D), BF16),
                        pltpu.SemaphoreType.DMA((2,))],
    )
    return pl.pallas_call(
        functools.partial(_expert_body, bm=bm, layer=layer),
        grid_spec=grid_spec,
        out_shape=jax.ShapeDtypeStruct(xs.shape, I32),
        compiler_params=_cparams("arbitrary"),
        name="moe_experts",
    )(block_e, n_live, n_valid, next_e, xs, w_gu, b_gu.reshape(L, E, 1, D2), w_down, b_down.reshape(L, E, 1, D))


def _combine_body(segg_ref, segl_ref, segn_ref, ys_ref, slot_ref, gate_ref, h_ref, g_ref, b_ref, o_ref,
                  stage, mix_s, sem, *, tm):
    i = pl.program_id(0)

    slot_i = i % 2

    def copier(buf):
        def make_copy(g, l, size):
            return pltpu.make_async_copy(ys_ref.at[pl.ds(g, size), :], stage.at[buf, pl.ds(l, size), :],
                                         sem.at[buf])
        return make_copy

    @pl.when(i == 0)
    def _():
        stage[...] = jnp.zeros(stage.shape, I32)
        _segment_copies(segg_ref, segl_ref, segn_ref, i, copier(slot_i), True)

    @pl.when(i + 1 < pl.num_programs(0))
    def _():
        _segment_copies(segg_ref, segl_ref, segn_ref, i + 1, copier(1 - slot_i), True)

    slot = slot_ref[...]
    gate = gate_ref[...]
    for c in range(MOE_STAGE // MOE_PERM_CHUNK):
        c0 = c * MOE_PERM_CHUNK
        cols = lax.broadcasted_iota(I32, (tm, MOE_PERM_CHUNK), 1) + c0
        w = jnp.zeros((tm, MOE_PERM_CHUNK), F32)
        for kk in range(TOP_K):
            w = jnp.where(cols == slot[:, kk:kk + 1], gate[:, kk:kk + 1], w)
        mix_s[:, c0:c0 + MOE_PERM_CHUNK] = w.astype(BF16)

    _segment_waits(segl_ref, segn_ref, i, copier(slot_i))
    left, right = _unpack_pairs(stage[slot_i])
    y = jnp.concatenate([jnp.dot(mix_s[...], left, preferred_element_type=F32),
                         jnp.dot(mix_s[...], right, preferred_element_type=F32)], axis=1)
    o_ref[...] = _ln(ALPHA * h_ref[...] + y, g_ref[...], b_ref[...])


def _combine(seg_g, seg_l, seg_n, ys, slot, gate, h, g, b, *, tm=MOE_TM):
    T, D = h.shape
    grid_spec = pltpu.PrefetchScalarGridSpec(
        num_scalar_prefetch=3,
        grid=(T // tm,),
        in_specs=[pl.BlockSpec(memory_space=pl.ANY),
                  pl.BlockSpec((tm, TOP_K), lambda i, *_: (i, 0)),
                  pl.BlockSpec((tm, TOP_K), lambda i, *_: (i, 0)),
                  pl.BlockSpec((tm, D), lambda i, *_: (i, 0)),
                  pl.BlockSpec((1, D), lambda i, *_: (0, 0)),
                  pl.BlockSpec((1, D), lambda i, *_: (0, 0))],
        out_specs=pl.BlockSpec((tm, D), lambda i, *_: (i, 0)),
        scratch_shapes=[pltpu.VMEM((2, MOE_STAGE, MOE_PACK_W), I32), pltpu.VMEM((tm, MOE_STAGE), BF16),
                        pltpu.SemaphoreType.DMA((2,))],
    )
    return pl.pallas_call(
        functools.partial(_combine_body, tm=tm),
        grid_spec=grid_spec,
        out_shape=jax.ShapeDtypeStruct((T, D), F32),
        compiler_params=_cparams("arbitrary"),
        name="moe_combine",
    )(seg_g, seg_l, seg_n, ys, slot, gate, h, g.reshape(1, D), b.reshape(1, D))


def _moe_block(h, layer, w_router, b_router, w_gu, b_gu, w_down, b_down, g, b):
    T = h.shape[0]
    bm, tm, E = MOE_BM, MOE_TM, N_EXPERTS
    nt = T // tm
    slot_t, gate_t, cnt, lbase, gbase = _router(h, w_router, b_router)
    cnt, lbase, gbase = (t.reshape(nt, E, LANES)[:, :, 0] for t in (cnt, lbase, gbase))
    total = gbase[-1] + cnt[-1]
    padded = (total + bm - 1) // bm * bm
    pend = jnp.cumsum(padded)
    pstart = pend - padded
    seg_g = (pstart[None, :] + gbase).reshape(-1).astype(I32)
    seg_l = lbase.reshape(-1)
    seg_n = cnt.reshape(-1)
    n_blocks = -(-(T * TOP_K + nt * E * (MOE_SEG - 1)) // bm) + E
    block_start = jnp.arange(n_blocks, dtype=I32) * bm
    block_e = jnp.minimum(jnp.sum((pend[None, :] <= block_start[:, None]).astype(I32), axis=1), E - 1)
    n_live = (pend[-1:] // bm).astype(I32)
    n_valid = jnp.clip((pstart + total)[block_e] - block_start, 0, bm).astype(I32)
    tail_blk = n_live[0] + jnp.arange(n_blocks - T * TOP_K // bm, dtype=I32)
    tail_blk = jnp.where(tail_blk < n_blocks, tail_blk * bm, -1)
    zero_rows = jnp.concatenate([pstart + total, padded - total, tail_blk]).astype(I32)
    xs = _dispatch(seg_g, seg_l, seg_n, zero_rows, slot_t, h, n_blocks * bm)
    owner = jnp.where(padded > 0, jnp.arange(E, dtype=I32), E)
    later = lax.cummin(owner[::-1])[::-1]
    next_e = jnp.concatenate([later[1:], jnp.full((1,), E, I32)])
    next_e = jnp.where(next_e < E, next_e, -1).astype(I32)
    ys = _experts(block_e, n_live, n_valid, next_e, xs, layer, w_gu, b_gu, w_down, b_down)
    return _combine(seg_g, seg_l, seg_n, ys, slot_t.T, gate_t.T, h, g, b)


def kernel(x, rel_bias, a_w_in, a_w_out, b_w_in, b_ln_g, b_ln_b, b_w_s, b_b_s, b_w_out,
           c_w_in, c_w_gk_up, c_b_gk, c_norm_g, c_w_out, ln_g, ln_b,
           moe_w_router, moe_b_router, moe_w_gate_up, moe_b_gate_up, moe_w_down, moe_b_down):
    B, S, D = x.shape
    T = B * S
    h = x.reshape(T, D)
    tables = _bias_tables(rel_bias)
    for i in range(DEPTH):
        j = i // N_MIXERS
        mixer = i % N_MIXERS
        if mixer == 0:
            q_fold = jnp.where(jnp.arange(3 * D) < D, LOG2E * A_DH ** -0.5, 1.0).astype(F32)
            qkv = _matmul(h, (a_w_in[j] * q_fold).astype(BF16), out_dtype=BF16)
            o = _moba_attention(qkv.reshape(B, S, 3 * D), rel_bias, tables)
            h = _matmul_res_ln(o.reshape(T, D), a_w_out[j], h, ln_g[i, 0], ln_b[i, 0])
        elif mixer == 1:
            z = _matmul(h, b_w_in[j].astype(BF16), act="gelu", out_dtype=BF16)
            h = _gmlp_gate_out(z, h, b_ln_g[j], b_ln_b[j], b_w_s[j], b_b_s[j], b_w_out[j],
                               ln_g[i, 0], ln_b[i, 0])
        else:
            pad = LANES - C_GATE_RANK
            w_in = jnp.pad(c_w_in[j], ((0, 0), (0, pad)))
            w_up = jnp.pad(c_w_gk_up[j], ((0, pad), (0, 0)))
            proj = _matmul(h, w_in.astype(BF16), tn=640, out_dtype=BF16)
            o = _gla_core(proj.reshape(B, S, C_IN_WIDTH + pad), w_up, c_b_gk[j], c_norm_g[j])
            h = _matmul_res_ln(o.reshape(T, D), c_w_out[j], h, ln_g[i, 0], ln_b[i, 0])
        h = _moe_block(h, i, moe_w_router[i], moe_b_router[i], moe_w_gate_up, moe_b_gate_up,
                       moe_w_down, moe_b_down, ln_g[i, 1], ln_b[i, 1])
    return h.reshape(B, S, D)
```

```python
import functools
import math

import numpy as np
import jax
import jax.numpy as jnp
from jax import lax
from jax.experimental import pallas as pl
from jax.experimental.pallas import tpu as pltpu

F32 = jnp.float32
BF16 = jnp.bfloat16
I32 = jnp.int32
HI = lax.Precision.HIGHEST

D_MODEL = 1024
DEPTH = 4
N_MIXERS = 3
ALPHA = (2.0 * DEPTH) ** 0.25
LN_EPS = 1e-5
LOG2E = math.log2(math.e)

A_HEADS = 8
A_DH = D_MODEL // A_HEADS
MOBA_BLOCK = 256
MOBA_TOPK = 3
REL_BUCKETS = 32
REL_MAX_DIST = 128

B_GROUPS = 8
B_WIDTH = 2 * D_MODEL
B_CHUNK = 128
B_GW = B_WIDTH // B_GROUPS

C_HEADS = 4
C_KEY_DIM = D_MODEL // 2
C_VAL_DIM = D_MODEL
C_DK = C_KEY_DIM // C_HEADS
C_DV = C_VAL_DIM // C_HEADS
C_GATE_RANK = 16
C_GATE_NORMALIZER = 16.0
C_IN_WIDTH = 2 * C_KEY_DIM + 2 * C_VAL_DIM + C_GATE_RANK

N_EXPERTS = 32
TOP_K = 4
SWIGLU_LIMIT = 7.0
SWIGLU_ALPHA = 1.702

LANES = 128
SUBLANES = 8
VMEM_LIMIT = 56 * 1024 * 1024

GLA_CHUNK = 128
GLA_SUB = 16
GLA_HEADS_PER_STEP = 4
GLA_SEQ_TILE = 1024
LN_ROW_GROUPS = 4
MOE_BM = 1024
MOE_SUB = 256
MOE_TM = 512
MOE_SEG = SUBLANES
MOE_SEG_BITS = (MOE_TM // MOE_SEG).bit_length()
MOE_STAGE = MOE_TM * TOP_K + N_EXPERTS * MOE_SEG
MOE_PERM_CHUNK = 256
MOE_PACK_W = D_MODEL // 2


def _cparams(*sem):
    return pltpu.CompilerParams(dimension_semantics=sem, vmem_limit_bytes=VMEM_LIMIT)


def _ln(x, g, b):
    mu = jnp.mean(x, axis=-1, keepdims=True)
    xc = x - mu
    var = jnp.mean(xc * xc, axis=-1, keepdims=True)
    return xc * lax.rsqrt(var + LN_EPS) * g + b


def _mm_body(x_ref, w_ref, o_ref, *, act, tn):
    x = x_ref[...].astype(BF16)
    for n0 in range(0, w_ref.shape[1], tn):
        acc = jnp.dot(x, w_ref[:, n0:n0 + tn], preferred_element_type=F32)
        if act == "gelu":
            acc = 0.5 * acc * (1.0 + lax.erf(acc * (2.0 ** -0.5)))
        o_ref[:, n0:n0 + tn] = acc.astype(o_ref.dtype)


def _matmul(x, w, *, act=None, tm=512, tn=512, out_dtype=F32):
    M, K = x.shape
    N = w.shape[1]
    assert M % tm == 0 and N % tn == 0
    return pl.pallas_call(
        functools.partial(_mm_body, act=act, tn=tn),
        grid=(M // tm,),
        in_specs=[pl.BlockSpec((tm, K), lambda i: (i, 0)),
                  pl.BlockSpec((K, N), lambda i: (0, 0))],
        out_specs=pl.BlockSpec((tm, N), lambda i: (i, 0)),
        out_shape=jax.ShapeDtypeStruct((M, N), out_dtype),
        compiler_params=_cparams("arbitrary"),
        name="proj_matmul",
    )(x, w)


def _mm_res_ln_body(x_ref, w_ref, h_ref, g_ref, b_ref, o_ref, wb_ref):
    @pl.when(pl.program_id(0) == 0)
    def _():
        wb_ref[...] = w_ref[...].astype(BF16)

    rows = x_ref.shape[0] // LN_ROW_GROUPS
    for r0 in range(0, x_ref.shape[0], rows):
        t = jnp.dot(x_ref[r0:r0 + rows, :].astype(BF16), wb_ref[...], preferred_element_type=F32)
        o_ref[r0:r0 + rows, :] = _ln(ALPHA * h_ref[r0:r0 + rows, :] + t, g_ref[...], b_ref[...])


def _matmul_res_ln(x, w, h, g, b, *, tm=512):
    M, K = x.shape
    N = w.shape[1]
    return pl.pallas_call(
        _mm_res_ln_body,
        grid=(M // tm,),
        in_specs=[pl.BlockSpec((tm, K), lambda i: (i, 0)),
                  pl.BlockSpec((K, N), lambda i: (0, 0)),
                  pl.BlockSpec((tm, N), lambda i: (i, 0)),
                  pl.BlockSpec((1, N), lambda i: (0, 0)),
                  pl.BlockSpec((1, N), lambda i: (0, 0))],
        out_specs=pl.BlockSpec((tm, N), lambda i: (i, 0)),
        out_shape=jax.ShapeDtypeStruct((M, N), F32),
        scratch_shapes=[pltpu.VMEM((K, N), BF16)],
        compiler_params=_cparams("arbitrary"),
        name="outproj_res_ln",
    )(x, w, h, g.reshape(1, N), b.reshape(1, N))


def _t5_bucket_lower_bounds():
    n = np.arange(0, 4 * REL_MAX_DIST, dtype=np.int64)
    max_exact = REL_BUCKETS // 2
    nf = np.maximum(n, 1).astype(np.float32)
    large = max_exact + (np.log(nf / np.float32(max_exact)) / np.float32(math.log(REL_MAX_DIST / max_exact))
                         * np.float32(REL_BUCKETS - max_exact)).astype(np.int32)
    large = np.minimum(large, REL_BUCKETS - 1)
    bucket = np.where(n < max_exact, n, large)
    assert np.all(np.diff(bucket) >= 0) and bucket[-1] == REL_BUCKETS - 1
    return [int(np.argmax(bucket >= b)) for b in range(REL_BUCKETS)]


_BUCKET_LO = _t5_bucket_lower_bounds()


def _bias_table_body(rel_ref, o_ref):
    blk = MOBA_BLOCK
    row = lax.broadcasted_iota(I32, (blk, blk), 0)
    col = lax.broadcasted_iota(I32, (blk, blk), 1)
    for t in range(2):
        d = row - col + t * blk
        for h in range(A_HEADS):
            val = jnp.full((blk, blk), rel_ref[REL_BUCKETS - 1, h], F32)
            for b in range(REL_BUCKETS - 2, -1, -1):
                val = jnp.where(d < _BUCKET_LO[b + 1], rel_ref[b, h], val)
            val = val * LOG2E
            if t == 0:
                val = jnp.where(d >= 0, val, -jnp.inf)
            o_ref[h, t] = val


def _bias_tables(rel_bias):
    return pl.pallas_call(
        _bias_table_body,
        in_specs=[pl.BlockSpec(memory_space=pltpu.SMEM)],
        out_shape=jax.ShapeDtypeStruct((A_HEADS, 2, MOBA_BLOCK, MOBA_BLOCK), F32),
        compiler_params=pltpu.CompilerParams(vmem_limit_bytes=VMEM_LIMIT),
        name="rel_bias_tables",
    )(rel_bias)


def _moba_body(rel_ref, q_ref, k_ref, v_ref, tb_ref, o_ref, s_s, *, seq):
    blk, dh = MOBA_BLOCK, A_DH
    nb = seq // blk
    h = pl.program_id(1)
    kmean = jnp.mean(k_ref[0].astype(F32).reshape(nb, blk, dh), axis=1)
    kmean_pad = jnp.concatenate([kmean, jnp.zeros((LANES - nb, dh), F32)], axis=0)
    b_far = rel_ref[REL_BUCKETS - 1, h] * LOG2E
    lane = lax.broadcasted_iota(I32, (blk, LANES), 1)
    ones_bf = jnp.ones((LANES, LANES), BF16)

    for n in range(nb):
        r0 = n * blk
        s_s[r0:seq, r0:r0 + blk] = lax.dot_general(q_ref[0, r0:seq, :], k_ref[0, r0:r0 + blk, :],
                                                   (((1,), (1,)), ((), ())), preferred_element_type=F32)

    for c in range(nb):
        q = q_ref[0, c * blk:(c + 1) * blk, :]
        n_keys = (c + 1) * blk
        s = s_s[c * blk:(c + 1) * blk, 0:n_keys]
        if c > MOBA_TOPK:
            gate = lax.dot_general(q.astype(F32), kmean_pad, (((1,), (1,)), ((), ())),
                                   precision=HI, preferred_element_type=F32)
            gate = jnp.where(lane < c, gate, -jnp.inf)
        pieces = []
        for n in range(c):
            shift = b_far
            if c > MOBA_TOPK:
                gn = gate[:, n:n + 1]
                beats = (gate > gn) | ((gate == gn) & (lane < n))
                rank = jnp.dot(jnp.where(beats, 1.0, 0.0).astype(BF16), ones_bf, preferred_element_type=F32)
                shift = jnp.where(rank < MOBA_TOPK, b_far, -jnp.inf)
                shift = jnp.concatenate([shift] * (blk // LANES), axis=1)
            if n == c - 1:
                shift = tb_ref[0, 1] + (shift - b_far)
            pieces.append(s[:, n * blk:(n + 1) * blk] + shift)
        pieces.append(s[:, c * blk:(c + 1) * blk] + tb_ref[0, 0])
        logits = jnp.concatenate(pieces, axis=1) if c > 0 else pieces[0]
        m = jnp.max(logits, axis=1, keepdims=True)
        p = jnp.exp2(logits - m)
        denom = jnp.sum(p, axis=1, keepdims=True)
        o = jnp.dot(p.astype(BF16), v_ref[0, 0:n_keys, :], preferred_element_type=F32)
        o_ref[0, c * blk:(c + 1) * blk, :] = o / denom


def _moba_attention(qkv, rel_bias, tables):
    B, S, _ = qkv.shape
    H, dh = A_HEADS, A_DH
    assert S % MOBA_BLOCK == 0 and S // MOBA_BLOCK <= SUBLANES
    return pl.pallas_call(
        functools.partial(_moba_body, seq=S),
        grid=(B, H),
        in_specs=[pl.BlockSpec(memory_space=pltpu.SMEM),
                  pl.BlockSpec((1, S, dh), lambda b, h: (b, 0, h)),
                  pl.BlockSpec((1, S, dh), lambda b, h: (b, 0, H + h)),
                  pl.BlockSpec((1, S, dh), lambda b, h: (b, 0, 2 * H + h)),
                  pl.BlockSpec((1, 2, MOBA_BLOCK, MOBA_BLOCK), lambda b, h: (h, 0, 0, 0))],
        out_specs=pl.BlockSpec((1, S, dh), lambda b, h: (b, 0, h)),
        out_shape=jax.ShapeDtypeStruct((B, S, H * dh), F32),
        scratch_shapes=[pltpu.VMEM((S, S), F32)],
        compiler_params=_cparams("arbitrary", "arbitrary"),
        name="moba_attention",
    )(rel_bias, qkv, qkv, qkv, tables)


def _gmlp_body(u_ref, v_ref, h_ref, vg_ref, vb_ref, ws_ref, bs_ref, wo_ref, g_ref, b_ref, o_ref,
               wst_s, wob_s, *, tm):
    @pl.when(pl.program_id(0) == 0)
    def _():
        row = lax.broadcasted_iota(I32, (B_CHUNK, B_CHUNK), 0)
        col = lax.broadcasted_iota(I32, (B_CHUNK, B_CHUNK), 1)
        for g in range(B_GROUPS):
            wst_s[g] = jnp.where(row >= col, ws_ref[g], 0.0).astype(BF16)
        wob_s[...] = wo_ref[...].astype(BF16)

    vn = _ln(v_ref[...].astype(F32), vg_ref[...], vb_ref[...]).astype(BF16)
    u = u_ref[...].astype(F32)
    for c in range(tm // B_CHUNK):
        r0 = c * B_CHUNK
        cols = []
        for g in range(B_GROUPS):
            c0 = g * B_GW
            mixed = jnp.dot(wst_s[g], vn[r0:r0 + B_CHUNK, c0:c0 + B_GW],
                            preferred_element_type=F32) + bs_ref[g]
            cols.append(u[r0:r0 + B_CHUNK, c0:c0 + B_GW] * mixed)
        y = jnp.concatenate(cols, axis=1).astype(BF16)
        t = jnp.dot(y, wob_s[...], preferred_element_type=F32)
        o_ref[r0:r0 + B_CHUNK, :] = _ln(ALPHA * h_ref[r0:r0 + B_CHUNK, :] + t, g_ref[...], b_ref[...])


def _gmlp_gate_out(z, h, v_g, v_b, w_s, b_s, w_out, g, b, *, tm=256):
    T = z.shape[0]
    D = D_MODEL
    return pl.pallas_call(
        functools.partial(_gmlp_body, tm=tm),
        grid=(T // tm,),
        in_specs=[pl.BlockSpec((tm, B_WIDTH), lambda i: (i, 0)),
                  pl.BlockSpec((tm, B_WIDTH), lambda i: (i, 1)),
                  pl.BlockSpec((tm, D), lambda i: (i, 0)),
                  pl.BlockSpec((1, B_WIDTH), lambda i: (0, 0)),
                  pl.BlockSpec((1, B_WIDTH), lambda i: (0, 0)),
                  pl.BlockSpec((B_GROUPS, B_CHUNK, B_CHUNK), lambda i: (0, 0, 0)),
                  pl.BlockSpec((B_GROUPS, B_CHUNK, 1), lambda i: (0, 0, 0)),
                  pl.BlockSpec((B_WIDTH, D), lambda i: (0, 0)),
                  pl.BlockSpec((1, D), lambda i: (0, 0)),
                  pl.BlockSpec((1, D), lambda i: (0, 0))],
        out_specs=pl.BlockSpec((tm, D), lambda i: (i, 0)),
        out_shape=jax.ShapeDtypeStruct((T, D), F32),
        scratch_shapes=[pltpu.VMEM((B_GROUPS, B_CHUNK, B_CHUNK), BF16), pltpu.VMEM((B_WIDTH, D), BF16)],
        compiler_params=_cparams("arbitrary"),
        name="gmlp_gate_out",
    )(z, z, h, v_g.reshape(1, B_WIDTH), v_b.reshape(1, B_WIDTH), w_s, b_s.reshape(B_GROUPS, B_CHUNK, 1),
      w_out, g.reshape(1, D), b.reshape(1, D))


def _gla_body(q_ref, k_ref, v_ref, g_ref, gl_ref, wup_ref, bgk_ref, ng_ref, o_ref, la_s, st_s, raw_s, *, seq):
    ch, sub, dk, dv = GLA_CHUNK, GLA_SUB, C_DK, C_DV
    x = jnp.dot(gl_ref[0].astype(F32), wup_ref[...], precision=HI, preferred_element_type=F32) + bgk_ref[...]
    la_s[...] = (jnp.minimum(x, 0.0) - jnp.log1p(jnp.exp(-jnp.abs(x)))) * (1.0 / C_GATE_NORMALIZER)

    @pl.when(pl.program_id(2) == 0)
    def _():
        st_s[...] = jnp.zeros(st_s.shape, F32)

    row_c = lax.broadcasted_iota(I32, (ch, ch), 0)
    col_c = lax.broadcasted_iota(I32, (ch, ch), 1)
    tri = (row_c >= col_c).astype(F32)
    eye = (row_c == col_c).astype(F32)
    sub_i = lax.broadcasted_iota(I32, (sub, ch), 0)
    lane_j = lax.broadcasted_iota(I32, (sub, ch), 1)

    def chunk(c, carry):
        for hh in range(GLA_HEADS_PER_STEP):
            chunk_head(c, hh)
        return carry

    def chunk_head(c, hh):
        s0 = pl.multiple_of(c * ch, ch)
        a = la_s[pl.ds(s0, ch), hh * dk:(hh + 1) * dk]
        b = jnp.dot(tri, a, precision=HI, preferred_element_type=F32)
        q = q_ref[0, pl.ds(s0, ch), hh * dk:(hh + 1) * dk].astype(F32) * (dk ** -0.5)
        k = k_ref[0, pl.ds(s0, ch), hh * dk:(hh + 1) * dk].astype(F32)
        v = v_ref[0, pl.ds(s0, ch), hh * dv:(hh + 1) * dv].astype(BF16)
        st = st_s[hh]
        o = jnp.dot((q * jnp.exp(b)).astype(BF16), st.astype(BF16), preferred_element_type=F32)

        blocks = []
        for sb in range(ch // sub):
            r0 = sb * sub
            q_i = q[r0:r0 + sub]
            b_i = b[r0:r0 + sub]
            k_i = k[r0:r0 + sub]
            att = jnp.zeros((sub, ch), F32)
            for j in range(sub):
                e = jnp.exp(jnp.minimum(b_i - b_i[j:j + 1, :], 0.0))
                colv = jnp.sum(q_i * k_i[j:j + 1, :] * e, axis=1, keepdims=True)
                att = jnp.where((lane_j == r0 + j) & (sub_i >= j), colv, att)
            if sb > 0:
                ref_b = b[r0 - 1:r0, :]
                q_t = q_i * jnp.exp(b_i - ref_b)
                k_t = k * jnp.exp(jnp.minimum(ref_b - b, 0.0))
                off = lax.dot_general(q_t.astype(BF16), k_t.astype(BF16), (((1,), (1,)), ((), ())),
                                      preferred_element_type=F32)
                att = jnp.where(lane_j < r0, off, att)
            blocks.append(att)
        att_full = jnp.concatenate(blocks, axis=0)
        o = o + jnp.dot(att_full.astype(BF16), v, preferred_element_type=F32)
        raw_s[pl.ds(s0, ch), hh * dv:(hh + 1) * dv] = o

        b_last = b[ch - 1:ch, :]
        k_d = k * jnp.exp(b_last - b)
        upd = jnp.dot(k_d.T.astype(BF16), v, preferred_element_type=F32)
        decay_col = lax.dot_general(eye, jnp.broadcast_to(jnp.exp(b_last), (SUBLANES, dk)),
                                    (((1,), (1,)), ((), ())), precision=HI,
                                    preferred_element_type=F32)[:, 0:1]
        st_s[hh] = decay_col * st + upd

    lax.fori_loop(0, seq // ch, chunk, 0)

    for hh in range(GLA_HEADS_PER_STEP):
        o = raw_s[:, hh * dv:(hh + 1) * dv]
        rms = o * lax.rsqrt(jnp.mean(o * o, axis=-1, keepdims=True) + LN_EPS) * ng_ref[...]
        gg = g_ref[0, :, hh * dv:(hh + 1) * dv].astype(F32)
        o_ref[0, :, hh * dv:(hh + 1) * dv] = rms * (gg * jax.nn.sigmoid(gg))


def _gla_core(proj, w_up_pad, b_gk, norm_g):
    B, S, _ = proj.shape
    H, hps = C_HEADS, GLA_HEADS_PER_STEP
    dk, dv = C_DK * hps, C_DV * hps
    st = GLA_SEQ_TILE
    assert C_DK == LANES and S % st == 0 and st % GLA_CHUNK == 0 and H % hps == 0
    k_off = C_KEY_DIM // dk
    v_off = 2 * C_KEY_DIM // dv
    g_off = (2 * C_KEY_DIM + C_VAL_DIM) // dv
    gl_off = (2 * C_KEY_DIM + 2 * C_VAL_DIM) // LANES
    return pl.pallas_call(
        functools.partial(_gla_body, seq=st),
        grid=(B, H // hps, S // st),
        in_specs=[pl.BlockSpec((1, st, dk), lambda b, h, s: (b, s, h)),
                  pl.BlockSpec((1, st, dk), lambda b, h, s: (b, s, k_off + h)),
                  pl.BlockSpec((1, st, dv), lambda b, h, s: (b, s, v_off + h)),
                  pl.BlockSpec((1, st, dv), lambda b, h, s: (b, s, g_off + h)),
                  pl.BlockSpec((1, st, LANES), lambda b, h, s: (b, s, gl_off)),
                  pl.BlockSpec((LANES, dk), lambda b, h, s: (0, h)),
                  pl.BlockSpec((1, dk), lambda b, h, s: (0, h)),
                  pl.BlockSpec((1, C_DV), lambda b, h, s: (0, 0))],
        out_specs=pl.BlockSpec((1, st, dv), lambda b, h, s: (b, s, h)),
        out_shape=jax.ShapeDtypeStruct((B, S, C_VAL_DIM), F32),
        scratch_shapes=[pltpu.VMEM((st, dk), F32), pltpu.VMEM((hps, C_DK, C_DV), F32), pltpu.VMEM((st, dv), F32)],
        compiler_params=_cparams("arbitrary", "arbitrary", "arbitrary"),
        name="gla_core",
    )(proj, proj, proj, proj, proj, w_up_pad, b_gk.reshape(1, C_KEY_DIM), norm_g.reshape(1, C_DV))


def _router_body(h_ref, wt_ref, b_ref, slot_ref, gate_ref, cnt_ref, lbase_ref, gbase_ref, run_s, ws_s, *, tm):
    E = N_EXPERTS

    @pl.when(pl.program_id(0) == 0)
    def _():
        run_s[...] = jnp.zeros(run_s.shape, F32)
        w = wt_ref[...]
        w1 = w.astype(BF16)
        ws_s[0:E, :] = w1
        ws_s[E:2 * E, :] = (w - w1.astype(F32)).astype(BF16)

    h = h_ref[...]
    h1 = h.astype(BF16)
    h2 = (h - h1.astype(F32)).astype(BF16)
    nt_dims = (((1,), (1,)), ((), ()))
    both = lax.dot_general(ws_s[...], h1, nt_dims, preferred_element_type=F32)
    logits = (both[0:E] + (both[E:2 * E] + lax.dot_general(ws_s[0:E, :], h2, nt_dims, preferred_element_type=F32))
              + b_ref[...])
    row = lax.broadcasted_iota(I32, (E, tm), 0)
    rest = logits
    top_val, onehot = [], []
    for _ in range(TOP_K):
        m = jnp.max(rest, axis=0, keepdims=True)
        idx = jnp.min(jnp.where(rest == m, row, E), axis=0, keepdims=True)
        oh = row == idx
        rest = jnp.where(oh, -jnp.inf, rest)
        top_val.append(m)
        onehot.append(oh)
    ex = [jnp.exp(v - top_val[0]) for v in top_val]
    den = ex[0] + ex[1] + ex[2] + ex[3]
    chosen = sum(oh.astype(F32) for oh in onehot)

    tr = lax.broadcasted_iota(I32, (tm, tm), 0)
    tc = lax.broadcasted_iota(I32, (tm, tm), 1)
    earlier = (tr < tc).astype(BF16)
    seen = jnp.dot(chosen.astype(BF16), earlier, preferred_element_type=F32)
    cnt = jnp.sum(chosen, axis=1, keepdims=True)
    cnt_pad = jnp.floor((cnt + (MOE_SEG - 1)) * (1.0 / MOE_SEG)) * MOE_SEG
    cnt_cols = jnp.broadcast_to(cnt_pad, (E, LANES))
    er = lax.broadcasted_iota(I32, (E, E), 0)
    ec = lax.broadcasted_iota(I32, (E, E), 1)
    lbase_cols = jnp.dot((ec < er).astype(F32), cnt_cols, precision=HI, preferred_element_type=F32)
    pos = seen + lbase_cols[:, 0:1]

    row_k = lax.broadcasted_iota(I32, (TOP_K, tm), 0)
    s_out = jnp.zeros((TOP_K, tm), I32)
    g_out = jnp.zeros((TOP_K, tm), F32)
    for kk in range(TOP_K):
        sk = jnp.sum(jnp.where(onehot[kk], pos, 0.0), axis=0, keepdims=True)
        s_out = jnp.where(row_k == kk, sk.astype(I32), s_out)
        g_out = jnp.where(row_k == kk, ex[kk] / den, g_out)
    slot_ref[...] = s_out
    gate_ref[...] = g_out
    cnt_ref[...] = cnt_cols.astype(I32)
    lbase_ref[...] = lbase_cols.astype(I32)
    gbase_ref[...] = run_s[...].astype(I32)
    run_s[...] = run_s[...] + cnt_cols


def _router(h, w, b, *, tm=MOE_TM):
    T, D = h.shape
    E = N_EXPERTS
    nt = T // tm
    tab = jax.ShapeDtypeStruct((nt * E, LANES), I32)
    tab_spec = pl.BlockSpec((E, LANES), lambda i: (i, 0))
    return pl.pallas_call(
        functools.partial(_router_body, tm=tm),
        grid=(nt,),
        in_specs=[pl.BlockSpec((tm, D), lambda i: (i, 0)),
                  pl.BlockSpec((E, D), lambda i: (0, 0)),
                  pl.BlockSpec((E, 1), lambda i: (0, 0))],
        out_specs=[pl.BlockSpec((TOP_K, tm), lambda i: (0, i)),
                   pl.BlockSpec((TOP_K, tm), lambda i: (0, i)),
                   tab_spec, tab_spec, tab_spec],
        out_shape=[jax.ShapeDtypeStruct((TOP_K, T), I32),
                   jax.ShapeDtypeStruct((TOP_K, T), F32),
                   tab, tab, tab],
        scratch_shapes=[pltpu.VMEM((E, LANES), F32), pltpu.VMEM((2 * E, D), BF16)],
        compiler_params=_cparams("arbitrary"),
        name="moe_router",
    )(h, w.T, b.reshape(E, 1))


def _pack_pairs(left, right):
    hi = lax.bitcast_convert_type(left, I32) & jnp.int32(-65536)
    lo = lax.shift_right_logical(lax.bitcast_convert_type(right, I32), 16)
    return hi | lo


def _unpack_pairs(word):
    left = lax.bitcast_convert_type(word & jnp.int32(-65536), F32)
    right = lax.bitcast_convert_type(lax.shift_left(word, 16), F32)
    return left.astype(BF16), right.astype(BF16)


def _segment_starts(seg_g, seg_l, seg_n, tile, make_copy):
    def per_expert(e, carry):
        idx = tile * N_EXPERTS + e
        n = seg_n[idx]
        g = seg_g[idx]
        l = seg_l[idx]
        for bit in range(MOE_SEG_BITS - 1, -1, -1):
            size = MOE_SEG << bit

            @pl.when((n & size) != 0)
            def _():
                off = n & ~(2 * size - 1)
                make_copy(pl.multiple_of(g + off, MOE_SEG), pl.multiple_of(l + off, MOE_SEG), size).start()
        return carry

    lax.fori_loop(0, N_EXPERTS, per_expert, 0)


def _segment_waits(seg_l, seg_n, tile, make_copy):
    last = tile * N_EXPERTS + N_EXPERTS - 1
    total = seg_l[last] + seg_n[last]
    for bit in range((MOE_STAGE // MOE_SEG).bit_length() - 1, -1, -1):
        size = MOE_SEG << bit

        @pl.when((total & size) != 0)
        def _():
            make_copy(0, 0, size).wait()


def _dispatch_body(segg_ref, segl_ref, segn_ref, zf_ref, slot_t_ref, h_ref, xs_ref, stage, zbuf, sem, *, tm, bm):
    i = pl.program_id(0)

    def zero_fill(start):
        def act(cp):
            if start:
                cp.start()
            else:
                cp.wait()

        def per_expert(e, carry):
            g = zf_ref[e]
            n = zf_ref[N_EXPERTS + e]
            for bit in range((bm // MOE_SEG).bit_length() - 2, -1, -1):
                size = MOE_SEG << bit

                @pl.when((n & size) != 0)
                def _():
                    off = n & ~(2 * size - 1)
                    act(pltpu.make_async_copy(
                        zbuf.at[pl.ds(0, size), :],
                        xs_ref.at[pl.ds(pl.multiple_of(g + off, MOE_SEG), size), :], sem.at[2]))
            return carry

        def per_tail(j, carry):
            row = zf_ref[2 * N_EXPERTS + j]

            @pl.when(row >= 0)
            def _():
                act(pltpu.make_async_copy(zbuf, xs_ref.at[pl.ds(pl.multiple_of(row, MOE_SEG), bm), :], sem.at[2]))
            return carry

        lax.fori_loop(0, N_EXPERTS, per_expert, 0)
        lax.fori_loop(0, zf_ref.shape[0] - 2 * N_EXPERTS, per_tail, 0)

    @pl.when(i == 0)
    def _():
        zbuf[...] = jnp.zeros(zbuf.shape, I32)
        zero_fill(True)

    @pl.when(i == pl.num_programs(0) - 1)
    def _():
        zero_fill(False)

    slot_i = i % 2
    hb = h_ref[...].astype(BF16)
    for c in range(MOE_STAGE // MOE_PERM_CHUNK):
        r0 = c * MOE_PERM_CHUNK
        rows = lax.broadcasted_iota(I32, (MOE_PERM_CHUNK, tm), 0) + r0
        hit = rows == slot_t_ref[0:1, :]
        for kk in range(1, TOP_K):
            hit = hit | (rows == slot_t_ref[kk:kk + 1, :])
        perm = jnp.where(hit, 1.0, 0.0).astype(BF16)
        rows_f = jnp.dot(perm, hb, preferred_element_type=F32)
        stage[slot_i, r0:r0 + MOE_PERM_CHUNK, :] = _pack_pairs(rows_f[:, :MOE_PACK_W], rows_f[:, MOE_PACK_W:])

    def copier(buf):
        def make_copy(g, l, size):
            return pltpu.make_async_copy(stage.at[buf, pl.ds(l, size), :], xs_ref.at[pl.ds(g, size), :],
                                         sem.at[buf])
        return make_copy

    _segment_starts(segg_ref, segl_ref, segn_ref, i, copier(slot_i))

    @pl.when(i > 0)
    def _():
        _segment_waits(segl_ref, segn_ref, i - 1, copier(1 - slot_i))

    @pl.when(i == pl.num_programs(0) - 1)
    def _():
        _segment_waits(segl_ref, segn_ref, i, copier(slot_i))


def _dispatch(seg_g, seg_l, seg_n, zero_rows, slot_t, h, n_rows, *, tm=MOE_TM, bm=MOE_BM):
    T, D = h.shape
    grid_spec = pltpu.PrefetchScalarGridSpec(
        num_scalar_prefetch=4,
        grid=(T // tm,),
        in_specs=[pl.BlockSpec((TOP_K, tm), lambda i, *_: (0, i)),
                  pl.BlockSpec((tm, D), lambda i, *_: (i, 0))],
        out_specs=pl.BlockSpec(memory_space=pl.ANY),
        scratch_shapes=[pltpu.VMEM((2, MOE_STAGE, MOE_PACK_W), I32), pltpu.VMEM((bm, MOE_PACK_W), I32),
                        pltpu.SemaphoreType.DMA((3,))],
    )
    return pl.pallas_call(
        functools.partial(_dispatch_body, tm=tm, bm=bm),
        grid_spec=grid_spec,
        out_shape=jax.ShapeDtypeStruct((n_rows, MOE_PACK_W), I32),
        compiler_params=pltpu.CompilerParams(dimension_semantics=("arbitrary",), vmem_limit_bytes=VMEM_LIMIT,
                                             has_side_effects=True),
        name="moe_dispatch",
    )(seg_g, seg_l, seg_n, zero_rows, slot_t, h)


def _expert_body(be_ref, nb_ref, nv_ref, nx_ref, xs_ref, wgu_hbm, bgu_ref, wd_hbm, bd_ref, ys_ref,
                 wgu_f, wd_f, wgu_s, wd_s, sem, *, bm, layer):
    i = pl.program_id(0)
    e = be_ref[i]
    prev = be_ref[jnp.maximum(i - 1, 0)]
    live = i < nb_ref[0]
    n_valid = jnp.where(live, nv_ref[i], 0)

    def weight_copies(expert):
        return (pltpu.make_async_copy(wgu_hbm.at[layer, expert], wgu_f, sem.at[0]),
                pltpu.make_async_copy(wd_hbm.at[layer, expert], wd_f, sem.at[1]))

    @pl.when(i == 0)
    def _():
        for cp in weight_copies(e):
            cp.start()

    @pl.when(live & ((i == 0) | (e != prev)))
    def _():
        for cp in weight_copies(e):
            cp.wait()
        wgu_s[...] = wgu_f[...].astype(BF16)
        wd_s[...] = wd_f[...].astype(BF16)
        nxt = nx_ref[e]

        @pl.when(nxt >= 0)
        def _():
            for cp in weight_copies(nxt):
                cp.start()

    def mlp(r0, rows):
        x = jnp.concatenate(_unpack_pairs(xs_ref[r0:r0 + rows, :]), axis=1)
        hh = jnp.dot(x, wgu_s[...], preferred_element_type=F32) + bgu_ref[...]
        d_e = hh.shape[1] // 2
        h_gate = jnp.minimum(hh[:, :d_e], SWIGLU_LIMIT)
        h_up = jnp.clip(hh[:, d_e:], -SWIGLU_LIMIT, SWIGLU_LIMIT)
        act = h_gate * jax.nn.sigmoid(SWIGLU_ALPHA * h_gate) * (h_up + 1.0)
        y = jnp.dot(act.astype(BF16), wd_s[...], preferred_element_type=F32) + bd_ref[...]
        y = y.astype(BF16).astype(F32)
        ys_ref[r0:r0 + rows, :] = _pack_pairs(y[:, :MOE_PACK_W], y[:, MOE_PACK_W:])

    full = n_valid > bm - MOE_SUB

    @pl.when(full)
    def _():
        mlp(0, bm)

    for r0 in range(0, bm, MOE_SUB):
        @pl.when(jnp.logical_not(full) & (n_valid > r0))
        def _(r0=r0):
            mlp(r0, MOE_SUB)

        @pl.when(n_valid <= r0)
        def _(r0=r0):
            ys_ref[r0:r0 + MOE_SUB, :] = jnp.zeros((MOE_SUB, ys_ref.shape[1]), I32)


def _experts(block_e, n_live, n_valid, next_e, xs, layer, w_gu, b_gu, w_down, b_down, *, bm=MOE_BM):
    L, E, D, D2 = w_gu.shape
    n_blocks = xs.shape[0] // bm

    def blk(i, nb):
        return jnp.maximum(jnp.minimum(i, nb[0] - 1), 0)

    grid_spec = pltpu.PrefetchScalarGridSpec(
        num_scalar_prefetch=4,
        grid=(n_blocks,),
        in_specs=[pl.BlockSpec((bm, MOE_PACK_W), lambda i, be, nb, nv, nx: (blk(i, nb), 0)),
                  pl.BlockSpec(memory_space=pl.ANY),
                  pl.BlockSpec((None, None, 1, D2), lambda i, be, nb, nv, nx: (layer, be[blk(i, nb)], 0, 0)),
                  pl.BlockSpec(memory_space=pl.ANY),
                  pl.BlockSpec((None, None, 1, D), lambda i, be, nb, nv, nx: (layer, be[blk(i, nb)], 0, 0))],
        out_specs=pl.BlockSpec((bm, MOE_PACK_W), lambda i, be, nb, nv, nx: (i, 0)),
        scratch_shapes=[pltpu.VMEM((D, D2), F32), pltpu.VMEM((D2 // 2, D), F32),
                        pltpu.VMEM((D, D2), BF16), pltpu.VMEM((D2 // 2, D), BF16),
                        pltpu.SemaphoreType.DMA((2,))],
    )
    return pl.pallas_call(
        functools.partial(_expert_body, bm=bm, layer=layer),
        grid_spec=grid_spec,
        out_shape=jax.ShapeDtypeStruct(xs.shape, I32),
        compiler_params=_cparams("arbitrary"),
        name="moe_experts",
    )(block_e, n_live, n_valid, next_e, xs, w_gu, b_gu.reshape(L, E, 1, D2), w_down, b_down.reshape(L, E, 1, D))


def _combine_body(segg_ref, segl_ref, segn_ref, ys_ref, slot_ref, gate_ref, h_ref, g_ref, b_ref, o_ref,
                  stage, mix_s, sem, *, tm):
    i = pl.program_id(0)

    slot_i = i % 2

    def copier(buf):
        def make_copy(g, l, size):
            return pltpu.make_async_copy(ys_ref.at[pl.ds(g, size), :], stage.at[buf, pl.ds(l, size), :],
                                         sem.at[buf])
        return make_copy

    @pl.when(i == 0)
    def _():
        stage[...] = jnp.zeros(stage.shape, I32)
        _segment_starts(segg_ref, segl_ref, segn_ref, i, copier(slot_i))

    @pl.when(i + 1 < pl.num_programs(0))
    def _():
        _segment_starts(segg_ref, segl_ref, segn_ref, i + 1, copier(1 - slot_i))

    slot = slot_ref[...]
    gate = gate_ref[...]
    for c in range(MOE_STAGE // MOE_PERM_CHUNK):
        c0 = c * MOE_PERM_CHUNK
        cols = lax.broadcasted_iota(I32, (tm, MOE_PERM_CHUNK), 1) + c0
        w = jnp.zeros((tm, MOE_PERM_CHUNK), F32)
        for kk in range(TOP_K):
            w = jnp.where(cols == slot[:, kk:kk + 1], gate[:, kk:kk + 1], w)
        mix_s[:, c0:c0 + MOE_PERM_CHUNK] = w.astype(BF16)

    _segment_waits(segl_ref, segn_ref, i, copier(slot_i))
    left, right = _unpack_pairs(stage[slot_i])
    rows = tm // 2
    for r0 in range(0, tm, rows):
        mix = mix_s[r0:r0 + rows, :]
        y = jnp.concatenate([jnp.dot(mix, left, preferred_element_type=F32),
                             jnp.dot(mix, right, preferred_element_type=F32)], axis=1)
        o_ref[r0:r0 + rows, :] = _ln(ALPHA * h_ref[r0:r0 + rows, :] + y, g_ref[...], b_ref[...])


def _combine(seg_g, seg_l, seg_n, ys, slot, gate, h, g, b, *, tm=MOE_TM):
    T, D = h.shape
    grid_spec = pltpu.PrefetchScalarGridSpec(
        num_scalar_prefetch=3,
        grid=(T // tm,),
        in_specs=[pl.BlockSpec(memory_space=pl.ANY),
                  pl.BlockSpec((tm, TOP_K), lambda i, *_: (i, 0)),
                  pl.BlockSpec((tm, TOP_K), lambda i, *_: (i, 0)),
                  pl.BlockSpec((tm, D), lambda i, *_: (i, 0)),
                  pl.BlockSpec((1, D), lambda i, *_: (0, 0)),
                  pl.BlockSpec((1, D), lambda i, *_: (0, 0))],
        out_specs=pl.BlockSpec((tm, D), lambda i, *_: (i, 0)),
        scratch_shapes=[pltpu.VMEM((2, MOE_STAGE, MOE_PACK_W), I32), pltpu.VMEM((tm, MOE_STAGE), BF16),
                        pltpu.SemaphoreType.DMA((2,))],
    )
    return pl.pallas_call(
        functools.partial(_combine_body, tm=tm),
        grid_spec=grid_spec,
        out_shape=jax.ShapeDtypeStruct((T, D), F32),
        compiler_params=_cparams("arbitrary"),
        name="moe_combine",
    )(seg_g, seg_l, seg_n, ys, slot, gate, h, g.reshape(1, D), b.reshape(1, D))


def _moe_block(h, layer, w_router, b_router, w_gu, b_gu, w_down, b_down, g, b):
    T = h.shape[0]
    bm, tm, E = MOE_BM, MOE_TM, N_EXPERTS
    nt = T // tm
    slot_t, gate_t, cnt, lbase, gbase = _router(h, w_router, b_router)
    cnt, lbase, gbase = (t.reshape(nt, E, LANES)[:, :, 0] for t in (cnt, lbase, gbase))
    total = gbase[-1] + cnt[-1]
    padded = (total + bm - 1) // bm * bm
    pend = jnp.cumsum(padded)
    pstart = pend - padded
    seg_g = (pstart[None, :] + gbase).reshape(-1).astype(I32)
    seg_l = lbase.reshape(-1)
    seg_n = cnt.reshape(-1)
    n_blocks = -(-(T * TOP_K + nt * E * (MOE_SEG - 1)) // bm) + E
    block_start = jnp.arange(n_blocks, dtype=I32) * bm
    block_e = jnp.minimum(jnp.sum((pend[None, :] <= block_start[:, None]).astype(I32), axis=1), E - 1)
    n_live = (pend[-1:] // bm).astype(I32)
    n_valid = jnp.clip((pstart + total)[block_e] - block_start, 0, bm).astype(I32)
    tail_blk = n_live[0] + jnp.arange(n_blocks - T * TOP_K // bm, dtype=I32)
    tail_blk = jnp.where(tail_blk < n_blocks, tail_blk * bm, -1)
    zero_rows = jnp.concatenate([pstart + total, padded - total, tail_blk]).astype(I32)
    xs = _dispatch(seg_g, seg_l, seg_n, zero_rows, slot_t, h, n_blocks * bm)
    owner = jnp.where(padded > 0, jnp.arange(E, dtype=I32), E)
    later = lax.cummin(owner[::-1])[::-1]
    next_e = jnp.concatenate([later[1:], jnp.full((1,), E, I32)])
    next_e = jnp.where(next_e < E, next_e, -1).astype(I32)
    ys = _experts(block_e, n_live, n_valid, next_e, xs, layer, w_gu, b_gu, w_down, b_down)
    return _combine(seg_g, seg_l, seg_n, ys, slot_t.T, gate_t.T, h, g, b)


def kernel(x, rel_bias, a_w_in, a_w_out, b_w_in, b_ln_g, b_ln_b, b_w_s, b_b_s, b_w_out,
           c_w_in, c_w_gk_up, c_b_gk, c_norm_g, c_w_out, ln_g, ln_b,
           moe_w_router, moe_b_router, moe_w_gate_up, moe_b_gate_up, moe_w_down, moe_b_down):
    B, S, D = x.shape
    T = B * S
    h = x.reshape(T, D)
    tables = _bias_tables(rel_bias)
    for i in range(DEPTH):
        j = i // N_MIXERS
        mixer = i % N_MIXERS
        if mixer == 0:
            q_fold = jnp.where(jnp.arange(3 * D) < D, LOG2E * A_DH ** -0.5, 1.0).astype(F32)
            qkv = _matmul(h, (a_w_in[j] * q_fold).astype(BF16), out_dtype=BF16)
            o = _moba_attention(qkv.reshape(B, S, 3 * D), rel_bias, tables)
            h = _matmul_res_ln(o.reshape(T, D), a_w_out[j], h, ln_g[i, 0], ln_b[i, 0])
        elif mixer == 1:
            z = _matmul(h, b_w_in[j].astype(BF16), act="gelu", out_dtype=BF16)
            h = _gmlp_gate_out(z, h, b_ln_g[j], b_ln_b[j], b_w_s[j], b_b_s[j], b_w_out[j],
                               ln_g[i, 0], ln_b[i, 0])
        else:
            pad = LANES - C_GATE_RANK
            w_in = jnp.pad(c_w_in[j], ((0, 0), (0, pad)))
            w_up = jnp.pad(c_w_gk_up[j], ((0, pad), (0, 0)))
            proj = _matmul(h, w_in.astype(BF16), tn=640, out_dtype=BF16)
            o = _gla_core(proj.reshape(B, S, C_IN_WIDTH + pad), w_up, c_b_gk[j], c_norm_g[j])
            h = _matmul_res_ln(o.reshape(T, D), c_w_out[j], h, ln_g[i, 0], ln_b[i, 0])
        h = _moe_block(h, i, moe_w_router[i], moe_b_router[i], moe_w_gate_up, moe_b_gate_up,
                       moe_w_down, moe_b_down, ln_g[i, 1], ln_b[i, 1])
    return h.reshape(B, S, D)
```

```python
import functools
import math

import numpy as np
import jax
import jax.numpy as jnp
from jax import lax
from jax.experimental import pallas as pl
from jax.experimental.pallas import tpu as pltpu

F32 = jnp.float32
BF16 = jnp.bfloat16
I32 = jnp.int32
HI = lax.Precision.HIGHEST

D_MODEL = 1024
DEPTH = 4
N_MIXERS = 3
ALPHA = (2.0 * DEPTH) ** 0.25
LN_EPS = 1e-5
LOG2E = math.log2(math.e)

A_HEADS = 8
A_DH = D_MODEL // A_HEADS
MOBA_BLOCK = 256
MOBA_TOPK = 3
REL_BUCKETS = 32
REL_MAX_DIST = 128

B_GROUPS = 8
B_WIDTH = 2 * D_MODEL
B_CHUNK = 128
B_GW = B_WIDTH // B_GROUPS

C_HEADS = 4
C_KEY_DIM = D_MODEL // 2
C_VAL_DIM = D_MODEL
C_DK = C_KEY_DIM // C_HEADS
C_DV = C_VAL_DIM // C_HEADS
C_GATE_RANK = 16
C_GATE_NORMALIZER = 16.0
C_IN_WIDTH = 2 * C_KEY_DIM + 2 * C_VAL_DIM + C_GATE_RANK

N_EXPERTS = 32
TOP_K = 4
SWIGLU_LIMIT = 7.0
SWIGLU_ALPHA = 1.702

LANES = 128
SUBLANES = 8
VMEM_LIMIT = 56 * 1024 * 1024

GLA_CHUNK = 128
GLA_SUB = 16
GLA_HEADS_PER_STEP = 4
GLA_SEQ_TILE = 1024
MOE_BM = 1024
MOE_SUB = 256
MOE_TM = 512
MOE_SEG = SUBLANES
MOE_SEG_BITS = (MOE_TM // MOE_SEG).bit_length()
MOE_STAGE = MOE_TM * TOP_K + N_EXPERTS * MOE_SEG
MOE_PERM_CHUNK = 256
MOE_PACK_W = D_MODEL // 2


def _cparams(*sem):
    return pltpu.CompilerParams(dimension_semantics=sem, vmem_limit_bytes=VMEM_LIMIT)


def _ln(x, g, b):
    mu = jnp.mean(x, axis=-1, keepdims=True)
    xc = x - mu
    var = jnp.mean(xc * xc, axis=-1, keepdims=True)
    return xc * lax.rsqrt(var + LN_EPS) * g + b


def _mm_body(x_ref, w_ref, o_ref, *, act, tn):
    x = x_ref[...].astype(BF16)
    for n0 in range(0, w_ref.shape[1], tn):
        acc = jnp.dot(x, w_ref[:, n0:n0 + tn], preferred_element_type=F32)
        if act == "gelu":
            acc = 0.5 * acc * (1.0 + lax.erf(acc * (2.0 ** -0.5)))
        o_ref[:, n0:n0 + tn] = acc.astype(o_ref.dtype)


def _matmul(x, w, *, act=None, tm=512, tn=512, out_dtype=F32):
    M, K = x.shape
    N = w.shape[1]
    assert M % tm == 0 and N % tn == 0
    return pl.pallas_call(
        functools.partial(_mm_body, act=act, tn=tn),
        grid=(M // tm,),
        in_specs=[pl.BlockSpec((tm, K), lambda i: (i, 0)),
                  pl.BlockSpec((K, N), lambda i: (0, 0))],
        out_specs=pl.BlockSpec((tm, N), lambda i: (i, 0)),
        out_shape=jax.ShapeDtypeStruct((M, N), out_dtype),
        compiler_params=_cparams("arbitrary"),
        name="proj_matmul",
    )(x, w)


def _mm_res_ln_body(x_ref, w_ref, h_ref, g_ref, b_ref, o_ref, wb_ref):
    @pl.when(pl.program_id(0) == 0)
    def _():
        wb_ref[...] = w_ref[...].astype(BF16)

    t = jnp.dot(x_ref[...].astype(BF16), wb_ref[...], preferred_element_type=F32)
    o_ref[...] = _ln(ALPHA * h_ref[...] + t, g_ref[...], b_ref[...])


def _matmul_res_ln(x, w, h, g, b, *, tm=512):
    M, K = x.shape
    N = w.shape[1]
    return pl.pallas_call(
        _mm_res_ln_body,
        grid=(M // tm,),
        in_specs=[pl.BlockSpec((tm, K), lambda i: (i, 0)),
                  pl.BlockSpec((K, N), lambda i: (0, 0)),
                  pl.BlockSpec((tm, N), lambda i: (i, 0)),
                  pl.BlockSpec((1, N), lambda i: (0, 0)),
                  pl.BlockSpec((1, N), lambda i: (0, 0))],
        out_specs=pl.BlockSpec((tm, N), lambda i: (i, 0)),
        out_shape=jax.ShapeDtypeStruct((M, N), F32),
        scratch_shapes=[pltpu.VMEM((K, N), BF16)],
        compiler_params=_cparams("arbitrary"),
        name="outproj_res_ln",
    )(x, w, h, g.reshape(1, N), b.reshape(1, N))


def _t5_bucket_lower_bounds():
    n = np.arange(0, 4 * REL_MAX_DIST, dtype=np.int64)
    max_exact = REL_BUCKETS // 2
    nf = np.maximum(n, 1).astype(np.float32)
    large = max_exact + (np.log(nf / np.float32(max_exact)) / np.float32(math.log(REL_MAX_DIST / max_exact))
                         * np.float32(REL_BUCKETS - max_exact)).astype(np.int32)
    large = np.minimum(large, REL_BUCKETS - 1)
    bucket = np.where(n < max_exact, n, large)
    assert np.all(np.diff(bucket) >= 0) and bucket[-1] == REL_BUCKETS - 1
    return [int(np.argmax(bucket >= b)) for b in range(REL_BUCKETS)]


_BUCKET_LO = _t5_bucket_lower_bounds()


def _bias_table_body(rel_ref, o_ref):
    blk = MOBA_BLOCK
    row = lax.broadcasted_iota(I32, (blk, blk), 0)
    col = lax.broadcasted_iota(I32, (blk, blk), 1)
    for t in range(2):
        d = row - col + t * blk
        for h in range(A_HEADS):
            val = jnp.full((blk, blk), rel_ref[REL_BUCKETS - 1, h], F32)
            for b in range(REL_BUCKETS - 2, -1, -1):
                val = jnp.where(d < _BUCKET_LO[b + 1], rel_ref[b, h], val)
            val = val * LOG2E
            if t == 0:
                val = jnp.where(d >= 0, val, -jnp.inf)
            o_ref[h, t] = val


def _bias_tables(rel_bias):
    return pl.pallas_call(
        _bias_table_body,
        in_specs=[pl.BlockSpec(memory_space=pltpu.SMEM)],
        out_shape=jax.ShapeDtypeStruct((A_HEADS, 2, MOBA_BLOCK, MOBA_BLOCK), F32),
        compiler_params=pltpu.CompilerParams(vmem_limit_bytes=VMEM_LIMIT),
        name="rel_bias_tables",
    )(rel_bias)


def _moba_body(rel_ref, q_ref, k_ref, v_ref, tb_ref, o_ref, s_s, *, seq):
    blk, dh = MOBA_BLOCK, A_DH
    nb = seq // blk
    h = pl.program_id(1)
    kmean = jnp.mean(k_ref[0].astype(F32).reshape(nb, blk, dh), axis=1)
    kmean_pad = jnp.concatenate([kmean, jnp.zeros((LANES - nb, dh), F32)], axis=0)
    b_far = rel_ref[REL_BUCKETS - 1, h] * LOG2E
    lane = lax.broadcasted_iota(I32, (blk, LANES), 1)
    ones_bf = jnp.ones((LANES, LANES), BF16)

    for n in range(nb):
        r0 = n * blk
        s_s[r0:seq, r0:r0 + blk] = lax.dot_general(q_ref[0, r0:seq, :], k_ref[0, r0:r0 + blk, :],
                                                   (((1,), (1,)), ((), ())), preferred_element_type=F32)

    for c in range(nb):
        q = q_ref[0, c * blk:(c + 1) * blk, :]
        n_keys = (c + 1) * blk
        s = s_s[c * blk:(c + 1) * blk, 0:n_keys]
        if c > MOBA_TOPK:
            gate = lax.dot_general(q.astype(F32), kmean_pad, (((1,), (1,)), ((), ())),
                                   precision=HI, preferred_element_type=F32)
            gate = jnp.where(lane < c, gate, -jnp.inf)
        pieces = []
        for n in range(c):
            shift = b_far
            if c > MOBA_TOPK:
                gn = gate[:, n:n + 1]
                beats = (gate > gn) | ((gate == gn) & (lane < n))
                rank = jnp.dot(jnp.where(beats, 1.0, 0.0).astype(BF16), ones_bf, preferred_element_type=F32)
                shift = jnp.where(rank < MOBA_TOPK, b_far, -jnp.inf)
                shift = jnp.concatenate([shift] * (blk // LANES), axis=1)
            if n == c - 1:
                shift = tb_ref[0, 1] + (shift - b_far)
            pieces.append(s[:, n * blk:(n + 1) * blk] + shift)
        pieces.append(s[:, c * blk:(c + 1) * blk] + tb_ref[0, 0])
        logits = jnp.concatenate(pieces, axis=1) if c > 0 else pieces[0]
        m = jnp.max(logits, axis=1, keepdims=True)
        p = jnp.exp2(logits - m)
        denom = jnp.sum(p, axis=1, keepdims=True)
        o = jnp.dot(p.astype(BF16), v_ref[0, 0:n_keys, :], preferred_element_type=F32)
        o_ref[0, c * blk:(c + 1) * blk, :] = o / denom


def _moba_attention(qkv, rel_bias, tables):
    B, S, _ = qkv.shape
    H, dh = A_HEADS, A_DH
    assert S % MOBA_BLOCK == 0 and S // MOBA_BLOCK <= SUBLANES
    return pl.pallas_call(
        functools.partial(_moba_body, seq=S),
        grid=(B, H),
        in_specs=[pl.BlockSpec(memory_space=pltpu.SMEM),
                  pl.BlockSpec((1, S, dh), lambda b, h: (b, 0, h)),
                  pl.BlockSpec((1, S, dh), lambda b, h: (b, 0, H + h)),
                  pl.BlockSpec((1, S, dh), lambda b, h: (b, 0, 2 * H + h)),
                  pl.BlockSpec((1, 2, MOBA_BLOCK, MOBA_BLOCK), lambda b, h: (h, 0, 0, 0))],
        out_specs=pl.BlockSpec((1, S, dh), lambda b, h: (b, 0, h)),
        out_shape=jax.ShapeDtypeStruct((B, S, H * dh), F32),
        scratch_shapes=[pltpu.VMEM((S, S), F32)],
        compiler_params=_cparams("arbitrary", "arbitrary"),
        name="moba_attention",
    )(rel_bias, qkv, qkv, qkv, tables)


def _gmlp_body(u_ref, v_ref, h_ref, vg_ref, vb_ref, ws_ref, bs_ref, wo_ref, g_ref, b_ref, o_ref,
               wst_s, wob_s, *, tm):
    @pl.when(pl.program_id(0) == 0)
    def _():
        row = lax.broadcasted_iota(I32, (B_CHUNK, B_CHUNK), 0)
        col = lax.broadcasted_iota(I32, (B_CHUNK, B_CHUNK), 1)
        for g in range(B_GROUPS):
            wst_s[g] = jnp.where(row >= col, ws_ref[g], 0.0).astype(BF16)
        wob_s[...] = wo_ref[...].astype(BF16)

    vn = _ln(v_ref[...].astype(F32), vg_ref[...], vb_ref[...]).astype(BF16)
    u = u_ref[...].astype(F32)
    rows = []
    for c in range(tm // B_CHUNK):
        r0 = c * B_CHUNK
        cols = []
        for g in range(B_GROUPS):
            c0 = g * B_GW
            mixed = jnp.dot(wst_s[g], vn[r0:r0 + B_CHUNK, c0:c0 + B_GW],
                            preferred_element_type=F32) + bs_ref[g]
            cols.append(u[r0:r0 + B_CHUNK, c0:c0 + B_GW] * mixed)
        rows.append(jnp.concatenate(cols, axis=1))
    y = jnp.concatenate(rows, axis=0).astype(BF16)
    t = jnp.dot(y, wob_s[...], preferred_element_type=F32)
    o_ref[...] = _ln(ALPHA * h_ref[...] + t, g_ref[...], b_ref[...])


def _gmlp_gate_out(z, h, v_g, v_b, w_s, b_s, w_out, g, b, *, tm=256):
    T = z.shape[0]
    D = D_MODEL
    return pl.pallas_call(
        functools.partial(_gmlp_body, tm=tm),
        grid=(T // tm,),
        in_specs=[pl.BlockSpec((tm, B_WIDTH), lambda i: (i, 0)),
                  pl.BlockSpec((tm, B_WIDTH), lambda i: (i, 1)),
                  pl.BlockSpec((tm, D), lambda i: (i, 0)),
                  pl.BlockSpec((1, B_WIDTH), lambda i: (0, 0)),
                  pl.BlockSpec((1, B_WIDTH), lambda i: (0, 0)),
                  pl.BlockSpec((B_GROUPS, B_CHUNK, B_CHUNK), lambda i: (0, 0, 0)),
                  pl.BlockSpec((B_GROUPS, B_CHUNK, 1), lambda i: (0, 0, 0)),
                  pl.BlockSpec((B_WIDTH, D), lambda i: (0, 0)),
                  pl.BlockSpec((1, D), lambda i: (0, 0)),
                  pl.BlockSpec((1, D), lambda i: (0, 0))],
        out_specs=pl.BlockSpec((tm, D), lambda i: (i, 0)),
        out_shape=jax.ShapeDtypeStruct((T, D), F32),
        scratch_shapes=[pltpu.VMEM((B_GROUPS, B_CHUNK, B_CHUNK), BF16), pltpu.VMEM((B_WIDTH, D), BF16)],
        compiler_params=_cparams("arbitrary"),
        name="gmlp_gate_out",
    )(z, z, h, v_g.reshape(1, B_WIDTH), v_b.reshape(1, B_WIDTH), w_s, b_s.reshape(B_GROUPS, B_CHUNK, 1),
      w_out, g.reshape(1, D), b.reshape(1, D))


def _gla_body(q_ref, k_ref, v_ref, g_ref, gl_ref, wup_ref, bgk_ref, ng_ref, o_ref, la_s, st_s, raw_s, *, seq):
    ch, sub, dk, dv = GLA_CHUNK, GLA_SUB, C_DK, C_DV
    x = jnp.dot(gl_ref[0].astype(F32), wup_ref[...], precision=HI, preferred_element_type=F32) + bgk_ref[...]
    la_s[...] = (jnp.minimum(x, 0.0) - jnp.log1p(jnp.exp(-jnp.abs(x)))) * (1.0 / C_GATE_NORMALIZER)

    @pl.when(pl.program_id(2) == 0)
    def _():
        st_s[...] = jnp.zeros(st_s.shape, F32)

    row_c = lax.broadcasted_iota(I32, (ch, ch), 0)
    col_c = lax.broadcasted_iota(I32, (ch, ch), 1)
    tri = (row_c >= col_c).astype(F32)
    eye = (row_c == col_c).astype(F32)
    sub_i = lax.broadcasted_iota(I32, (sub, ch), 0)
    lane_j = lax.broadcasted_iota(I32, (sub, ch), 1)

    def chunk(c, carry):
        for hh in range(GLA_HEADS_PER_STEP):
            chunk_head(c, hh)
        return carry

    def chunk_head(c, hh):
        s0 = pl.multiple_of(c * ch, ch)
        a = la_s[pl.ds(s0, ch), hh * dk:(hh + 1) * dk]
        b = jnp.dot(tri, a, precision=HI, preferred_element_type=F32)
        q = q_ref[0, pl.ds(s0, ch), hh * dk:(hh + 1) * dk].astype(F32) * (dk ** -0.5)
        k = k_ref[0, pl.ds(s0, ch), hh * dk:(hh + 1) * dk].astype(F32)
        v = v_ref[0, pl.ds(s0, ch), hh * dv:(hh + 1) * dv].astype(BF16)
        st = st_s[hh]
        o = jnp.dot((q * jnp.exp(b)).astype(BF16), st.astype(BF16), preferred_element_type=F32)

        blocks = []
        for sb in range(ch // sub):
            r0 = sb * sub
            q_i = q[r0:r0 + sub]
            b_i = b[r0:r0 + sub]
            k_i = k[r0:r0 + sub]
            att = jnp.zeros((sub, ch), F32)
            for j in range(sub):
                e = jnp.exp(jnp.minimum(b_i - b_i[j:j + 1, :], 0.0))
                colv = jnp.sum(q_i * k_i[j:j + 1, :] * e, axis=1, keepdims=True)
                att = jnp.where((lane_j == r0 + j) & (sub_i >= j), colv, att)
            if sb > 0:
                ref_b = b[r0 - 1:r0, :]
                q_t = q_i * jnp.exp(b_i - ref_b)
                k_t = k * jnp.exp(jnp.minimum(ref_b - b, 0.0))
                off = lax.dot_general(q_t.astype(BF16), k_t.astype(BF16), (((1,), (1,)), ((), ())),
                                      preferred_element_type=F32)
                att = jnp.where(lane_j < r0, off, att)
            blocks.append(att)
        att_full = jnp.concatenate(blocks, axis=0)
        o = o + jnp.dot(att_full.astype(BF16), v, preferred_element_type=F32)
        raw_s[pl.ds(s0, ch), hh * dv:(hh + 1) * dv] = o

        b_last = b[ch - 1:ch, :]
        k_d = k * jnp.exp(b_last - b)
        upd = jnp.dot(k_d.T.astype(BF16), v, preferred_element_type=F32)
        decay_col = lax.dot_general(eye, jnp.broadcast_to(jnp.exp(b_last), (SUBLANES, dk)),
                                    (((1,), (1,)), ((), ())), precision=HI,
                                    preferred_element_type=F32)[:, 0:1]
        st_s[hh] = decay_col * st + upd

    lax.fori_loop(0, seq // ch, chunk, 0)

    for hh in range(GLA_HEADS_PER_STEP):
        o = raw_s[:, hh * dv:(hh + 1) * dv]
        rms = o * lax.rsqrt(jnp.mean(o * o, axis=-1, keepdims=True) + LN_EPS) * ng_ref[...]
        gg = g_ref[0, :, hh * dv:(hh + 1) * dv].astype(F32)
        o_ref[0, :, hh * dv:(hh + 1) * dv] = rms * (gg * jax.nn.sigmoid(gg))


def _gla_core(proj, w_up_pad, b_gk, norm_g):
    B, S, _ = proj.shape
    H, hps = C_HEADS, GLA_HEADS_PER_STEP
    dk, dv = C_DK * hps, C_DV * hps
    st = GLA_SEQ_TILE
    assert C_DK == LANES and S % st == 0 and st % GLA_CHUNK == 0 and H % hps == 0
    k_off = C_KEY_DIM // dk
    v_off = 2 * C_KEY_DIM // dv
    g_off = (2 * C_KEY_DIM + C_VAL_DIM) // dv
    gl_off = (2 * C_KEY_DIM + 2 * C_VAL_DIM) // LANES
    return pl.pallas_call(
        functools.partial(_gla_body, seq=st),
        grid=(B, H // hps, S // st),
        in_specs=[pl.BlockSpec((1, st, dk), lambda b, h, s: (b, s, h)),
                  pl.BlockSpec((1, st, dk), lambda b, h, s: (b, s, k_off + h)),
                  pl.BlockSpec((1, st, dv), lambda b, h, s: (b, s, v_off + h)),
                  pl.BlockSpec((1, st, dv), lambda b, h, s: (b, s, g_off + h)),
                  pl.BlockSpec((1, st, LANES), lambda b, h, s: (b, s, gl_off)),
                  pl.BlockSpec((LANES, dk), lambda b, h, s: (0, h)),
                  pl.BlockSpec((1, dk), lambda b, h, s: (0, h)),
                  pl.BlockSpec((1, C_DV), lambda b, h, s: (0, 0))],
        out_specs=pl.BlockSpec((1, st, dv), lambda b, h, s: (b, s, h)),
        out_shape=jax.ShapeDtypeStruct((B, S, C_VAL_DIM), F32),
        scratch_shapes=[pltpu.VMEM((st, dk), F32), pltpu.VMEM((hps, C_DK, C_DV), F32), pltpu.VMEM((st, dv), F32)],
        compiler_params=_cparams("arbitrary", "arbitrary", "arbitrary"),
        name="gla_core",
    )(proj, proj, proj, proj, proj, w_up_pad, b_gk.reshape(1, C_KEY_DIM), norm_g.reshape(1, C_DV))


def _router_body(h_ref, wt_ref, b_ref, slot_ref, gate_ref, cnt_ref, lbase_ref, gbase_ref, run_s, ws_s, *, tm):
    E = N_EXPERTS

    @pl.when(pl.program_id(0) == 0)
    def _():
        run_s[...] = jnp.zeros(run_s.shape, F32)
        w = wt_ref[...]
        w1 = w.astype(BF16)
        ws_s[0:E, :] = w1
        ws_s[E:2 * E, :] = (w - w1.astype(F32)).astype(BF16)

    h = h_ref[...]
    h1 = h.astype(BF16)
    h2 = (h - h1.astype(F32)).astype(BF16)
    nt_dims = (((1,), (1,)), ((), ()))
    both = lax.dot_general(ws_s[...], h1, nt_dims, preferred_element_type=F32)
    logits = (both[0:E] + (both[E:2 * E] + lax.dot_general(ws_s[0:E, :], h2, nt_dims, preferred_element_type=F32))
              + b_ref[...])
    row = lax.broadcasted_iota(I32, (E, tm), 0)
    rest = logits
    top_val, onehot = [], []
    for _ in range(TOP_K):
        m = jnp.max(rest, axis=0, keepdims=True)
        idx = jnp.min(jnp.where(rest == m, row, E), axis=0, keepdims=True)
        oh = row == idx
        rest = jnp.where(oh, -jnp.inf, rest)
        top_val.append(m)
        onehot.append(oh)
    ex = [jnp.exp(v - top_val[0]) for v in top_val]
    den = ex[0] + ex[1] + ex[2] + ex[3]
    chosen = sum(oh.astype(F32) for oh in onehot)

    tr = lax.broadcasted_iota(I32, (tm, tm), 0)
    tc = lax.broadcasted_iota(I32, (tm, tm), 1)
    earlier = (tr < tc).astype(BF16)
    seen = jnp.dot(chosen.astype(BF16), earlier, preferred_element_type=F32)
    cnt = jnp.sum(chosen, axis=1, keepdims=True)
    cnt_pad = jnp.floor((cnt + (MOE_SEG - 1)) * (1.0 / MOE_SEG)) * MOE_SEG
    cnt_cols = jnp.broadcast_to(cnt_pad, (E, LANES))
    er = lax.broadcasted_iota(I32, (E, E), 0)
    ec = lax.broadcasted_iota(I32, (E, E), 1)
    lbase_cols = jnp.dot((ec < er).astype(F32), cnt_cols, precision=HI, preferred_element_type=F32)
    pos = seen + lbase_cols[:, 0:1]

    row_k = lax.broadcasted_iota(I32, (TOP_K, tm), 0)
    s_out = jnp.zeros((TOP_K, tm), I32)
    g_out = jnp.zeros((TOP_K, tm), F32)
    for kk in range(TOP_K):
        sk = jnp.sum(jnp.where(onehot[kk], pos, 0.0), axis=0, keepdims=True)
        s_out = jnp.where(row_k == kk, sk.astype(I32), s_out)
        g_out = jnp.where(row_k == kk, ex[kk] / den, g_out)
    slot_ref[...] = s_out
    gate_ref[...] = g_out
    cnt_ref[...] = cnt_cols.astype(I32)
    lbase_ref[...] = lbase_cols.astype(I32)
    gbase_ref[...] = run_s[...].astype(I32)
    run_s[...] = run_s[...] + cnt_cols


def _router(h, w, b, *, tm=MOE_TM):
    T, D = h.shape
    E = N_EXPERTS
    nt = T // tm
    tab = jax.ShapeDtypeStruct((nt * E, LANES), I32)
    tab_spec = pl.BlockSpec((E, LANES), lambda i: (i, 0))
    return pl.pallas_call(
        functools.partial(_router_body, tm=tm),
        grid=(nt,),
        in_specs=[pl.BlockSpec((tm, D), lambda i: (i, 0)),
                  pl.BlockSpec((E, D), lambda i: (0, 0)),
                  pl.BlockSpec((E, 1), lambda i: (0, 0))],
        out_specs=[pl.BlockSpec((TOP_K, tm), lambda i: (0, i)),
                   pl.BlockSpec((TOP_K, tm), lambda i: (0, i)),
                   tab_spec, tab_spec, tab_spec],
        out_shape=[jax.ShapeDtypeStruct((TOP_K, T), I32),
                   jax.ShapeDtypeStruct((TOP_K, T), F32),
                   tab, tab, tab],
        scratch_shapes=[pltpu.VMEM((E, LANES), F32), pltpu.VMEM((2 * E, D), BF16)],
        compiler_params=_cparams("arbitrary"),
        name="moe_router",
    )(h, w.T, b.reshape(E, 1))


def _pack_pairs(left, right):
    hi = lax.bitcast_convert_type(left, I32) & jnp.int32(-65536)
    lo = lax.shift_right_logical(lax.bitcast_convert_type(right, I32), 16)
    return hi | lo


def _unpack_pairs(word):
    left = lax.bitcast_convert_type(word & jnp.int32(-65536), F32)
    right = lax.bitcast_convert_type(lax.shift_left(word, 16), F32)
    return left.astype(BF16), right.astype(BF16)


def _segment_starts(seg_g, seg_l, seg_n, tile, make_copy):
    def per_expert(e, carry):
        idx = tile * N_EXPERTS + e
        n = seg_n[idx]
        g = seg_g[idx]
        l = seg_l[idx]
        for bit in range(MOE_SEG_BITS - 1, -1, -1):
            size = MOE_SEG << bit

            @pl.when((n & size) != 0)
            def _():
                off = n & ~(2 * size - 1)
                make_copy(pl.multiple_of(g + off, MOE_SEG), pl.multiple_of(l + off, MOE_SEG), size).start()
        return carry

    lax.fori_loop(0, N_EXPERTS, per_expert, 0)


def _segment_waits(seg_l, seg_n, tile, make_copy):
    last = tile * N_EXPERTS + N_EXPERTS - 1
    total = seg_l[last] + seg_n[last]
    for bit in range((MOE_STAGE // MOE_SEG).bit_length() - 1, -1, -1):
        size = MOE_SEG << bit

        @pl.when((total & size) != 0)
        def _():
            make_copy(0, 0, size).wait()


def _dispatch_body(segg_ref, segl_ref, segn_ref, zf_ref, slot_t_ref, h_ref, xs_ref, stage, zbuf, sem, *, tm, bm):
    i = pl.program_id(0)

    def zero_fill(start):
        def act(cp):
            if start:
                cp.start()
            else:
                cp.wait()

        def per_expert(e, carry):
            g = zf_ref[e]
            n = zf_ref[N_EXPERTS + e]
            for bit in range((bm // MOE_SEG).bit_length() - 2, -1, -1):
                size = MOE_SEG << bit

                @pl.when((n & size) != 0)
                def _():
                    off = n & ~(2 * size - 1)
                    act(pltpu.make_async_copy(
                        zbuf.at[pl.ds(0, size), :],
                        xs_ref.at[pl.ds(pl.multiple_of(g + off, MOE_SEG), size), :], sem.at[2]))
            return carry

        def per_tail(j, carry):
            row = zf_ref[2 * N_EXPERTS + j]

            @pl.when(row >= 0)
            def _():
                act(pltpu.make_async_copy(zbuf, xs_ref.at[pl.ds(pl.multiple_of(row, MOE_SEG), bm), :], sem.at[2]))
            return carry

        lax.fori_loop(0, N_EXPERTS, per_expert, 0)
        lax.fori_loop(0, zf_ref.shape[0] - 2 * N_EXPERTS, per_tail, 0)

    @pl.when(i == 0)
    def _():
        zbuf[...] = jnp.zeros(zbuf.shape, I32)
        zero_fill(True)

    @pl.when(i == pl.num_programs(0) - 1)
    def _():
        zero_fill(False)

    slot_i = i % 2
    hb = h_ref[...].astype(BF16)
    for c in range(MOE_STAGE // MOE_PERM_CHUNK):
        r0 = c * MOE_PERM_CHUNK
        rows = lax.broadcasted_iota(I32, (MOE_PERM_CHUNK, tm), 0) + r0
        hit = rows == slot_t_ref[0:1, :]
        for kk in range(1, TOP_K):
            hit = hit | (rows == slot_t_ref[kk:kk + 1, :])
        perm = jnp.where(hit, 1.0, 0.0).astype(BF16)
        rows_f = jnp.dot(perm, hb, preferred_element_type=F32)
        stage[slot_i, r0:r0 + MOE_PERM_CHUNK, :] = _pack_pairs(rows_f[:, :MOE_PACK_W], rows_f[:, MOE_PACK_W:])

    def copier(buf):
        def make_copy(g, l, size):
            return pltpu.make_async_copy(stage.at[buf, pl.ds(l, size), :], xs_ref.at[pl.ds(g, size), :],
                                         sem.at[buf])
        return make_copy

    _segment_starts(segg_ref, segl_ref, segn_ref, i, copier(slot_i))

    @pl.when(i > 0)
    def _():
        _segment_waits(segl_ref, segn_ref, i - 1, copier(1 - slot_i))

    @pl.when(i == pl.num_programs(0) - 1)
    def _():
        _segment_waits(segl_ref, segn_ref, i, copier(slot_i))


def _dispatch(seg_g, seg_l, seg_n, zero_rows, slot_t, h, n_rows, *, tm=MOE_TM, bm=MOE_BM):
    T, D = h.shape
    grid_spec = pltpu.PrefetchScalarGridSpec(
        num_scalar_prefetch=4,
        grid=(T // tm,),
        in_specs=[pl.BlockSpec((TOP_K, tm), lambda i, *_: (0, i)),
                  pl.BlockSpec((tm, D), lambda i, *_: (i, 0))],
        out_specs=pl.BlockSpec(memory_space=pl.ANY),
        scratch_shapes=[pltpu.VMEM((2, MOE_STAGE, MOE_PACK_W), I32), pltpu.VMEM((bm, MOE_PACK_W), I32),
                        pltpu.SemaphoreType.DMA((3,))],
    )
    return pl.pallas_call(
        functools.partial(_dispatch_body, tm=tm, bm=bm),
        grid_spec=grid_spec,
        out_shape=jax.ShapeDtypeStruct((n_rows, MOE_PACK_W), I32),
        compiler_params=pltpu.CompilerParams(dimension_semantics=("arbitrary",), vmem_limit_bytes=VMEM_LIMIT,
                                             has_side_effects=True),
        name="moe_dispatch",
    )(seg_g, seg_l, seg_n, zero_rows, slot_t, h)


def _expert_body(be_ref, nb_ref, nv_ref, nx_ref, xs_ref, wgu_hbm, bgu_ref, wd_hbm, bd_ref, ys_ref,
                 wgu_f, wd_f, wgu_s, wd_s, sem, *, bm, layer):
    i = pl.program_id(0)
    e = be_ref[i]
    prev = be_ref[jnp.maximum(i - 1, 0)]
    live = i < nb_ref[0]
    n_valid = jnp.where(live, nv_ref[i], 0)

    def weight_copies(expert):
        return (pltpu.make_async_copy(wgu_hbm.at[layer, expert], wgu_f, sem.at[0]),
                pltpu.make_async_copy(wd_hbm.at[layer, expert], wd_f, sem.at[1]))

    @pl.when(i == 0)
    def _():
        for cp in weight_copies(e):
            cp.start()

    @pl.when(live & ((i == 0) | (e != prev)))
    def _():
        for cp in weight_copies(e):
            cp.wait()
        wgu_s[...] = wgu_f[...].astype(BF16)
        wd_s[...] = wd_f[...].astype(BF16)
        nxt = nx_ref[e]

        @pl.when(nxt >= 0)
        def _():
            for cp in weight_copies(nxt):
                cp.start()

    def mlp(r0, rows):
        x = jnp.concatenate(_unpack_pairs(xs_ref[r0:r0 + rows, :]), axis=1)
        hh = jnp.dot(x, wgu_s[...], preferred_element_type=F32) + bgu_ref[...]
        d_e = hh.shape[1] // 2
        h_gate = jnp.minimum(hh[:, :d_e], SWIGLU_LIMIT)
        h_up = jnp.clip(hh[:, d_e:], -SWIGLU_LIMIT, SWIGLU_LIMIT)
        act = h_gate * jax.nn.sigmoid(SWIGLU_ALPHA * h_gate) * (h_up + 1.0)
        y = jnp.dot(act.astype(BF16), wd_s[...], preferred_element_type=F32) + bd_ref[...]
        y = y.astype(BF16).astype(F32)
        ys_ref[r0:r0 + rows, :] = _pack_pairs(y[:, :MOE_PACK_W], y[:, MOE_PACK_W:])

    full = n_valid > bm - MOE_SUB

    @pl.when(full)
    def _():
        mlp(0, bm)

    for r0 in range(0, bm, MOE_SUB):
        @pl.when(jnp.logical_not(full) & (n_valid > r0))
        def _(r0=r0):
            mlp(r0, MOE_SUB)

        @pl.when(n_valid <= r0)
        def _(r0=r0):
            ys_ref[r0:r0 + MOE_SUB, :] = jnp.zeros((MOE_SUB, ys_ref.shape[1]), I32)


def _experts(block_e, n_live, n_valid, next_e, xs, layer, w_gu, b_gu, w_down, b_down, *, bm=MOE_BM):
    L, E, D, D2 = w_gu.shape
    n_blocks = xs.shape[0] // bm

    def blk(i, nb):
        return jnp.maximum(jnp.minimum(i, nb[0] - 1), 0)

    grid_spec = pltpu.PrefetchScalarGridSpec(
        num_scalar_prefetch=4,
        grid=(n_blocks,),
        in_specs=[pl.BlockSpec((bm, MOE_PACK_W), lambda i, be, nb, nv, nx: (blk(i, nb), 0)),
                  pl.BlockSpec(memory_space=pl.ANY),
                  pl.BlockSpec((None, None, 1, D2), lambda i, be, nb, nv, nx: (layer, be[blk(i, nb)], 0, 0)),
                  pl.BlockSpec(memory_space=pl.ANY),
                  pl.BlockSpec((None, None, 1, D), lambda i, be, nb, nv, nx: (layer, be[blk(i, nb)], 0, 0))],
        out_specs=pl.BlockSpec((bm, MOE_PACK_W), lambda i, be, nb, nv, nx: (i, 0)),
        scratch_shapes=[pltpu.VMEM((D, D2), F32), pltpu.VMEM((D2 // 2, D), F32),
                        pltpu.VMEM((D, D2), BF16), pltpu.VMEM((D2 // 2, D), BF16),
                        pltpu.SemaphoreType.DMA((2,))],
    )
    return pl.pallas_call(
        functools.partial(_expert_body, bm=bm, layer=layer),
        grid_spec=grid_spec,
        out_shape=jax.ShapeDtypeStruct(xs.shape, I32),
        compiler_params=_cparams("arbitrary"),
        name="moe_experts",
    )(block_e, n_live, n_valid, next_e, xs, w_gu, b_gu.reshape(L, E, 1, D2), w_down, b_down.reshape(L, E, 1, D))


def _combine_body(segg_ref, segl_ref, segn_ref, ys_ref, slot_ref, gate_ref, h_ref, g_ref, b_ref, o_ref,
                  stage, mix_s, sem, *, tm):
    i = pl.program_id(0)

    slot_i = i % 2

    def copier(buf):
        def make_copy(g, l, size):
            return pltpu.make_async_copy(ys_ref.at[pl.ds(g, size), :], stage.at[buf, pl.ds(l, size), :],
                                         sem.at[buf])
        return make_copy

    @pl.when(i == 0)
    def _():
        stage[...] = jnp.zeros(stage.shape, I32)
        _segment_starts(segg_ref, segl_ref, segn_ref, i, copier(slot_i))

    @pl.when(i + 1 < pl.num_programs(0))
    def _():
        _segment_starts(segg_ref, segl_ref, segn_ref, i + 1, copier(1 - slot_i))

    slot = slot_ref[...]
    gate = gate_ref[...]
    for c in range(MOE_STAGE // MOE_PERM_CHUNK):
        c0 = c * MOE_PERM_CHUNK
        cols = lax.broadcasted_iota(I32, (tm, MOE_PERM_CHUNK), 1) + c0
        w = jnp.zeros((tm, MOE_PERM_CHUNK), F32)
        for kk in range(TOP_K):
            w = jnp.where(cols == slot[:, kk:kk + 1], gate[:, kk:kk + 1], w)
        mix_s[:, c0:c0 + MOE_PERM_CHUNK] = w.astype(BF16)

    _segment_waits(segl_ref, segn_ref, i, copier(slot_i))
    left, right = _unpack_pairs(stage[slot_i])
    rows = tm // 2
    for r0 in range(0, tm, rows):
        mix = mix_s[r0:r0 + rows, :]
        y = jnp.concatenate([jnp.dot(mix, left, preferred_element_type=F32),
                             jnp.dot(mix, right, preferred_element_type=F32)], axis=1)
        o_ref[r0:r0 + rows, :] = _ln(ALPHA * h_ref[r0:r0 + rows, :] + y, g_ref[...], b_ref[...])


def _combine(seg_g, seg_l, seg_n, ys, slot, gate, h, g, b, *, tm=MOE_TM):
    T, D = h.shape
    grid_spec = pltpu.PrefetchScalarGridSpec(
        num_scalar_prefetch=3,
        grid=(T // tm,),
        in_specs=[pl.BlockSpec(memory_space=pl.ANY),
                  pl.BlockSpec((tm, TOP_K), lambda i, *_: (i, 0)),
                  pl.BlockSpec((tm, TOP_K), lambda i, *_: (i, 0)),
                  pl.BlockSpec((tm, D), lambda i, *_: (i, 0)),
                  pl.BlockSpec((1, D), lambda i, *_: (0, 0)),
                  pl.BlockSpec((1, D), lambda i, *_: (0, 0))],
        out_specs=pl.BlockSpec((tm, D), lambda i, *_: (i, 0)),
        scratch_shapes=[pltpu.VMEM((2, MOE_STAGE, MOE_PACK_W), I32), pltpu.VMEM((tm, MOE_STAGE), BF16),
                        pltpu.SemaphoreType.DMA((2,))],
    )
    return pl.pallas_call(
        functools.partial(_combine_body, tm=tm),
        grid_spec=grid_spec,
        out_shape=jax.ShapeDtypeStruct((T, D), F32),
        compiler_params=_cparams("arbitrary"),
        name="moe_combine",
    )(seg_g, seg_l, seg_n, ys, slot, gate, h, g.reshape(1, D), b.reshape(1, D))


def _moe_block(h, layer, w_router, b_router, w_gu, b_gu, w_down, b_down, g, b):
    T = h.shape[0]
    bm, tm, E = MOE_BM, MOE_TM, N_EXPERTS
    nt = T // tm
    slot_t, gate_t, cnt, lbase, gbase = _router(h, w_router, b_router)
    cnt, lbase, gbase = (t.reshape(nt, E, LANES)[:, :, 0] for t in (cnt, lbase, gbase))
    total = gbase[-1] + cnt[-1]
    padded = (total + bm - 1) // bm * bm
    pend = jnp.cumsum(padded)
    pstart = pend - padded
    seg_g = (pstart[None, :] + gbase).reshape(-1).astype(I32)
    seg_l = lbase.reshape(-1)
    seg_n = cnt.reshape(-1)
    n_blocks = -(-(T * TOP_K + nt * E * (MOE_SEG - 1)) // bm) + E
    block_start = jnp.arange(n_blocks, dtype=I32) * bm
    block_e = jnp.minimum(jnp.sum((pend[None, :] <= block_start[:, None]).astype(I32), axis=1), E - 1)
    n_live = (pend[-1:] // bm).astype(I32)
    n_valid = jnp.clip((pstart + total)[block_e] - block_start, 0, bm).astype(I32)
    tail_blk = n_live[0] + jnp.arange(n_blocks - T * TOP_K // bm, dtype=I32)
    tail_blk = jnp.where(tail_blk < n_blocks, tail_blk * bm, -1)
    zero_rows = jnp.concatenate([pstart + total, padded - total, tail_blk]).astype(I32)
    xs = _dispatch(seg_g, seg_l, seg_n, zero_rows, slot_t, h, n_blocks * bm)
    owner = jnp.where(padded > 0, jnp.arange(E, dtype=I32), E)
    later = lax.cummin(owner[::-1])[::-1]
    next_e = jnp.concatenate([later[1:], jnp.full((1,), E, I32)])
    next_e = jnp.where(next_e < E, next_e, -1).astype(I32)
    ys = _experts(block_e, n_live, n_valid, next_e, xs, layer, w_gu, b_gu, w_down, b_down)
    return _combine(seg_g, seg_l, seg_n, ys, slot_t.T, gate_t.T, h, g, b)


def kernel(x, rel_bias, a_w_in, a_w_out, b_w_in, b_ln_g, b_ln_b, b_w_s, b_b_s, b_w_out,
           c_w_in, c_w_gk_up, c_b_gk, c_norm_g, c_w_out, ln_g, ln_b,
           moe_w_router, moe_b_router, moe_w_gate_up, moe_b_gate_up, moe_w_down, moe_b_down):
    B, S, D = x.shape
    T = B * S
    h = x.reshape(T, D)
    tables = _bias_tables(rel_bias)
    for i in range(DEPTH):
        j = i // N_MIXERS
        mixer = i % N_MIXERS
        if mixer == 0:
            q_fold = jnp.where(jnp.arange(3 * D) < D, LOG2E * A_DH ** -0.5, 1.0).astype(F32)
            qkv = _matmul(h, (a_w_in[j] * q_fold).astype(BF16), out_dtype=BF16)
            o = _moba_attention(qkv.reshape(B, S, 3 * D), rel_bias, tables)
            h = _matmul_res_ln(o.reshape(T, D), a_w_out[j], h, ln_g[i, 0], ln_b[i, 0])
        elif mixer == 1:
            z = _matmul(h, b_w_in[j].astype(BF16), act="gelu", out_dtype=BF16)
            h = _gmlp_gate_out(z, h, b_ln_g[j], b_ln_b[j], b_w_s[j], b_b_s[j], b_w_out[j],
                               ln_g[i, 0], ln_b[i, 0])
        else:
            pad = LANES - C_GATE_RANK
            w_in = jnp.pad(c_w_in[j], ((0, 0), (0, pad)))
            w_up = jnp.pad(c_w_gk_up[j], ((0, pad), (0, 0)))
            proj = _matmul(h, w_in.astype(BF16), tn=640, out_dtype=BF16)
            o = _gla_core(proj.reshape(B, S, C_IN_WIDTH + pad), w_up, c_b_gk[j], c_norm_g[j])
            h = _matmul_res_ln(o.reshape(T, D), c_w_out[j], h, ln_g[i, 0], ln_b[i, 0])
        h = _moe_block(h, i, moe_w_router[i], moe_b_router[i], moe_w_gate_up, moe_b_gate_up,
                       moe_w_down, moe_b_down, ln_g[i, 1], ln_b[i, 1])
    return h.reshape(B, S, D)
```

```python
import functools
import math

import numpy as np
import jax
import jax.numpy as jnp
from jax import lax
from jax.experimental import pallas as pl
from jax.experimental.pallas import tpu as pltpu

F32 = jnp.float32
BF16 = jnp.bfloat16
I32 = jnp.int32
HI = lax.Precision.HIGHEST

D_MODEL = 1024
DEPTH = 4
N_MIXERS = 3
ALPHA = (2.0 * DEPTH) ** 0.25
LN_EPS = 1e-5
LOG2E = math.log2(math.e)

A_HEADS = 8
A_DH = D_MODEL // A_HEADS
MOBA_BLOCK = 256
MOBA_TOPK = 3
REL_BUCKETS = 32
REL_MAX_DIST = 128

B_GROUPS = 8
B_WIDTH = 2 * D_MODEL
B_CHUNK = 128
B_GW = B_WIDTH // B_GROUPS

C_HEADS = 4
C_KEY_DIM = D_MODEL // 2
C_VAL_DIM = D_MODEL
C_DK = C_KEY_DIM // C_HEADS
C_DV = C_VAL_DIM // C_HEADS
C_GATE_RANK = 16
C_GATE_NORMALIZER = 16.0
C_IN_WIDTH = 2 * C_KEY_DIM + 2 * C_VAL_DIM + C_GATE_RANK

N_EXPERTS = 32
TOP_K = 4
SWIGLU_LIMIT = 7.0
SWIGLU_ALPHA = 1.702

LANES = 128
SUBLANES = 8
VMEM_LIMIT = 56 * 1024 * 1024

GLA_CHUNK = 128
GLA_SUB = 16
GLA_HEADS_PER_STEP = 4
GLA_SEQ_TILE = 1024
MOE_BM = 1024
MOE_SUB = 256
MOE_TM = 512
MOE_SEG = SUBLANES
MOE_SEG_BITS = (MOE_TM // MOE_SEG).bit_length()
MOE_STAGE = MOE_TM * TOP_K + N_EXPERTS * MOE_SEG
MOE_PERM_CHUNK = 256
MOE_PACK_W = D_MODEL // 2


def _cparams(*sem):
    return pltpu.CompilerParams(dimension_semantics=sem, vmem_limit_bytes=VMEM_LIMIT)


def _ln(x, g, b):
    mu = jnp.mean(x, axis=-1, keepdims=True)
    xc = x - mu
    var = jnp.mean(xc * xc, axis=-1, keepdims=True)
    return xc * lax.rsqrt(var + LN_EPS) * g + b


def _mm_body(x_ref, w_ref, o_ref, *, act, tn):
    x = x_ref[...].astype(BF16)
    for n0 in range(0, w_ref.shape[1], tn):
        acc = jnp.dot(x, w_ref[:, n0:n0 + tn], preferred_element_type=F32)
        if act == "gelu":
            acc = 0.5 * acc * (1.0 + lax.erf(acc * (2.0 ** -0.5)))
        o_ref[:, n0:n0 + tn] = acc.astype(o_ref.dtype)


def _matmul(x, w, *, act=None, tm=512, tn=512, out_dtype=F32):
    M, K = x.shape
    N = w.shape[1]
    assert M % tm == 0 and N % tn == 0
    return pl.pallas_call(
        functools.partial(_mm_body, act=act, tn=tn),
        grid=(M // tm,),
        in_specs=[pl.BlockSpec((tm, K), lambda i: (i, 0)),
                  pl.BlockSpec((K, N), lambda i: (0, 0))],
        out_specs=pl.BlockSpec((tm, N), lambda i: (i, 0)),
        out_shape=jax.ShapeDtypeStruct((M, N), out_dtype),
        compiler_params=_cparams("arbitrary"),
        name="proj_matmul",
    )(x, w)


def _mm_res_ln_body(x_ref, w_ref, h_ref, g_ref, b_ref, o_ref, wb_ref):
    @pl.when(pl.program_id(0) == 0)
    def _():
        wb_ref[...] = w_ref[...].astype(BF16)

    t = jnp.dot(x_ref[...].astype(BF16), wb_ref[...], preferred_element_type=F32)
    o_ref[...] = _ln(ALPHA * h_ref[...] + t, g_ref[...], b_ref[...])


def _matmul_res_ln(x, w, h, g, b, *, tm=512):
    M, K = x.shape
    N = w.shape[1]
    return pl.pallas_call(
        _mm_res_ln_body,
        grid=(M // tm,),
        in_specs=[pl.BlockSpec((tm, K), lambda i: (i, 0)),
                  pl.BlockSpec((K, N), lambda i: (0, 0)),
                  pl.BlockSpec((tm, N), lambda i: (i, 0)),
                  pl.BlockSpec((1, N), lambda i: (0, 0)),
                  pl.BlockSpec((1, N), lambda i: (0, 0))],
        out_specs=pl.BlockSpec((tm, N), lambda i: (i, 0)),
        out_shape=jax.ShapeDtypeStruct((M, N), F32),
        scratch_shapes=[pltpu.VMEM((K, N), BF16)],
        compiler_params=_cparams("arbitrary"),
        name="outproj_res_ln",
    )(x, w, h, g.reshape(1, N), b.reshape(1, N))


def _t5_bucket_lower_bounds():
    n = np.arange(0, 4 * REL_MAX_DIST, dtype=np.int64)
    max_exact = REL_BUCKETS // 2
    nf = np.maximum(n, 1).astype(np.float32)
    large = max_exact + (np.log(nf / np.float32(max_exact)) / np.float32(math.log(REL_MAX_DIST / max_exact))
                         * np.float32(REL_BUCKETS - max_exact)).astype(np.int32)
    large = np.minimum(large, REL_BUCKETS - 1)
    bucket = np.where(n < max_exact, n, large)
    assert np.all(np.diff(bucket) >= 0) and bucket[-1] == REL_BUCKETS - 1
    return [int(np.argmax(bucket >= b)) for b in range(REL_BUCKETS)]


_BUCKET_LO = _t5_bucket_lower_bounds()


def _bias_table_body(rel_ref, o_ref):
    blk = MOBA_BLOCK
    row = lax.broadcasted_iota(I32, (blk, blk), 0)
    col = lax.broadcasted_iota(I32, (blk, blk), 1)
    for t in range(2):
        d = row - col + t * blk
        for h in range(A_HEADS):
            val = jnp.full((blk, blk), rel_ref[REL_BUCKETS - 1, h], F32)
            for b in range(REL_BUCKETS - 2, -1, -1):
                val = jnp.where(d < _BUCKET_LO[b + 1], rel_ref[b, h], val)
            val = val * LOG2E
            if t == 0:
                val = jnp.where(d >= 0, val, -jnp.inf)
            o_ref[h, t] = val


def _bias_tables(rel_bias):
    return pl.pallas_call(
        _bias_table_body,
        in_specs=[pl.BlockSpec(memory_space=pltpu.SMEM)],
        out_shape=jax.ShapeDtypeStruct((A_HEADS, 2, MOBA_BLOCK, MOBA_BLOCK), F32),
        compiler_params=pltpu.CompilerParams(vmem_limit_bytes=VMEM_LIMIT),
        name="rel_bias_tables",
    )(rel_bias)


def _moba_body(rel_ref, q_ref, k_ref, v_ref, tb_ref, o_ref, s_s, *, seq):
    blk, dh = MOBA_BLOCK, A_DH
    nb = seq // blk
    h = pl.program_id(1)
    kmean = jnp.mean(k_ref[0].astype(F32).reshape(nb, blk, dh), axis=1)
    kmean_pad = jnp.concatenate([kmean, jnp.zeros((LANES - nb, dh), F32)], axis=0)
    b_far = rel_ref[REL_BUCKETS - 1, h] * LOG2E
    lane = lax.broadcasted_iota(I32, (blk, LANES), 1)
    ones_bf = jnp.ones((LANES, LANES), BF16)

    for n in range(nb):
        r0 = n * blk
        s_s[r0:seq, r0:r0 + blk] = lax.dot_general(q_ref[0, r0:seq, :], k_ref[0, r0:r0 + blk, :],
                                                   (((1,), (1,)), ((), ())), preferred_element_type=F32)

    for c in range(nb):
        q = q_ref[0, c * blk:(c + 1) * blk, :]
        n_keys = (c + 1) * blk
        s = s_s[c * blk:(c + 1) * blk, 0:n_keys]
        if c > MOBA_TOPK:
            gate = lax.dot_general(q.astype(F32), kmean_pad, (((1,), (1,)), ((), ())),
                                   precision=HI, preferred_element_type=F32)
            gate = jnp.where(lane < c, gate, -jnp.inf)
        pieces = []
        for n in range(c):
            shift = b_far
            if c > MOBA_TOPK:
                gn = gate[:, n:n + 1]
                beats = (gate > gn) | ((gate == gn) & (lane < n))
                rank = jnp.dot(jnp.where(beats, 1.0, 0.0).astype(BF16), ones_bf, preferred_element_type=F32)
                shift = jnp.where(rank < MOBA_TOPK, b_far, -jnp.inf)
                shift = jnp.concatenate([shift] * (blk // LANES), axis=1)
            if n == c - 1:
                shift = tb_ref[0, 1] + (shift - b_far)
            pieces.append(s[:, n * blk:(n + 1) * blk] + shift)
        pieces.append(s[:, c * blk:(c + 1) * blk] + tb_ref[0, 0])
        logits = jnp.concatenate(pieces, axis=1) if c > 0 else pieces[0]
        m = jnp.max(logits, axis=1, keepdims=True)
        p = jnp.exp2(logits - m)
        denom = jnp.sum(p, axis=1, keepdims=True)
        o = jnp.dot(p.astype(BF16), v_ref[0, 0:n_keys, :], preferred_element_type=F32)
        o_ref[0, c * blk:(c + 1) * blk, :] = o / denom


def _moba_attention(qkv, rel_bias, tables):
    B, S, _ = qkv.shape
    H, dh = A_HEADS, A_DH
    assert S % MOBA_BLOCK == 0 and S // MOBA_BLOCK <= SUBLANES
    return pl.pallas_call(
        functools.partial(_moba_body, seq=S),
        grid=(B, H),
        in_specs=[pl.BlockSpec(memory_space=pltpu.SMEM),
                  pl.BlockSpec((1, S, dh), lambda b, h: (b, 0, h)),
                  pl.BlockSpec((1, S, dh), lambda b, h: (b, 0, H + h)),
                  pl.BlockSpec((1, S, dh), lambda b, h: (b, 0, 2 * H + h)),
                  pl.BlockSpec((1, 2, MOBA_BLOCK, MOBA_BLOCK), lambda b, h: (h, 0, 0, 0))],
        out_specs=pl.BlockSpec((1, S, dh), lambda b, h: (b, 0, h)),
        out_shape=jax.ShapeDtypeStruct((B, S, H * dh), F32),
        scratch_shapes=[pltpu.VMEM((S, S), F32)],
        compiler_params=_cparams("arbitrary", "arbitrary"),
        name="moba_attention",
    )(rel_bias, qkv, qkv, qkv, tables)


def _gmlp_body(u_ref, v_ref, h_ref, vg_ref, vb_ref, ws_ref, bs_ref, wo_ref, g_ref, b_ref, o_ref,
               wst_s, wob_s, *, tm):
    @pl.when(pl.program_id(0) == 0)
    def _():
        row = lax.broadcasted_iota(I32, (B_CHUNK, B_CHUNK), 0)
        col = lax.broadcasted_iota(I32, (B_CHUNK, B_CHUNK), 1)
        for g in range(B_GROUPS):
            wst_s[g] = jnp.where(row >= col, ws_ref[g], 0.0).astype(BF16)
        wob_s[...] = wo_ref[...].astype(BF16)

    vn = _ln(v_ref[...].astype(F32), vg_ref[...], vb_ref[...]).astype(BF16)
    u = u_ref[...].astype(F32)
    rows = []
    for c in range(tm // B_CHUNK):
        r0 = c * B_CHUNK
        cols = []
        for g in range(B_GROUPS):
            c0 = g * B_GW
            mixed = jnp.dot(wst_s[g], vn[r0:r0 + B_CHUNK, c0:c0 + B_GW],
                            preferred_element_type=F32) + bs_ref[g]
            cols.append(u[r0:r0 + B_CHUNK, c0:c0 + B_GW] * mixed)
        rows.append(jnp.concatenate(cols, axis=1))
    y = jnp.concatenate(rows, axis=0).astype(BF16)
    t = jnp.dot(y, wob_s[...], preferred_element_type=F32)
    o_ref[...] = _ln(ALPHA * h_ref[...] + t, g_ref[...], b_ref[...])


def _gmlp_gate_out(z, h, v_g, v_b, w_s, b_s, w_out, g, b, *, tm=256):
    T = z.shape[0]
    D = D_MODEL
    return pl.pallas_call(
        functools.partial(_gmlp_body, tm=tm),
        grid=(T // tm,),
        in_specs=[pl.BlockSpec((tm, B_WIDTH), lambda i: (i, 0)),
                  pl.BlockSpec((tm, B_WIDTH), lambda i: (i, 1)),
                  pl.BlockSpec((tm, D), lambda i: (i, 0)),
                  pl.BlockSpec((1, B_WIDTH), lambda i: (0, 0)),
                  pl.BlockSpec((1, B_WIDTH), lambda i: (0, 0)),
                  pl.BlockSpec((B_GROUPS, B_CHUNK, B_CHUNK), lambda i: (0, 0, 0)),
                  pl.BlockSpec((B_GROUPS, B_CHUNK, 1), lambda i: (0, 0, 0)),
                  pl.BlockSpec((B_WIDTH, D), lambda i: (0, 0)),
                  pl.BlockSpec((1, D), lambda i: (0, 0)),
                  pl.BlockSpec((1, D), lambda i: (0, 0))],
        out_specs=pl.BlockSpec((tm, D), lambda i: (i, 0)),
        out_shape=jax.ShapeDtypeStruct((T, D), F32),
        scratch_shapes=[pltpu.VMEM((B_GROUPS, B_CHUNK, B_CHUNK), BF16), pltpu.VMEM((B_WIDTH, D), BF16)],
        compiler_params=_cparams("arbitrary"),
        name="gmlp_gate_out",
    )(z, z, h, v_g.reshape(1, B_WIDTH), v_b.reshape(1, B_WIDTH), w_s, b_s.reshape(B_GROUPS, B_CHUNK, 1),
      w_out, g.reshape(1, D), b.reshape(1, D))


def _gla_body(q_ref, k_ref, v_ref, g_ref, gl_ref, wup_ref, bgk_ref, ng_ref, o_ref, la_s, st_s, raw_s, *, seq):
    ch, sub, dk, dv = GLA_CHUNK, GLA_SUB, C_DK, C_DV
    x = jnp.dot(gl_ref[0].astype(F32), wup_ref[...], precision=HI, preferred_element_type=F32) + bgk_ref[...]
    la_s[...] = (jnp.minimum(x, 0.0) - jnp.log1p(jnp.exp(-jnp.abs(x)))) * (LOG2E / C_GATE_NORMALIZER)

    @pl.when(pl.program_id(2) == 0)
    def _():
        st_s[...] = jnp.zeros(st_s.shape, F32)

    row_c = lax.broadcasted_iota(I32, (ch, ch), 0)
    col_c = lax.broadcasted_iota(I32, (ch, ch), 1)
    tri = (row_c >= col_c).astype(F32)
    eye = (row_c == col_c).astype(F32)
    sub_i = lax.broadcasted_iota(I32, (sub, ch), 0)
    lane_j = lax.broadcasted_iota(I32, (sub, ch), 1)

    def chunk(c, carry):
        for hh in range(GLA_HEADS_PER_STEP):
            chunk_head(c, hh)
        return carry

    def chunk_head(c, hh):
        s0 = pl.multiple_of(c * ch, ch)
        a = la_s[pl.ds(s0, ch), hh * dk:(hh + 1) * dk]
        b = jnp.dot(tri, a, precision=HI, preferred_element_type=F32)
        q = q_ref[0, pl.ds(s0, ch), hh * dk:(hh + 1) * dk].astype(F32) * (dk ** -0.5)
        k = k_ref[0, pl.ds(s0, ch), hh * dk:(hh + 1) * dk].astype(F32)
        v = v_ref[0, pl.ds(s0, ch), hh * dv:(hh + 1) * dv].astype(BF16)
        st = st_s[hh]
        o = jnp.dot((q * jnp.exp2(b)).astype(BF16), st.astype(BF16), preferred_element_type=F32)

        blocks = []
        for sb in range(ch // sub):
            r0 = sb * sub
            q_i = q[r0:r0 + sub]
            b_i = b[r0:r0 + sub]
            k_i = k[r0:r0 + sub]
            att = jnp.zeros((sub, ch), F32)
            for j in range(sub):
                e = jnp.exp2(jnp.minimum(b_i - b_i[j:j + 1, :], 0.0))
                colv = jnp.sum(q_i * k_i[j:j + 1, :] * e, axis=1, keepdims=True)
                att = jnp.where((lane_j == r0 + j) & (sub_i >= j), colv, att)
            if sb > 0:
                ref_b = b[r0 - 1:r0, :]
                q_t = q_i * jnp.exp2(b_i - ref_b)
                k_t = k * jnp.exp2(jnp.minimum(ref_b - b, 0.0))
                off = lax.dot_general(q_t.astype(BF16), k_t.astype(BF16), (((1,), (1,)), ((), ())),
                                      preferred_element_type=F32)
                att = jnp.where(lane_j < r0, off, att)
            blocks.append(att)
        att_full = jnp.concatenate(blocks, axis=0)
        o = o + jnp.dot(att_full.astype(BF16), v, preferred_element_type=F32)
        raw_s[pl.ds(s0, ch), hh * dv:(hh + 1) * dv] = o

        b_last = b[ch - 1:ch, :]
        k_d = k * jnp.exp2(b_last - b)
        upd = jnp.dot(k_d.T.astype(BF16), v, preferred_element_type=F32)
        decay_col = lax.dot_general(eye, jnp.broadcast_to(jnp.exp2(b_last), (SUBLANES, dk)),
                                    (((1,), (1,)), ((), ())), precision=HI,
                                    preferred_element_type=F32)[:, 0:1]
        st_s[hh] = decay_col * st + upd

    lax.fori_loop(0, seq // ch, chunk, 0)

    for hh in range(GLA_HEADS_PER_STEP):
        o = raw_s[:, hh * dv:(hh + 1) * dv]
        rms = o * lax.rsqrt(jnp.mean(o * o, axis=-1, keepdims=True) + LN_EPS) * ng_ref[...]
        gg = g_ref[0, :, hh * dv:(hh + 1) * dv].astype(F32)
        o_ref[0, :, hh * dv:(hh + 1) * dv] = rms * (gg * jax.nn.sigmoid(gg))


def _gla_core(proj, w_up_pad, b_gk, norm_g):
    B, S, _ = proj.shape
    H, hps = C_HEADS, GLA_HEADS_PER_STEP
    dk, dv = C_DK * hps, C_DV * hps
    st = GLA_SEQ_TILE
    assert C_DK == LANES and S % st == 0 and st % GLA_CHUNK == 0 and H % hps == 0
    k_off = C_KEY_DIM // dk
    v_off = 2 * C_KEY_DIM // dv
    g_off = (2 * C_KEY_DIM + C_VAL_DIM) // dv
    gl_off = (2 * C_KEY_DIM + 2 * C_VAL_DIM) // LANES
    return pl.pallas_call(
        functools.partial(_gla_body, seq=st),
        grid=(B, H // hps, S // st),
        in_specs=[pl.BlockSpec((1, st, dk), lambda b, h, s: (b, s, h)),
                  pl.BlockSpec((1, st, dk), lambda b, h, s: (b, s, k_off + h)),
                  pl.BlockSpec((1, st, dv), lambda b, h, s: (b, s, v_off + h)),
                  pl.BlockSpec((1, st, dv), lambda b, h, s: (b, s, g_off + h)),
                  pl.BlockSpec((1, st, LANES), lambda b, h, s: (b, s, gl_off)),
                  pl.BlockSpec((LANES, dk), lambda b, h, s: (0, h)),
                  pl.BlockSpec((1, dk), lambda b, h, s: (0, h)),
                  pl.BlockSpec((1, C_DV), lambda b, h, s: (0, 0))],
        out_specs=pl.BlockSpec((1, st, dv), lambda b, h, s: (b, s, h)),
        out_shape=jax.ShapeDtypeStruct((B, S, C_VAL_DIM), F32),
        scratch_shapes=[pltpu.VMEM((st, dk), F32), pltpu.VMEM((hps, C_DK, C_DV), F32), pltpu.VMEM((st, dv), F32)],
        compiler_params=_cparams("arbitrary", "arbitrary", "arbitrary"),
        name="gla_core",
    )(proj, proj, proj, proj, proj, w_up_pad, b_gk.reshape(1, C_KEY_DIM), norm_g.reshape(1, C_DV))


def _router_body(h_ref, wt_ref, b_ref, slot_ref, gate_ref, cnt_ref, lbase_ref, gbase_ref, run_s, ws_s, *, tm):
    E = N_EXPERTS

    @pl.when(pl.program_id(0) == 0)
    def _():
        run_s[...] = jnp.zeros(run_s.shape, F32)
        w = wt_ref[...]
        w1 = w.astype(BF16)
        ws_s[0:E, :] = w1
        ws_s[E:2 * E, :] = (w - w1.astype(F32)).astype(BF16)

    h = h_ref[...]
    h1 = h.astype(BF16)
    h2 = (h - h1.astype(F32)).astype(BF16)
    nt_dims = (((1,), (1,)), ((), ()))
    both = lax.dot_general(ws_s[...], h1, nt_dims, preferred_element_type=F32)
    logits = (both[0:E] + (both[E:2 * E] + lax.dot_general(ws_s[0:E, :], h2, nt_dims, preferred_element_type=F32))
              + b_ref[...])
    row = lax.broadcasted_iota(I32, (E, tm), 0)
    rest = logits
    top_val, onehot = [], []
    for _ in range(TOP_K):
        m = jnp.max(rest, axis=0, keepdims=True)
        idx = jnp.min(jnp.where(rest == m, row, E), axis=0, keepdims=True)
        oh = row == idx
        rest = jnp.where(oh, -jnp.inf, rest)
        top_val.append(m)
        onehot.append(oh)
    ex = [jnp.exp(v - top_val[0]) for v in top_val]
    den = ex[0] + ex[1] + ex[2] + ex[3]
    chosen = sum(oh.astype(F32) for oh in onehot)

    tr = lax.broadcasted_iota(I32, (tm, tm), 0)
    tc = lax.broadcasted_iota(I32, (tm, tm), 1)
    earlier = (tr < tc).astype(BF16)
    seen = jnp.dot(chosen.astype(BF16), earlier, preferred_element_type=F32)
    cnt = jnp.sum(chosen, axis=1, keepdims=True)
    cnt_pad = jnp.floor((cnt + (MOE_SEG - 1)) * (1.0 / MOE_SEG)) * MOE_SEG
    cnt_cols = jnp.broadcast_to(cnt_pad, (E, LANES))
    er = lax.broadcasted_iota(I32, (E, E), 0)
    ec = lax.broadcasted_iota(I32, (E, E), 1)
    lbase_cols = jnp.dot((ec < er).astype(F32), cnt_cols, precision=HI, preferred_element_type=F32)
    pos = seen + lbase_cols[:, 0:1]

    row_k = lax.broadcasted_iota(I32, (TOP_K, tm), 0)
    s_out = jnp.zeros((TOP_K, tm), I32)
    g_out = jnp.zeros((TOP_K, tm), F32)
    for kk in range(TOP_K):
        sk = jnp.sum(jnp.where(onehot[kk], pos, 0.0), axis=0, keepdims=True)
        s_out = jnp.where(row_k == kk, sk.astype(I32), s_out)
        g_out = jnp.where(row_k == kk, ex[kk] / den, g_out)
    slot_ref[...] = s_out
    gate_ref[...] = g_out
    cnt_ref[...] = cnt_cols.astype(I32)
    lbase_ref[...] = lbase_cols.astype(I32)
    gbase_ref[...] = run_s[...].astype(I32)
    run_s[...] = run_s[...] + cnt_cols


def _router(h, w, b, *, tm=MOE_TM):
    T, D = h.shape
    E = N_EXPERTS
    nt = T // tm
    tab = jax.ShapeDtypeStruct((nt * E, LANES), I32)
    tab_spec = pl.BlockSpec((E, LANES), lambda i: (i, 0))
    return pl.pallas_call(
        functools.partial(_router_body, tm=tm),
        grid=(nt,),
        in_specs=[pl.BlockSpec((tm, D), lambda i: (i, 0)),
                  pl.BlockSpec((E, D), lambda i: (0, 0)),
                  pl.BlockSpec((E, 1), lambda i: (0, 0))],
        out_specs=[pl.BlockSpec((TOP_K, tm), lambda i: (0, i)),
                   pl.BlockSpec((TOP_K, tm), lambda i: (0, i)),
                   tab_spec, tab_spec, tab_spec],
        out_shape=[jax.ShapeDtypeStruct((TOP_K, T), I32),
                   jax.ShapeDtypeStruct((TOP_K, T), F32),
                   tab, tab, tab],
        scratch_shapes=[pltpu.VMEM((E, LANES), F32), pltpu.VMEM((2 * E, D), BF16)],
        compiler_params=_cparams("arbitrary"),
        name="moe_router",
    )(h, w.T, b.reshape(E, 1))


def _pack_pairs(left, right):
    hi = lax.bitcast_convert_type(left, I32) & jnp.int32(-65536)
    lo = lax.shift_right_logical(lax.bitcast_convert_type(right, I32), 16)
    return hi | lo


def _unpack_pairs(word):
    left = lax.bitcast_convert_type(word & jnp.int32(-65536), F32)
    right = lax.bitcast_convert_type(lax.shift_left(word, 16), F32)
    return left.astype(BF16), right.astype(BF16)


def _segment_starts(seg_g, seg_l, seg_n, tile, make_copy):
    def per_expert(e, carry):
        idx = tile * N_EXPERTS + e
        n = seg_n[idx]
        g = seg_g[idx]
        l = seg_l[idx]
        for bit in range(MOE_SEG_BITS - 1, -1, -1):
            size = MOE_SEG << bit

            @pl.when((n & size) != 0)
            def _():
                off = n & ~(2 * size - 1)
                make_copy(pl.multiple_of(g + off, MOE_SEG), pl.multiple_of(l + off, MOE_SEG), size).start()
        return carry

    lax.fori_loop(0, N_EXPERTS, per_expert, 0)


def _segment_waits(seg_l, seg_n, tile, make_copy):
    last = tile * N_EXPERTS + N_EXPERTS - 1
    total = seg_l[last] + seg_n[last]
    for bit in range((MOE_STAGE // MOE_SEG).bit_length() - 1, -1, -1):
        size = MOE_SEG << bit

        @pl.when((total & size) != 0)
        def _():
            make_copy(0, 0, size).wait()


def _dispatch_body(segg_ref, segl_ref, segn_ref, zf_ref, slot_t_ref, h_ref, xs_ref, stage, zbuf, sem, *, tm, bm):
    i = pl.program_id(0)

    def zero_fill(start):
        def act(cp):
            if start:
                cp.start()
            else:
                cp.wait()

        def per_expert(e, carry):
            g = zf_ref[e]
            n = zf_ref[N_EXPERTS + e]
            for bit in range((bm // MOE_SEG).bit_length() - 2, -1, -1):
                size = MOE_SEG << bit

                @pl.when((n & size) != 0)
                def _():
                    off = n & ~(2 * size - 1)
                    act(pltpu.make_async_copy(
                        zbuf.at[pl.ds(0, size), :],
                        xs_ref.at[pl.ds(pl.multiple_of(g + off, MOE_SEG), size), :], sem.at[2]))
            return carry

        def per_tail(j, carry):
            row = zf_ref[2 * N_EXPERTS + j]

            @pl.when(row >= 0)
            def _():
                act(pltpu.make_async_copy(zbuf, xs_ref.at[pl.ds(pl.multiple_of(row, MOE_SEG), bm), :], sem.at[2]))
            return carry

        lax.fori_loop(0, N_EXPERTS, per_expert, 0)
        lax.fori_loop(0, zf_ref.shape[0] - 2 * N_EXPERTS, per_tail, 0)

    @pl.when(i == 0)
    def _():
        zbuf[...] = jnp.zeros(zbuf.shape, I32)
        zero_fill(True)

    @pl.when(i == pl.num_programs(0) - 1)
    def _():
        zero_fill(False)

    slot_i = i % 2
    hb = h_ref[...].astype(BF16)
    for c in range(MOE_STAGE // MOE_PERM_CHUNK):
        r0 = c * MOE_PERM_CHUNK
        rows = lax.broadcasted_iota(I32, (MOE_PERM_CHUNK, tm), 0) + r0
        hit = rows == slot_t_ref[0:1, :]
        for kk in range(1, TOP_K):
            hit = hit | (rows == slot_t_ref[kk:kk + 1, :])
        perm = jnp.where(hit, 1.0, 0.0).astype(BF16)
        rows_f = jnp.dot(perm, hb, preferred_element_type=F32)
        stage[slot_i, r0:r0 + MOE_PERM_CHUNK, :] = _pack_pairs(rows_f[:, :MOE_PACK_W], rows_f[:, MOE_PACK_W:])

    def copier(buf):
        def make_copy(g, l, size):
            return pltpu.make_async_copy(stage.at[buf, pl.ds(l, size), :], xs_ref.at[pl.ds(g, size), :],
                                         sem.at[buf])
        return make_copy

    _segment_starts(segg_ref, segl_ref, segn_ref, i, copier(slot_i))

    @pl.when(i > 0)
    def _():
        _segment_waits(segl_ref, segn_ref, i - 1, copier(1 - slot_i))

    @pl.when(i == pl.num_programs(0) - 1)
    def _():
        _segment_waits(segl_ref, segn_ref, i, copier(slot_i))


def _dispatch(seg_g, seg_l, seg_n, zero_rows, slot_t, h, n_rows, *, tm=MOE_TM, bm=MOE_BM):
    T, D = h.shape
    grid_spec = pltpu.PrefetchScalarGridSpec(
        num_scalar_prefetch=4,
        grid=(T // tm,),
        in_specs=[pl.BlockSpec((TOP_K, tm), lambda i, *_: (0, i)),
                  pl.BlockSpec((tm, D), lambda i, *_: (i, 0))],
        out_specs=pl.BlockSpec(memory_space=pl.ANY),
        scratch_shapes=[pltpu.VMEM((2, MOE_STAGE, MOE_PACK_W), I32), pltpu.VMEM((bm, MOE_PACK_W), I32),
                        pltpu.SemaphoreType.DMA((3,))],
    )
    return pl.pallas_call(
        functools.partial(_dispatch_body, tm=tm, bm=bm),
        grid_spec=grid_spec,
        out_shape=jax.ShapeDtypeStruct((n_rows, MOE_PACK_W), I32),
        compiler_params=pltpu.CompilerParams(dimension_semantics=("arbitrary",), vmem_limit_bytes=VMEM_LIMIT,
                                             has_side_effects=True),
        name="moe_dispatch",
    )(seg_g, seg_l, seg_n, zero_rows, slot_t, h)


def _expert_body(be_ref, nb_ref, nv_ref, nx_ref, xs_ref, wgu_hbm, bgu_ref, wd_hbm, bd_ref, ys_ref,
                 wgu_f, wd_f, wgu_s, wd_s, sem, *, bm, layer):
    i = pl.program_id(0)
    e = be_ref[i]
    prev = be_ref[jnp.maximum(i - 1, 0)]
    live = i < nb_ref[0]
    n_valid = jnp.where(live, nv_ref[i], 0)

    def weight_copies(expert):
        return (pltpu.make_async_copy(wgu_hbm.at[layer, expert], wgu_f, sem.at[0]),
                pltpu.make_async_copy(wd_hbm.at[layer, expert], wd_f, sem.at[1]))

    @pl.when(i == 0)
    def _():
        for cp in weight_copies(e):
            cp.start()

    @pl.when(live & ((i == 0) | (e != prev)))
    def _():
        for cp in weight_copies(e):
            cp.wait()
        wgu_s[...] = wgu_f[...].astype(BF16)
        wd_s[...] = wd_f[...].astype(BF16)
        nxt = nx_ref[e]

        @pl.when(nxt >= 0)
        def _():
            for cp in weight_copies(nxt):
                cp.start()

    def mlp(r0, rows):
        x = jnp.concatenate(_unpack_pairs(xs_ref[r0:r0 + rows, :]), axis=1)
        hh = jnp.dot(x, wgu_s[...], preferred_element_type=F32) + bgu_ref[...]
        d_e = hh.shape[1] // 2
        h_gate = jnp.minimum(hh[:, :d_e], SWIGLU_LIMIT)
        h_up = jnp.clip(hh[:, d_e:], -SWIGLU_LIMIT, SWIGLU_LIMIT)
        act = h_gate * jax.nn.sigmoid(SWIGLU_ALPHA * h_gate) * (h_up + 1.0)
        y = jnp.dot(act.astype(BF16), wd_s[...], preferred_element_type=F32) + bd_ref[...]
        y = y.astype(BF16).astype(F32)
        ys_ref[r0:r0 + rows, :] = _pack_pairs(y[:, :MOE_PACK_W], y[:, MOE_PACK_W:])

    full = n_valid > bm - MOE_SUB

    @pl.when(full)
    def _():
        mlp(0, bm)

    for r0 in range(0, bm, MOE_SUB):
        @pl.when(jnp.logical_not(full) & (n_valid > r0))
        def _(r0=r0):
            mlp(r0, MOE_SUB)

        @pl.when(n_valid <= r0)
        def _(r0=r0):
            ys_ref[r0:r0 + MOE_SUB, :] = jnp.zeros((MOE_SUB, ys_ref.shape[1]), I32)


def _experts(block_e, n_live, n_valid, next_e, xs, layer, w_gu, b_gu, w_down, b_down, *, bm=MOE_BM):
    L, E, D, D2 = w_gu.shape
    n_blocks = xs.shape[0] // bm

    def blk(i, nb):
        return jnp.maximum(jnp.minimum(i, nb[0] - 1), 0)

    grid_spec = pltpu.PrefetchScalarGridSpec(
        num_scalar_prefetch=4,
        grid=(n_blocks,),
        in_specs=[pl.BlockSpec((bm, MOE_PACK_W), lambda i, be, nb, nv, nx: (blk(i, nb), 0)),
                  pl.BlockSpec(memory_space=pl.ANY),
                  pl.BlockSpec((None, None, 1, D2), lambda i, be, nb, nv, nx: (layer, be[blk(i, nb)], 0, 0)),
                  pl.BlockSpec(memory_space=pl.ANY),
                  pl.BlockSpec((None, None, 1, D), lambda i, be, nb, nv, nx: (layer, be[blk(i, nb)], 0, 0))],
        out_specs=pl.BlockSpec((bm, MOE_PACK_W), lambda i, be, nb, nv, nx: (i, 0)),
        scratch_shapes=[pltpu.VMEM((D, D2), F32), pltpu.VMEM((D2 // 2, D), F32),
                        pltpu.VMEM((D, D2), BF16), pltpu.VMEM((D2 // 2, D), BF16),
                        pltpu.SemaphoreType.DMA((2,))],
    )
    return pl.pallas_call(
        functools.partial(_expert_body, bm=bm, layer=layer),
        grid_spec=grid_spec,
        out_shape=jax.ShapeDtypeStruct(xs.shape, I32),
        compiler_params=_cparams("arbitrary"),
        name="moe_experts",
    )(block_e, n_live, n_valid, next_e, xs, w_gu, b_gu.reshape(L, E, 1, D2), w_down, b_down.reshape(L, E, 1, D))


def _combine_body(segg_ref, segl_ref, segn_ref, ys_ref, slot_ref, gate_ref, h_ref, g_ref, b_ref, o_ref,
                  stage, mix_s, sem, *, tm):
    i = pl.program_id(0)

    slot_i = i % 2

    def copier(buf):
        def make_copy(g, l, size):
            return pltpu.make_async_copy(ys_ref.at[pl.ds(g, size), :], stage.at[buf, pl.ds(l, size), :],
                                         sem.at[buf])
        return make_copy

    @pl.when(i == 0)
    def _():
        stage[...] = jnp.zeros(stage.shape, I32)
        _segment_starts(segg_ref, segl_ref, segn_ref, i, copier(slot_i))

    @pl.when(i + 1 < pl.num_programs(0))
    def _():
        _segment_starts(segg_ref, segl_ref, segn_ref, i + 1, copier(1 - slot_i))

    slot = slot_ref[...]
    gate = gate_ref[...]
    for c in range(MOE_STAGE // MOE_PERM_CHUNK):
        c0 = c * MOE_PERM_CHUNK
        cols = lax.broadcasted_iota(I32, (tm, MOE_PERM_CHUNK), 1) + c0
        w = jnp.zeros((tm, MOE_PERM_CHUNK), F32)
        for kk in range(TOP_K):
            w = jnp.where(cols == slot[:, kk:kk + 1], gate[:, kk:kk + 1], w)
        mix_s[:, c0:c0 + MOE_PERM_CHUNK] = w.astype(BF16)

    _segment_waits(segl_ref, segn_ref, i, copier(slot_i))
    left, right = _unpack_pairs(stage[slot_i])
    rows = tm // 2
    for r0 in range(0, tm, rows):
        mix = mix_s[r0:r0 + rows, :]
        y = jnp.concatenate([jnp.dot(mix, left, preferred_element_type=F32),
                             jnp.dot(mix, right, preferred_element_type=F32)], axis=1)
        o_ref[r0:r0 + rows, :] = _ln(ALPHA * h_ref[r0:r0 + rows, :] + y, g_ref[...], b_ref[...])


def _combine(seg_g, seg_l, seg_n, ys, slot, gate, h, g, b, *, tm=MOE_TM):
    T, D = h.shape
    grid_spec = pltpu.PrefetchScalarGridSpec(
        num_scalar_prefetch=3,
        grid=(T // tm,),
        in_specs=[pl.BlockSpec(memory_space=pl.ANY),
                  pl.BlockSpec((tm, TOP_K), lambda i, *_: (i, 0)),
                  pl.BlockSpec((tm, TOP_K), lambda i, *_: (i, 0)),
                  pl.BlockSpec((tm, D), lambda i, *_: (i, 0)),
                  pl.BlockSpec((1, D), lambda i, *_: (0, 0)),
                  pl.BlockSpec((1, D), lambda i, *_: (0, 0))],
        out_specs=pl.BlockSpec((tm, D), lambda i, *_: (i, 0)),
        scratch_shapes=[pltpu.VMEM((2, MOE_STAGE, MOE_PACK_W), I32), pltpu.VMEM((tm, MOE_STAGE), BF16),
                        pltpu.SemaphoreType.DMA((2,))],
    )
    return pl.pallas_call(
        functools.partial(_combine_body, tm=tm),
        grid_spec=grid_spec,
        out_shape=jax.ShapeDtypeStruct((T, D), F32),
        compiler_params=_cparams("arbitrary"),
        name="moe_combine",
    )(seg_g, seg_l, seg_n, ys, slot, gate, h, g.reshape(1, D), b.reshape(1, D))


def _moe_block(h, layer, w_router, b_router, w_gu, b_gu, w_down, b_down, g, b):
    T = h.shape[0]
    bm, tm, E = MOE_BM, MOE_TM, N_EXPERTS
    nt = T // tm
    slot_t, gate_t, cnt, lbase, gbase = _router(h, w_router, b_router)
    cnt, lbase, gbase = (t.reshape(nt, E, LANES)[:, :, 0] for t in (cnt, lbase, gbase))
    total = gbase[-1] + cnt[-1]
    padded = (total + bm - 1) // bm * bm
    pend = jnp.cumsum(padded)
    pstart = pend - padded
    seg_g = (pstart[None, :] + gbase).reshape(-1).astype(I32)
    seg_l = lbase.reshape(-1)
    seg_n = cnt.reshape(-1)
    n_blocks = -(-(T * TOP_K + nt * E * (MOE_SEG - 1)) // bm) + E
    block_start = jnp.arange(n_blocks, dtype=I32) * bm
    block_e = jnp.minimum(jnp.sum((pend[None, :] <= block_start[:, None]).astype(I32), axis=1), E - 1)
    n_live = (pend[-1:] // bm).astype(I32)
    n_valid = jnp.clip((pstart + total)[block_e] - block_start, 0, bm).astype(I32)
    tail_blk = n_live[0] + jnp.arange(n_blocks - T * TOP_K // bm, dtype=I32)
    tail_blk = jnp.where(tail_blk < n_blocks, tail_blk * bm, -1)
    zero_rows = jnp.concatenate([pstart + total, padded - total, tail_blk]).astype(I32)
    xs = _dispatch(seg_g, seg_l, seg_n, zero_rows, slot_t, h, n_blocks * bm)
    owner = jnp.where(padded > 0, jnp.arange(E, dtype=I32), E)
    later = lax.cummin(owner[::-1])[::-1]
    next_e = jnp.concatenate([later[1:], jnp.full((1,), E, I32)])
    next_e = jnp.where(next_e < E, next_e, -1).astype(I32)
    ys = _experts(block_e, n_live, n_valid, next_e, xs, layer, w_gu, b_gu, w_down, b_down)
    return _combine(seg_g, seg_l, seg_n, ys, slot_t.T, gate_t.T, h, g, b)


def kernel(x, rel_bias, a_w_in, a_w_out, b_w_in, b_ln_g, b_ln_b, b_w_s, b_b_s, b_w_out,
           c_w_in, c_w_gk_up, c_b_gk, c_norm_g, c_w_out, ln_g, ln_b,
           moe_w_router, moe_b_router, moe_w_gate_up, moe_b_gate_up, moe_w_down, moe_b_down):
    B, S, D = x.shape
    T = B * S
    h = x.reshape(T, D)
    tables = _bias_tables(rel_bias)
    for i in range(DEPTH):
        j = i // N_MIXERS
        mixer = i % N_MIXERS
        if mixer == 0:
            q_fold = jnp.where(jnp.arange(3 * D) < D, LOG2E * A_DH ** -0.5, 1.0).astype(F32)
            qkv = _matmul(h, (a_w_in[j] * q_fold).astype(BF16), out_dtype=BF16)
            o = _moba_attention(qkv.reshape(B, S, 3 * D), rel_bias, tables)
            h = _matmul_res_ln(o.reshape(T, D), a_w_out[j], h, ln_g[i, 0], ln_b[i, 0])
        elif mixer == 1:
            z = _matmul(h, b_w_in[j].astype(BF16), act="gelu", out_dtype=BF16)
            h = _gmlp_gate_out(z, h, b_ln_g[j], b_ln_b[j], b_w_s[j], b_b_s[j], b_w_out[j],
                               ln_g[i, 0], ln_b[i, 0])
        else:
            pad = LANES - C_GATE_RANK
            w_in = jnp.pad(c_w_in[j], ((0, 0), (0, pad)))
            w_up = jnp.pad(c_w_gk_up[j], ((0, pad), (0, 0)))
            proj = _matmul(h, w_in.astype(BF16), tn=640, out_dtype=BF16)
            o = _gla_core(proj.reshape(B, S, C_IN_WIDTH + pad), w_up, c_b_gk[j], c_norm_g[j])
            h = _matmul_res_ln(o.reshape(T, D), c_w_out[j], h, ln_g[i, 0], ln_b[i, 0])
        h = _moe_block(h, i, moe_w_router[i], moe_b_router[i], moe_w_gate_up, moe_b_gate_up,
                       moe_w_down, moe_b_down, ln_g[i, 1], ln_b[i, 1])
    return h.reshape(B, S, D)
```

```python
import functools
import math

import numpy as np
import jax
import jax.numpy as jnp
from jax import lax
from jax.experimental import pallas as pl
from jax.experimental.pallas import tpu as pltpu

F32 = jnp.float32
BF16 = jnp.bfloat16
I32 = jnp.int32
HI = lax.Precision.HIGHEST

D_MODEL = 1024
DEPTH = 4
N_MIXERS = 3
ALPHA = (2.0 * DEPTH) ** 0.25
LN_EPS = 1e-5
LOG2E = math.log2(math.e)

A_HEADS = 8
A_DH = D_MODEL // A_HEADS
MOBA_BLOCK = 256
MOBA_TOPK = 3
REL_BUCKETS = 32
REL_MAX_DIST = 128

B_GROUPS = 8
B_WIDTH = 2 * D_MODEL
B_CHUNK = 128
B_GW = B_WIDTH // B_GROUPS

C_HEADS = 4
C_KEY_DIM = D_MODEL // 2
C_VAL_DIM = D_MODEL
C_DK = C_KEY_DIM // C_HEADS
C_DV = C_VAL_DIM // C_HEADS
C_GATE_RANK = 16
C_GATE_NORMALIZER = 16.0
C_IN_WIDTH = 2 * C_KEY_DIM + 2 * C_VAL_DIM + C_GATE_RANK

N_EXPERTS = 32
TOP_K = 4
SWIGLU_LIMIT = 7.0
SWIGLU_ALPHA = 1.702

LANES = 128
SUBLANES = 8
VMEM_LIMIT = 56 * 1024 * 1024

GLA_CHUNK = 128
GLA_SUB = 16
GLA_HEADS_PER_STEP = 4
GLA_SEQ_TILE = 1024
MOE_BM = 1024
MOE_SUB = 256
MOE_TM = 512
MOE_SEG = SUBLANES
MOE_SEG_BITS = (MOE_TM // MOE_SEG).bit_length()
MOE_STAGE = MOE_TM * TOP_K + N_EXPERTS * MOE_SEG
MOE_PERM_CHUNK = 256
MOE_PACK_W = D_MODEL // 2


def _cparams(*sem):
    return pltpu.CompilerParams(dimension_semantics=sem, vmem_limit_bytes=VMEM_LIMIT)


def _ln(x, g, b):
    mu = jnp.mean(x, axis=-1, keepdims=True)
    xc = x - mu
    var = jnp.mean(xc * xc, axis=-1, keepdims=True)
    return xc * lax.rsqrt(var + LN_EPS) * g + b


def _mm_body(x_ref, w_ref, o_ref, *, act, tn):
    x = x_ref[...].astype(BF16)
    for n0 in range(0, w_ref.shape[1], tn):
        acc = jnp.dot(x, w_ref[:, n0:n0 + tn], preferred_element_type=F32)
        if act == "gelu":
            acc = 0.5 * acc * (1.0 + lax.erf(acc * (2.0 ** -0.5)))
        o_ref[:, n0:n0 + tn] = acc.astype(o_ref.dtype)


def _matmul(x, w, *, act=None, tm=512, tn=512, out_dtype=F32):
    M, K = x.shape
    N = w.shape[1]
    assert M % tm == 0 and N % tn == 0
    return pl.pallas_call(
        functools.partial(_mm_body, act=act, tn=tn),
        grid=(M // tm,),
        in_specs=[pl.BlockSpec((tm, K), lambda i: (i, 0)),
                  pl.BlockSpec((K, N), lambda i: (0, 0))],
        out_specs=pl.BlockSpec((tm, N), lambda i: (i, 0)),
        out_shape=jax.ShapeDtypeStruct((M, N), out_dtype),
        compiler_params=_cparams("arbitrary"),
        name="proj_matmul",
    )(x, w)


def _mm_res_ln_body(x_ref, w_ref, h_ref, g_ref, b_ref, o_ref, wb_ref):
    @pl.when(pl.program_id(0) == 0)
    def _():
        wb_ref[...] = w_ref[...].astype(BF16)

    t = jnp.dot(x_ref[...].astype(BF16), wb_ref[...], preferred_element_type=F32)
    o_ref[...] = _ln(ALPHA * h_ref[...] + t, g_ref[...], b_ref[...])


def _matmul_res_ln(x, w, h, g, b, *, tm=512):
    M, K = x.shape
    N = w.shape[1]
    return pl.pallas_call(
        _mm_res_ln_body,
        grid=(M // tm,),
        in_specs=[pl.BlockSpec((tm, K), lambda i: (i, 0)),
                  pl.BlockSpec((K, N), lambda i: (0, 0)),
                  pl.BlockSpec((tm, N), lambda i: (i, 0)),
                  pl.BlockSpec((1, N), lambda i: (0, 0)),
                  pl.BlockSpec((1, N), lambda i: (0, 0))],
        out_specs=pl.BlockSpec((tm, N), lambda i: (i, 0)),
        out_shape=jax.ShapeDtypeStruct((M, N), F32),
        scratch_shapes=[pltpu.VMEM((K, N), BF16)],
        compiler_params=_cparams("arbitrary"),
        name="outproj_res_ln",
    )(x, w, h, g.reshape(1, N), b.reshape(1, N))


def _t5_bucket_lower_bounds():
    n = np.arange(0, 4 * REL_MAX_DIST, dtype=np.int64)
    max_exact = REL_BUCKETS // 2
    nf = np.maximum(n, 1).astype(np.float32)
    large = max_exact + (np.log(nf / np.float32(max_exact)) / np.float32(math.log(REL_MAX_DIST / max_exact))
                         * np.float32(REL_BUCKETS - max_exact)).astype(np.int32)
    large = np.minimum(large, REL_BUCKETS - 1)
    bucket = np.where(n < max_exact, n, large)
    assert np.all(np.diff(bucket) >= 0) and bucket[-1] == REL_BUCKETS - 1
    return [int(np.argmax(bucket >= b)) for b in range(REL_BUCKETS)]


_BUCKET_LO = _t5_bucket_lower_bounds()


def _bias_table_body(rel_ref, o_ref):
    blk = MOBA_BLOCK
    row = lax.broadcasted_iota(I32, (blk, blk), 0)
    col = lax.broadcasted_iota(I32, (blk, blk), 1)
    for t in range(2):
        d = row - col + t * blk
        for h in range(A_HEADS):
            val = jnp.full((blk, blk), rel_ref[REL_BUCKETS - 1, h], F32)
            for b in range(REL_BUCKETS - 2, -1, -1):
                val = jnp.where(d < _BUCKET_LO[b + 1], rel_ref[b, h], val)
            val = val * LOG2E
            if t == 0:
                val = jnp.where(d >= 0, val, -jnp.inf)
            o_ref[h, t] = val


def _bias_tables(rel_bias):
    return pl.pallas_call(
        _bias_table_body,
        in_specs=[pl.BlockSpec(memory_space=pltpu.SMEM)],
        out_shape=jax.ShapeDtypeStruct((A_HEADS, 2, MOBA_BLOCK, MOBA_BLOCK), F32),
        compiler_params=pltpu.CompilerParams(vmem_limit_bytes=VMEM_LIMIT),
        name="rel_bias_tables",
    )(rel_bias)


def _moba_body(rel_ref, q_ref, k_ref, v_ref, tb_ref, o_ref, s_s, *, seq):
    blk, dh = MOBA_BLOCK, A_DH
    nb = seq // blk
    h = pl.program_id(1)
    kmean = jnp.mean(k_ref[0].astype(F32).reshape(nb, blk, dh), axis=1)
    kmean_pad = jnp.concatenate([kmean, jnp.zeros((LANES - nb, dh), F32)], axis=0)
    b_far = rel_ref[REL_BUCKETS - 1, h] * LOG2E
    lane = lax.broadcasted_iota(I32, (blk, LANES), 1)
    ones_bf = jnp.ones((LANES, LANES), BF16)

    for n in range(nb):
        r0 = n * blk
        s_s[r0:seq, r0:r0 + blk] = lax.dot_general(q_ref[0, r0:seq, :], k_ref[0, r0:r0 + blk, :],
                                                   (((1,), (1,)), ((), ())), preferred_element_type=F32)

    for c in range(nb):
        q = q_ref[0, c * blk:(c + 1) * blk, :]
        n_keys = (c + 1) * blk
        s = s_s[c * blk:(c + 1) * blk, 0:n_keys]
        if c > MOBA_TOPK:
            gate = lax.dot_general(q.astype(F32), kmean_pad, (((1,), (1,)), ((), ())),
                                   precision=HI, preferred_element_type=F32)
            gate = jnp.where(lane < c, gate, -jnp.inf)
        pieces = []
        for n in range(c):
            shift = b_far
            if c > MOBA_TOPK:
                gn = gate[:, n:n + 1]
                beats = (gate > gn) | ((gate == gn) & (lane < n))
                rank = jnp.dot(jnp.where(beats, 1.0, 0.0).astype(BF16), ones_bf, preferred_element_type=F32)
                shift = jnp.where(rank < MOBA_TOPK, b_far, -jnp.inf)
                shift = jnp.concatenate([shift] * (blk // LANES), axis=1)
            if n == c - 1:
                shift = tb_ref[0, 1] + (shift - b_far)
            pieces.append(s[:, n * blk:(n + 1) * blk] + shift)
        pieces.append(s[:, c * blk:(c + 1) * blk] + tb_ref[0, 0])
        logits = jnp.concatenate(pieces, axis=1) if c > 0 else pieces[0]
        m = jnp.max(logits, axis=1, keepdims=True)
        p = jnp.exp2(logits - m)
        denom = jnp.sum(p, axis=1, keepdims=True)
        o = jnp.dot(p.astype(BF16), v_ref[0, 0:n_keys, :], preferred_element_type=F32)
        o_ref[0, c * blk:(c + 1) * blk, :] = (o / denom).astype(o_ref.dtype)


def _moba_attention(qkv, rel_bias, tables):
    B, S, _ = qkv.shape
    H, dh = A_HEADS, A_DH
    assert S % MOBA_BLOCK == 0 and S // MOBA_BLOCK <= SUBLANES
    return pl.pallas_call(
        functools.partial(_moba_body, seq=S),
        grid=(B, H),
        in_specs=[pl.BlockSpec(memory_space=pltpu.SMEM),
                  pl.BlockSpec((1, S, dh), lambda b, h: (b, 0, h)),
                  pl.BlockSpec((1, S, dh), lambda b, h: (b, 0, H + h)),
                  pl.BlockSpec((1, S, dh), lambda b, h: (b, 0, 2 * H + h)),
                  pl.BlockSpec((1, 2, MOBA_BLOCK, MOBA_BLOCK), lambda b, h: (h, 0, 0, 0))],
        out_specs=pl.BlockSpec((1, S, dh), lambda b, h: (b, 0, h)),
        out_shape=jax.ShapeDtypeStruct((B, S, H * dh), BF16),
        scratch_shapes=[pltpu.VMEM((S, S), F32)],
        compiler_params=_cparams("arbitrary", "arbitrary"),
        name="moba_attention",
    )(rel_bias, qkv, qkv, qkv, tables)


def _gmlp_body(u_ref, v_ref, h_ref, vg_ref, vb_ref, ws_ref, bs_ref, wo_ref, g_ref, b_ref, o_ref,
               wst_s, wob_s, *, tm):
    @pl.when(pl.program_id(0) == 0)
    def _():
        row = lax.broadcasted_iota(I32, (B_CHUNK, B_CHUNK), 0)
        col = lax.broadcasted_iota(I32, (B_CHUNK, B_CHUNK), 1)
        for g in range(B_GROUPS):
            wst_s[g] = jnp.where(row >= col, ws_ref[g], 0.0).astype(BF16)
        wob_s[...] = wo_ref[...].astype(BF16)

    vn = _ln(v_ref[...].astype(F32), vg_ref[...], vb_ref[...]).astype(BF16)
    u = u_ref[...].astype(F32)
    rows = []
    for c in range(tm // B_CHUNK):
        r0 = c * B_CHUNK
        cols = []
        for g in range(B_GROUPS):
            c0 = g * B_GW
            mixed = jnp.dot(wst_s[g], vn[r0:r0 + B_CHUNK, c0:c0 + B_GW],
                            preferred_element_type=F32) + bs_ref[g]
            cols.append(u[r0:r0 + B_CHUNK, c0:c0 + B_GW] * mixed)
        rows.append(jnp.concatenate(cols, axis=1))
    y = jnp.concatenate(rows, axis=0).astype(BF16)
    t = jnp.dot(y, wob_s[...], preferred_element_type=F32)
    o_ref[...] = _ln(ALPHA * h_ref[...] + t, g_ref[...], b_ref[...])


def _gmlp_gate_out(z, h, v_g, v_b, w_s, b_s, w_out, g, b, *, tm=256):
    T = z.shape[0]
    D = D_MODEL
    return pl.pallas_call(
        functools.partial(_gmlp_body, tm=tm),
        grid=(T // tm,),
        in_specs=[pl.BlockSpec((tm, B_WIDTH), lambda i: (i, 0)),
                  pl.BlockSpec((tm, B_WIDTH), lambda i: (i, 1)),
                  pl.BlockSpec((tm, D), lambda i: (i, 0)),
                  pl.BlockSpec((1, B_WIDTH), lambda i: (0, 0)),
                  pl.BlockSpec((1, B_WIDTH), lambda i: (0, 0)),
                  pl.BlockSpec((B_GROUPS, B_CHUNK, B_CHUNK), lambda i: (0, 0, 0)),
                  pl.BlockSpec((B_GROUPS, B_CHUNK, 1), lambda i: (0, 0, 0)),
                  pl.BlockSpec((B_WIDTH, D), lambda i: (0, 0)),
                  pl.BlockSpec((1, D), lambda i: (0, 0)),
                  pl.BlockSpec((1, D), lambda i: (0, 0))],
        out_specs=pl.BlockSpec((tm, D), lambda i: (i, 0)),
        out_shape=jax.ShapeDtypeStruct((T, D), F32),
        scratch_shapes=[pltpu.VMEM((B_GROUPS, B_CHUNK, B_CHUNK), BF16), pltpu.VMEM((B_WIDTH, D), BF16)],
        compiler_params=_cparams("arbitrary"),
        name="gmlp_gate_out",
    )(z, z, h, v_g.reshape(1, B_WIDTH), v_b.reshape(1, B_WIDTH), w_s, b_s.reshape(B_GROUPS, B_CHUNK, 1),
      w_out, g.reshape(1, D), b.reshape(1, D))


def _gla_body(q_ref, k_ref, v_ref, g_ref, gl_ref, wup_ref, bgk_ref, ng_ref, o_ref, la_s, st_s, raw_s, *, seq):
    ch, sub, dk, dv = GLA_CHUNK, GLA_SUB, C_DK, C_DV
    x = jnp.dot(gl_ref[0].astype(F32), wup_ref[...], precision=HI, preferred_element_type=F32) + bgk_ref[...]
    la_s[...] = (jnp.minimum(x, 0.0) - jnp.log1p(jnp.exp(-jnp.abs(x)))) * (LOG2E / C_GATE_NORMALIZER)

    @pl.when(pl.program_id(2) == 0)
    def _():
        st_s[...] = jnp.zeros(st_s.shape, F32)

    row_c = lax.broadcasted_iota(I32, (ch, ch), 0)
    col_c = lax.broadcasted_iota(I32, (ch, ch), 1)
    tri = (row_c >= col_c).astype(F32)
    eye = (row_c == col_c).astype(F32)
    sub_i = lax.broadcasted_iota(I32, (sub, ch), 0)
    lane_j = lax.broadcasted_iota(I32, (sub, ch), 1)

    def chunk(c, carry):
        for hh in range(GLA_HEADS_PER_STEP):
            chunk_head(c, hh)
        return carry

    def chunk_head(c, hh):
        s0 = pl.multiple_of(c * ch, ch)
        a = la_s[pl.ds(s0, ch), hh * dk:(hh + 1) * dk]
        b = jnp.dot(tri, a, precision=HI, preferred_element_type=F32)
        q = q_ref[0, pl.ds(s0, ch), hh * dk:(hh + 1) * dk].astype(F32) * (dk ** -0.5)
        k = k_ref[0, pl.ds(s0, ch), hh * dk:(hh + 1) * dk].astype(F32)
        v = v_ref[0, pl.ds(s0, ch), hh * dv:(hh + 1) * dv].astype(BF16)
        st = st_s[hh]
        o = jnp.dot((q * jnp.exp2(b)).astype(BF16), st.astype(BF16), preferred_element_type=F32)

        blocks = []
        for sb in range(ch // sub):
            r0 = sb * sub
            q_i = q[r0:r0 + sub]
            b_i = b[r0:r0 + sub]
            k_i = k[r0:r0 + sub]
            att = jnp.zeros((sub, ch), F32)
            for j in range(sub):
                e = jnp.exp2(jnp.minimum(b_i - b_i[j:j + 1, :], 0.0))
                colv = jnp.sum(q_i * k_i[j:j + 1, :] * e, axis=1, keepdims=True)
                att = jnp.where((lane_j == r0 + j) & (sub_i >= j), colv, att)
            if sb > 0:
                ref_b = b[r0 - 1:r0, :]
                q_t = q_i * jnp.exp2(b_i - ref_b)
                k_t = k * jnp.exp2(jnp.minimum(ref_b - b, 0.0))
                off = lax.dot_general(q_t.astype(BF16), k_t.astype(BF16), (((1,), (1,)), ((), ())),
                                      preferred_element_type=F32)
                att = jnp.where(lane_j < r0, off, att)
            blocks.append(att)
        att_full = jnp.concatenate(blocks, axis=0)
        o = o + jnp.dot(att_full.astype(BF16), v, preferred_element_type=F32)
        raw_s[pl.ds(s0, ch), hh * dv:(hh + 1) * dv] = o

        b_last = b[ch - 1:ch, :]
        k_d = k * jnp.exp2(b_last - b)
        upd = jnp.dot(k_d.T.astype(BF16), v, preferred_element_type=F32)
        decay_col = lax.dot_general(eye, jnp.broadcast_to(jnp.exp2(b_last), (SUBLANES, dk)),
                                    (((1,), (1,)), ((), ())), precision=HI,
                                    preferred_element_type=F32)[:, 0:1]
        st_s[hh] = decay_col * st + upd

    lax.fori_loop(0, seq // ch, chunk, 0)

    for hh in range(GLA_HEADS_PER_STEP):
        o = raw_s[:, hh * dv:(hh + 1) * dv]
        rms = o * lax.rsqrt(jnp.mean(o * o, axis=-1, keepdims=True) + LN_EPS) * ng_ref[...]
        gg = g_ref[0, :, hh * dv:(hh + 1) * dv].astype(F32)
        o_ref[0, :, hh * dv:(hh + 1) * dv] = (rms * (gg * jax.nn.sigmoid(gg))).astype(o_ref.dtype)


def _gla_core(proj, w_up_pad, b_gk, norm_g):
    B, S, _ = proj.shape
    H, hps = C_HEADS, GLA_HEADS_PER_STEP
    dk, dv = C_DK * hps, C_DV * hps
    st = GLA_SEQ_TILE
    assert C_DK == LANES and S % st == 0 and st % GLA_CHUNK == 0 and H % hps == 0
    k_off = C_KEY_DIM // dk
    v_off = 2 * C_KEY_DIM // dv
    g_off = (2 * C_KEY_DIM + C_VAL_DIM) // dv
    gl_off = (2 * C_KEY_DIM + 2 * C_VAL_DIM) // LANES
    return pl.pallas_call(
        functools.partial(_gla_body, seq=st),
        grid=(B, H // hps, S // st),
        in_specs=[pl.BlockSpec((1, st, dk), lambda b, h, s: (b, s, h)),
                  pl.BlockSpec((1, st, dk), lambda b, h, s: (b, s, k_off + h)),
                  pl.BlockSpec((1, st, dv), lambda b, h, s: (b, s, v_off + h)),
                  pl.BlockSpec((1, st, dv), lambda b, h, s: (b, s, g_off + h)),
                  pl.BlockSpec((1, st, LANES), lambda b, h, s: (b, s, gl_off)),
                  pl.BlockSpec((LANES, dk), lambda b, h, s: (0, h)),
                  pl.BlockSpec((1, dk), lambda b, h, s: (0, h)),
                  pl.BlockSpec((1, C_DV), lambda b, h, s: (0, 0))],
        out_specs=pl.BlockSpec((1, st, dv), lambda b, h, s: (b, s, h)),
        out_shape=jax.ShapeDtypeStruct((B, S, C_VAL_DIM), BF16),
        scratch_shapes=[pltpu.VMEM((st, dk), F32), pltpu.VMEM((hps, C_DK, C_DV), F32), pltpu.VMEM((st, dv), F32)],
        compiler_params=_cparams("arbitrary", "arbitrary", "arbitrary"),
        name="gla_core",
    )(proj, proj, proj, proj, proj, w_up_pad, b_gk.reshape(1, C_KEY_DIM), norm_g.reshape(1, C_DV))


def _router_body(h_ref, wt_ref, b_ref, slot_ref, gate_ref, cnt_ref, lbase_ref, gbase_ref, run_s, ws_s, *, tm):
    E = N_EXPERTS

    @pl.when(pl.program_id(0) == 0)
    def _():
        run_s[...] = jnp.zeros(run_s.shape, F32)
        w = wt_ref[...]
        w1 = w.astype(BF16)
        ws_s[0:E, :] = w1
        ws_s[E:2 * E, :] = (w - w1.astype(F32)).astype(BF16)

    h = h_ref[...]
    h1 = h.astype(BF16)
    h2 = (h - h1.astype(F32)).astype(BF16)
    nt_dims = (((1,), (1,)), ((), ()))
    both = lax.dot_general(ws_s[...], h1, nt_dims, preferred_element_type=F32)
    logits = (both[0:E] + (both[E:2 * E] + lax.dot_general(ws_s[0:E, :], h2, nt_dims, preferred_element_type=F32))
              + b_ref[...])
    row = lax.broadcasted_iota(I32, (E, tm), 0)
    rest = logits
    top_val, onehot = [], []
    for _ in range(TOP_K):
        m = jnp.max(rest, axis=0, keepdims=True)
        idx = jnp.min(jnp.where(rest == m, row, E), axis=0, keepdims=True)
        oh = row == idx
        rest = jnp.where(oh, -jnp.inf, rest)
        top_val.append(m)
        onehot.append(oh)
    ex = [jnp.exp(v - top_val[0]) for v in top_val]
    den = ex[0] + ex[1] + ex[2] + ex[3]
    chosen = sum(oh.astype(F32) for oh in onehot)

    tr = lax.broadcasted_iota(I32, (tm, tm), 0)
    tc = lax.broadcasted_iota(I32, (tm, tm), 1)
    earlier = (tr < tc).astype(BF16)
    seen = jnp.dot(chosen.astype(BF16), earlier, preferred_element_type=F32)
    cnt = jnp.sum(chosen, axis=1, keepdims=True)
    cnt_pad = jnp.floor((cnt + (MOE_SEG - 1)) * (1.0 / MOE_SEG)) * MOE_SEG
    cnt_cols = jnp.broadcast_to(cnt_pad, (E, LANES))
    er = lax.broadcasted_iota(I32, (E, E), 0)
    ec = lax.broadcasted_iota(I32, (E, E), 1)
    lbase_cols = jnp.dot((ec < er).astype(F32), cnt_cols, precision=HI, preferred_element_type=F32)
    pos = seen + lbase_cols[:, 0:1]

    row_k = lax.broadcasted_iota(I32, (TOP_K, tm), 0)
    s_out = jnp.zeros((TOP_K, tm), I32)
    g_out = jnp.zeros((TOP_K, tm), F32)
    for kk in range(TOP_K):
        sk = jnp.sum(jnp.where(onehot[kk], pos, 0.0), axis=0, keepdims=True)
        s_out = jnp.where(row_k == kk, sk.astype(I32), s_out)
        g_out = jnp.where(row_k == kk, ex[kk] / den, g_out)
    slot_ref[...] = s_out
    gate_ref[...] = g_out
    cnt_ref[...] = cnt_cols.astype(I32)
    lbase_ref[...] = lbase_cols.astype(I32)
    gbase_ref[...] = run_s[...].astype(I32)
    run_s[...] = run_s[...] + cnt_cols


def _router(h, w, b, *, tm=MOE_TM):
    T, D = h.shape
    E = N_EXPERTS
    nt = T // tm
    tab = jax.ShapeDtypeStruct((nt * E, LANES), I32)
    tab_spec = pl.BlockSpec((E, LANES), lambda i: (i, 0))
    return pl.pallas_call(
        functools.partial(_router_body, tm=tm),
        grid=(nt,),
        in_specs=[pl.BlockSpec((tm, D), lambda i: (i, 0)),
                  pl.BlockSpec((E, D), lambda i: (0, 0)),
                  pl.BlockSpec((E, 1), lambda i: (0, 0))],
        out_specs=[pl.BlockSpec((TOP_K, tm), lambda i: (0, i)),
                   pl.BlockSpec((TOP_K, tm), lambda i: (0, i)),
                   tab_spec, tab_spec, tab_spec],
        out_shape=[jax.ShapeDtypeStruct((TOP_K, T), I32),
                   jax.ShapeDtypeStruct((TOP_K, T), F32),
                   tab, tab, tab],
        scratch_shapes=[pltpu.VMEM((E, LANES), F32), pltpu.VMEM((2 * E, D), BF16)],
        compiler_params=_cparams("arbitrary"),
        name="moe_router",
    )(h, w.T, b.reshape(E, 1))


def _pack_pairs(left, right):
    hi = lax.bitcast_convert_type(left, I32) & jnp.int32(-65536)
    lo = lax.shift_right_logical(lax.bitcast_convert_type(right, I32), 16)
    return hi | lo


def _unpack_pairs(word):
    left = lax.bitcast_convert_type(word & jnp.int32(-65536), F32)
    right = lax.bitcast_convert_type(lax.shift_left(word, 16), F32)
    return left.astype(BF16), right.astype(BF16)


def _segment_starts(seg_g, seg_l, seg_n, tile, make_copy):
    def per_expert(e, carry):
        idx = tile * N_EXPERTS + e
        n = seg_n[idx]
        g = seg_g[idx]
        l = seg_l[idx]
        for bit in range(MOE_SEG_BITS - 1, -1, -1):
            size = MOE_SEG << bit

            @pl.when((n & size) != 0)
            def _():
                off = n & ~(2 * size - 1)
                make_copy(pl.multiple_of(g + off, MOE_SEG), pl.multiple_of(l + off, MOE_SEG), size).start()
        return carry

    lax.fori_loop(0, N_EXPERTS, per_expert, 0)


def _segment_waits(seg_l, seg_n, tile, make_copy):
    last = tile * N_EXPERTS + N_EXPERTS - 1
    total = seg_l[last] + seg_n[last]
    for bit in range((MOE_STAGE // MOE_SEG).bit_length() - 1, -1, -1):
        size = MOE_SEG << bit

        @pl.when((total & size) != 0)
        def _():
            make_copy(0, 0, size).wait()


def _dispatch_body(segg_ref, segl_ref, segn_ref, zf_ref, slot_t_ref, h_ref, xs_ref, stage, zbuf, sem, *, tm, bm):
    i = pl.program_id(0)

    def zero_fill(start):
        def act(cp):
            if start:
                cp.start()
            else:
                cp.wait()

        def per_expert(e, carry):
            g = zf_ref[e]
            n = zf_ref[N_EXPERTS + e]
            for bit in range((bm // MOE_SEG).bit_length() - 2, -1, -1):
                size = MOE_SEG << bit

                @pl.when((n & size) != 0)
                def _():
                    off = n & ~(2 * size - 1)
                    act(pltpu.make_async_copy(
                        zbuf.at[pl.ds(0, size), :],
                        xs_ref.at[pl.ds(pl.multiple_of(g + off, MOE_SEG), size), :], sem.at[2]))
            return carry

        def per_tail(j, carry):
            row = zf_ref[2 * N_EXPERTS + j]

            @pl.when(row >= 0)
            def _():
                act(pltpu.make_async_copy(zbuf, xs_ref.at[pl.ds(pl.multiple_of(row, MOE_SEG), bm), :], sem.at[2]))
            return carry

        lax.fori_loop(0, N_EXPERTS, per_expert, 0)
        lax.fori_loop(0, zf_ref.shape[0] - 2 * N_EXPERTS, per_tail, 0)

    @pl.when(i == 0)
    def _():
        zbuf[...] = jnp.zeros(zbuf.shape, I32)
        zero_fill(True)

    @pl.when(i == pl.num_programs(0) - 1)
    def _():
        zero_fill(False)

    slot_i = i % 2
    hb = h_ref[...].astype(BF16)
    for c in range(MOE_STAGE // MOE_PERM_CHUNK):
        r0 = c * MOE_PERM_CHUNK
        rows = lax.broadcasted_iota(I32, (MOE_PERM_CHUNK, tm), 0) + r0
        hit = rows == slot_t_ref[0:1, :]
        for kk in range(1, TOP_K):
            hit = hit | (rows == slot_t_ref[kk:kk + 1, :])
        perm = jnp.where(hit, 1.0, 0.0).astype(BF16)
        rows_f = jnp.dot(perm, hb, preferred_element_type=F32)
        stage[slot_i, r0:r0 + MOE_PERM_CHUNK, :] = _pack_pairs(rows_f[:, :MOE_PACK_W], rows_f[:, MOE_PACK_W:])

    def copier(buf):
        def make_copy(g, l, size):
            return pltpu.make_async_copy(stage.at[buf, pl.ds(l, size), :], xs_ref.at[pl.ds(g, size), :],
                                         sem.at[buf])
        return make_copy

    _segment_starts(segg_ref, segl_ref, segn_ref, i, copier(slot_i))

    @pl.when(i > 0)
    def _():
        _segment_waits(segl_ref, segn_ref, i - 1, copier(1 - slot_i))

    @pl.when(i == pl.num_programs(0) - 1)
    def _():
        _segment_waits(segl_ref, segn_ref, i, copier(slot_i))


def _dispatch(seg_g, seg_l, seg_n, zero_rows, slot_t, h, n_rows, *, tm=MOE_TM, bm=MOE_BM):
    T, D = h.shape
    grid_spec = pltpu.PrefetchScalarGridSpec(
        num_scalar_prefetch=4,
        grid=(T // tm,),
        in_specs=[pl.BlockSpec((TOP_K, tm), lambda i, *_: (0, i)),
                  pl.BlockSpec((tm, D), lambda i, *_: (i, 0))],
        out_specs=pl.BlockSpec(memory_space=pl.ANY),
        scratch_shapes=[pltpu.VMEM((2, MOE_STAGE, MOE_PACK_W), I32), pltpu.VMEM((bm, MOE_PACK_W), I32),
                        pltpu.SemaphoreType.DMA((3,))],
    )
    return pl.pallas_call(
        functools.partial(_dispatch_body, tm=tm, bm=bm),
        grid_spec=grid_spec,
        out_shape=jax.ShapeDtypeStruct((n_rows, MOE_PACK_W), I32),
        compiler_params=pltpu.CompilerParams(dimension_semantics=("arbitrary",), vmem_limit_bytes=VMEM_LIMIT,
                                             has_side_effects=True),
        name="moe_dispatch",
    )(seg_g, seg_l, seg_n, zero_rows, slot_t, h)


def _expert_body(be_ref, nb_ref, nv_ref, nx_ref, xs_ref, wgu_hbm, bgu_ref, wd_hbm, bd_ref, ys_ref,
                 wgu_f, wd_f, wgu_s, wd_s, sem, *, bm, layer):
    i = pl.program_id(0)
    e = be_ref[i]
    prev = be_ref[jnp.maximum(i - 1, 0)]
    live = i < nb_ref[0]
    n_valid = jnp.where(live, nv_ref[i], 0)

    def weight_copies(expert):
        return (pltpu.make_async_copy(wgu_hbm.at[layer, expert], wgu_f, sem.at[0]),
                pltpu.make_async_copy(wd_hbm.at[layer, expert], wd_f, sem.at[1]))

    @pl.when(i == 0)
    def _():
        for cp in weight_copies(e):
            cp.start()

    @pl.when(live & ((i == 0) | (e != prev)))
    def _():
        for cp in weight_copies(e):
            cp.wait()
        wgu_s[...] = wgu_f[...].astype(BF16)
        wd_s[...] = wd_f[...].astype(BF16)
        nxt = nx_ref[e]

        @pl.when(nxt >= 0)
        def _():
            for cp in weight_copies(nxt):
                cp.start()

    def mlp(r0, rows):
        x = jnp.concatenate(_unpack_pairs(xs_ref[r0:r0 + rows, :]), axis=1)
        hh = jnp.dot(x, wgu_s[...], preferred_element_type=F32) + bgu_ref[...]
        d_e = hh.shape[1] // 2
        h_gate = jnp.minimum(hh[:, :d_e], SWIGLU_LIMIT)
        h_up = jnp.clip(hh[:, d_e:], -SWIGLU_LIMIT, SWIGLU_LIMIT)
        act = h_gate * jax.nn.sigmoid(SWIGLU_ALPHA * h_gate) * (h_up + 1.0)
        y = jnp.dot(act.astype(BF16), wd_s[...], preferred_element_type=F32) + bd_ref[...]
        y = y.astype(BF16).astype(F32)
        ys_ref[r0:r0 + rows, :] = _pack_pairs(y[:, :MOE_PACK_W], y[:, MOE_PACK_W:])

    full = n_valid > bm - MOE_SUB

    @pl.when(full)
    def _():
        mlp(0, bm)

    for r0 in range(0, bm, MOE_SUB):
        @pl.when(jnp.logical_not(full) & (n_valid > r0))
        def _(r0=r0):
            mlp(r0, MOE_SUB)

        @pl.when(n_valid <= r0)
        def _(r0=r0):
            ys_ref[r0:r0 + MOE_SUB, :] = jnp.zeros((MOE_SUB, ys_ref.shape[1]), I32)


def _experts(block_e, n_live, n_valid, next_e, xs, layer, w_gu, b_gu, w_down, b_down, *, bm=MOE_BM):
    L, E, D, D2 = w_gu.shape
    n_blocks = xs.shape[0] // bm

    def blk(i, nb):
        return jnp.maximum(jnp.minimum(i, nb[0] - 1), 0)

    grid_spec = pltpu.PrefetchScalarGridSpec(
        num_scalar_prefetch=4,
        grid=(n_blocks,),
        in_specs=[pl.BlockSpec((bm, MOE_PACK_W), lambda i, be, nb, nv, nx: (blk(i, nb), 0)),
                  pl.BlockSpec(memory_space=pl.ANY),
                  pl.BlockSpec((None, None, 1, D2), lambda i, be, nb, nv, nx: (layer, be[blk(i, nb)], 0, 0)),
                  pl.BlockSpec(memory_space=pl.ANY),
                  pl.BlockSpec((None, None, 1, D), lambda i, be, nb, nv, nx: (layer, be[blk(i, nb)], 0, 0))],
        out_specs=pl.BlockSpec((bm, MOE_PACK_W), lambda i, be, nb, nv, nx: (i, 0)),
        scratch_shapes=[pltpu.VMEM((D, D2), F32), pltpu.VMEM((D2 // 2, D), F32),
                        pltpu.VMEM((D, D2), BF16), pltpu.VMEM((D2 // 2, D), BF16),
                        pltpu.SemaphoreType.DMA((2,))],
    )
    return pl.pallas_call(
        functools.partial(_expert_body, bm=bm, layer=layer),
        grid_spec=grid_spec,
        out_shape=jax.ShapeDtypeStruct(xs.shape, I32),
        compiler_params=_cparams("arbitrary"),
        name="moe_experts",
    )(block_e, n_live, n_valid, next_e, xs, w_gu, b_gu.reshape(L, E, 1, D2), w_down, b_down.reshape(L, E, 1, D))


def _combine_body(segg_ref, segl_ref, segn_ref, ys_ref, slot_ref, gate_ref, h_ref, g_ref, b_ref, o_ref,
                  stage, mix_s, sem, *, tm):
    i = pl.program_id(0)

    slot_i = i % 2

    def copier(buf):
        def make_copy(g, l, size):
            return pltpu.make_async_copy(ys_ref.at[pl.ds(g, size), :], stage.at[buf, pl.ds(l, size), :],
                                         sem.at[buf])
        return make_copy

    @pl.when(i == 0)
    def _():
        stage[...] = jnp.zeros(stage.shape, I32)
        _segment_starts(segg_ref, segl_ref, segn_ref, i, copier(slot_i))

    @pl.when(i + 1 < pl.num_programs(0))
    def _():
        _segment_starts(segg_ref, segl_ref, segn_ref, i + 1, copier(1 - slot_i))

    slot = slot_ref[...]
    gate = gate_ref[...]
    for c in range(MOE_STAGE // MOE_PERM_CHUNK):
        c0 = c * MOE_PERM_CHUNK
        cols = lax.broadcasted_iota(I32, (tm, MOE_PERM_CHUNK), 1) + c0
        w = jnp.zeros((tm, MOE_PERM_CHUNK), F32)
        for kk in range(TOP_K):
            w = jnp.where(cols == slot[:, kk:kk + 1], gate[:, kk:kk + 1], w)
        mix_s[:, c0:c0 + MOE_PERM_CHUNK] = w.astype(BF16)

    _segment_waits(segl_ref, segn_ref, i, copier(slot_i))
    left, right = _unpack_pairs(stage[slot_i])
    rows = tm // 2
    for r0 in range(0, tm, rows):
        mix = mix_s[r0:r0 + rows, :]
        y = jnp.concatenate([jnp.dot(mix, left, preferred_element_type=F32),
                             jnp.dot(mix, right, preferred_element_type=F32)], axis=1)
        o_ref[r0:r0 + rows, :] = _ln(ALPHA * h_ref[r0:r0 + rows, :] + y, g_ref[...], b_ref[...])


def _combine(seg_g, seg_l, seg_n, ys, slot, gate, h, g, b, *, tm=MOE_TM):
    T, D = h.shape
    grid_spec = pltpu.PrefetchScalarGridSpec(
        num_scalar_prefetch=3,
        grid=(T // tm,),
        in_specs=[pl.BlockSpec(memory_space=pl.ANY),
                  pl.BlockSpec((tm, TOP_K), lambda i, *_: (i, 0)),
                  pl.BlockSpec((tm, TOP_K), lambda i, *_: (i, 0)),
                  pl.BlockSpec((tm, D), lambda i, *_: (i, 0)),
                  pl.BlockSpec((1, D), lambda i, *_: (0, 0)),
                  pl.BlockSpec((1, D), lambda i, *_: (0, 0))],
        out_specs=pl.BlockSpec((tm, D), lambda i, *_: (i, 0)),
        scratch_shapes=[pltpu.VMEM((2, MOE_STAGE, MOE_PACK_W), I32), pltpu.VMEM((tm, MOE_STAGE), BF16),
                        pltpu.SemaphoreType.DMA((2,))],
    )
    return pl.pallas_call(
        functools.partial(_combine_body, tm=tm),
        grid_spec=grid_spec,
        out_shape=jax.ShapeDtypeStruct((T, D), F32),
        compiler_params=_cparams("arbitrary"),
        name="moe_combine",
    )(seg_g, seg_l, seg_n, ys, slot, gate, h, g.reshape(1, D), b.reshape(1, D))


def _moe_block(h, layer, w_router, b_router, w_gu, b_gu, w_down, b_down, g, b):
    T = h.shape[0]
    bm, tm, E = MOE_BM, MOE_TM, N_EXPERTS
    nt = T // tm
    slot_t, gate_t, cnt, lbase, gbase = _router(h, w_router, b_router)
    cnt, lbase, gbase = (t.reshape(nt, E, LANES)[:, :, 0] for t in (cnt, lbase, gbase))
    total = gbase[-1] + cnt[-1]
    padded = (total + bm - 1) // bm * bm
    pend = jnp.cumsum(padded)
    pstart = pend - padded
    seg_g = (pstart[None, :] + gbase).reshape(-1).astype(I32)
    seg_l = lbase.reshape(-1)
    seg_n = cnt.reshape(-1)
    n_blocks = -(-(T * TOP_K + nt * E * (MOE_SEG - 1)) // bm) + E
    block_start = jnp.arange(n_blocks, dtype=I32) * bm
    block_e = jnp.minimum(jnp.sum((pend[None, :] <= block_start[:, None]).astype(I32), axis=1), E - 1)
    n_live = (pend[-1:] // bm).astype(I32)
    n_valid = jnp.clip((pstart + total)[block_e] - block_start, 0, bm).astype(I32)
    tail_blk = n_live[0] + jnp.arange(n_blocks - T * TOP_K // bm, dtype=I32)
    tail_blk = jnp.where(tail_blk < n_blocks, tail_blk * bm, -1)
    zero_rows = jnp.concatenate([pstart + total, padded - total, tail_blk]).astype(I32)
    xs = _dispatch(seg_g, seg_l, seg_n, zero_rows, slot_t, h, n_blocks * bm)
    owner = jnp.where(padded > 0, jnp.arange(E, dtype=I32), E)
    later = lax.cummin(owner[::-1])[::-1]
    next_e = jnp.concatenate([later[1:], jnp.full((1,), E, I32)])
    next_e = jnp.where(next_e < E, next_e, -1).astype(I32)
    ys = _experts(block_e, n_live, n_valid, next_e, xs, layer, w_gu, b_gu, w_down, b_down)
    return _combine(seg_g, seg_l, seg_n, ys, slot_t.T, gate_t.T, h, g, b)


def kernel(x, rel_bias, a_w_in, a_w_out, b_w_in, b_ln_g, b_ln_b, b_w_s, b_b_s, b_w_out,
           c_w_in, c_w_gk_up, c_b_gk, c_norm_g, c_w_out, ln_g, ln_b,
           moe_w_router, moe_b_router, moe_w_gate_up, moe_b_gate_up, moe_w_down, moe_b_down):
    B, S, D = x.shape
    T = B * S
    h = x.reshape(T, D)
    tables = _bias_tables(rel_bias)
    for i in range(DEPTH):
        j = i // N_MIXERS
        mixer = i % N_MIXERS
        if mixer == 0:
            q_fold = jnp.where(jnp.arange(3 * D) < D, LOG2E * A_DH ** -0.5, 1.0).astype(F32)
            qkv = _matmul(h, (a_w_in[j] * q_fold).astype(BF16), out_dtype=BF16)
            o = _moba_attention(qkv.reshape(B, S, 3 * D), rel_bias, tables)
            h = _matmul_res_ln(o.reshape(T, D), a_w_out[j], h, ln_g[i, 0], ln_b[i, 0])
        elif mixer == 1:
            z = _matmul(h, b_w_in[j].astype(BF16), act="gelu", out_dtype=BF16)
            h = _gmlp_gate_out(z, h, b_ln_g[j], b_ln_b[j], b_w_s[j], b_b_s[j], b_w_out[j],
                               ln_g[i, 0], ln_b[i, 0])
        else:
            pad = LANES - C_GATE_RANK
            w_in = jnp.pad(c_w_in[j], ((0, 0), (0, pad)))
            w_up = jnp.pad(c_w_gk_up[j], ((0, pad), (0, 0)))
            proj = _matmul(h, w_in.astype(BF16), tn=640, out_dtype=BF16)
            o = _gla_core(proj.reshape(B, S, C_IN_WIDTH + pad), w_up, c_b_gk[j], c_norm_g[j])
            h = _matmul_res_ln(o.reshape(T, D), c_w_out[j], h, ln_g[i, 0], ln_b[i, 0])
        h = _moe_block(h, i, moe_w_router[i], moe_b_router[i], moe_w_gate_up, moe_b_gate_up,
                       moe_w_down, moe_b_down, ln_g[i, 1], ln_b[i, 1])
    return h.reshape(B, S, D)
```
